```python
import jax, jax.numpy as jnp
from jax import lax
import numpy as np

D_MODEL = 2048
BATCH = 8
SEQ = 2048
DEPTH = 2

HEAD_DIM = 128
N_HEADS = D_MODEL // HEAD_DIM
ATT_WIDTH = N_HEADS * HEAD_DIM
ALIBI_MAX_BIAS = 8.0
N_MIXERS = 2
N_NSA_LAYERS = (DEPTH + 1) // 2
N_DIL_LAYERS = DEPTH // 2

NSA_KV_HEADS = N_HEADS // 4
NSA_GROUP = N_HEADS // NSA_KV_HEADS
NSA_KV_WIDTH = NSA_KV_HEADS * HEAD_DIM
CMP_BLOCK = 32
CMP_STRIDE = 16
SEL_BLOCK = 64
N_SEL = 16
NSA_WINDOW = 512
SEL_QUERY_BLOCK = 16
FORCE_SCORE = 1.0e6
NSA_PROJ = ATT_WIDTH + 6 * NSA_KV_WIDTH + 3 * N_HEADS

DIL_PAIRS = ((128, 1), (512, 4), (2048, 16))
DIL_PROJ = ATT_WIDTH * (1 + 2 * len(DIL_PAIRS))
BAND_BLOCK = 128

N_EXPERTS = 16
N_EXPERT_GROUPS = 4
EXPERTS_PER_GROUP = N_EXPERTS // N_EXPERT_GROUPS
TOP_K = 2
D_EXPERT = (D_MODEL * 11) // 16
MOE_BLOCK = 256

RMS_EPS = 1e-6
NEG_INF = -1.0e30

kernel_name = 'hybrid_nsa_dilated_moe_adaln'


def rms_norm(x, g):
    xf = x.astype(jnp.float32)
    y = xf * lax.rsqrt(jnp.mean(xf * xf, axis=-1, keepdims=True) + RMS_EPS)
    return (y * g.astype(jnp.float32)).astype(x.dtype)


def alibi_slopes(n):
    return 2.0 ** (-ALIBI_MAX_BIAS * jnp.arange(1, n + 1, dtype=jnp.float32) / n)


def banded_attention(q, k, v, window, dist_scale, slopes, with_lse):
    N, L, H, dh = q.shape
    Hkv = k.shape[2]
    G = H // Hkv
    n_prev = -(-window // BAND_BLOCK)
    nb = -(-L // BAND_BLOCK)
    Lp = nb * BAND_BLOCK
    front = n_prev * BAND_BLOCK
    kl = front + BAND_BLOCK
    qp = jnp.pad(q, ((0, 0), (0, Lp - L), (0, 0), (0, 0)))
    kp = jnp.pad(k, ((0, 0), (front, Lp - L), (0, 0), (0, 0)))
    vp = jnp.pad(v, ((0, 0), (front, Lp - L), (0, 0), (0, 0)))
    sl = slopes.reshape(Hkv, G)[None, :, :, None, None]
    scale = dh ** -0.5

    def block(i):
        start = i * BAND_BLOCK
        qb = lax.dynamic_slice_in_dim(qp, start, BAND_BLOCK, axis=1).reshape(N, BAND_BLOCK, Hkv, G, dh)
        kb = lax.dynamic_slice_in_dim(kp, start, kl, axis=1)
        vb = lax.dynamic_slice_in_dim(vp, start, kl, axis=1)
        s = jnp.einsum('nqgrd,nkgd->ngrqk', qb, kb).astype(jnp.float32) * scale
        qpos = start + jnp.arange(BAND_BLOCK)
        kpos = start - front + jnp.arange(kl)
        dist = qpos[:, None] - kpos[None, :]
        valid = (dist >= 0) & (dist <= window) & (kpos[None, :] >= 0)
        s = jnp.where(valid, s - sl * (dist * dist_scale).astype(jnp.float32), NEG_INF)
        lse = jax.nn.logsumexp(s, axis=-1)
        p = jnp.exp(s - lse[..., None])
        o = jnp.einsum('ngrqk,nkgd->nqgrd', p.astype(vb.dtype), vb).reshape(N, BAND_BLOCK, H, dh)
        if with_lse:
            return o, lse.transpose(0, 3, 1, 2).reshape(N, BAND_BLOCK, H)
        return o

    res = lax.map(block, jnp.arange(nb))
    if with_lse:
        o, lse = res
        o = jnp.moveaxis(o, 0, 1).reshape(N, Lp, H, dh)[:, :L]
        lse = jnp.moveaxis(lse, 0, 1).reshape(N, Lp, H)[:, :L]
        return o, lse
    return jnp.moveaxis(res, 0, 1).reshape(N, Lp, H, dh)[:, :L]


def nsa_mixer(h, w_in, w_phi1, w_phi2, phi_pos, w_out):
    B, S, _ = h.shape
    H, G, Hkv, dh = N_HEADS, NSA_GROUP, NSA_KV_HEADS, HEAD_DIM
    proj = h @ w_in
    q = proj[..., :ATT_WIDTH].reshape(B, S, H, dh)
    kvs = [proj[..., ATT_WIDTH + j * NSA_KV_WIDTH: ATT_WIDTH + (j + 1) * NSA_KV_WIDTH].reshape(B, S, Hkv, dh)
           for j in range(6)]
    kc, vc, ks, vs, kw, vw = kvs
    gates = jax.nn.sigmoid(proj[..., ATT_WIDTH + 6 * NSA_KV_WIDTH:].reshape(B, S, H, 3))
    slopes = alibi_slopes(H)
    sl = slopes.reshape(Hkv, G)
    scale = dh ** -0.5
    qg = q.reshape(B, S, Hkv, G, dh)
    tpos = jnp.arange(S)

    n_sub = CMP_BLOCK // CMP_STRIDE
    NC = S // CMP_STRIDE - n_sub + 1

    def compress(t, j):
        ch = t.reshape(B, S // CMP_STRIDE, CMP_STRIDE, Hkv, dh)
        blk = jnp.concatenate([ch[:, i:i + NC] for i in range(n_sub)], axis=2)
        blk = blk + phi_pos[j][None, None, :, None, :]
        hid = jax.nn.gelu(jnp.einsum('bnlgd,lde->bnge', blk, w_phi1[j]))
        return jnp.einsum('bnge,ef->bngf', hid, w_phi2[j])

    k_cmp = compress(kc, 0)
    v_cmp = compress(vc, 1)
    blk_start = jnp.arange(NC) * CMP_STRIDE
    visible = (blk_start + CMP_BLOCK - 1)[None, :] <= tpos[:, None]
    dist_c = (tpos[:, None] - (blk_start + (CMP_BLOCK - 1) / 2.0)[None, :]).astype(jnp.float32)
    s = jnp.einsum('btgrd,bngd->bgrtn', qg, k_cmp).astype(jnp.float32) * scale
    s = jnp.where(visible, s - sl[None, :, :, None, None] * dist_c, NEG_INF)
    m = jnp.max(s, axis=-1, keepdims=True)
    e = jnp.where(visible, jnp.exp(s - m), 0.0)
    p_cmp = e / jnp.maximum(jnp.sum(e, axis=-1, keepdims=True), 1e-30)
    o_cmp = jnp.einsum('bgrtn,bngd->btgrd', p_cmp.astype(v_cmp.dtype), v_cmp).reshape(B, S, H, dh)

    NS = S // SEL_BLOCK
    n_sel = min(N_SEL, NS)
    ci = np.arange(NC)[:, None]
    sj = np.arange(NS)[None, :]
    overlap = np.clip(np.minimum(ci * CMP_STRIDE + CMP_BLOCK, (sj + 1) * SEL_BLOCK)
                      - np.maximum(ci * CMP_STRIDE, sj * SEL_BLOCK), 0, None) / CMP_STRIDE
    overlap = jnp.asarray(overlap, dtype=jnp.float32)
    imp = jnp.einsum('bgtn,nj->bgtj', p_cmp.sum(axis=2), overlap)
    jblk = jnp.arange(NS)[None, :]
    cur = (tpos // SEL_BLOCK)[:, None]
    forced = (jblk == 0) | (jblk == cur) | (jblk == cur - 1)
    causal_blk = jblk * SEL_BLOCK <= tpos[:, None]
    score = jnp.where(forced, FORCE_SCORE, jnp.where(causal_blk, imp, -1.0))
    _, sel_idx = lax.top_k(score, n_sel)

    kblk = ks.reshape(B, NS, SEL_BLOCK, Hkv, dh).transpose(0, 3, 1, 2, 4)
    vblk = vs.reshape(B, NS, SEL_BLOCK, Hkv, dh).transpose(0, 3, 1, 2, 4)
    QB = SEL_QUERY_BLOCK
    nq = S // QB
    q_c = qg.reshape(B, nq, QB, Hkv, G, dh).transpose(1, 0, 2, 3, 4, 5)
    idx_c = sel_idx.reshape(B, Hkv, nq, QB, n_sel).transpose(2, 0, 1, 3, 4)
    bi = jnp.arange(B)[:, None, None, None]
    gi = jnp.arange(Hkv)[None, :, None, None]

    def sel_chunk(args):
        c_i, qc, ic = args
        kg = kblk[bi, gi, ic]
        vg = vblk[bi, gi, ic]
        sc = jnp.einsum('bqgrd,bgqnld->bgrqnl', qc, kg).astype(jnp.float32) * scale
        tq = c_i * QB + jnp.arange(QB)
        kpos = ic[..., None] * SEL_BLOCK + jnp.arange(SEL_BLOCK)
        dist = tq[None, None, :, None, None] - kpos
        sc = jnp.where((dist >= 0)[:, :, None],
                       sc - sl[None, :, :, None, None, None] * dist[:, :, None].astype(jnp.float32), NEG_INF)
        p = jax.nn.softmax(sc.reshape(B, Hkv, G, QB, n_sel * SEL_BLOCK), axis=-1)
        o = jnp.einsum('bgrqm,bgqmd->bqgrd', p.astype(vg.dtype), vg.reshape(B, Hkv, QB, n_sel * SEL_BLOCK, dh))
        return o.reshape(B, QB, H, dh)

    o_slc = lax.map(sel_chunk, (jnp.arange(nq), q_c, idx_c))
    o_slc = jnp.moveaxis(o_slc, 0, 1).reshape(B, S, H, dh)

    o_win = banded_attention(q, kw, vw, NSA_WINDOW - 1, 1, slopes, False)

    o = gates[..., 0:1] * o_cmp + gates[..., 1:2] * o_slc + gates[..., 2:3] * o_win
    return o.reshape(B, S, ATT_WIDTH) @ w_out


def dilated_mixer(h, w_in, w_out):
    B, S, _ = h.shape
    H, dh = N_HEADS, HEAD_DIM
    proj = h @ w_in
    q = proj[..., :ATT_WIDTH].reshape(B, S, H, dh)
    slopes = alibi_slopes(H)
    outs, lses = [], []
    for gidx, (win, r) in enumerate(DIL_PAIRS):
        off = ATT_WIDTH + gidx * 2 * ATT_WIDTH
        k = proj[..., off:off + ATT_WIDTH].reshape(B, S, H, dh)
        v = proj[..., off + ATT_WIDTH:off + 2 * ATT_WIDTH].reshape(B, S, H, dh)
        L = S // r

        def to_sub(t):
            return t.reshape(B, L, r, H, dh).transpose(0, 2, 1, 3, 4).reshape(B * r, L, H, dh)

        o, lse = banded_attention(to_sub(q), to_sub(k), to_sub(v), win // r, r, slopes, True)
        outs.append(o.reshape(B, r, L, H, dh).transpose(0, 2, 1, 3, 4).reshape(B, S, H, dh))
        lses.append(lse.reshape(B, r, L, H).transpose(0, 2, 1, 3).reshape(B, S, H))
    wts = jax.nn.softmax(jnp.stack(lses, axis=0), axis=0)
    o = sum(wts[gidx][..., None] * outs[gidx].astype(jnp.float32) for gidx in range(len(DIL_PAIRS)))
    return o.astype(h.dtype).reshape(B, S, ATT_WIDTH) @ w_out


def route(hf, router_w, router_b):
    T = hf.shape[0]
    scores = jax.nn.sigmoid((hf @ router_w).astype(jnp.float32))
    biased = scores + router_b.astype(jnp.float32)
    grp_score = lax.top_k(biased.reshape(T, N_EXPERT_GROUPS, EXPERTS_PER_GROUP), TOP_K)[0].sum(-1)
    best = jnp.argmax(grp_score, axis=-1)
    in_group = (jnp.arange(N_EXPERTS) // EXPERTS_PER_GROUP)[None, :] == best[:, None]
    _, idx = lax.top_k(jnp.where(in_group, biased, NEG_INF), TOP_K)
    w = jnp.take_along_axis(scores, idx, axis=-1)
    return idx, w / jnp.sum(w, axis=-1, keepdims=True)


def moe_ffn(h, router_w, router_b, w_gate, w_up, w_down):
    B, S, D = h.shape
    T = B * S
    hf = h.reshape(T, D)
    idx, wts = route(hf, router_w, router_b)
    A = T * TOP_K
    flat_e = idx.reshape(A)
    flat_w = wts.reshape(A)
    flat_tok = jnp.arange(A, dtype=jnp.int32) // TOP_K
    order = jnp.argsort(flat_e)
    se = flat_e[order]
    counts = jnp.bincount(flat_e, length=N_EXPERTS)
    padded = (counts + MOE_BLOCK - 1) // MOE_BLOCK * MOE_BLOCK
    pad_end = jnp.cumsum(padded)
    pad_start = pad_end - padded
    start = jnp.cumsum(counts) - counts
    dest = pad_start[se] + jnp.arange(A) - start[se]
    nblk = -(-(A + N_EXPERTS * MOE_BLOCK) // MOE_BLOCK)
    cap = nblk * MOE_BLOCK
    slot_tok = jnp.zeros((cap,), jnp.int32).at[dest].set(flat_tok[order])
    slot_w = jnp.zeros((cap,), jnp.float32).at[dest].set(flat_w[order])
    blk_e = jnp.minimum(jnp.searchsorted(pad_end, jnp.arange(nblk) * MOE_BLOCK, side='right'), N_EXPERTS - 1)
    xin = hf[slot_tok].reshape(nblk, MOE_BLOCK, D)

    def expert_block(args):
        xb, e = args
        return (jax.nn.silu(xb @ w_gate[e]) * (xb @ w_up[e])) @ w_down[e]

    yo = lax.map(expert_block, (xin, blk_e)).reshape(cap, D)
    y = jax.ops.segment_sum(yo * slot_w[:, None].astype(yo.dtype), slot_tok, num_segments=T)
    return y.reshape(B, S, D)


def setup_inputs(seed: int = 0) -> dict:
    key = jax.random.key(seed)
    ks = jax.random.split(key, 20)
    D = D_MODEL

    def nrm(k, shape, scale):
        return jax.random.normal(k, shape, jnp.float32) * scale

    return {
        'x': nrm(ks[0], (BATCH, SEQ, D), 1.0),
        'c': nrm(ks[1], (BATCH, D), 1.0),
        'ada_w': nrm(ks[2], (DEPTH, D, 6 * D), 0.5 * D ** -0.5),
        'ada_b': nrm(ks[3], (DEPTH, 6 * D), 0.02),
        'norm_mix': 1.0 + nrm(ks[4], (DEPTH, D), 0.02),
        'norm_ffn': 1.0 + nrm(ks[5], (DEPTH, D), 0.02),
        'norm_final': 1.0 + nrm(ks[6], (D,), 0.02),
        'nsa_w_in': nrm(ks[7], (N_NSA_LAYERS, D, NSA_PROJ), D ** -0.5),
        'nsa_w_phi1': nrm(ks[8], (N_NSA_LAYERS, 2, CMP_BLOCK, HEAD_DIM, HEAD_DIM), (CMP_BLOCK * HEAD_DIM) ** -0.5),
        'nsa_w_phi2': nrm(ks[9], (N_NSA_LAYERS, 2, HEAD_DIM, HEAD_DIM), HEAD_DIM ** -0.5),
        'nsa_phi_pos': nrm(ks[10], (N_NSA_LAYERS, 2, CMP_BLOCK, HEAD_DIM), 0.1),
        'nsa_w_out': nrm(ks[11], (N_NSA_LAYERS, ATT_WIDTH, D), ATT_WIDTH ** -0.5),
        'dil_w_in': nrm(ks[12], (N_DIL_LAYERS, D, DIL_PROJ), D ** -0.5),
        'dil_w_out': nrm(ks[13], (N_DIL_LAYERS, ATT_WIDTH, D), ATT_WIDTH ** -0.5),
        'router_w': nrm(ks[14], (D, N_EXPERTS), D ** -0.5),
        'router_b': nrm(ks[15], (N_EXPERTS,), 0.01),
        'exp_w_gate': nrm(ks[16], (DEPTH, N_EXPERTS, D, D_EXPERT), D ** -0.5),
        'exp_w_up': nrm(ks[17], (DEPTH, N_EXPERTS, D, D_EXPERT), D ** -0.5),
        'exp_w_down': nrm(ks[18], (DEPTH, N_EXPERTS, D_EXPERT, D), D_EXPERT ** -0.5),
    }


def reference(x, c, ada_w, ada_b, norm_mix, norm_ffn, norm_final, nsa_w_in, nsa_w_phi1, nsa_w_phi2,
              nsa_phi_pos, nsa_w_out, dil_w_in, dil_w_out, router_w, router_b, exp_w_gate, exp_w_up,
              exp_w_down):
    cond = jax.nn.silu(c)
    for i in range(DEPTH):
        mod = cond @ ada_w[i] + ada_b[i]
        sh_m, sc_m, g_m, sh_f, sc_f, g_f = jnp.split(mod, 6, axis=-1)
        h = rms_norm(x, norm_mix[i]) * (1.0 + sc_m[:, None]) + sh_m[:, None]
        j = i // N_MIXERS
        if i % N_MIXERS == 0:
            y = nsa_mixer(h, nsa_w_in[j], nsa_w_phi1[j], nsa_w_phi2[j], nsa_phi_pos[j], nsa_w_out[j])
        else:
            y = dilated_mixer(h, dil_w_in[j], dil_w_out[j])
        x = x + g_m[:, None] * y
        h = rms_norm(x, norm_ffn[i]) * (1.0 + sc_f[:, None]) + sh_f[:, None]
        x = x + g_f[:, None] * moe_ffn(h, router_w, router_b, exp_w_gate[i], exp_w_up[i], exp_w_down[i])
    return rms_norm(x, norm_final)
```

```python
import functools

import jax
import jax.numpy as jnp
import numpy as np
from jax import lax
from jax.experimental import pallas as pl
from jax.experimental.pallas import tpu as pltpu

HEAD_DIM = 128
N_HEADS = 16
ATT_WIDTH = N_HEADS * HEAD_DIM
ALIBI_MAX_BIAS = 8.0

NSA_KV_HEADS = 4
NSA_GROUP = N_HEADS // NSA_KV_HEADS
NSA_KV_WIDTH = NSA_KV_HEADS * HEAD_DIM
CMP_BLOCK = 32
CMP_STRIDE = 16
SEL_BLOCK = 64
N_SEL = 16
NSA_WINDOW = 512
FORCE_SCORE = 1.0e6
NSA_QKV = ATT_WIDTH + 6 * NSA_KV_WIDTH
NSA_GATES = 3 * NSA_GROUP

DIL_PAIRS = ((128, 1), (512, 4), (2048, 16))
DIL_PROJ = ATT_WIDTH * (1 + 2 * len(DIL_PAIRS))

N_EXPERTS = 16
N_EXPERT_GROUPS = 4
EXPERTS_PER_GROUP = N_EXPERTS // N_EXPERT_GROUPS
TOP_K = 2
MOE_BLOCK = 256

RMS_EPS = 1e-6
NEG_INF = -1.0e30

LANES = 128
ATT_TILE = 128
VMEM_LIMIT = 56 * 1024 * 1024

_HI = lax.Precision.HIGHEST
_NT = (((1,), (1,)), ((), ()))


def _params(sem, vmem=VMEM_LIMIT):
    return pltpu.CompilerParams(dimension_semantics=sem, vmem_limit_bytes=vmem)


def _mod_kernel(c_ref, w_ref, b_ref, o_ref):
    c = c_ref[...]
    cond = c * jax.nn.sigmoid(c)
    o_ref[0] = jnp.dot(cond, w_ref[0], precision=_HI, preferred_element_type=jnp.float32) + b_ref[0]


def _modulation(c, ada_w, ada_b):
    depth, d, n = ada_w.shape
    b = c.shape[0]
    tn = 512
    return pl.pallas_call(
        _mod_kernel,
        grid=(depth, n // tn),
        in_specs=[pl.BlockSpec((b, d), lambda i, j: (0, 0)),
                  pl.BlockSpec((1, d, tn), lambda i, j: (i, 0, j)),
                  pl.BlockSpec((1, 1, tn), lambda i, j: (i, 0, j))],
        out_specs=pl.BlockSpec((1, b, tn), lambda i, j: (i, 0, j)),
        out_shape=jax.ShapeDtypeStruct((depth, b, n), jnp.float32),
        compiler_params=_params(("parallel", "parallel")),
        name="adaln_modulation",
    )(c, ada_w, ada_b.reshape(depth, 1, n))


def _rms(x, g):
    return x * lax.rsqrt(jnp.mean(x * x, axis=-1, keepdims=True) + RMS_EPS) * g


def _norm_mod_kernel(x_ref, g_ref, sc_ref, sh_ref, o_ref):
    h = _rms(x_ref[0], g_ref[...]) * (1.0 + sc_ref[0]) + sh_ref[0]
    o_ref[0] = h.astype(o_ref.dtype)


def _norm_mod(x, g, sc, sh, out_dtype):
    b, s, d = x.shape
    ts = min(256, s)
    return pl.pallas_call(
        _norm_mod_kernel,
        grid=(b, s // ts),
        in_specs=[pl.BlockSpec((1, ts, d), lambda i, j: (i, j, 0)),
                  pl.BlockSpec((1, d), lambda i, j: (0, 0)),
                  pl.BlockSpec((1, 1, d), lambda i, j: (i, 0, 0)),
                  pl.BlockSpec((1, 1, d), lambda i, j: (i, 0, 0))],
        out_specs=pl.BlockSpec((1, ts, d), lambda i, j: (i, j, 0)),
        out_shape=jax.ShapeDtypeStruct((b, s, d), out_dtype),
        compiler_params=_params(("parallel", "parallel")),
        name="rmsnorm_adaln",
    )(x, g.reshape(1, d), sc.reshape(b, 1, d), sh.reshape(b, 1, d))


def _final_norm_kernel(x_ref, g_ref, o_ref):
    o_ref[0] = _rms(x_ref[0], g_ref[...])


def _final_norm(x, g):
    b, s, d = x.shape
    ts = min(256, s)
    return pl.pallas_call(
        _final_norm_kernel,
        grid=(b, s // ts),
        in_specs=[pl.BlockSpec((1, ts, d), lambda i, j: (i, j, 0)),
                  pl.BlockSpec((1, d), lambda i, j: (0, 0))],
        out_specs=pl.BlockSpec((1, ts, d), lambda i, j: (i, j, 0)),
        out_shape=jax.ShapeDtypeStruct((b, s, d), jnp.float32),
        compiler_params=_params(("parallel", "parallel")),
        name="rmsnorm_final",
    )(x, g.reshape(1, d))


def _pair_max(vals):
    out = None
    for a in range(len(vals)):
        for b in range(a + 1, len(vals)):
            s = vals[a] + vals[b]
            out = s if out is None else jnp.maximum(out, s)
    return out


def _norm_route_kernel(x_ref, g_ref, sc_ref, sh_ref, rw_ref, rb_ref, h_ref, idx_ref, wt_ref):
    h = _rms(x_ref[0], g_ref[...]) * (1.0 + sc_ref[0]) + sh_ref[0]
    h_ref[0] = h
    logits = lax.dot_general(rw_ref[...], h, _NT, precision=_HI, preferred_element_type=jnp.float32)
    scores = jax.nn.sigmoid(logits)
    biased = scores + rb_ref[...][:, 0:1]
    rows = [biased[e:e + 1, :] for e in range(N_EXPERTS)]
    srow = [scores[e:e + 1, :] for e in range(N_EXPERTS)]
    grp = [_pair_max(rows[q * EXPERTS_PER_GROUP:(q + 1) * EXPERTS_PER_GROUP]) for q in range(N_EXPERT_GROUPS)]
    best_v = grp[0]
    best = jnp.zeros(best_v.shape, jnp.int32)
    for q in range(1, N_EXPERT_GROUPS):
        take = grp[q] > best_v
        best = jnp.where(take, q, best)
        best_v = jnp.where(take, grp[q], best_v)
    v1 = jnp.full(best_v.shape, NEG_INF, jnp.float32)
    i1 = jnp.zeros(best_v.shape, jnp.int32)
    for e in range(N_EXPERTS):
        cand = jnp.where(best == e // EXPERTS_PER_GROUP, rows[e], NEG_INF)
        take = cand > v1
        i1 = jnp.where(take, e, i1)
        v1 = jnp.where(take, cand, v1)
    v2 = jnp.full(best_v.shape, NEG_INF, jnp.float32)
    i2 = jnp.zeros(best_v.shape, jnp.int32)
    for e in range(N_EXPERTS):
        cand = jnp.where(best == e // EXPERTS_PER_GROUP, jnp.where(i1 == e, NEG_INF, rows[e]), NEG_INF)
        take = cand > v2
        i2 = jnp.where(take, e, i2)
        v2 = jnp.where(take, cand, v2)
    w1 = jnp.zeros(best_v.shape, jnp.float32)
    w2 = jnp.zeros(best_v.shape, jnp.float32)
    for e in range(N_EXPERTS):
        w1 = jnp.where(i1 == e, srow[e], w1)
        w2 = jnp.where(i2 == e, srow[e], w2)
    tot = w1 + w2
    idx_ref[0] = jnp.concatenate([i1, i2], axis=0)
    wt_ref[0] = jnp.concatenate([w1 / tot, w2 / tot], axis=0)


def _norm_route(x, g, sc, sh, router_w, router_b):
    b, s, d = x.shape
    ts = min(256, s)
    rwt = router_w.T
    rb = jnp.broadcast_to(router_b.reshape(N_EXPERTS, 1), (N_EXPERTS, LANES))
    return pl.pallas_call(
        _norm_route_kernel,
        grid=(b, s // ts),
        in_specs=[pl.BlockSpec((1, ts, d), lambda i, j: (i, j, 0)),
                  pl.BlockSpec((1, d), lambda i, j: (0, 0)),
                  pl.BlockSpec((1, 1, d), lambda i, j: (i, 0, 0)),
                  pl.BlockSpec((1, 1, d), lambda i, j: (i, 0, 0)),
                  pl.BlockSpec((N_EXPERTS, d), lambda i, j: (0, 0)),
                  pl.BlockSpec((N_EXPERTS, LANES), lambda i, j: (0, 0))],
        out_specs=[pl.BlockSpec((1, ts, d), lambda i, j: (i, j, 0)),
                   pl.BlockSpec((1, TOP_K, ts), lambda i, j: (i, 0, j)),
                   pl.BlockSpec((1, TOP_K, ts), lambda i, j: (i, 0, j))],
        out_shape=[jax.ShapeDtypeStruct((b, s, d), jnp.float32),
                   jax.ShapeDtypeStruct((b, TOP_K, s), jnp.int32),
                   jax.ShapeDtypeStruct((b, TOP_K, s), jnp.float32)],
        compiler_params=_params(("parallel", "parallel")),
        name="rmsnorm_adaln_router",
    )(x, g.reshape(1, d), sc.reshape(b, 1, d), sh.reshape(b, 1, d), rwt, rb)


def _mm_kernel(a_ref, w_ref, o_ref, *, act):
    acc = jnp.dot(a_ref[...], w_ref[...], preferred_element_type=jnp.float32)
    if act == "sigmoid":
        acc = jax.nn.sigmoid(acc)
    o_ref[...] = acc.astype(o_ref.dtype)


def _matmul(a, w, out_dtype, act=None, tm=512, tn=1024):
    m, k = a.shape
    n = w.shape[1]
    tn = min(tn, n)
    tm = min(tm, m)
    return pl.pallas_call(
        functools.partial(_mm_kernel, act=act),
        grid=(n // tn, m // tm),
        in_specs=[pl.BlockSpec((tm, k), lambda j, i: (i, 0)),
                  pl.BlockSpec((k, tn), lambda j, i: (0, j))],
        out_specs=pl.BlockSpec((tm, tn), lambda j, i: (i, j)),
        out_shape=jax.ShapeDtypeStruct((m, n), out_dtype),
        compiler_params=_params(("parallel", "parallel")),
        name="matmul" if act is None else "matmul_" + act,
    )(a, w)


def _mm_residual_kernel(a_ref, w_ref, x_ref, g_ref, o_ref):
    acc = jnp.dot(a_ref[0], w_ref[...], preferred_element_type=jnp.float32)
    o_ref[0] = x_ref[0] + g_ref[0] * acc


def _matmul_residual(a, w, x, gate, tm=512, tn=1024):
    b, s, k = a.shape
    n = w.shape[1]
    tn = min(tn, n)
    tm = min(tm, s)
    return pl.pallas_call(
        _mm_residual_kernel,
        grid=(n // tn, b, s // tm),
        in_specs=[pl.BlockSpec((1, tm, k), lambda j, bi, i: (bi, i, 0)),
                  pl.BlockSpec((k, tn), lambda j, bi, i: (0, j)),
                  pl.BlockSpec((1, tm, tn), lambda j, bi, i: (bi, i, j)),
                  pl.BlockSpec((1, 1, tn), lambda j, bi, i: (bi, 0, j))],
        out_specs=pl.BlockSpec((1, tm, tn), lambda j, bi, i: (bi, i, j)),
        out_shape=jax.ShapeDtypeStruct((b, s, n), jnp.float32),
        compiler_params=_params(("parallel", "parallel", "parallel")),
        name="matmul_gated_residual",
    )(a, w, x, gate.reshape(b, 1, n))


def _gelu_tanh(x):
    return 0.5 * x * (1.0 + jnp.tanh(0.7978845608028654 * (x + 0.044715 * (x * x * x))))


def _compress_kernel(c_ref, pa_ref, pb_ref, wa_ref, wb_ref, w2_ref, o_ref):
    c = c_ref[0, 0, 0].astype(jnp.float32)
    lo = (c + pa_ref[0]).astype(jnp.bfloat16)
    hi = (c + pb_ref[0]).astype(jnp.bfloat16)
    ha = jnp.dot(lo, wa_ref[0], preferred_element_type=jnp.float32)
    hb = jnp.dot(hi, wb_ref[0], preferred_element_type=jnp.float32)
    n = ha.shape[0]
    hid = _gelu_tanh(ha + pltpu.roll(hb, n - 1, 0))
    out = jnp.dot(hid.astype(jnp.bfloat16), w2_ref[0], preferred_element_type=jnp.float32)
    row = lax.broadcasted_iota(jnp.int32, out.shape, 0)
    o_ref[0, 0, 0] = jnp.where(row < n - 1, out, 0.0).astype(o_ref.dtype)


def _compress(kv_chunks, w_phi1, w_phi2, phi_pos):
    b, two, hkv, nch, width = kv_chunks.shape
    dh = HEAD_DIM
    half = CMP_BLOCK // 2
    w1 = w_phi1.reshape(2, 2, half * dh, dh).astype(jnp.bfloat16)
    pos = phi_pos.reshape(2, 2, 1, half * dh)
    return pl.pallas_call(
        _compress_kernel,
        grid=(b, two, hkv),
        in_specs=[pl.BlockSpec((1, 1, 1, nch, width), lambda i, j, g: (i, j, g, 0, 0)),
                  pl.BlockSpec((1, 1, width), lambda i, j, g: (j, 0, 0)),
                  pl.BlockSpec((1, 1, width), lambda i, j, g: (j, 0, 0)),
                  pl.BlockSpec((1, width, dh), lambda i, j, g: (j, 0, 0)),
                  pl.BlockSpec((1, width, dh), lambda i, j, g: (j, 0, 0)),
                  pl.BlockSpec((1, dh, dh), lambda i, j, g: (j, 0, 0))],
        out_specs=pl.BlockSpec((1, 1, 1, nch, dh), lambda i, j, g: (i, j, g, 0, 0)),
        out_shape=jax.ShapeDtypeStruct((b, two, hkv, nch, dh), jnp.bfloat16),
        compiler_params=_params(("parallel", "parallel", "parallel")),
        name="nsa_compress",
    )(kv_chunks, pos[:, 0], pos[:, 1], w1[:, 0], w1[:, 1], w_phi2.astype(jnp.bfloat16))


def _flash_step(sr, v, m_sc, l_sc, acc_sc, rows):
    m_prev = m_sc[rows]
    m_next = jnp.maximum(m_prev, jnp.max(sr, axis=-1, keepdims=True))
    alpha = jnp.exp(m_prev - m_next)
    p = jnp.exp(sr - m_next)
    l_sc[rows] = alpha * l_sc[rows] + jnp.sum(p, axis=-1, keepdims=True)
    acc_sc[rows] = alpha * acc_sc[rows] + jnp.dot(p.astype(v.dtype), v, preferred_element_type=jnp.float32)
    m_sc[rows] = m_next


def _nsa_attn_kernel(sl_ref, q_ref, kc_ref, vc_ref, ks_ref, vs_ref, kw_ref, vw_ref, gt_ref, ovl_ref, exp_ref,
                     o_ref, m_sc, l_sc, acc_sc, ocmp_sc, oslc_sc, bias_sc, *, n_cmp, n_blk, n_sel):
    t = ATT_TILE
    g = pl.program_id(1)
    qt = pl.program_id(2)
    qstart = qt * t
    scale = HEAD_DIM ** -0.5
    slopes = [sl_ref[g * NSA_GROUP + r] for r in range(NSA_GROUP)]
    q = q_ref[0]
    q4 = jnp.concatenate([q[:, r * HEAD_DIM:(r + 1) * HEAD_DIM] for r in range(NSA_GROUP)], axis=0)
    row = lax.broadcasted_iota(jnp.int32, (t, LANES), 0)
    col = lax.broadcasted_iota(jnp.int32, (t, LANES), 1)
    tpos = qstart + row

    kc = kc_ref[0, 0, 0]
    vc = vc_ref[0, 0, 0]
    s = lax.dot_general(q4, kc, _NT, preferred_element_type=jnp.float32) * scale
    visible = (col * CMP_STRIDE + (CMP_BLOCK - 1) <= tpos) & (col < n_cmp)
    dist_c = tpos.astype(jnp.float32) - (col.astype(jnp.float32) * CMP_STRIDE + (CMP_BLOCK - 1) / 2.0)
    psum = jnp.zeros((t, LANES), jnp.float32)
    for r in range(NSA_GROUP):
        rows = slice(r * t, (r + 1) * t)
        sr = jnp.where(visible, s[rows] - slopes[r] * dist_c, NEG_INF)
        e = jnp.where(visible, jnp.exp(sr - jnp.max(sr, axis=-1, keepdims=True)), 0.0)
        p = e / jnp.maximum(jnp.sum(e, axis=-1, keepdims=True), 1e-30)
        psum = psum + p
        ocmp_sc[rows] = jnp.dot(p.astype(vc.dtype), vc, preferred_element_type=jnp.float32)

    imp = jnp.dot(psum, ovl_ref[...], precision=_HI, preferred_element_type=jnp.float32)
    cur = jnp.right_shift(tpos, int(np.log2(SEL_BLOCK)))
    forced = (col == 0) | (col == cur) | (col == cur - 1)
    score = jnp.where(forced, FORCE_SCORE, jnp.where(col * SEL_BLOCK <= tpos, imp, -1.0))
    rank = jnp.zeros((t, LANES), jnp.float32)
    for j in range(n_blk):
        cj = score[:, j:j + 1]
        ahead = (cj > score) | ((cj == score) & (col > j))
        rank = rank + jnp.where(ahead, 1.0, 0.0)
    chosen = jnp.where((rank < n_sel) & (col < n_blk), 1.0, 0.0).astype(jnp.bfloat16)
    keymask = jnp.dot(chosen, exp_ref[...], preferred_element_type=jnp.float32)
    bias_sc[...] = (keymask - 1.0) * (-NEG_INF)

    dist0 = (row - col).astype(jnp.float32)

    def init():
        m_sc[...] = jnp.full(m_sc.shape, NEG_INF, jnp.float32)
        l_sc[...] = jnp.zeros(l_sc.shape, jnp.float32)
        acc_sc[...] = jnp.zeros(acc_sc.shape, jnp.float32)

    init()

    def sel_body(kt, carry):
        k0 = pl.multiple_of(kt * t, t)
        k = ks_ref[0, pl.ds(k0, t), :]
        v = vs_ref[0, pl.ds(k0, t), :]
        sb = bias_sc[:, pl.ds(k0, t)]
        sk = lax.dot_general(q4, k, _NT, preferred_element_type=jnp.float32) * scale
        dist = dist0 + (qstart - k0).astype(jnp.float32)
        for r in range(NSA_GROUP):
            rows = slice(r * t, (r + 1) * t)
            sr = jnp.where(dist >= 0, sk[rows] - slopes[r] * dist + sb, NEG_INF)
            _flash_step(sr, v, m_sc, l_sc, acc_sc, rows)
        return carry

    lax.fori_loop(0, qt + 1, sel_body, 0)
    oslc_sc[...] = acc_sc[...] / l_sc[...]

    init()

    def win_body(kt, carry):
        k0 = pl.multiple_of(kt * t, t)
        k = kw_ref[0, pl.ds(k0, t), :]
        v = vw_ref[0, pl.ds(k0, t), :]
        sk = lax.dot_general(q4, k, _NT, preferred_element_type=jnp.float32) * scale
        dist = dist0 + (qstart - k0).astype(jnp.float32)
        valid = (dist >= 0) & (dist <= NSA_WINDOW - 1)
        for r in range(NSA_GROUP):
            rows = slice(r * t, (r + 1) * t)
            sr = jnp.where(valid, sk[rows] - slopes[r] * dist, NEG_INF)
            _flash_step(sr, v, m_sc, l_sc, acc_sc, rows)
        return carry

    lax.fori_loop(jnp.maximum(qt - NSA_WINDOW // t, 0), qt + 1, win_body, 0)

    gt = gt_ref[0]
    for r in range(NSA_GROUP):
        rows = slice(r * t, (r + 1) * t)
        o = (gt[:, 3 * r:3 * r + 1] * ocmp_sc[rows] + gt[:, 3 * r + 1:3 * r + 2] * oslc_sc[rows]
             + gt[:, 3 * r + 2:3 * r + 3] * (acc_sc[rows] / l_sc[rows]))
        o_ref[0, :, r * HEAD_DIM:(r + 1) * HEAD_DIM] = o.astype(o_ref.dtype)


def _nsa_tables(s):
    n_cmp = s // CMP_STRIDE - CMP_BLOCK // CMP_STRIDE + 1
    n_blk = s // SEL_BLOCK
    ci = np.arange(LANES)[:, None]
    sj = np.arange(LANES)[None, :]
    ovl = np.clip(np.minimum(ci * CMP_STRIDE + CMP_BLOCK, (sj + 1) * SEL_BLOCK)
                  - np.maximum(ci * CMP_STRIDE, sj * SEL_BLOCK), 0, None) / CMP_STRIDE
    ovl = np.where((ci < n_cmp) & (sj < n_blk), ovl, 0.0).astype(np.float32)
    expand = (np.arange(LANES)[:, None] == (np.arange(s)[None, :] // SEL_BLOCK)).astype(np.float32)
    return n_cmp, n_blk, jnp.asarray(ovl), jnp.asarray(expand, dtype=jnp.bfloat16)


def _nsa_attention(proj, cmp_kv, gates, slopes):
    b, s, _ = proj.shape
    t = ATT_TILE
    n_cmp, n_blk, ovl, expand = _nsa_tables(s)
    assert n_cmp < LANES and n_blk <= LANES and cmp_kv.shape[3] == LANES
    qw = NSA_GROUP * HEAD_DIM
    kv0 = ATT_WIDTH // HEAD_DIM

    def kv_spec(j):
        return pl.BlockSpec((1, s, HEAD_DIM), lambda bi, g, qi, j=j: (bi, 0, kv0 + j * NSA_KV_HEADS + g))

    return pl.pallas_call(
        functools.partial(_nsa_attn_kernel, n_cmp=n_cmp, n_blk=n_blk, n_sel=min(N_SEL, n_blk)),
        grid=(b, NSA_KV_HEADS, s // t),
        in_specs=[pl.BlockSpec(memory_space=pltpu.SMEM),
                  pl.BlockSpec((1, t, qw), lambda bi, g, qi: (bi, qi, g)),
                  pl.BlockSpec((1, 1, 1, LANES, HEAD_DIM), lambda bi, g, qi: (bi, 0, g, 0, 0)),
                  pl.BlockSpec((1, 1, 1, LANES, HEAD_DIM), lambda bi, g, qi: (bi, 1, g, 0, 0)),
                  kv_spec(2), kv_spec(3), kv_spec(4), kv_spec(5),
                  pl.BlockSpec((1, t, LANES), lambda bi, g, qi: (bi, qi, g)),
                  pl.BlockSpec((LANES, LANES), lambda bi, g, qi: (0, 0)),
                  pl.BlockSpec((LANES, s), lambda bi, g, qi: (0, 0))],
        out_specs=pl.BlockSpec((1, t, qw), lambda bi, g, qi: (bi, qi, g)),
        out_shape=jax.ShapeDtypeStruct((b, s, ATT_WIDTH), jnp.bfloat16),
        scratch_shapes=[pltpu.VMEM((NSA_GROUP * t, LANES), jnp.float32) for _ in range(5)]
        + [pltpu.VMEM((t, s), jnp.float32)],
        compiler_params=_params(("parallel", "parallel", "arbitrary")),
        name="nsa_attention",
    )(slopes, proj, cmp_kv, cmp_kv, proj, proj, proj, proj, gates, ovl, expand)


def _dil_attn_kernel(sl_ref, q_ref, kp_ref, kc_ref, vp_ref, vc_ref, o_ref, lse_ref, *, window, dist_scale):
    t = ATT_TILE
    i = pl.program_id(2)
    scale = HEAD_DIM ** -0.5
    row = lax.broadcasted_iota(jnp.int32, (t, 2 * t), 0)
    col = lax.broadcasted_iota(jnp.int32, (t, 2 * t), 1)
    dist = row - col + t
    valid = (dist >= 0) & (dist <= window) & (col >= jnp.where(i > 0, 0, t))
    distf = (dist * dist_scale).astype(jnp.float32)
    lane = lax.broadcasted_iota(jnp.int32, (t, LANES), 1)
    lse_all = jnp.zeros((t, LANES), jnp.float32)
    for h in range(N_HEADS):
        hs = slice(h * HEAD_DIM, (h + 1) * HEAD_DIM)
        q = q_ref[0, :, hs]
        k = jnp.concatenate([kp_ref[0, :, hs], kc_ref[0, :, hs]], axis=0)
        v = jnp.concatenate([vp_ref[0, :, hs], vc_ref[0, :, hs]], axis=0)
        s = lax.dot_general(q, k, _NT, preferred_element_type=jnp.float32) * scale
        s = jnp.where(valid, s - sl_ref[h] * distf, NEG_INF)
        m = jnp.max(s, axis=-1, keepdims=True)
        e = jnp.exp(s - m)
        l = jnp.sum(e, axis=-1, keepdims=True)
        o = jnp.dot(e.astype(v.dtype), v, preferred_element_type=jnp.float32) / l
        o_ref[0, :, hs] = o.astype(o_ref.dtype)
        lse_all = jnp.where(lane == h, m + jnp.log(l), lse_all)
    lse_ref[0, 0] = lse_all


def _dil_attention(proj, slopes, gidx, win, r):
    b, s, w = proj.shape
    t = ATT_TILE
    ln = s // r
    nb = w // ATT_WIDTH
    view = proj.reshape(b, ln, r * w)
    kcol, vcol = 1 + 2 * gidx, 2 + 2 * gidx

    def cur(c):
        return pl.BlockSpec((1, t, ATT_WIDTH), lambda bi, rho, i, c=c: (bi, i, rho * nb + c))

    def prev(c):
        return pl.BlockSpec((1, t, ATT_WIDTH), lambda bi, rho, i, c=c: (bi, jnp.maximum(i - 1, 0), rho * nb + c))

    o, lse = pl.pallas_call(
        functools.partial(_dil_attn_kernel, window=win // r, dist_scale=r),
        grid=(b, r, ln // t),
        in_specs=[pl.BlockSpec(memory_space=pltpu.SMEM),
                  cur(0), prev(kcol), cur(kcol), prev(vcol), cur(vcol)],
        out_specs=[pl.BlockSpec((1, t, ATT_WIDTH), lambda bi, rho, i: (bi, i, rho)),
                   pl.BlockSpec((1, 1, t, LANES), lambda bi, rho, i: (bi, rho, i, 0))],
        out_shape=[jax.ShapeDtypeStruct((b, ln, r * ATT_WIDTH), jnp.bfloat16),
                   jax.ShapeDtypeStruct((b, r, ln, LANES), jnp.float32)],
        compiler_params=_params(("parallel", "parallel", "arbitrary")),
        name="dilated_attention_r%d" % r,
    )(slopes, view, view, view, view, view)
    return o.reshape(b, s, ATT_WIDTH), lse.transpose(0, 2, 1, 3).reshape(b, s, LANES)


def _dil_merge_kernel(o0_ref, o1_ref, o2_ref, l0_ref, l1_ref, l2_ref, o_ref):
    ls = [l0_ref[0], l1_ref[0], l2_ref[0]]
    os_ = [o0_ref, o1_ref, o2_ref]
    m = jnp.maximum(jnp.maximum(ls[0], ls[1]), ls[2])
    es = [jnp.exp(l - m) for l in ls]
    den = es[0] + es[1] + es[2]
    ws = [e / den for e in es]
    for h in range(N_HEADS):
        hs = slice(h * HEAD_DIM, (h + 1) * HEAD_DIM)
        acc = ws[0][:, h:h + 1] * os_[0][0, :, hs].astype(jnp.float32)
        acc = acc + ws[1][:, h:h + 1] * os_[1][0, :, hs].astype(jnp.float32)
        acc = acc + ws[2][:, h:h + 1] * os_[2][0, :, hs].astype(jnp.float32)
        o_ref[0, :, hs] = acc.astype(o_ref.dtype)


def _dil_merge(outs, lses):
    b, s, w = outs[0].shape
    ts = min(256, s)
    ospec = pl.BlockSpec((1, ts, w), lambda i, j: (i, j, 0))
    lspec = pl.BlockSpec((1, ts, LANES), lambda i, j: (i, j, 0))
    return pl.pallas_call(
        _dil_merge_kernel,
        grid=(b, s // ts),
        in_specs=[ospec, ospec, ospec, lspec, lspec, lspec],
        out_specs=ospec,
        out_shape=jax.ShapeDtypeStruct((b, s, w), jnp.bfloat16),
        compiler_params=_params(("parallel", "parallel")),
        name="dilated_merge",
    )(*outs, *lses)


def _gather_rows(idx_ref, src_hbm, dst, sem, n):
    def body(r, carry):
        pltpu.make_async_copy(src_hbm.at[pl.ds(idx_ref[0, 0, r], 1)], dst.at[pl.ds(r, 1)], sem).start()
        return carry
    lax.fori_loop(0, n, body, 0)


def _wait_rows(src_hbm, dst, sem, n):
    def body(r, carry):
        pltpu.make_async_copy(src_hbm.at[pl.ds(0, 1)], dst.at[pl.ds(r, 1)], sem).wait()
        return carry
    lax.fori_loop(0, n, body, 0)


def _expert_kernel(be_ref, tokc_ref, tokn_ref, h_hbm, wg_ref, wu_ref, wd_ref, o_ref, xbuf, sem):
    del be_ref
    blk = pl.program_id(0)
    nblk = pl.num_programs(0)
    slot = blk % 2

    @pl.when(blk == 0)
    def _():
        _gather_rows(tokc_ref, h_hbm, xbuf.at[0], sem.at[0], MOE_BLOCK)

    @pl.when(blk + 1 < nblk)
    def _():
        _gather_rows(tokn_ref, h_hbm, xbuf.at[1 - slot], sem.at[1 - slot], MOE_BLOCK)

    _wait_rows(h_hbm, xbuf.at[slot], sem.at[slot], MOE_BLOCK)
    x = xbuf[slot].astype(jnp.bfloat16)
    gate = jnp.dot(x, wg_ref[0], preferred_element_type=jnp.float32)
    up = jnp.dot(x, wu_ref[0], preferred_element_type=jnp.float32)
    hid = (gate * jax.nn.sigmoid(gate) * up).astype(jnp.bfloat16)
    o_ref[...] = jnp.dot(hid, wd_ref[0], preferred_element_type=jnp.float32)


def _expert_blocks(h2d, slot_tok, blk_e, w_gate, w_up, w_down):
    t, d = h2d.shape
    nblk = blk_e.shape[0]
    de = w_gate.shape[2]
    tok = slot_tok.reshape(nblk, 1, MOE_BLOCK)
    grid_spec = pltpu.PrefetchScalarGridSpec(
        num_scalar_prefetch=1,
        grid=(nblk,),
        in_specs=[pl.BlockSpec((1, 1, MOE_BLOCK), lambda i, be: (i, 0, 0), memory_space=pltpu.SMEM),
                  pl.BlockSpec((1, 1, MOE_BLOCK), lambda i, be: (jnp.minimum(i + 1, nblk - 1), 0, 0),
                               memory_space=pltpu.SMEM),
                  pl.BlockSpec(memory_space=pl.ANY),
                  pl.BlockSpec((1, d, de), lambda i, be: (be[i], 0, 0)),
                  pl.BlockSpec((1, d, de), lambda i, be: (be[i], 0, 0)),
                  pl.BlockSpec((1, de, d), lambda i, be: (be[i], 0, 0))],
        out_specs=pl.BlockSpec((MOE_BLOCK, d), lambda i, be: (i, 0)),
        scratch_shapes=[pltpu.VMEM((2, MOE_BLOCK, d), jnp.float32), pltpu.SemaphoreType.DMA((2,))],
    )
    return pl.pallas_call(
        _expert_kernel,
        grid_spec=grid_spec,
        out_shape=jax.ShapeDtypeStruct((nblk * MOE_BLOCK, d), jnp.float32),
        compiler_params=_params(("arbitrary",)),
        name="moe_expert_blocks",
    )(blk_e, tok, tok, h2d, w_gate, w_up, w_down)


def _combine_kernel(d0c_ref, d1c_ref, d0n_ref, d1n_ref, x_ref, w0_ref, w1_ref, g_ref, yo_hbm, o_ref, buf, sem):
    i = pl.program_id(0)
    n = pl.num_programs(0)
    slot = i % 2
    tt = x_ref.shape[0]

    @pl.when(i == 0)
    def _():
        _gather_rows(d0c_ref, yo_hbm, buf.at[0, 0], sem.at[0], tt)
        _gather_rows(d1c_ref, yo_hbm, buf.at[0, 1], sem.at[0], tt)

    @pl.when(i + 1 < n)
    def _():
        _gather_rows(d0n_ref, yo_hbm, buf.at[1 - slot, 0], sem.at[1 - slot], tt)
        _gather_rows(d1n_ref, yo_hbm, buf.at[1 - slot, 1], sem.at[1 - slot], tt)

    _wait_rows(yo_hbm, buf.at[slot, 0], sem.at[slot], tt)
    _wait_rows(yo_hbm, buf.at[slot, 1], sem.at[slot], tt)
    y = w0_ref[...][:, 0:1] * buf[slot, 0] + w1_ref[...][:, 0:1] * buf[slot, 1]
    o_ref[...] = x_ref[...] + g_ref[0] * y


def _moe_combine(x2d, yo, dest, wts, gate, s):
    t, d = x2d.shape
    tt = min(256, s)
    nt = t // tt
    d0 = dest[:, 0].reshape(nt, 1, tt)
    d1 = dest[:, 1].reshape(nt, 1, tt)
    w0 = jnp.broadcast_to(wts[:, 0:1], (t, LANES))
    w1 = jnp.broadcast_to(wts[:, 1:2], (t, LANES))
    b = gate.shape[0]
    cur = pl.BlockSpec((1, 1, tt), lambda i: (i, 0, 0), memory_space=pltpu.SMEM)
    nxt = pl.BlockSpec((1, 1, tt), lambda i: (jnp.minimum(i + 1, nt - 1), 0, 0), memory_space=pltpu.SMEM)
    return pl.pallas_call(
        _combine_kernel,
        grid=(nt,),
        in_specs=[cur, cur, nxt, nxt,
                  pl.BlockSpec((tt, d), lambda i: (i, 0)),
                  pl.BlockSpec((tt, LANES), lambda i: (i, 0)),
                  pl.BlockSpec((tt, LANES), lambda i: (i, 0)),
                  pl.BlockSpec((1, 1, d), lambda i: (i * tt // s, 0, 0)),
                  pl.BlockSpec(memory_space=pl.ANY)],
        out_specs=pl.BlockSpec((tt, d), lambda i: (i, 0)),
        out_shape=jax.ShapeDtypeStruct((t, d), jnp.float32),
        scratch_shapes=[pltpu.VMEM((2, 2, tt, d), jnp.float32), pltpu.SemaphoreType.DMA((2,))],
        compiler_params=_params(("arbitrary",)),
        name="moe_combine_residual",
    )(d0, d1, d0, d1, x2d, w0, w1, gate.reshape(b, 1, d), yo)


def _dispatch_tables(idx):
    t = idx.shape[0]
    a = t * TOP_K
    flat_e = idx.reshape(a)
    onehot = (flat_e[:, None] == jnp.arange(N_EXPERTS, dtype=jnp.int32)[None, :]).astype(jnp.int32)
    csum = jnp.cumsum(onehot, axis=0)
    rank = jnp.take_along_axis(csum, flat_e[:, None], axis=1)[:, 0] - 1
    counts = csum[-1]
    padded = (counts + MOE_BLOCK - 1) // MOE_BLOCK * MOE_BLOCK
    pad_end = jnp.cumsum(padded)
    pad_start = pad_end - padded
    dest = pad_start[flat_e] + rank
    nblk = -(-(a + N_EXPERTS * MOE_BLOCK) // MOE_BLOCK)
    cap = nblk * MOE_BLOCK
    slot_tok = jnp.zeros((cap,), jnp.int32).at[dest].set(jnp.arange(a, dtype=jnp.int32) // TOP_K)
    blk_e = jnp.minimum(jnp.searchsorted(pad_end, jnp.arange(nblk, dtype=jnp.int32) * MOE_BLOCK, side='right'),
                        N_EXPERTS - 1).astype(jnp.int32)
    return dest.reshape(t, TOP_K).astype(jnp.int32), slot_tok, blk_e


def _moe_layer(x, g, sc, sh, gate, router_w, router_b, w_gate, w_up, w_down):
    b, s, d = x.shape
    t = b * s
    h, idx, wts = _norm_route(x, g, sc, sh, router_w, router_b)
    idx = idx.transpose(0, 2, 1).reshape(t, TOP_K)
    wts = wts.transpose(0, 2, 1).reshape(t, TOP_K)
    dest, slot_tok, blk_e = _dispatch_tables(idx)
    yo = _expert_blocks(h.reshape(t, d), slot_tok, blk_e, w_gate.astype(jnp.bfloat16),
                        w_up.astype(jnp.bfloat16), w_down.astype(jnp.bfloat16))
    return _moe_combine(x.reshape(t, d), yo, dest, wts, gate, s).reshape(b, s, d)


def _nsa_layer(h, x, gate, slopes, w_in, w_phi1, w_phi2, phi_pos, w_out):
    b, s, d = h.shape
    t = b * s
    h2d = h.reshape(t, d)
    proj = _matmul(h2d, w_in[:, :NSA_QKV].astype(jnp.bfloat16), jnp.bfloat16).reshape(b, s, NSA_QKV)
    wg = w_in[:, NSA_QKV:].reshape(d, NSA_KV_HEADS, NSA_GATES)
    wg = jnp.pad(wg, ((0, 0), (0, 0), (0, LANES - NSA_GATES))).reshape(d, NSA_KV_HEADS * LANES)
    gates = _matmul(h2d, wg.astype(jnp.bfloat16), jnp.float32, act="sigmoid").reshape(b, s, NSA_KV_HEADS * LANES)
    nch = s // CMP_STRIDE
    kv = proj[:, :, ATT_WIDTH:ATT_WIDTH + 2 * NSA_KV_WIDTH]
    kv = kv.reshape(b, nch, CMP_STRIDE, 2, NSA_KV_HEADS, HEAD_DIM).transpose(0, 3, 4, 1, 2, 5)
    cmp_kv = _compress(kv.reshape(b, 2, NSA_KV_HEADS, nch, CMP_STRIDE * HEAD_DIM), w_phi1, w_phi2, phi_pos)
    o = _nsa_attention(proj, cmp_kv, gates, slopes)
    return _matmul_residual(o, w_out.astype(jnp.bfloat16), x, gate)


def _dil_layer(h, x, gate, slopes, w_in, w_out):
    b, s, d = h.shape
    proj = _matmul(h.reshape(b * s, d), w_in.astype(jnp.bfloat16), jnp.bfloat16).reshape(b, s, DIL_PROJ)
    outs, lses = [], []
    for gidx, (win, r) in enumerate(DIL_PAIRS):
        o, lse = _dil_attention(proj, slopes, gidx, win, r)
        outs.append(o)
        lses.append(lse)
    o = _dil_merge(outs, lses)
    return _matmul_residual(o, w_out.astype(jnp.bfloat16), x, gate)


def kernel(x, c, ada_w, ada_b, norm_mix, norm_ffn, norm_final, nsa_w_in, nsa_w_phi1, nsa_w_phi2, nsa_phi_pos,
           nsa_w_out, dil_w_in, dil_w_out, router_w, router_b, exp_w_gate, exp_w_up, exp_w_down):
    depth = ada_w.shape[0]
    d = x.shape[-1]
    mod = _modulation(c, ada_w, ada_b)
    slopes = 2.0 ** (-ALIBI_MAX_BIAS * jnp.arange(1, N_HEADS + 1, dtype=jnp.float32) / N_HEADS)
    for i in range(depth):
        sh_m, sc_m, g_m, sh_f, sc_f, g_f = [mod[i, :, k * d:(k + 1) * d] for k in range(6)]
        h = _norm_mod(x, norm_mix[i], sc_m, sh_m, jnp.bfloat16)
        j = i // 2
        if i % 2 == 0:
            x = _nsa_layer(h, x, g_m, slopes, nsa_w_in[j], nsa_w_phi1[j], nsa_w_phi2[j], nsa_phi_pos[j],
                           nsa_w_out[j])
        else:
            x = _dil_layer(h, x, g_m, slopes, dil_w_in[j], dil_w_out[j])
        x = _moe_layer(x, norm_ffn[i], sc_f, sh_f, g_f, router_w, router_b, exp_w_gate[i], exp_w_up[i],
                       exp_w_down[i])
    return _final_norm(x, norm_final)
```

```python
import functools

import jax
import jax.numpy as jnp
import numpy as np
from jax import lax
from jax.experimental import pallas as pl
from jax.experimental.pallas import tpu as pltpu

HEAD_DIM = 128
N_HEADS = 16
ATT_WIDTH = N_HEADS * HEAD_DIM
ALIBI_MAX_BIAS = 8.0

NSA_KV_HEADS = 4
NSA_GROUP = N_HEADS // NSA_KV_HEADS
NSA_KV_WIDTH = NSA_KV_HEADS * HEAD_DIM
CMP_BLOCK = 32
CMP_STRIDE = 16
SEL_BLOCK = 64
N_SEL = 16
NSA_WINDOW = 512
FORCE_SCORE = 1.0e6
NSA_QKV = ATT_WIDTH + 6 * NSA_KV_WIDTH
NSA_GATES = 3 * NSA_GROUP

DIL_PAIRS = ((128, 1), (512, 4), (2048, 16))
DIL_PROJ = ATT_WIDTH * (1 + 2 * len(DIL_PAIRS))

N_EXPERTS = 16
N_EXPERT_GROUPS = 4
EXPERTS_PER_GROUP = N_EXPERTS // N_EXPERT_GROUPS
TOP_K = 2
MOE_BLOCK = 256

RMS_EPS = 1e-6
NEG_INF = -1.0e30

LANES = 128
ATT_TILE = 128
VMEM_LIMIT = 56 * 1024 * 1024

_HI = lax.Precision.HIGHEST
_NT = (((1,), (1,)), ((), ()))


def _params(sem, vmem=VMEM_LIMIT):
    return pltpu.CompilerParams(dimension_semantics=sem, vmem_limit_bytes=vmem)


def _mod_kernel(c_ref, w_ref, b_ref, o_ref):
    c = c_ref[...]
    cond = c * jax.nn.sigmoid(c)
    o_ref[0] = jnp.dot(cond, w_ref[0], precision=_HI, preferred_element_type=jnp.float32) + b_ref[0]


def _modulation(c, ada_w, ada_b):
    depth, d, n = ada_w.shape
    b = c.shape[0]
    tn = 512
    return pl.pallas_call(
        _mod_kernel,
        grid=(depth, n // tn),
        in_specs=[pl.BlockSpec((b, d), lambda i, j: (0, 0)),
                  pl.BlockSpec((1, d, tn), lambda i, j: (i, 0, j)),
                  pl.BlockSpec((1, 1, tn), lambda i, j: (i, 0, j))],
        out_specs=pl.BlockSpec((1, b, tn), lambda i, j: (i, 0, j)),
        out_shape=jax.ShapeDtypeStruct((depth, b, n), jnp.float32),
        compiler_params=_params(("parallel", "parallel")),
        name="adaln_modulation",
    )(c, ada_w, ada_b.reshape(depth, 1, n))


def _rms(x, g):
    return x * lax.rsqrt(jnp.mean(x * x, axis=-1, keepdims=True) + RMS_EPS) * g


def _norm_mod_kernel(x_ref, g_ref, sc_ref, sh_ref, o_ref):
    h = _rms(x_ref[0], g_ref[...]) * (1.0 + sc_ref[0]) + sh_ref[0]
    o_ref[0] = h.astype(o_ref.dtype)


def _norm_mod(x, g, sc, sh, out_dtype):
    b, s, d = x.shape
    ts = min(256, s)
    return pl.pallas_call(
        _norm_mod_kernel,
        grid=(b, s // ts),
        in_specs=[pl.BlockSpec((1, ts, d), lambda i, j: (i, j, 0)),
                  pl.BlockSpec((1, d), lambda i, j: (0, 0)),
                  pl.BlockSpec((1, 1, d), lambda i, j: (i, 0, 0)),
                  pl.BlockSpec((1, 1, d), lambda i, j: (i, 0, 0))],
        out_specs=pl.BlockSpec((1, ts, d), lambda i, j: (i, j, 0)),
        out_shape=jax.ShapeDtypeStruct((b, s, d), out_dtype),
        compiler_params=_params(("parallel", "parallel")),
        name="rmsnorm_adaln",
    )(x, g.reshape(1, d), sc.reshape(b, 1, d), sh.reshape(b, 1, d))


def _store_streams(val, src_sc, out_refs, dilations):
    n = val.shape[0]
    for c in range(src_sc.shape[0]):
        src_sc[c] = val[:, c * LANES:(c + 1) * LANES]
    for r, ref in zip(dilations, out_refs):
        if r == 1:
            ref[0, 0] = val.astype(ref.dtype)
            continue
        for rho in range(r):
            for c in range(src_sc.shape[0]):
                ref[0, rho, :, c * LANES:(c + 1) * LANES] = (
                    src_sc[c, pl.ds(rho, n // r, stride=r), :].astype(ref.dtype))


def _norm_mod_streams_kernel(x_ref, g_ref, sc_ref, sh_ref, *refs, dilations):
    out_refs, h_sc = refs[:-1], refs[-1]
    h = _rms(x_ref[0], g_ref[...]) * (1.0 + sc_ref[0]) + sh_ref[0]
    _store_streams(h, h_sc, out_refs, dilations)


def _norm_mod_streams(x, g, sc, sh, dilations):
    b, s, d = x.shape
    ts = min(256, s)
    return pl.pallas_call(
        functools.partial(_norm_mod_streams_kernel, dilations=dilations),
        grid=(b, s // ts),
        in_specs=[pl.BlockSpec((1, ts, d), lambda i, j: (i, j, 0)),
                  pl.BlockSpec((1, d), lambda i, j: (0, 0)),
                  pl.BlockSpec((1, 1, d), lambda i, j: (i, 0, 0)),
                  pl.BlockSpec((1, 1, d), lambda i, j: (i, 0, 0))],
        out_specs=[pl.BlockSpec((1, r, ts // r, d), lambda i, j: (i, 0, j, 0)) for r in dilations],
        out_shape=[jax.ShapeDtypeStruct((b, r, s // r, d), jnp.bfloat16) for r in dilations],
        scratch_shapes=[pltpu.VMEM((d // LANES, ts, LANES), jnp.float32)],
        compiler_params=_params(("parallel", "parallel")),
        name="rmsnorm_adaln_streams",
    )(x, g.reshape(1, d), sc.reshape(b, 1, d), sh.reshape(b, 1, d))


def _final_norm_kernel(x_ref, g_ref, o_ref):
    o_ref[0] = _rms(x_ref[0], g_ref[...])


def _final_norm(x, g):
    b, s, d = x.shape
    ts = min(256, s)
    return pl.pallas_call(
        _final_norm_kernel,
        grid=(b, s // ts),
        in_specs=[pl.BlockSpec((1, ts, d), lambda i, j: (i, j, 0)),
                  pl.BlockSpec((1, d), lambda i, j: (0, 0))],
        out_specs=pl.BlockSpec((1, ts, d), lambda i, j: (i, j, 0)),
        out_shape=jax.ShapeDtypeStruct((b, s, d), jnp.float32),
        compiler_params=_params(("parallel", "parallel")),
        name="rmsnorm_final",
    )(x, g.reshape(1, d))


def _pair_max(vals):
    out = None
    for a in range(len(vals)):
        for b in range(a + 1, len(vals)):
            s = vals[a] + vals[b]
            out = s if out is None else jnp.maximum(out, s)
    return out


def _norm_route_kernel(x_ref, g_ref, sc_ref, sh_ref, rw_ref, rb_ref, h_ref, idx_ref, wt_ref):
    h = _rms(x_ref[0], g_ref[...]) * (1.0 + sc_ref[0]) + sh_ref[0]
    h_ref[0] = h
    logits = lax.dot_general(rw_ref[...], h, _NT, precision=_HI, preferred_element_type=jnp.float32)
    scores = jax.nn.sigmoid(logits)
    biased = scores + rb_ref[...][:, 0:1]
    rows = [biased[e:e + 1, :] for e in range(N_EXPERTS)]
    srow = [scores[e:e + 1, :] for e in range(N_EXPERTS)]
    grp = [_pair_max(rows[q * EXPERTS_PER_GROUP:(q + 1) * EXPERTS_PER_GROUP]) for q in range(N_EXPERT_GROUPS)]
    best_v = grp[0]
    best = jnp.zeros(best_v.shape, jnp.int32)
    for q in range(1, N_EXPERT_GROUPS):
        take = grp[q] > best_v
        best = jnp.where(take, q, best)
        best_v = jnp.where(take, grp[q], best_v)
    v1 = jnp.full(best_v.shape, NEG_INF, jnp.float32)
    i1 = jnp.zeros(best_v.shape, jnp.int32)
    for e in range(N_EXPERTS):
        cand = jnp.where(best == e // EXPERTS_PER_GROUP, rows[e], NEG_INF)
        take = cand > v1
        i1 = jnp.where(take, e, i1)
        v1 = jnp.where(take, cand, v1)
    v2 = jnp.full(best_v.shape, NEG_INF, jnp.float32)
    i2 = jnp.zeros(best_v.shape, jnp.int32)
    for e in range(N_EXPERTS):
        cand = jnp.where(best == e // EXPERTS_PER_GROUP, jnp.where(i1 == e, NEG_INF, rows[e]), NEG_INF)
        take = cand > v2
        i2 = jnp.where(take, e, i2)
        v2 = jnp.where(take, cand, v2)
    w1 = jnp.zeros(best_v.shape, jnp.float32)
    w2 = jnp.zeros(best_v.shape, jnp.float32)
    for e in range(N_EXPERTS):
        w1 = jnp.where(i1 == e, srow[e], w1)
        w2 = jnp.where(i2 == e, srow[e], w2)
    tot = w1 + w2
    idx_ref[0] = jnp.concatenate([i1, i2], axis=0)
    wt_ref[0] = jnp.concatenate([w1 / tot, w2 / tot], axis=0)


def _norm_route(x, g, sc, sh, router_w, router_b):
    b, s, d = x.shape
    ts = min(256, s)
    rwt = router_w.T
    rb = jnp.broadcast_to(router_b.reshape(N_EXPERTS, 1), (N_EXPERTS, LANES))
    return pl.pallas_call(
        _norm_route_kernel,
        grid=(b, s // ts),
        in_specs=[pl.BlockSpec((1, ts, d), lambda i, j: (i, j, 0)),
                  pl.BlockSpec((1, d), lambda i, j: (0, 0)),
                  pl.BlockSpec((1, 1, d), lambda i, j: (i, 0, 0)),
                  pl.BlockSpec((1, 1, d), lambda i, j: (i, 0, 0)),
                  pl.BlockSpec((N_EXPERTS, d), lambda i, j: (0, 0)),
                  pl.BlockSpec((N_EXPERTS, LANES), lambda i, j: (0, 0))],
        out_specs=[pl.BlockSpec((1, ts, d), lambda i, j: (i, j, 0)),
                   pl.BlockSpec((1, TOP_K, ts), lambda i, j: (i, 0, j)),
                   pl.BlockSpec((1, TOP_K, ts), lambda i, j: (i, 0, j))],
        out_shape=[jax.ShapeDtypeStruct((b, s, d), jnp.float32),
                   jax.ShapeDtypeStruct((b, TOP_K, s), jnp.int32),
                   jax.ShapeDtypeStruct((b, TOP_K, s), jnp.float32)],
        compiler_params=_params(("parallel", "parallel")),
        name="rmsnorm_adaln_router",
    )(x, g.reshape(1, d), sc.reshape(b, 1, d), sh.reshape(b, 1, d), rwt, rb)


def _mm_kernel(a_ref, w_ref, o_ref, *, act):
    acc = jnp.dot(a_ref[...], w_ref[...], preferred_element_type=jnp.float32)
    if act == "sigmoid":
        acc = jax.nn.sigmoid(acc)
    o_ref[...] = acc.astype(o_ref.dtype)


def _matmul(a, w, out_dtype, act=None, tm=512, tn=1024):
    m, k = a.shape
    n = w.shape[1]
    tn = min(tn, n)
    tm = min(tm, m)
    return pl.pallas_call(
        functools.partial(_mm_kernel, act=act),
        grid=(n // tn, m // tm),
        in_specs=[pl.BlockSpec((tm, k), lambda j, i: (i, 0)),
                  pl.BlockSpec((k, tn), lambda j, i: (0, j))],
        out_specs=pl.BlockSpec((tm, tn), lambda j, i: (i, j)),
        out_shape=jax.ShapeDtypeStruct((m, n), out_dtype),
        compiler_params=_params(("parallel", "parallel")),
        name="matmul" if act is None else "matmul_" + act,
    )(a, w)


def _mm_streams_kernel(a_ref, w_ref, *refs, dilations):
    out_refs, acc_sc = refs[:-1], refs[-1]
    acc = jnp.dot(a_ref[0], w_ref[...], preferred_element_type=jnp.float32)
    _store_streams(acc, acc_sc, out_refs, dilations)


def _matmul_streams(a, w, dilations, tm=512, tn=1024):
    b, s, k = a.shape
    n = w.shape[1]
    tn = min(tn, n)
    tm = min(tm, s)
    return pl.pallas_call(
        functools.partial(_mm_streams_kernel, dilations=dilations),
        grid=(n // tn, b, s // tm),
        in_specs=[pl.BlockSpec((1, tm, k), lambda j, bi, i: (bi, i, 0)),
                  pl.BlockSpec((k, tn), lambda j, bi, i: (0, j))],
        out_specs=[pl.BlockSpec((1, r, tm // r, tn), lambda j, bi, i: (bi, 0, i, j)) for r in dilations],
        out_shape=[jax.ShapeDtypeStruct((b, r, s // r, n), jnp.bfloat16) for r in dilations],
        scratch_shapes=[pltpu.VMEM((tn // LANES, tm, LANES), jnp.float32)],
        compiler_params=_params(("parallel", "parallel", "parallel")),
        name="matmul_streams",
    )(a, w)


def _mm_residual_kernel(a_ref, w_ref, x_ref, g_ref, o_ref):
    acc = jnp.dot(a_ref[0], w_ref[...], preferred_element_type=jnp.float32)
    o_ref[0] = x_ref[0] + g_ref[0] * acc


def _matmul_residual(a, w, x, gate, tm=512, tn=1024):
    b, s, k = a.shape
    n = w.shape[1]
    tn = min(tn, n)
    tm = min(tm, s)
    return pl.pallas_call(
        _mm_residual_kernel,
        grid=(n // tn, b, s // tm),
        in_specs=[pl.BlockSpec((1, tm, k), lambda j, bi, i: (bi, i, 0)),
                  pl.BlockSpec((k, tn), lambda j, bi, i: (0, j)),
                  pl.BlockSpec((1, tm, tn), lambda j, bi, i: (bi, i, j)),
                  pl.BlockSpec((1, 1, tn), lambda j, bi, i: (bi, 0, j))],
        out_specs=pl.BlockSpec((1, tm, tn), lambda j, bi, i: (bi, i, j)),
        out_shape=jax.ShapeDtypeStruct((b, s, n), jnp.float32),
        compiler_params=_params(("parallel", "parallel", "parallel")),
        name="matmul_gated_residual",
    )(a, w, x, gate.reshape(b, 1, n))


def _gelu_tanh(x):
    return 0.5 * x * (1.0 + jnp.tanh(0.7978845608028654 * (x + 0.044715 * (x * x * x))))


def _compress_kernel(c_ref, pa_ref, pb_ref, wa_ref, wb_ref, w2_ref, o_ref):
    c = c_ref[0, 0, 0].astype(jnp.float32)
    lo = (c + pa_ref[0]).astype(jnp.bfloat16)
    hi = (c + pb_ref[0]).astype(jnp.bfloat16)
    ha = jnp.dot(lo, wa_ref[0], preferred_element_type=jnp.float32)
    hb = jnp.dot(hi, wb_ref[0], preferred_element_type=jnp.float32)
    n = ha.shape[0]
    hid = _gelu_tanh(ha + pltpu.roll(hb, n - 1, 0))
    out = jnp.dot(hid.astype(jnp.bfloat16), w2_ref[0], preferred_element_type=jnp.float32)
    row = lax.broadcasted_iota(jnp.int32, out.shape, 0)
    o_ref[0, 0, 0] = jnp.where(row < n - 1, out, 0.0).astype(o_ref.dtype)


def _compress(kv_chunks, w_phi1, w_phi2, phi_pos):
    b, two, hkv, nch, width = kv_chunks.shape
    dh = HEAD_DIM
    half = CMP_BLOCK // 2
    w1 = w_phi1.reshape(2, 2, half * dh, dh).astype(jnp.bfloat16)
    pos = phi_pos.reshape(2, 2, 1, half * dh)
    return pl.pallas_call(
        _compress_kernel,
        grid=(b, two, hkv),
        in_specs=[pl.BlockSpec((1, 1, 1, nch, width), lambda i, j, g: (i, j, g, 0, 0)),
                  pl.BlockSpec((1, 1, width), lambda i, j, g: (j, 0, 0)),
                  pl.BlockSpec((1, 1, width), lambda i, j, g: (j, 0, 0)),
                  pl.BlockSpec((1, width, dh), lambda i, j, g: (j, 0, 0)),
                  pl.BlockSpec((1, width, dh), lambda i, j, g: (j, 0, 0)),
                  pl.BlockSpec((1, dh, dh), lambda i, j, g: (j, 0, 0))],
        out_specs=pl.BlockSpec((1, 1, 1, nch, dh), lambda i, j, g: (i, j, g, 0, 0)),
        out_shape=jax.ShapeDtypeStruct((b, two, hkv, nch, dh), jnp.bfloat16),
        compiler_params=_params(("parallel", "parallel", "parallel")),
        name="nsa_compress",
    )(kv_chunks, pos[:, 0], pos[:, 1], w1[:, 0], w1[:, 1], w_phi2.astype(jnp.bfloat16))


def _flash_step(sr, v, m_sc, l_sc, acc_sc, rows):
    m_prev = m_sc[rows]
    m_next = jnp.maximum(m_prev, jnp.max(sr, axis=-1, keepdims=True))
    alpha = jnp.exp(m_prev - m_next)
    p = jnp.exp(sr - m_next)
    l_sc[rows] = alpha * l_sc[rows] + jnp.sum(p, axis=-1, keepdims=True)
    acc_sc[rows] = alpha * acc_sc[rows] + jnp.dot(p.astype(v.dtype), v, preferred_element_type=jnp.float32)
    m_sc[rows] = m_next


def _nsa_attn_kernel(sl_ref, q_ref, kc_ref, vc_ref, ks_ref, vs_ref, kw_ref, vw_ref, gt_ref, ovl_ref, exp_ref,
                     o_ref, m_sc, l_sc, acc_sc, ocmp_sc, oslc_sc, bias_sc, *, n_cmp, n_blk, n_sel):
    t = ATT_TILE
    g = pl.program_id(1)
    qt = pl.program_id(2)
    qstart = qt * t
    scale = HEAD_DIM ** -0.5
    slopes = [sl_ref[g * NSA_GROUP + r] for r in range(NSA_GROUP)]
    q = q_ref[0]
    q4 = jnp.concatenate([q[:, r * HEAD_DIM:(r + 1) * HEAD_DIM] for r in range(NSA_GROUP)], axis=0)
    row = lax.broadcasted_iota(jnp.int32, (t, LANES), 0)
    col = lax.broadcasted_iota(jnp.int32, (t, LANES), 1)
    tpos = qstart + row

    kc = kc_ref[0, 0, 0]
    vc = vc_ref[0, 0, 0]
    s = lax.dot_general(q4, kc, _NT, preferred_element_type=jnp.float32) * scale
    visible = (col * CMP_STRIDE + (CMP_BLOCK - 1) <= tpos) & (col < n_cmp)
    dist_c = tpos.astype(jnp.float32) - (col.astype(jnp.float32) * CMP_STRIDE + (CMP_BLOCK - 1) / 2.0)
    psum = jnp.zeros((t, LANES), jnp.float32)
    for r in range(NSA_GROUP):
        rows = slice(r * t, (r + 1) * t)
        sr = jnp.where(visible, s[rows] - slopes[r] * dist_c, NEG_INF)
        e = jnp.where(visible, jnp.exp(sr - jnp.max(sr, axis=-1, keepdims=True)), 0.0)
        p = e / jnp.maximum(jnp.sum(e, axis=-1, keepdims=True), 1e-30)
        psum = psum + p
        ocmp_sc[rows] = jnp.dot(p.astype(vc.dtype), vc, preferred_element_type=jnp.float32)

    imp = jnp.dot(psum, ovl_ref[...], precision=_HI, preferred_element_type=jnp.float32)
    cur = jnp.right_shift(tpos, int(np.log2(SEL_BLOCK)))
    forced = (col == 0) | (col == cur) | (col == cur - 1)
    score = jnp.where(forced, FORCE_SCORE, jnp.where(col * SEL_BLOCK <= tpos, imp, -1.0))
    rank = jnp.zeros((t, LANES), jnp.float32)
    for j in range(n_blk):
        cj = score[:, j:j + 1]
        ahead = (cj > score) | ((cj == score) & (col > j))
        rank = rank + jnp.where(ahead, 1.0, 0.0)
    chosen = jnp.where((rank < n_sel) & (col < n_blk), 1.0, 0.0).astype(jnp.bfloat16)
    keymask = jnp.dot(chosen, exp_ref[...], preferred_element_type=jnp.float32)
    bias_sc[...] = (keymask - 1.0) * (-NEG_INF)

    dist0 = (row - col).astype(jnp.float32)

    def init():
        m_sc[...] = jnp.full(m_sc.shape, NEG_INF, jnp.float32)
        l_sc[...] = jnp.zeros(l_sc.shape, jnp.float32)
        acc_sc[...] = jnp.zeros(acc_sc.shape, jnp.float32)

    init()

    def sel_body(kt, carry):
        k0 = pl.multiple_of(kt * t, t)
        k = ks_ref[0, pl.ds(k0, t), :]
        v = vs_ref[0, pl.ds(k0, t), :]
        sb = bias_sc[:, pl.ds(k0, t)]
        sk = lax.dot_general(q4, k, _NT, preferred_element_type=jnp.float32) * scale
        dist = dist0 + (qstart - k0).astype(jnp.float32)
        for r in range(NSA_GROUP):
            rows = slice(r * t, (r + 1) * t)
            sr = jnp.where(dist >= 0, sk[rows] - slopes[r] * dist + sb, NEG_INF)
            _flash_step(sr, v, m_sc, l_sc, acc_sc, rows)
        return carry

    lax.fori_loop(0, qt + 1, sel_body, 0)
    oslc_sc[...] = acc_sc[...] / l_sc[...]

    init()

    def win_body(kt, carry):
        k0 = pl.multiple_of(kt * t, t)
        k = kw_ref[0, pl.ds(k0, t), :]
        v = vw_ref[0, pl.ds(k0, t), :]
        sk = lax.dot_general(q4, k, _NT, preferred_element_type=jnp.float32) * scale
        dist = dist0 + (qstart - k0).astype(jnp.float32)
        valid = (dist >= 0) & (dist <= NSA_WINDOW - 1)
        for r in range(NSA_GROUP):
            rows = slice(r * t, (r + 1) * t)
            sr = jnp.where(valid, sk[rows] - slopes[r] * dist, NEG_INF)
            _flash_step(sr, v, m_sc, l_sc, acc_sc, rows)
        return carry

    lax.fori_loop(jnp.maximum(qt - NSA_WINDOW // t, 0), qt + 1, win_body, 0)

    gt = gt_ref[0]
    for r in range(NSA_GROUP):
        rows = slice(r * t, (r + 1) * t)
        o = (gt[:, 3 * r:3 * r + 1] * ocmp_sc[rows] + gt[:, 3 * r + 1:3 * r + 2] * oslc_sc[rows]
             + gt[:, 3 * r + 2:3 * r + 3] * (acc_sc[rows] / l_sc[rows]))
        o_ref[0, :, r * HEAD_DIM:(r + 1) * HEAD_DIM] = o.astype(o_ref.dtype)


def _nsa_tables(s):
    n_cmp = s // CMP_STRIDE - CMP_BLOCK // CMP_STRIDE + 1
    n_blk = s // SEL_BLOCK
    ci = np.arange(LANES)[:, None]
    sj = np.arange(LANES)[None, :]
    ovl = np.clip(np.minimum(ci * CMP_STRIDE + CMP_BLOCK, (sj + 1) * SEL_BLOCK)
                  - np.maximum(ci * CMP_STRIDE, sj * SEL_BLOCK), 0, None) / CMP_STRIDE
    ovl = np.where((ci < n_cmp) & (sj < n_blk), ovl, 0.0).astype(np.float32)
    expand = (np.arange(LANES)[:, None] == (np.arange(s)[None, :] // SEL_BLOCK)).astype(np.float32)
    return n_cmp, n_blk, jnp.asarray(ovl), jnp.asarray(expand, dtype=jnp.bfloat16)


def _nsa_attention(proj, cmp_kv, gates, slopes):
    b, s, _ = proj.shape
    t = ATT_TILE
    n_cmp, n_blk, ovl, expand = _nsa_tables(s)
    assert n_cmp < LANES and n_blk <= LANES and cmp_kv.shape[3] == LANES
    qw = NSA_GROUP * HEAD_DIM
    kv0 = ATT_WIDTH // HEAD_DIM

    def kv_spec(j):
        return pl.BlockSpec((1, s, HEAD_DIM), lambda bi, g, qi, j=j: (bi, 0, kv0 + j * NSA_KV_HEADS + g))

    return pl.pallas_call(
        functools.partial(_nsa_attn_kernel, n_cmp=n_cmp, n_blk=n_blk, n_sel=min(N_SEL, n_blk)),
        grid=(b, NSA_KV_HEADS, s // t),
        in_specs=[pl.BlockSpec(memory_space=pltpu.SMEM),
                  pl.BlockSpec((1, t, qw), lambda bi, g, qi: (bi, qi, g)),
                  pl.BlockSpec((1, 1, 1, LANES, HEAD_DIM), lambda bi, g, qi: (bi, 0, g, 0, 0)),
                  pl.BlockSpec((1, 1, 1, LANES, HEAD_DIM), lambda bi, g, qi: (bi, 1, g, 0, 0)),
                  kv_spec(2), kv_spec(3), kv_spec(4), kv_spec(5),
                  pl.BlockSpec((1, t, LANES), lambda bi, g, qi: (bi, qi, g)),
                  pl.BlockSpec((LANES, LANES), lambda bi, g, qi: (0, 0)),
                  pl.BlockSpec((LANES, s), lambda bi, g, qi: (0, 0))],
        out_specs=pl.BlockSpec((1, t, qw), lambda bi, g, qi: (bi, qi, g)),
        out_shape=jax.ShapeDtypeStruct((b, s, ATT_WIDTH), jnp.bfloat16),
        scratch_shapes=[pltpu.VMEM((NSA_GROUP * t, LANES), jnp.float32) for _ in range(5)]
        + [pltpu.VMEM((t, s), jnp.float32)],
        compiler_params=_params(("parallel", "parallel", "arbitrary")),
        name="nsa_attention",
    )(slopes, proj, cmp_kv, cmp_kv, proj, proj, proj, proj, gates, ovl, expand)


def _dil_attn_kernel(sl_ref, q_ref, kp_ref, kc_ref, vp_ref, vc_ref, o_ref, lse_ref, *, window, dist_scale):
    t = ATT_TILE
    i = pl.program_id(2)
    scale = HEAD_DIM ** -0.5
    row = lax.broadcasted_iota(jnp.int32, (t, 2 * t), 0)
    col = lax.broadcasted_iota(jnp.int32, (t, 2 * t), 1)
    dist = row - col + t
    valid = (dist >= 0) & (dist <= window) & (col >= jnp.where(i > 0, 0, t))
    distf = (dist * dist_scale).astype(jnp.float32)
    lane = lax.broadcasted_iota(jnp.int32, (t, LANES), 1)
    lse_all = jnp.zeros((t, LANES), jnp.float32)
    for h in range(N_HEADS):
        hs = slice(h * HEAD_DIM, (h + 1) * HEAD_DIM)
        q = q_ref[0, 0, :, hs]
        k = jnp.concatenate([kp_ref[0, 0, :, hs], kc_ref[0, 0, :, hs]], axis=0)
        v = jnp.concatenate([vp_ref[0, 0, :, hs], vc_ref[0, 0, :, hs]], axis=0)
        s = lax.dot_general(q, k, _NT, preferred_element_type=jnp.float32) * scale
        s = jnp.where(valid, s - sl_ref[h] * distf, NEG_INF)
        m = jnp.max(s, axis=-1, keepdims=True)
        e = jnp.exp(s - m)
        l = jnp.sum(e, axis=-1, keepdims=True)
        o = jnp.dot(e.astype(v.dtype), v, preferred_element_type=jnp.float32) / l
        o_ref[0, 0, :, hs] = o.astype(o_ref.dtype)
        lse_all = jnp.where(lane == h, m + jnp.log(l), lse_all)
    lse_ref[0, 0] = lse_all


def _dil_attention(q, kv, slopes, win, r):
    b, _, ln, _ = q.shape
    t = ATT_TILE

    def cur(c):
        return pl.BlockSpec((1, 1, t, ATT_WIDTH), lambda bi, rho, i, c=c: (bi, rho, i, c))

    def prev(c):
        return pl.BlockSpec((1, 1, t, ATT_WIDTH), lambda bi, rho, i, c=c: (bi, rho, jnp.maximum(i - 1, 0), c))

    return pl.pallas_call(
        functools.partial(_dil_attn_kernel, window=win // r, dist_scale=r),
        grid=(b, r, ln // t),
        in_specs=[pl.BlockSpec(memory_space=pltpu.SMEM), cur(0), prev(0), cur(0), prev(1), cur(1)],
        out_specs=[pl.BlockSpec((1, 1, t, ATT_WIDTH), lambda bi, rho, i: (bi, rho, i, 0)),
                   pl.BlockSpec((1, 1, t, LANES), lambda bi, rho, i: (bi, rho, i, 0))],
        out_shape=[jax.ShapeDtypeStruct((b, r, ln, ATT_WIDTH), jnp.bfloat16),
                   jax.ShapeDtypeStruct((b, r, ln, LANES), jnp.float32)],
        compiler_params=_params(("parallel", "parallel", "arbitrary")),
        name="dilated_attention_r%d" % r,
    )(slopes, q, kv, kv, kv, kv)


def _dil_merge_kernel(*refs, dilations):
    ng = len(dilations)
    o_refs, l_refs, out_ref, o_sc, l_sc = refs[:ng], refs[ng:2 * ng], refs[2 * ng], refs[2 * ng + 1], refs[2 * ng + 2]
    ts = out_ref.shape[1]
    ls = []
    for gi, r in enumerate(dilations):
        if r == 1:
            ls.append(l_refs[gi][0, 0])
            continue
        for rho in range(r):
            l_sc[gi, pl.ds(rho, ts // r, stride=r), :] = l_refs[gi][0, rho]
            for h in range(N_HEADS):
                o_sc[gi, h, pl.ds(rho, ts // r, stride=r), :] = (
                    o_refs[gi][0, rho, :, h * HEAD_DIM:(h + 1) * HEAD_DIM].astype(jnp.float32))
        ls.append(l_sc[gi])
    m = functools.reduce(jnp.maximum, ls)
    es = [jnp.exp(l - m) for l in ls]
    den = functools.reduce(lambda a, b: a + b, es)
    ws = [e / den for e in es]
    for h in range(N_HEADS):
        hs = slice(h * HEAD_DIM, (h + 1) * HEAD_DIM)
        acc = None
        for gi, r in enumerate(dilations):
            og = o_refs[gi][0, 0, :, hs].astype(jnp.float32) if r == 1 else o_sc[gi, h]
            term = ws[gi][:, h:h + 1] * og
            acc = term if acc is None else acc + term
        out_ref[0, :, hs] = acc.astype(out_ref.dtype)


def _dil_merge(outs, lses, dilations):
    b, _, _, w = outs[0].shape
    s = outs[0].shape[1] * outs[0].shape[2]
    ts = min(256, s)
    ng = len(dilations)

    def spec(r, width):
        return pl.BlockSpec((1, r, ts // r, width), lambda i, j: (i, 0, j, 0))

    return pl.pallas_call(
        functools.partial(_dil_merge_kernel, dilations=dilations),
        grid=(b, s // ts),
        in_specs=[spec(r, w) for r in dilations] + [spec(r, LANES) for r in dilations],
        out_specs=pl.BlockSpec((1, ts, w), lambda i, j: (i, j, 0)),
        out_shape=jax.ShapeDtypeStruct((b, s, w), jnp.bfloat16),
        scratch_shapes=[pltpu.VMEM((ng, w // HEAD_DIM, ts, HEAD_DIM), jnp.float32),
                        pltpu.VMEM((ng, ts, LANES), jnp.float32)],
        compiler_params=_params(("parallel", "parallel")),
        name="dilated_merge",
    )(*outs, *lses)


def _gather_rows(idx_ref, src_hbm, dst, sem, n):
    sub = 8
    def body(i, carry):
        base = pl.multiple_of(i * sub, sub)
        for j in range(sub):
            pltpu.make_async_copy(src_hbm.at[pl.ds(idx_ref[0, 0, base + j], 1)], dst.at[pl.ds(base + j, 1)],
                                  sem).start()
        return carry
    lax.fori_loop(0, n // sub, body, 0)


def _wait_rows(src_hbm, dst, sem, n):
    pltpu.make_async_copy(src_hbm.at[pl.ds(0, n)], dst, sem).wait()


def _expert_kernel(be_ref, nu_ref, tokc_ref, tokn_ref, h_hbm, wg_ref, wu_ref, wd_ref, o_ref, xbuf, sem):
    del be_ref
    blk = pl.program_id(0)
    n_used = nu_ref[0]
    slot = blk % 2

    @pl.when(blk == 0)
    def _():
        _gather_rows(tokc_ref, h_hbm, xbuf.at[0], sem.at[0], MOE_BLOCK)

    @pl.when(blk + 1 < n_used)
    def _():
        _gather_rows(tokn_ref, h_hbm, xbuf.at[1 - slot], sem.at[1 - slot], MOE_BLOCK)

    @pl.when(blk < n_used)
    def _():
        _wait_rows(h_hbm, xbuf.at[slot], sem.at[slot], MOE_BLOCK)
        x = xbuf[slot].astype(jnp.bfloat16)
        gate = jnp.dot(x, wg_ref[0], preferred_element_type=jnp.float32)
        up = jnp.dot(x, wu_ref[0], preferred_element_type=jnp.float32)
        hid = (gate * jax.nn.sigmoid(gate) * up).astype(jnp.bfloat16)
        o_ref[...] = jnp.dot(hid, wd_ref[0], preferred_element_type=jnp.float32)

    @pl.when(blk >= n_used)
    def _():
        o_ref[...] = jnp.zeros(o_ref.shape, o_ref.dtype)


def _expert_blocks(h2d, slot_tok, blk_e, n_used, w_gate, w_up, w_down):
    t, d = h2d.shape
    nblk = blk_e.shape[0]
    de = w_gate.shape[2]
    tok = slot_tok.reshape(nblk, 1, MOE_BLOCK)
    grid_spec = pltpu.PrefetchScalarGridSpec(
        num_scalar_prefetch=2,
        grid=(nblk,),
        in_specs=[pl.BlockSpec((1, 1, MOE_BLOCK), lambda i, be, nu: (i, 0, 0), memory_space=pltpu.SMEM),
                  pl.BlockSpec((1, 1, MOE_BLOCK), lambda i, be, nu: (jnp.minimum(i + 1, nblk - 1), 0, 0),
                               memory_space=pltpu.SMEM),
                  pl.BlockSpec(memory_space=pl.ANY),
                  pl.BlockSpec((1, d, de), lambda i, be, nu: (be[i], 0, 0)),
                  pl.BlockSpec((1, d, de), lambda i, be, nu: (be[i], 0, 0)),
                  pl.BlockSpec((1, de, d), lambda i, be, nu: (be[i], 0, 0))],
        out_specs=pl.BlockSpec((MOE_BLOCK, d), lambda i, be, nu: (i, 0)),
        scratch_shapes=[pltpu.VMEM((2, MOE_BLOCK, d), jnp.float32), pltpu.SemaphoreType.DMA((2,))],
    )
    return pl.pallas_call(
        _expert_kernel,
        grid_spec=grid_spec,
        out_shape=jax.ShapeDtypeStruct((nblk * MOE_BLOCK, d), jnp.float32),
        compiler_params=_params(("arbitrary",)),
        name="moe_expert_blocks",
    )(blk_e, n_used, tok, tok, h2d, w_gate, w_up, w_down)


def _combine_kernel(d0c_ref, d1c_ref, d0n_ref, d1n_ref, x_ref, w0_ref, w1_ref, g_ref, yo_hbm, o_ref, buf, sem):
    i = pl.program_id(0)
    n = pl.num_programs(0)
    slot = i % 2
    tt = x_ref.shape[0]

    @pl.when(i == 0)
    def _():
        _gather_rows(d0c_ref, yo_hbm, buf.at[0, 0], sem.at[0], tt)
        _gather_rows(d1c_ref, yo_hbm, buf.at[0, 1], sem.at[0], tt)

    @pl.when(i + 1 < n)
    def _():
        _gather_rows(d0n_ref, yo_hbm, buf.at[1 - slot, 0], sem.at[1 - slot], tt)
        _gather_rows(d1n_ref, yo_hbm, buf.at[1 - slot, 1], sem.at[1 - slot], tt)

    _wait_rows(yo_hbm, buf.at[slot, 0], sem.at[slot], tt)
    _wait_rows(yo_hbm, buf.at[slot, 1], sem.at[slot], tt)
    y = w0_ref[...][:, 0:1] * buf[slot, 0] + w1_ref[...][:, 0:1] * buf[slot, 1]
    o_ref[...] = x_ref[...] + g_ref[0] * y


def _moe_combine(x2d, yo, dest, wts, gate, s):
    t, d = x2d.shape
    tt = min(256, s)
    nt = t // tt
    d0 = dest[:, 0].reshape(nt, 1, tt)
    d1 = dest[:, 1].reshape(nt, 1, tt)
    w0 = jnp.broadcast_to(wts[:, 0:1], (t, LANES))
    w1 = jnp.broadcast_to(wts[:, 1:2], (t, LANES))
    b = gate.shape[0]
    cur = pl.BlockSpec((1, 1, tt), lambda i: (i, 0, 0), memory_space=pltpu.SMEM)
    nxt = pl.BlockSpec((1, 1, tt), lambda i: (jnp.minimum(i + 1, nt - 1), 0, 0), memory_space=pltpu.SMEM)
    return pl.pallas_call(
        _combine_kernel,
        grid=(nt,),
        in_specs=[cur, cur, nxt, nxt,
                  pl.BlockSpec((tt, d), lambda i: (i, 0)),
                  pl.BlockSpec((tt, LANES), lambda i: (i, 0)),
                  pl.BlockSpec((tt, LANES), lambda i: (i, 0)),
                  pl.BlockSpec((1, 1, d), lambda i: (i * tt // s, 0, 0)),
                  pl.BlockSpec(memory_space=pl.ANY)],
        out_specs=pl.BlockSpec((tt, d), lambda i: (i, 0)),
        out_shape=jax.ShapeDtypeStruct((t, d), jnp.float32),
        scratch_shapes=[pltpu.VMEM((2, 2, tt, d), jnp.float32), pltpu.SemaphoreType.DMA((2,))],
        compiler_params=_params(("arbitrary",)),
        name="moe_combine_residual",
    )(d0, d1, d0, d1, x2d, w0, w1, gate.reshape(b, 1, d), yo)


def _dispatch_tables(idx):
    t = idx.shape[0]
    a = t * TOP_K
    flat_e = idx.reshape(a)
    onehot = (flat_e[:, None] == jnp.arange(N_EXPERTS, dtype=jnp.int32)[None, :]).astype(jnp.int32)
    csum = jnp.cumsum(onehot, axis=0)
    rank = jnp.take_along_axis(csum, flat_e[:, None], axis=1)[:, 0] - 1
    counts = csum[-1]
    padded = (counts + MOE_BLOCK - 1) // MOE_BLOCK * MOE_BLOCK
    pad_end = jnp.cumsum(padded)
    pad_start = pad_end - padded
    dest = pad_start[flat_e] + rank
    nblk = -(-(a + N_EXPERTS * MOE_BLOCK) // MOE_BLOCK)
    cap = nblk * MOE_BLOCK
    slot_tok = jnp.zeros((cap,), jnp.int32).at[dest].set(jnp.arange(a, dtype=jnp.int32) // TOP_K)
    blk_start = jnp.arange(nblk, dtype=jnp.int32) * MOE_BLOCK
    blk_e = jnp.minimum(jnp.sum((pad_end[None, :] <= blk_start[:, None]).astype(jnp.int32), axis=1), N_EXPERTS - 1)
    n_used = (pad_end[-1:] // MOE_BLOCK).astype(jnp.int32)
    return dest.reshape(t, TOP_K).astype(jnp.int32), slot_tok, blk_e.astype(jnp.int32), n_used


def _moe_layer(x, g, sc, sh, gate, router_w, router_b, w_gate, w_up, w_down):
    b, s, d = x.shape
    t = b * s
    h, idx, wts = _norm_route(x, g, sc, sh, router_w, router_b)
    idx = idx.transpose(0, 2, 1).reshape(t, TOP_K)
    wts = wts.transpose(0, 2, 1).reshape(t, TOP_K)
    dest, slot_tok, blk_e, n_used = _dispatch_tables(idx)
    yo = _expert_blocks(h.reshape(t, d), slot_tok, blk_e, n_used, w_gate.astype(jnp.bfloat16),
                        w_up.astype(jnp.bfloat16), w_down.astype(jnp.bfloat16))
    return _moe_combine(x.reshape(t, d), yo, dest, wts, gate, s).reshape(b, s, d)


def _nsa_layer(h, x, gate, slopes, w_in, w_phi1, w_phi2, phi_pos, w_out):
    b, s, d = h.shape
    t = b * s
    h2d = h.reshape(t, d)
    proj = _matmul(h2d, w_in[:, :NSA_QKV].astype(jnp.bfloat16), jnp.bfloat16).reshape(b, s, NSA_QKV)
    wg = w_in[:, NSA_QKV:].reshape(d, NSA_KV_HEADS, NSA_GATES)
    wg = jnp.pad(wg, ((0, 0), (0, 0), (0, LANES - NSA_GATES))).reshape(d, NSA_KV_HEADS * LANES)
    gates = _matmul(h2d, wg.astype(jnp.bfloat16), jnp.float32, act="sigmoid").reshape(b, s, NSA_KV_HEADS * LANES)
    nch = s // CMP_STRIDE
    kv = proj[:, :, ATT_WIDTH:ATT_WIDTH + 2 * NSA_KV_WIDTH]
    kv = kv.reshape(b, nch, CMP_STRIDE, 2, NSA_KV_HEADS, HEAD_DIM).transpose(0, 3, 4, 1, 2, 5)
    cmp_kv = _compress(kv.reshape(b, 2, NSA_KV_HEADS, nch, CMP_STRIDE * HEAD_DIM), w_phi1, w_phi2, phi_pos)
    o = _nsa_attention(proj, cmp_kv, gates, slopes)
    return _matmul_residual(o, w_out.astype(jnp.bfloat16), x, gate)


def _dil_layer(x, g, sc, sh, gate, slopes, w_in, w_out):
    b, s, d = x.shape
    dilations = tuple(r for _, r in DIL_PAIRS)
    assert dilations[0] == 1
    hs = _norm_mod_streams(x, g, sc, sh, dilations)
    w = w_in.astype(jnp.bfloat16)
    qs = _matmul_streams(hs[0].reshape(b, s, d), w[:, :ATT_WIDTH], dilations)
    outs, lses = [], []
    for gidx, (win, r) in enumerate(DIL_PAIRS):
        off = ATT_WIDTH * (1 + 2 * gidx)
        kv = _matmul(hs[gidx].reshape(b * s, d), w[:, off:off + 2 * ATT_WIDTH], jnp.bfloat16)
        o, lse = _dil_attention(qs[gidx], kv.reshape(b, r, s // r, 2 * ATT_WIDTH), slopes, win, r)
        outs.append(o)
        lses.append(lse)
    o = _dil_merge(outs, lses, dilations)
    return _matmul_residual(o, w_out.astype(jnp.bfloat16), x, gate)


def kernel(x, c, ada_w, ada_b, norm_mix, norm_ffn, norm_final, nsa_w_in, nsa_w_phi1, nsa_w_phi2, nsa_phi_pos,
           nsa_w_out, dil_w_in, dil_w_out, router_w, router_b, exp_w_gate, exp_w_up, exp_w_down):
    depth = ada_w.shape[0]
    d = x.shape[-1]
    mod = _modulation(c, ada_w, ada_b)
    slopes = 2.0 ** (-ALIBI_MAX_BIAS * jnp.arange(1, N_HEADS + 1, dtype=jnp.float32) / N_HEADS)
    for i in range(depth):
        sh_m, sc_m, g_m, sh_f, sc_f, g_f = [mod[i, :, k * d:(k + 1) * d] for k in range(6)]
        j = i // 2
        if i % 2 == 0:
            h = _norm_mod(x, norm_mix[i], sc_m, sh_m, jnp.bfloat16)
            x = _nsa_layer(h, x, g_m, slopes, nsa_w_in[j], nsa_w_phi1[j], nsa_w_phi2[j], nsa_phi_pos[j],
                           nsa_w_out[j])
        else:
            x = _dil_layer(x, norm_mix[i], sc_m, sh_m, g_m, slopes, dil_w_in[j], dil_w_out[j])
        x = _moe_layer(x, norm_ffn[i], sc_f, sh_f, g_f, router_w, router_b, exp_w_gate[i], exp_w_up[i],
                       exp_w_down[i])
    return _final_norm(x, norm_final)
```

```python
import functools

import jax
import jax.numpy as jnp
import numpy as np
from jax import lax
from jax.experimental import pallas as pl
from jax.experimental.pallas import tpu as pltpu

HEAD_DIM = 128
N_HEADS = 16
ATT_WIDTH = N_HEADS * HEAD_DIM
ALIBI_MAX_BIAS = 8.0

NSA_KV_HEADS = 4
NSA_GROUP = N_HEADS // NSA_KV_HEADS
NSA_KV_WIDTH = NSA_KV_HEADS * HEAD_DIM
CMP_BLOCK = 32
CMP_STRIDE = 16
SEL_BLOCK = 64
N_SEL = 16
NSA_WINDOW = 512
FORCE_SCORE = 1.0e6
NSA_QKV = ATT_WIDTH + 6 * NSA_KV_WIDTH
NSA_GATES = 3 * NSA_GROUP

DIL_PAIRS = ((128, 1), (512, 4), (2048, 16))
DIL_PROJ = ATT_WIDTH * (1 + 2 * len(DIL_PAIRS))

N_EXPERTS = 16
N_EXPERT_GROUPS = 4
EXPERTS_PER_GROUP = N_EXPERTS // N_EXPERT_GROUPS
TOP_K = 2
MOE_BLOCK = 256

RMS_EPS = 1e-6
NEG_INF = -1.0e30

LANES = 128
ATT_TILE = 128
VMEM_LIMIT = 56 * 1024 * 1024

_HI = lax.Precision.HIGHEST
_NT = (((1,), (1,)), ((), ()))


def _params(sem, vmem=VMEM_LIMIT):
    return pltpu.CompilerParams(dimension_semantics=sem, vmem_limit_bytes=vmem)


def _mod_kernel(c_ref, w_ref, b_ref, o_ref):
    c = c_ref[...]
    cond = c * jax.nn.sigmoid(c)
    o_ref[0] = jnp.dot(cond, w_ref[0], precision=_HI, preferred_element_type=jnp.float32) + b_ref[0]


def _modulation(c, ada_w, ada_b):
    depth, d, n = ada_w.shape
    b = c.shape[0]
    tn = 512
    return pl.pallas_call(
        _mod_kernel,
        grid=(depth, n // tn),
        in_specs=[pl.BlockSpec((b, d), lambda i, j: (0, 0)),
                  pl.BlockSpec((1, d, tn), lambda i, j: (i, 0, j)),
                  pl.BlockSpec((1, 1, tn), lambda i, j: (i, 0, j))],
        out_specs=pl.BlockSpec((1, b, tn), lambda i, j: (i, 0, j)),
        out_shape=jax.ShapeDtypeStruct((depth, b, n), jnp.float32),
        compiler_params=_params(("parallel", "parallel")),
        name="adaln_modulation",
    )(c, ada_w, ada_b.reshape(depth, 1, n))


def _rms(x, g):
    return x * lax.rsqrt(jnp.mean(x * x, axis=-1, keepdims=True) + RMS_EPS) * g


def _norm_mod_kernel(x_ref, g_ref, sc_ref, sh_ref, o_ref):
    h = _rms(x_ref[0], g_ref[...]) * (1.0 + sc_ref[0]) + sh_ref[0]
    o_ref[0] = h.astype(o_ref.dtype)


def _norm_mod(x, g, sc, sh, out_dtype):
    b, s, d = x.shape
    ts = min(256, s)
    return pl.pallas_call(
        _norm_mod_kernel,
        grid=(b, s // ts),
        in_specs=[pl.BlockSpec((1, ts, d), lambda i, j: (i, j, 0)),
                  pl.BlockSpec((1, d), lambda i, j: (0, 0)),
                  pl.BlockSpec((1, 1, d), lambda i, j: (i, 0, 0)),
                  pl.BlockSpec((1, 1, d), lambda i, j: (i, 0, 0))],
        out_specs=pl.BlockSpec((1, ts, d), lambda i, j: (i, j, 0)),
        out_shape=jax.ShapeDtypeStruct((b, s, d), out_dtype),
        compiler_params=_params(("parallel", "parallel")),
        name="rmsnorm_adaln",
    )(x, g.reshape(1, d), sc.reshape(b, 1, d), sh.reshape(b, 1, d))


def _store_streams(val, src_sc, out_refs, dilations):
    n = val.shape[0]
    for c in range(src_sc.shape[0]):
        src_sc[c] = val[:, c * LANES:(c + 1) * LANES]
    for r, ref in zip(dilations, out_refs):
        if r == 1:
            ref[0, 0] = val.astype(ref.dtype)
            continue
        for rho in range(r):
            for c in range(src_sc.shape[0]):
                ref[0, rho, :, c * LANES:(c + 1) * LANES] = (
                    src_sc[c, pl.ds(rho, n // r, stride=r), :].astype(ref.dtype))


def _norm_mod_streams_kernel(x_ref, g_ref, sc_ref, sh_ref, *refs, dilations):
    out_refs, h_sc = refs[:-1], refs[-1]
    h = _rms(x_ref[0], g_ref[...]) * (1.0 + sc_ref[0]) + sh_ref[0]
    _store_streams(h, h_sc, out_refs, dilations)


def _norm_mod_streams(x, g, sc, sh, dilations):
    b, s, d = x.shape
    ts = min(256, s)
    return pl.pallas_call(
        functools.partial(_norm_mod_streams_kernel, dilations=dilations),
        grid=(b, s // ts),
        in_specs=[pl.BlockSpec((1, ts, d), lambda i, j: (i, j, 0)),
                  pl.BlockSpec((1, d), lambda i, j: (0, 0)),
                  pl.BlockSpec((1, 1, d), lambda i, j: (i, 0, 0)),
                  pl.BlockSpec((1, 1, d), lambda i, j: (i, 0, 0))],
        out_specs=[pl.BlockSpec((1, r, ts // r, d), lambda i, j: (i, 0, j, 0)) for r in dilations],
        out_shape=[jax.ShapeDtypeStruct((b, r, s // r, d), jnp.bfloat16) for r in dilations],
        scratch_shapes=[pltpu.VMEM((d // LANES, ts, LANES), jnp.float32)],
        compiler_params=_params(("parallel", "parallel")),
        name="rmsnorm_adaln_streams",
    )(x, g.reshape(1, d), sc.reshape(b, 1, d), sh.reshape(b, 1, d))


def _final_norm_kernel(x_ref, g_ref, o_ref):
    o_ref[0] = _rms(x_ref[0], g_ref[...])


def _final_norm(x, g):
    b, s, d = x.shape
    ts = min(256, s)
    return pl.pallas_call(
        _final_norm_kernel,
        grid=(b, s // ts),
        in_specs=[pl.BlockSpec((1, ts, d), lambda i, j: (i, j, 0)),
                  pl.BlockSpec((1, d), lambda i, j: (0, 0))],
        out_specs=pl.BlockSpec((1, ts, d), lambda i, j: (i, j, 0)),
        out_shape=jax.ShapeDtypeStruct((b, s, d), jnp.float32),
        compiler_params=_params(("parallel", "parallel")),
        name="rmsnorm_final",
    )(x, g.reshape(1, d))


def _pair_max(vals):
    out = None
    for a in range(len(vals)):
        for b in range(a + 1, len(vals)):
            s = vals[a] + vals[b]
            out = s if out is None else jnp.maximum(out, s)
    return out


def _norm_route_kernel(x_ref, g_ref, sc_ref, sh_ref, rw_ref, rb_ref, h_ref, idx_ref, wt_ref):
    h = _rms(x_ref[0], g_ref[...]) * (1.0 + sc_ref[0]) + sh_ref[0]
    h_ref[0] = h
    logits = lax.dot_general(rw_ref[...], h, _NT, precision=_HI, preferred_element_type=jnp.float32)
    scores = jax.nn.sigmoid(logits)
    biased = scores + rb_ref[...][:, 0:1]
    rows = [biased[e:e + 1, :] for e in range(N_EXPERTS)]
    srow = [scores[e:e + 1, :] for e in range(N_EXPERTS)]
    grp = [_pair_max(rows[q * EXPERTS_PER_GROUP:(q + 1) * EXPERTS_PER_GROUP]) for q in range(N_EXPERT_GROUPS)]
    best_v = grp[0]
    best = jnp.zeros(best_v.shape, jnp.int32)
    for q in range(1, N_EXPERT_GROUPS):
        take = grp[q] > best_v
        best = jnp.where(take, q, best)
        best_v = jnp.where(take, grp[q], best_v)
    v1 = jnp.full(best_v.shape, NEG_INF, jnp.float32)
    i1 = jnp.zeros(best_v.shape, jnp.int32)
    for e in range(N_EXPERTS):
        cand = jnp.where(best == e // EXPERTS_PER_GROUP, rows[e], NEG_INF)
        take = cand > v1
        i1 = jnp.where(take, e, i1)
        v1 = jnp.where(take, cand, v1)
    v2 = jnp.full(best_v.shape, NEG_INF, jnp.float32)
    i2 = jnp.zeros(best_v.shape, jnp.int32)
    for e in range(N_EXPERTS):
        cand = jnp.where(best == e // EXPERTS_PER_GROUP, jnp.where(i1 == e, NEG_INF, rows[e]), NEG_INF)
        take = cand > v2
        i2 = jnp.where(take, e, i2)
        v2 = jnp.where(take, cand, v2)
    w1 = jnp.zeros(best_v.shape, jnp.float32)
    w2 = jnp.zeros(best_v.shape, jnp.float32)
    for e in range(N_EXPERTS):
        w1 = jnp.where(i1 == e, srow[e], w1)
        w2 = jnp.where(i2 == e, srow[e], w2)
    tot = w1 + w2
    idx_ref[0] = jnp.concatenate([i1, i2], axis=0)
    wt_ref[0] = jnp.concatenate([w1 / tot, w2 / tot], axis=0)


def _norm_route(x, g, sc, sh, router_w, router_b):
    b, s, d = x.shape
    ts = min(256, s)
    rwt = router_w.T
    rb = jnp.broadcast_to(router_b.reshape(N_EXPERTS, 1), (N_EXPERTS, LANES))
    return pl.pallas_call(
        _norm_route_kernel,
        grid=(b, s // ts),
        in_specs=[pl.BlockSpec((1, ts, d), lambda i, j: (i, j, 0)),
                  pl.BlockSpec((1, d), lambda i, j: (0, 0)),
                  pl.BlockSpec((1, 1, d), lambda i, j: (i, 0, 0)),
                  pl.BlockSpec((1, 1, d), lambda i, j: (i, 0, 0)),
                  pl.BlockSpec((N_EXPERTS, d), lambda i, j: (0, 0)),
                  pl.BlockSpec((N_EXPERTS, LANES), lambda i, j: (0, 0))],
        out_specs=[pl.BlockSpec((1, ts, d), lambda i, j: (i, j, 0)),
                   pl.BlockSpec((1, TOP_K, ts), lambda i, j: (i, 0, j)),
                   pl.BlockSpec((1, TOP_K, ts), lambda i, j: (i, 0, j))],
        out_shape=[jax.ShapeDtypeStruct((b, s, d), jnp.float32),
                   jax.ShapeDtypeStruct((b, TOP_K, s), jnp.int32),
                   jax.ShapeDtypeStruct((b, TOP_K, s), jnp.float32)],
        compiler_params=_params(("parallel", "parallel")),
        name="rmsnorm_adaln_router",
    )(x, g.reshape(1, d), sc.reshape(b, 1, d), sh.reshape(b, 1, d), rwt, rb)


def _mm_kernel(a_ref, w_ref, o_ref, *, act):
    acc = jnp.dot(a_ref[...], w_ref[...], preferred_element_type=jnp.float32)
    if act == "sigmoid":
        acc = jax.nn.sigmoid(acc)
    o_ref[...] = acc.astype(o_ref.dtype)


def _matmul(a, w, out_dtype, act=None, tm=512, tn=1024):
    m, k = a.shape
    n = w.shape[1]
    tn = min(tn, n)
    tm = min(tm, m)
    return pl.pallas_call(
        functools.partial(_mm_kernel, act=act),
        grid=(n // tn, m // tm),
        in_specs=[pl.BlockSpec((tm, k), lambda j, i: (i, 0)),
                  pl.BlockSpec((k, tn), lambda j, i: (0, j))],
        out_specs=pl.BlockSpec((tm, tn), lambda j, i: (i, j)),
        out_shape=jax.ShapeDtypeStruct((m, n), out_dtype),
        compiler_params=_params(("parallel", "parallel")),
        name="matmul" if act is None else "matmul_" + act,
    )(a, w)


def _mm_streams_kernel(a_ref, w_ref, *refs, dilations):
    out_refs, acc_sc = refs[:-1], refs[-1]
    acc = jnp.dot(a_ref[0], w_ref[...], preferred_element_type=jnp.float32)
    _store_streams(acc, acc_sc, out_refs, dilations)


def _matmul_streams(a, w, dilations, tm=512, tn=1024):
    b, s, k = a.shape
    n = w.shape[1]
    tn = min(tn, n)
    tm = min(tm, s)
    return pl.pallas_call(
        functools.partial(_mm_streams_kernel, dilations=dilations),
        grid=(n // tn, b, s // tm),
        in_specs=[pl.BlockSpec((1, tm, k), lambda j, bi, i: (bi, i, 0)),
                  pl.BlockSpec((k, tn), lambda j, bi, i: (0, j))],
        out_specs=[pl.BlockSpec((1, r, tm // r, tn), lambda j, bi, i: (bi, 0, i, j)) for r in dilations],
        out_shape=[jax.ShapeDtypeStruct((b, r, s // r, n), jnp.bfloat16) for r in dilations],
        scratch_shapes=[pltpu.VMEM((tn // LANES, tm, LANES), jnp.float32)],
        compiler_params=_params(("parallel", "parallel", "parallel")),
        name="matmul_streams",
    )(a, w)


def _mm_residual_kernel(a_ref, w_ref, x_ref, g_ref, o_ref):
    acc = jnp.dot(a_ref[0], w_ref[...], preferred_element_type=jnp.float32)
    o_ref[0] = x_ref[0] + g_ref[0] * acc


def _matmul_residual(a, w, x, gate, tm=512, tn=1024):
    b, s, k = a.shape
    n = w.shape[1]
    tn = min(tn, n)
    tm = min(tm, s)
    return pl.pallas_call(
        _mm_residual_kernel,
        grid=(n // tn, b, s // tm),
        in_specs=[pl.BlockSpec((1, tm, k), lambda j, bi, i: (bi, i, 0)),
                  pl.BlockSpec((k, tn), lambda j, bi, i: (0, j)),
                  pl.BlockSpec((1, tm, tn), lambda j, bi, i: (bi, i, j)),
                  pl.BlockSpec((1, 1, tn), lambda j, bi, i: (bi, 0, j))],
        out_specs=pl.BlockSpec((1, tm, tn), lambda j, bi, i: (bi, i, j)),
        out_shape=jax.ShapeDtypeStruct((b, s, n), jnp.float32),
        compiler_params=_params(("parallel", "parallel", "parallel")),
        name="matmul_gated_residual",
    )(a, w, x, gate.reshape(b, 1, n))


def _gelu_tanh(x):
    return 0.5 * x * (1.0 + jnp.tanh(0.7978845608028654 * (x + 0.044715 * (x * x * x))))


def _compress_kernel(c_ref, pa_ref, pb_ref, wa_ref, wb_ref, w2_ref, o_ref):
    c = c_ref[0, 0, 0].astype(jnp.float32)
    lo = (c + pa_ref[0]).astype(jnp.bfloat16)
    hi = (c + pb_ref[0]).astype(jnp.bfloat16)
    ha = jnp.dot(lo, wa_ref[0], preferred_element_type=jnp.float32)
    hb = jnp.dot(hi, wb_ref[0], preferred_element_type=jnp.float32)
    n = ha.shape[0]
    hid = _gelu_tanh(ha + pltpu.roll(hb, n - 1, 0))
    out = jnp.dot(hid.astype(jnp.bfloat16), w2_ref[0], preferred_element_type=jnp.float32)
    row = lax.broadcasted_iota(jnp.int32, out.shape, 0)
    o_ref[0, 0, 0] = jnp.where(row < n - 1, out, 0.0).astype(o_ref.dtype)


def _compress(kv_chunks, w_phi1, w_phi2, phi_pos):
    b, two, hkv, nch, width = kv_chunks.shape
    dh = HEAD_DIM
    half = CMP_BLOCK // 2
    w1 = w_phi1.reshape(2, 2, half * dh, dh).astype(jnp.bfloat16)
    pos = phi_pos.reshape(2, 2, 1, half * dh)
    return pl.pallas_call(
        _compress_kernel,
        grid=(b, two, hkv),
        in_specs=[pl.BlockSpec((1, 1, 1, nch, width), lambda i, j, g: (i, j, g, 0, 0)),
                  pl.BlockSpec((1, 1, width), lambda i, j, g: (j, 0, 0)),
                  pl.BlockSpec((1, 1, width), lambda i, j, g: (j, 0, 0)),
                  pl.BlockSpec((1, width, dh), lambda i, j, g: (j, 0, 0)),
                  pl.BlockSpec((1, width, dh), lambda i, j, g: (j, 0, 0)),
                  pl.BlockSpec((1, dh, dh), lambda i, j, g: (j, 0, 0))],
        out_specs=pl.BlockSpec((1, 1, 1, nch, dh), lambda i, j, g: (i, j, g, 0, 0)),
        out_shape=jax.ShapeDtypeStruct((b, two, hkv, nch, dh), jnp.bfloat16),
        compiler_params=_params(("parallel", "parallel", "parallel")),
        name="nsa_compress",
    )(kv_chunks, pos[:, 0], pos[:, 1], w1[:, 0], w1[:, 1], w_phi2.astype(jnp.bfloat16))


MASK_BIG = 2.0 ** 100
AUX_SLOPE = 64
AUX_PAD = 70
SEL_CHUNK = 512
LOG2E = 1.4426950408889634


def _nsa_attn_kernel(sl_ref, q_ref, kc_ref, vc_ref, ks_ref, vs_ref, kw_ref, vw_ref, gt_ref, ovl_ref, qtab_ref,
                     kauxs_ref, kauxw_ref, o_ref, m_sc, acc_sc, ocmp_sc, oslc_sc, qaug_sc, p_sc, pw_sc,
                     kaug_s, vaug_s, kaug_w, vaug_w, *, n_cmp, n_blk, n_sel):
    t = ATT_TILE
    dh = HEAD_DIM
    s_len = ks_ref.shape[1]
    wpad = NSA_WINDOW
    g = pl.program_id(1)
    qt = pl.program_id(2)
    qstart = pl.multiple_of(qt * t, t)

    @pl.when(qt == 0)
    def _():
        ones = jnp.ones((s_len, dh), jnp.bfloat16)
        kaug_s[:, :dh] = ks_ref[0]
        kaug_s[:, dh:] = kauxs_ref[...]
        vaug_s[:, :dh] = vs_ref[0]
        vaug_s[:, dh:] = ones
        kaug_w[:wpad, :dh] = jnp.zeros((wpad, dh), jnp.bfloat16)
        kaug_w[wpad:, :dh] = kw_ref[0]
        kaug_w[:, dh:] = kauxw_ref[...]
        vaug_w[:wpad, :dh] = jnp.zeros((wpad, dh), jnp.bfloat16)
        vaug_w[wpad:, :dh] = vw_ref[0]
        vaug_w[:wpad, dh:] = jnp.ones((wpad, dh), jnp.bfloat16)
        vaug_w[wpad:, dh:] = ones

    slopes = [sl_ref[g * NSA_GROUP + r] for r in range(NSA_GROUP)]
    q = q_ref[0]
    q4 = jnp.concatenate([q[:, r * dh:(r + 1) * dh] for r in range(NSA_GROUP)], axis=0)
    row = lax.broadcasted_iota(jnp.int32, (t, LANES), 0)
    col = lax.broadcasted_iota(jnp.int32, (t, LANES), 1)
    tpos = qstart + row

    kc = kc_ref[0, 0, 0]
    vc = vc_ref[0, 0, 0]
    s = lax.dot_general(q4, kc, _NT, preferred_element_type=jnp.float32)
    visible = (col * CMP_STRIDE + (CMP_BLOCK - 1) <= tpos) & (col < n_cmp)
    dist_c = tpos.astype(jnp.float32) - (col.astype(jnp.float32) * CMP_STRIDE + (CMP_BLOCK - 1) / 2.0)
    psum = jnp.zeros((t, LANES), jnp.float32)
    for r in range(NSA_GROUP):
        rows = slice(r * t, (r + 1) * t)
        sr = jnp.where(visible, s[rows] - slopes[r] * dist_c, NEG_INF)
        e = jnp.where(visible, jnp.exp2(sr - jnp.max(sr, axis=-1, keepdims=True)), 0.0)
        p = e / jnp.maximum(jnp.sum(e, axis=-1, keepdims=True), 1e-30)
        psum = psum + p
        ocmp_sc[rows] = jnp.dot(p.astype(vc.dtype), vc, preferred_element_type=jnp.float32)

    nb = -(-n_blk // 8) * 8
    imp_t = lax.dot_general(ovl_ref[...], psum, _NT, precision=_HI, preferred_element_type=jnp.float32)[:nb]
    jrow = lax.broadcasted_iota(jnp.int32, (nb, t), 0)
    tpos_t = qstart + lax.broadcasted_iota(jnp.int32, (nb, t), 1)
    cur = jnp.right_shift(tpos_t, int(np.log2(SEL_BLOCK)))
    forced = (jrow == 0) | (jrow == cur) | (jrow == cur - 1)
    score = jnp.where(forced, FORCE_SCORE, jnp.where(jrow * SEL_BLOCK <= tpos_t, imp_t, -1.0))
    rank = jnp.zeros((nb, t), jnp.float32)
    for j in range(n_blk):
        cj = score[j:j + 1, :]
        ahead = (cj > score) | ((cj == score) & (jrow > j))
        rank = rank + jnp.where(ahead, 1.0, 0.0)
    mask_t = jnp.where((rank < n_sel) & (jrow < n_blk), 0.0, -MASK_BIG)
    mask_q = jnp.concatenate([mask_t, jnp.zeros((LANES - nb, t), jnp.float32)], axis=0).T

    qtab = qtab_ref[0]
    for r in range(NSA_GROUP):
        rows = slice(r * t, (r + 1) * t)
        qaug_sc[rows, :dh] = q[:, r * dh:(r + 1) * dh]
        qaug_sc[rows, dh:] = (mask_q + qtab[r:r + 1, :]).astype(jnp.bfloat16)

    m_sc[...] = jnp.full(m_sc.shape, NEG_INF, jnp.float32)
    acc_sc[...] = jnp.zeros(acc_sc.shape, jnp.float32)
    ch = SEL_CHUNK
    nl = ch // LANES
    rc = lax.broadcasted_iota(jnp.int32, (t, ch), 1) - lax.broadcasted_iota(jnp.int32, (t, ch), 0)

    def sel_chunk(c, causal):
        k0 = pl.multiple_of(c * ch, ch)
        kk = kaug_s[pl.ds(k0, ch), :]
        vv = vaug_s[pl.ds(k0, ch), :]
        prows = [slice(pair * 2 * t, (pair + 1) * 2 * t) for pair in range(NSA_GROUP // 2)]
        sks = [lax.dot_general(qaug_sc[prow], kk, _NT, preferred_element_type=jnp.float32) for prow in prows]
        for pair, prow in enumerate(prows):
            sk = sks[pair]
            for r2 in range(2):
                rows = slice((2 * pair + r2) * t, (2 * pair + r2 + 1) * t)
                sr = sk[r2 * t:(r2 + 1) * t]
                if causal:
                    sr = jnp.where(rc <= qstart - k0, sr, NEG_INF)
                mx = functools.reduce(jnp.maximum, [sr[:, j * LANES:(j + 1) * LANES] for j in range(nl)])
                m_prev = m_sc[rows]
                m_next = jnp.maximum(m_prev, jnp.max(mx, axis=-1, keepdims=True))
                alpha = jnp.exp2(m_prev - m_next)
                p_sc[rows] = jnp.exp2(sr - jnp.concatenate([m_next] * nl, axis=1)).astype(jnp.bfloat16)
                acc_sc[rows] = acc_sc[rows] * jnp.concatenate([alpha, alpha], axis=1)
                m_sc[rows] = m_next
            acc_sc[prow] = acc_sc[prow] + jnp.dot(p_sc[prow], vv, preferred_element_type=jnp.float32)

    n_full = qstart // ch

    def sel_body(c, carry):
        sel_chunk(c, False)
        return carry

    lax.fori_loop(0, n_full, sel_body, 0)
    sel_chunk(n_full, True)
    oslc_sc[...] = acc_sc[:, :dh] / acc_sc[:, dh:]

    wk = wpad + t
    nw = wk // LANES
    kw = kaug_w[pl.ds(qstart, wk), :]
    vw = vaug_w[pl.ds(qstart, wk), :]
    gt = gt_ref[0]
    prows = [slice(pair * 2 * t, (pair + 1) * 2 * t) for pair in range(NSA_GROUP // 2)]
    sws = [lax.dot_general(qaug_sc[prow], kw, _NT, preferred_element_type=jnp.float32) for prow in prows]
    for pair, prow in enumerate(prows):
        sw = sws[pair]
        for r2 in range(2):
            rows = slice((2 * pair + r2) * t, (2 * pair + r2 + 1) * t)
            tiles = [sw[r2 * t:(r2 + 1) * t, j * LANES:(j + 1) * LANES] for j in range(nw)]
            tiles[0] = jnp.where(col > row, tiles[0], NEG_INF)
            tiles[-1] = jnp.where(col <= row, tiles[-1], NEG_INF)
            mx = functools.reduce(jnp.maximum, tiles)
            m = jnp.broadcast_to(jnp.max(mx, axis=-1, keepdims=True), (t, LANES))
            for j in range(nw):
                pw_sc[rows, j * LANES:(j + 1) * LANES] = jnp.exp2(tiles[j] - m).astype(jnp.bfloat16)
        ow = jnp.dot(pw_sc[prow], vw, preferred_element_type=jnp.float32)
        for r2 in range(2):
            r = 2 * pair + r2
            rows = slice(r * t, (r + 1) * t)
            orow = slice(r2 * t, (r2 + 1) * t)
            o = (gt[:, 3 * r:3 * r + 1] * ocmp_sc[rows] + gt[:, 3 * r + 1:3 * r + 2] * oslc_sc[rows]
                 + gt[:, 3 * r + 2:3 * r + 3] * (ow[orow, :dh] / ow[orow, dh:]))
            o_ref[0, :, r * dh:(r + 1) * dh] = o.astype(o_ref.dtype)


def _bf16_pieces(x):
    a0 = x.astype(jnp.bfloat16).astype(jnp.float32)
    a1 = (x - a0).astype(jnp.bfloat16).astype(jnp.float32)
    a2 = (x - a0 - a1).astype(jnp.bfloat16).astype(jnp.float32)
    return [a0, a1, a2]


def _nsa_tables(s, slopes2):
    n_cmp = s // CMP_STRIDE - CMP_BLOCK // CMP_STRIDE + 1
    n_blk = s // SEL_BLOCK
    assert n_cmp < LANES and n_blk <= AUX_SLOPE
    bj = np.arange(LANES)[:, None]
    cn = np.arange(LANES)[None, :]
    ovl_t = np.clip(np.minimum(cn * CMP_STRIDE + CMP_BLOCK, (bj + 1) * SEL_BLOCK)
                    - np.maximum(cn * CMP_STRIDE, bj * SEL_BLOCK), 0, None) / CMP_STRIDE
    ovl_t = np.where((cn < n_cmp) & (bj < n_blk), ovl_t, 0.0).astype(np.float32)
    pos = np.arange(s)
    kaux = np.zeros((s, LANES), np.float32)
    kaux[pos, pos // SEL_BLOCK] = 1.0
    kaux[:, AUX_SLOPE:AUX_SLOPE + 3] = (pos // LANES * LANES)[:, None]
    kaux[:, AUX_SLOPE + 3:AUX_SLOPE + 6] = (pos % LANES)[:, None]
    kaux_w = np.zeros((NSA_WINDOW + s, LANES), np.float32)
    kaux_w[NSA_WINDOW:, AUX_SLOPE:AUX_SLOPE + 6] = kaux[:, AUX_SLOPE:AUX_SLOPE + 6]
    kaux_w[:NSA_WINDOW, AUX_PAD] = 1.0
    pieces = jnp.stack(_bf16_pieces(slopes2) * 2, axis=-1)
    qtab = jnp.zeros((N_HEADS, LANES), jnp.float32)
    qtab = qtab.at[:, AUX_SLOPE:AUX_SLOPE + 6].set(pieces).at[:, AUX_PAD].set(-MASK_BIG)
    qtab = jnp.pad(qtab.reshape(NSA_KV_HEADS, NSA_GROUP, LANES), ((0, 0), (0, 8 - NSA_GROUP), (0, 0)))
    return (n_cmp, n_blk, jnp.asarray(ovl_t), qtab, jnp.asarray(kaux, dtype=jnp.bfloat16),
            jnp.asarray(kaux_w, dtype=jnp.bfloat16))


def _nsa_attention(proj, cmp_kv, gates, slopes2):
    b, s, _ = proj.shape
    t = ATT_TILE
    n_cmp, n_blk, ovl_t, qtab, kaux_s, kaux_w = _nsa_tables(s, slopes2)
    assert cmp_kv.shape[3] == LANES and s % SEL_CHUNK == 0 and NSA_WINDOW % t == 0
    qw = NSA_GROUP * HEAD_DIM
    rows = NSA_GROUP * t
    kv0 = ATT_WIDTH // HEAD_DIM

    def kv_spec(j):
        return pl.BlockSpec((1, s, HEAD_DIM), lambda bi, g, qi, j=j: (bi, 0, kv0 + j * NSA_KV_HEADS + g))

    def const(shape):
        return pl.BlockSpec(shape, lambda bi, g, qi: (0,) * len(shape))

    return pl.pallas_call(
        functools.partial(_nsa_attn_kernel, n_cmp=n_cmp, n_blk=n_blk, n_sel=min(N_SEL, n_blk)),
        grid=(b, NSA_KV_HEADS, s // t),
        in_specs=[pl.BlockSpec(memory_space=pltpu.SMEM),
                  pl.BlockSpec((1, t, qw), lambda bi, g, qi: (bi, qi, g)),
                  pl.BlockSpec((1, 1, 1, LANES, HEAD_DIM), lambda bi, g, qi: (bi, 0, g, 0, 0)),
                  pl.BlockSpec((1, 1, 1, LANES, HEAD_DIM), lambda bi, g, qi: (bi, 1, g, 0, 0)),
                  kv_spec(2), kv_spec(3), kv_spec(4), kv_spec(5),
                  pl.BlockSpec((1, t, LANES), lambda bi, g, qi: (bi, qi, g)),
                  const((LANES, LANES)),
                  pl.BlockSpec((1, 8, LANES), lambda bi, g, qi: (g, 0, 0)),
                  const((s, LANES)), const((NSA_WINDOW + s, LANES))],
        out_specs=pl.BlockSpec((1, t, qw), lambda bi, g, qi: (bi, qi, g)),
        out_shape=jax.ShapeDtypeStruct((b, s, ATT_WIDTH), jnp.bfloat16),
        scratch_shapes=[pltpu.VMEM((rows, LANES), jnp.float32),
                        pltpu.VMEM((rows, 2 * HEAD_DIM), jnp.float32),
                        pltpu.VMEM((rows, HEAD_DIM), jnp.float32),
                        pltpu.VMEM((rows, HEAD_DIM), jnp.float32),
                        pltpu.VMEM((rows, 2 * HEAD_DIM), jnp.bfloat16),
                        pltpu.VMEM((rows, SEL_CHUNK), jnp.bfloat16),
                        pltpu.VMEM((rows, NSA_WINDOW + t), jnp.bfloat16),
                        pltpu.VMEM((s, 2 * HEAD_DIM), jnp.bfloat16),
                        pltpu.VMEM((s, 2 * HEAD_DIM), jnp.bfloat16),
                        pltpu.VMEM((NSA_WINDOW + s, 2 * HEAD_DIM), jnp.bfloat16),
                        pltpu.VMEM((NSA_WINDOW + s, 2 * HEAD_DIM), jnp.bfloat16)],
        compiler_params=_params(("arbitrary", "arbitrary", "arbitrary")),
        name="nsa_attention",
    )(slopes2, proj, cmp_kv, cmp_kv, proj, proj, proj, proj, gates, ovl_t, qtab, kaux_s, kaux_w)


def _dil_attn_kernel(sl_ref, q_ref, kp_ref, kc_ref, vp_ref, vc_ref, o_ref, lse_ref, *, window, dist_scale):
    t = ATT_TILE
    i = pl.program_id(2)
    scale = HEAD_DIM ** -0.5
    row = lax.broadcasted_iota(jnp.int32, (t, 2 * t), 0)
    col = lax.broadcasted_iota(jnp.int32, (t, 2 * t), 1)
    dist = row - col + t
    valid = (dist >= 0) & (dist <= window) & (col >= jnp.where(i > 0, 0, t))
    distf = (dist * dist_scale).astype(jnp.float32)
    lane = lax.broadcasted_iota(jnp.int32, (t, LANES), 1)
    lse_all = jnp.zeros((t, LANES), jnp.float32)
    for h in range(N_HEADS):
        hs = slice(h * HEAD_DIM, (h + 1) * HEAD_DIM)
        q = q_ref[0, 0, :, hs]
        k = jnp.concatenate([kp_ref[0, 0, :, hs], kc_ref[0, 0, :, hs]], axis=0)
        v = jnp.concatenate([vp_ref[0, 0, :, hs], vc_ref[0, 0, :, hs]], axis=0)
        s = lax.dot_general(q, k, _NT, preferred_element_type=jnp.float32) * scale
        s = jnp.where(valid, s - sl_ref[h] * distf, NEG_INF)
        m = jnp.max(s, axis=-1, keepdims=True)
        e = jnp.exp(s - m)
        l = jnp.sum(e, axis=-1, keepdims=True)
        o = jnp.dot(e.astype(v.dtype), v, preferred_element_type=jnp.float32) / l
        o_ref[0, 0, :, hs] = o.astype(o_ref.dtype)
        lse_all = jnp.where(lane == h, m + jnp.log(l), lse_all)
    lse_ref[0, 0] = lse_all


def _dil_attention(q, kv, slopes, win, r):
    b, _, ln, _ = q.shape
    t = ATT_TILE

    def cur(c):
        return pl.BlockSpec((1, 1, t, ATT_WIDTH), lambda bi, rho, i, c=c: (bi, rho, i, c))

    def prev(c):
        return pl.BlockSpec((1, 1, t, ATT_WIDTH), lambda bi, rho, i, c=c: (bi, rho, jnp.maximum(i - 1, 0), c))

    return pl.pallas_call(
        functools.partial(_dil_attn_kernel, window=win // r, dist_scale=r),
        grid=(b, r, ln // t),
        in_specs=[pl.BlockSpec(memory_space=pltpu.SMEM), cur(0), prev(0), cur(0), prev(1), cur(1)],
        out_specs=[pl.BlockSpec((1, 1, t, ATT_WIDTH), lambda bi, rho, i: (bi, rho, i, 0)),
                   pl.BlockSpec((1, 1, t, LANES), lambda bi, rho, i: (bi, rho, i, 0))],
        out_shape=[jax.ShapeDtypeStruct((b, r, ln, ATT_WIDTH), jnp.bfloat16),
                   jax.ShapeDtypeStruct((b, r, ln, LANES), jnp.float32)],
        compiler_params=_params(("parallel", "parallel", "arbitrary")),
        name="dilated_attention_r%d" % r,
    )(slopes, q, kv, kv, kv, kv)


def _dil_merge_kernel(*refs, dilations):
    ng = len(dilations)
    o_refs, l_refs, out_ref, o_sc, l_sc = refs[:ng], refs[ng:2 * ng], refs[2 * ng], refs[2 * ng + 1], refs[2 * ng + 2]
    ts = out_ref.shape[1]
    ls = []
    for gi, r in enumerate(dilations):
        if r == 1:
            ls.append(l_refs[gi][0, 0])
            continue
        for rho in range(r):
            l_sc[gi, pl.ds(rho, ts // r, stride=r), :] = l_refs[gi][0, rho]
            for h in range(N_HEADS):
                o_sc[gi, h, pl.ds(rho, ts // r, stride=r), :] = (
                    o_refs[gi][0, rho, :, h * HEAD_DIM:(h + 1) * HEAD_DIM].astype(jnp.float32))
        ls.append(l_sc[gi])
    m = functools.reduce(jnp.maximum, ls)
    es = [jnp.exp(l - m) for l in ls]
    den = functools.reduce(lambda a, b: a + b, es)
    ws = [e / den for e in es]
    for h in range(N_HEADS):
        hs = slice(h * HEAD_DIM, (h + 1) * HEAD_DIM)
        acc = None
        for gi, r in enumerate(dilations):
            og = o_refs[gi][0, 0, :, hs].astype(jnp.float32) if r == 1 else o_sc[gi, h]
            term = ws[gi][:, h:h + 1] * og
            acc = term if acc is None else acc + term
        out_ref[0, :, hs] = acc.astype(out_ref.dtype)


def _dil_merge(outs, lses, dilations):
    b, _, _, w = outs[0].shape
    s = outs[0].shape[1] * outs[0].shape[2]
    ts = min(256, s)
    ng = len(dilations)

    def spec(r, width):
        return pl.BlockSpec((1, r, ts // r, width), lambda i, j: (i, 0, j, 0))

    return pl.pallas_call(
        functools.partial(_dil_merge_kernel, dilations=dilations),
        grid=(b, s // ts),
        in_specs=[spec(r, w) for r in dilations] + [spec(r, LANES) for r in dilations],
        out_specs=pl.BlockSpec((1, ts, w), lambda i, j: (i, j, 0)),
        out_shape=jax.ShapeDtypeStruct((b, s, w), jnp.bfloat16),
        scratch_shapes=[pltpu.VMEM((ng, w // HEAD_DIM, ts, HEAD_DIM), jnp.float32),
                        pltpu.VMEM((ng, ts, LANES), jnp.float32)],
        compiler_params=_params(("parallel", "parallel")),
        name="dilated_merge",
    )(*outs, *lses)


ROW_TILE = 8


def _gather_rows(idx_ref, src_hbm, dst, sem):
    def body(i, carry):
        for j in range(ROW_TILE):
            pltpu.make_async_copy(src_hbm.at[pl.ds(idx_ref[0, 0, i * ROW_TILE + j], 1)],
                                  dst.at[i, pl.ds(j, 1)], sem).start()
        return carry
    lax.fori_loop(0, dst.shape[0], body, 0)


def _wait_rows(dst, sem):
    pltpu.make_async_copy(dst, dst, sem).wait()


def _expert_kernel(be_ref, nu_ref, tokc_ref, tokn_ref, h_hbm, wg_ref, wu_ref, wd_ref, o_ref, xbuf, sem):
    del be_ref
    blk = pl.program_id(0)
    n_used = nu_ref[0]
    slot = blk % 2

    @pl.when(blk == 0)
    def _():
        _gather_rows(tokc_ref, h_hbm, xbuf.at[0], sem.at[0])

    @pl.when(blk + 1 < n_used)
    def _():
        _gather_rows(tokn_ref, h_hbm, xbuf.at[1 - slot], sem.at[1 - slot])

    @pl.when(blk < n_used)
    def _():
        _wait_rows(xbuf.at[slot], sem.at[slot])
        x = xbuf[slot].reshape(o_ref.shape).astype(jnp.bfloat16)
        gate = jnp.dot(x, wg_ref[0], preferred_element_type=jnp.float32)
        up = jnp.dot(x, wu_ref[0], preferred_element_type=jnp.float32)
        hid = (gate * jax.nn.sigmoid(gate) * up).astype(jnp.bfloat16)
        o_ref[...] = jnp.dot(hid, wd_ref[0], preferred_element_type=jnp.float32)

    @pl.when(blk >= n_used)
    def _():
        o_ref[...] = jnp.zeros(o_ref.shape, o_ref.dtype)


def _expert_blocks(h2d, slot_tok, blk_e, n_used, w_gate, w_up, w_down):
    t, d = h2d.shape
    nblk = blk_e.shape[0]
    de = w_gate.shape[2]
    tok = slot_tok.reshape(nblk, 1, MOE_BLOCK)
    grid_spec = pltpu.PrefetchScalarGridSpec(
        num_scalar_prefetch=2,
        grid=(nblk,),
        in_specs=[pl.BlockSpec((1, 1, MOE_BLOCK), lambda i, be, nu: (i, 0, 0), memory_space=pltpu.SMEM),
                  pl.BlockSpec((1, 1, MOE_BLOCK), lambda i, be, nu: (jnp.minimum(i + 1, nblk - 1), 0, 0),
                               memory_space=pltpu.SMEM),
                  pl.BlockSpec(memory_space=pl.ANY),
                  pl.BlockSpec((1, d, de), lambda i, be, nu: (be[i], 0, 0)),
                  pl.BlockSpec((1, d, de), lambda i, be, nu: (be[i], 0, 0)),
                  pl.BlockSpec((1, de, d), lambda i, be, nu: (be[i], 0, 0))],
        out_specs=pl.BlockSpec((MOE_BLOCK, d), lambda i, be, nu: (i, 0)),
        scratch_shapes=[pltpu.VMEM((2, MOE_BLOCK // ROW_TILE, ROW_TILE, d), jnp.float32),
                        pltpu.SemaphoreType.DMA((2,))],
    )
    return pl.pallas_call(
        _expert_kernel,
        grid_spec=grid_spec,
        out_shape=jax.ShapeDtypeStruct((nblk * MOE_BLOCK, d), jnp.float32),
        compiler_params=_params(("arbitrary",)),
        name="moe_expert_blocks",
    )(blk_e, n_used, tok, tok, h2d, w_gate, w_up, w_down)


def _combine_kernel(d0c_ref, d1c_ref, d0n_ref, d1n_ref, x_ref, w0_ref, w1_ref, g_ref, yo_hbm, o_ref, buf, sem):
    i = pl.program_id(0)
    n = pl.num_programs(0)
    slot = i % 2
    tt = x_ref.shape[0]

    @pl.when(i == 0)
    def _():
        _gather_rows(d0c_ref, yo_hbm, buf.at[0, 0], sem.at[0])
        _gather_rows(d1c_ref, yo_hbm, buf.at[0, 1], sem.at[0])

    @pl.when(i + 1 < n)
    def _():
        _gather_rows(d0n_ref, yo_hbm, buf.at[1 - slot, 0], sem.at[1 - slot])
        _gather_rows(d1n_ref, yo_hbm, buf.at[1 - slot, 1], sem.at[1 - slot])

    _wait_rows(buf.at[slot, 0], sem.at[slot])
    _wait_rows(buf.at[slot, 1], sem.at[slot])
    y = (w0_ref[...][:, 0:1] * buf[slot, 0].reshape(x_ref.shape)
         + w1_ref[...][:, 0:1] * buf[slot, 1].reshape(x_ref.shape))
    o_ref[...] = x_ref[...] + g_ref[0] * y


def _moe_combine(x2d, yo, dest, wts, gate, s):
    t, d = x2d.shape
    tt = min(256, s)
    nt = t // tt
    d0 = dest[:, 0].reshape(nt, 1, tt)
    d1 = dest[:, 1].reshape(nt, 1, tt)
    w0 = jnp.broadcast_to(wts[:, 0:1], (t, LANES))
    w1 = jnp.broadcast_to(wts[:, 1:2], (t, LANES))
    b = gate.shape[0]
    cur = pl.BlockSpec((1, 1, tt), lambda i: (i, 0, 0), memory_space=pltpu.SMEM)
    nxt = pl.BlockSpec((1, 1, tt), lambda i: (jnp.minimum(i + 1, nt - 1), 0, 0), memory_space=pltpu.SMEM)
    return pl.pallas_call(
        _combine_kernel,
        grid=(nt,),
        in_specs=[cur, cur, nxt, nxt,
                  pl.BlockSpec((tt, d), lambda i: (i, 0)),
                  pl.BlockSpec((tt, LANES), lambda i: (i, 0)),
                  pl.BlockSpec((tt, LANES), lambda i: (i, 0)),
                  pl.BlockSpec((1, 1, d), lambda i: (i * tt // s, 0, 0)),
                  pl.BlockSpec(memory_space=pl.ANY)],
        out_specs=pl.BlockSpec((tt, d), lambda i: (i, 0)),
        out_shape=jax.ShapeDtypeStruct((t, d), jnp.float32),
        scratch_shapes=[pltpu.VMEM((2, 2, tt // ROW_TILE, ROW_TILE, d), jnp.float32),
                        pltpu.SemaphoreType.DMA((2,))],
        compiler_params=_params(("arbitrary",)),
        name="moe_combine_residual",
    )(d0, d1, d0, d1, x2d, w0, w1, gate.reshape(b, 1, d), yo)


def _dispatch_tables(idx):
    t = idx.shape[0]
    a = t * TOP_K
    flat_e = idx.reshape(a)
    onehot = (flat_e[:, None] == jnp.arange(N_EXPERTS, dtype=jnp.int32)[None, :]).astype(jnp.int32)
    csum = jnp.cumsum(onehot, axis=0)
    rank = jnp.take_along_axis(csum, flat_e[:, None], axis=1)[:, 0] - 1
    counts = csum[-1]
    padded = (counts + MOE_BLOCK - 1) // MOE_BLOCK * MOE_BLOCK
    pad_end = jnp.cumsum(padded)
    pad_start = pad_end - padded
    dest = pad_start[flat_e] + rank
    nblk = -(-(a + N_EXPERTS * MOE_BLOCK) // MOE_BLOCK)
    cap = nblk * MOE_BLOCK
    slot_tok = jnp.zeros((cap,), jnp.int32).at[dest].set(jnp.arange(a, dtype=jnp.int32) // TOP_K)
    blk_start = jnp.arange(nblk, dtype=jnp.int32) * MOE_BLOCK
    blk_e = jnp.minimum(jnp.sum((pad_end[None, :] <= blk_start[:, None]).astype(jnp.int32), axis=1), N_EXPERTS - 1)
    n_used = (pad_end[-1:] // MOE_BLOCK).astype(jnp.int32)
    return dest.reshape(t, TOP_K).astype(jnp.int32), slot_tok, blk_e.astype(jnp.int32), n_used


def _moe_layer(x, g, sc, sh, gate, router_w, router_b, w_gate, w_up, w_down):
    b, s, d = x.shape
    t = b * s
    h, idx, wts = _norm_route(x, g, sc, sh, router_w, router_b)
    idx = idx.transpose(0, 2, 1).reshape(t, TOP_K)
    wts = wts.transpose(0, 2, 1).reshape(t, TOP_K)
    dest, slot_tok, blk_e, n_used = _dispatch_tables(idx)
    yo = _expert_blocks(h.reshape(t, d), slot_tok, blk_e, n_used, w_gate.astype(jnp.bfloat16),
                        w_up.astype(jnp.bfloat16), w_down.astype(jnp.bfloat16))
    return _moe_combine(x.reshape(t, d), yo, dest, wts, gate, s).reshape(b, s, d)


def _nsa_layer(h, x, gate, slopes, w_in, w_phi1, w_phi2, phi_pos, w_out):
    b, s, d = h.shape
    t = b * s
    h2d = h.reshape(t, d)
    qscale = jnp.where(jnp.arange(NSA_QKV) < ATT_WIDTH, HEAD_DIM ** -0.5 * LOG2E, 1.0).astype(jnp.float32)
    w_qkv = (w_in[:, :NSA_QKV] * qscale[None, :]).astype(jnp.bfloat16)
    proj = _matmul(h2d, w_qkv, jnp.bfloat16).reshape(b, s, NSA_QKV)
    wg = w_in[:, NSA_QKV:].reshape(d, NSA_KV_HEADS, NSA_GATES)
    wg = jnp.pad(wg, ((0, 0), (0, 0), (0, LANES - NSA_GATES))).reshape(d, NSA_KV_HEADS * LANES)
    gates = _matmul(h2d, wg.astype(jnp.bfloat16), jnp.float32, act="sigmoid").reshape(b, s, NSA_KV_HEADS * LANES)
    nch = s // CMP_STRIDE
    kv = proj[:, :, ATT_WIDTH:ATT_WIDTH + 2 * NSA_KV_WIDTH]
    kv = kv.reshape(b, nch, CMP_STRIDE, 2, NSA_KV_HEADS, HEAD_DIM).transpose(0, 3, 4, 1, 2, 5)
    cmp_kv = _compress(kv.reshape(b, 2, NSA_KV_HEADS, nch, CMP_STRIDE * HEAD_DIM), w_phi1, w_phi2, phi_pos)
    o = _nsa_attention(proj, cmp_kv, gates, slopes * LOG2E)
    return _matmul_residual(o, w_out.astype(jnp.bfloat16), x, gate)


def _dil_layer(x, g, sc, sh, gate, slopes, w_in, w_out):
    b, s, d = x.shape
    dilations = tuple(r for _, r in DIL_PAIRS)
    assert dilations[0] == 1
    hs = _norm_mod_streams(x, g, sc, sh, dilations)
    w = w_in.astype(jnp.bfloat16)
    qs = _matmul_streams(hs[0].reshape(b, s, d), w[:, :ATT_WIDTH], dilations)
    outs, lses = [], []
    for gidx, (win, r) in enumerate(DIL_PAIRS):
        off = ATT_WIDTH * (1 + 2 * gidx)
        kv = _matmul(hs[gidx].reshape(b * s, d), w[:, off:off + 2 * ATT_WIDTH], jnp.bfloat16)
        o, lse = _dil_attention(qs[gidx], kv.reshape(b, r, s // r, 2 * ATT_WIDTH), slopes, win, r)
        outs.append(o)
        lses.append(lse)
    o = _dil_merge(outs, lses, dilations)
    return _matmul_residual(o, w_out.astype(jnp.bfloat16), x, gate)


def kernel(x, c, ada_w, ada_b, norm_mix, norm_ffn, norm_final, nsa_w_in, nsa_w_phi1, nsa_w_phi2, nsa_phi_pos,
           nsa_w_out, dil_w_in, dil_w_out, router_w, router_b, exp_w_gate, exp_w_up, exp_w_down):
    depth = ada_w.shape[0]
    d = x.shape[-1]
    mod = _modulation(c, ada_w, ada_b)
    slopes = 2.0 ** (-ALIBI_MAX_BIAS * jnp.arange(1, N_HEADS + 1, dtype=jnp.float32) / N_HEADS)
    for i in range(depth):
        sh_m, sc_m, g_m, sh_f, sc_f, g_f = [mod[i, :, k * d:(k + 1) * d] for k in range(6)]
        j = i // 2
        if i % 2 == 0:
            h = _norm_mod(x, norm_mix[i], sc_m, sh_m, jnp.bfloat16)
            x = _nsa_layer(h, x, g_m, slopes, nsa_w_in[j], nsa_w_phi1[j], nsa_w_phi2[j], nsa_phi_pos[j],
                           nsa_w_out[j])
        else:
            x = _dil_layer(x, norm_mix[i], sc_m, sh_m, g_m, slopes, dil_w_in[j], dil_w_out[j])
        x = _moe_layer(x, norm_ffn[i], sc_f, sh_f, g_f, router_w, router_b, exp_w_gate[i], exp_w_up[i],
                       exp_w_down[i])
    return _final_norm(x, norm_final)
```

```python
import functools

import jax
import jax.numpy as jnp
import numpy as np
from jax import lax
from jax.experimental import pallas as pl
from jax.experimental.pallas import tpu as pltpu

HEAD_DIM = 128
N_HEADS = 16
ATT_WIDTH = N_HEADS * HEAD_DIM
ALIBI_MAX_BIAS = 8.0

NSA_KV_HEADS = 4
NSA_GROUP = N_HEADS // NSA_KV_HEADS
NSA_KV_WIDTH = NSA_KV_HEADS * HEAD_DIM
CMP_BLOCK = 32
CMP_STRIDE = 16
SEL_BLOCK = 64
N_SEL = 16
NSA_WINDOW = 512
FORCE_SCORE = 1.0e6
NSA_QKV = ATT_WIDTH + 6 * NSA_KV_WIDTH
NSA_GATES = 3 * NSA_GROUP

DIL_PAIRS = ((128, 1), (512, 4), (2048, 16))
DIL_PROJ = ATT_WIDTH * (1 + 2 * len(DIL_PAIRS))

N_EXPERTS = 16
N_EXPERT_GROUPS = 4
EXPERTS_PER_GROUP = N_EXPERTS // N_EXPERT_GROUPS
TOP_K = 2
MOE_BLOCK = 256

RMS_EPS = 1e-6
NEG_INF = -1.0e30

LANES = 128
ATT_TILE = 128
VMEM_LIMIT = 56 * 1024 * 1024

_HI = lax.Precision.HIGHEST
_NT = (((1,), (1,)), ((), ()))


def _params(sem, vmem=VMEM_LIMIT):
    return pltpu.CompilerParams(dimension_semantics=sem, vmem_limit_bytes=vmem)


def _mod_kernel(c_ref, w_ref, b_ref, o_ref):
    c = c_ref[...]
    cond = c * jax.nn.sigmoid(c)
    o_ref[0] = jnp.dot(cond, w_ref[0], precision=_HI, preferred_element_type=jnp.float32) + b_ref[0]


def _modulation(c, ada_w, ada_b):
    depth, d, n = ada_w.shape
    b = c.shape[0]
    tn = 512
    return pl.pallas_call(
        _mod_kernel,
        grid=(depth, n // tn),
        in_specs=[pl.BlockSpec((b, d), lambda i, j: (0, 0)),
                  pl.BlockSpec((1, d, tn), lambda i, j: (i, 0, j)),
                  pl.BlockSpec((1, 1, tn), lambda i, j: (i, 0, j))],
        out_specs=pl.BlockSpec((1, b, tn), lambda i, j: (i, 0, j)),
        out_shape=jax.ShapeDtypeStruct((depth, b, n), jnp.float32),
        compiler_params=_params(("parallel", "parallel")),
        name="adaln_modulation",
    )(c, ada_w, ada_b.reshape(depth, 1, n))


def _rms(x, g):
    return x * lax.rsqrt(jnp.mean(x * x, axis=-1, keepdims=True) + RMS_EPS) * g


def _norm_mod_kernel(x_ref, g_ref, sc_ref, sh_ref, o_ref):
    h = _rms(x_ref[0], g_ref[...]) * (1.0 + sc_ref[0]) + sh_ref[0]
    o_ref[0] = h.astype(o_ref.dtype)


def _norm_mod(x, g, sc, sh, out_dtype):
    b, s, d = x.shape
    ts = min(256, s)
    return pl.pallas_call(
        _norm_mod_kernel,
        grid=(b, s // ts),
        in_specs=[pl.BlockSpec((1, ts, d), lambda i, j: (i, j, 0)),
                  pl.BlockSpec((1, d), lambda i, j: (0, 0)),
                  pl.BlockSpec((1, 1, d), lambda i, j: (i, 0, 0)),
                  pl.BlockSpec((1, 1, d), lambda i, j: (i, 0, 0))],
        out_specs=pl.BlockSpec((1, ts, d), lambda i, j: (i, j, 0)),
        out_shape=jax.ShapeDtypeStruct((b, s, d), out_dtype),
        compiler_params=_params(("parallel", "parallel")),
        name="rmsnorm_adaln",
    )(x, g.reshape(1, d), sc.reshape(b, 1, d), sh.reshape(b, 1, d))


def _store_streams(val, src_sc, out_refs, dilations):
    n = val.shape[0]
    for c in range(src_sc.shape[0]):
        src_sc[c] = val[:, c * LANES:(c + 1) * LANES]
    for r, ref in zip(dilations, out_refs):
        if r == 1:
            ref[0, 0] = val.astype(ref.dtype)
            continue
        for rho in range(r):
            for c in range(src_sc.shape[0]):
                ref[0, rho, :, c * LANES:(c + 1) * LANES] = (
                    src_sc[c, pl.ds(rho, n // r, stride=r), :].astype(ref.dtype))


def _norm_mod_streams_kernel(x_ref, g_ref, sc_ref, sh_ref, *refs, dilations):
    out_refs, h_sc = refs[:-1], refs[-1]
    h = _rms(x_ref[0], g_ref[...]) * (1.0 + sc_ref[0]) + sh_ref[0]
    _store_streams(h, h_sc, out_refs, dilations)


def _norm_mod_streams(x, g, sc, sh, dilations):
    b, s, d = x.shape
    ts = min(256, s)
    return pl.pallas_call(
        functools.partial(_norm_mod_streams_kernel, dilations=dilations),
        grid=(b, s // ts),
        in_specs=[pl.BlockSpec((1, ts, d), lambda i, j: (i, j, 0)),
                  pl.BlockSpec((1, d), lambda i, j: (0, 0)),
                  pl.BlockSpec((1, 1, d), lambda i, j: (i, 0, 0)),
                  pl.BlockSpec((1, 1, d), lambda i, j: (i, 0, 0))],
        out_specs=[pl.BlockSpec((1, r, ts // r, d), lambda i, j: (i, 0, j, 0)) for r in dilations],
        out_shape=[jax.ShapeDtypeStruct((b, r, s // r, d), jnp.bfloat16) for r in dilations],
        scratch_shapes=[pltpu.VMEM((d // LANES, ts, LANES), jnp.float32)],
        compiler_params=_params(("parallel", "parallel")),
        name="rmsnorm_adaln_streams",
    )(x, g.reshape(1, d), sc.reshape(b, 1, d), sh.reshape(b, 1, d))


def _final_norm_kernel(x_ref, g_ref, o_ref):
    o_ref[0] = _rms(x_ref[0], g_ref[...])


def _final_norm(x, g):
    b, s, d = x.shape
    ts = min(256, s)
    return pl.pallas_call(
        _final_norm_kernel,
        grid=(b, s // ts),
        in_specs=[pl.BlockSpec((1, ts, d), lambda i, j: (i, j, 0)),
                  pl.BlockSpec((1, d), lambda i, j: (0, 0))],
        out_specs=pl.BlockSpec((1, ts, d), lambda i, j: (i, j, 0)),
        out_shape=jax.ShapeDtypeStruct((b, s, d), jnp.float32),
        compiler_params=_params(("parallel", "parallel")),
        name="rmsnorm_final",
    )(x, g.reshape(1, d))


def _pair_max(vals):
    out = None
    for a in range(len(vals)):
        for b in range(a + 1, len(vals)):
            s = vals[a] + vals[b]
            out = s if out is None else jnp.maximum(out, s)
    return out


def _norm_route_kernel(x_ref, g_ref, sc_ref, sh_ref, rw_ref, rb_ref, h_ref, idx_ref, wt_ref):
    h = _rms(x_ref[0], g_ref[...]) * (1.0 + sc_ref[0]) + sh_ref[0]
    h_ref[0] = h
    logits = lax.dot_general(rw_ref[...], h, _NT, precision=_HI, preferred_element_type=jnp.float32)
    scores = jax.nn.sigmoid(logits)
    biased = scores + rb_ref[...][:, 0:1]
    rows = [biased[e:e + 1, :] for e in range(N_EXPERTS)]
    srow = [scores[e:e + 1, :] for e in range(N_EXPERTS)]
    grp = [_pair_max(rows[q * EXPERTS_PER_GROUP:(q + 1) * EXPERTS_PER_GROUP]) for q in range(N_EXPERT_GROUPS)]
    best_v = grp[0]
    best = jnp.zeros(best_v.shape, jnp.int32)
    for q in range(1, N_EXPERT_GROUPS):
        take = grp[q] > best_v
        best = jnp.where(take, q, best)
        best_v = jnp.where(take, grp[q], best_v)
    v1 = jnp.full(best_v.shape, NEG_INF, jnp.float32)
    i1 = jnp.zeros(best_v.shape, jnp.int32)
    for e in range(N_EXPERTS):
        cand = jnp.where(best == e // EXPERTS_PER_GROUP, rows[e], NEG_INF)
        take = cand > v1
        i1 = jnp.where(take, e, i1)
        v1 = jnp.where(take, cand, v1)
    v2 = jnp.full(best_v.shape, NEG_INF, jnp.float32)
    i2 = jnp.zeros(best_v.shape, jnp.int32)
    for e in range(N_EXPERTS):
        cand = jnp.where(best == e // EXPERTS_PER_GROUP, jnp.where(i1 == e, NEG_INF, rows[e]), NEG_INF)
        take = cand > v2
        i2 = jnp.where(take, e, i2)
        v2 = jnp.where(take, cand, v2)
    w1 = jnp.zeros(best_v.shape, jnp.float32)
    w2 = jnp.zeros(best_v.shape, jnp.float32)
    for e in range(N_EXPERTS):
        w1 = jnp.where(i1 == e, srow[e], w1)
        w2 = jnp.where(i2 == e, srow[e], w2)
    tot = w1 + w2
    idx_ref[0] = jnp.concatenate([i1, i2], axis=0)
    wt_ref[0] = jnp.concatenate([w1 / tot, w2 / tot], axis=0)


def _norm_route(x, g, sc, sh, router_w, router_b):
    b, s, d = x.shape
    ts = min(256, s)
    rwt = router_w.T
    rb = jnp.broadcast_to(router_b.reshape(N_EXPERTS, 1), (N_EXPERTS, LANES))
    return pl.pallas_call(
        _norm_route_kernel,
        grid=(b, s // ts),
        in_specs=[pl.BlockSpec((1, ts, d), lambda i, j: (i, j, 0)),
                  pl.BlockSpec((1, d), lambda i, j: (0, 0)),
                  pl.BlockSpec((1, 1, d), lambda i, j: (i, 0, 0)),
                  pl.BlockSpec((1, 1, d), lambda i, j: (i, 0, 0)),
                  pl.BlockSpec((N_EXPERTS, d), lambda i, j: (0, 0)),
                  pl.BlockSpec((N_EXPERTS, LANES), lambda i, j: (0, 0))],
        out_specs=[pl.BlockSpec((1, ts, d), lambda i, j: (i, j, 0)),
                   pl.BlockSpec((1, TOP_K, ts), lambda i, j: (i, 0, j)),
                   pl.BlockSpec((1, TOP_K, ts), lambda i, j: (i, 0, j))],
        out_shape=[jax.ShapeDtypeStruct((b, s, d), jnp.float32),
                   jax.ShapeDtypeStruct((b, TOP_K, s), jnp.int32),
                   jax.ShapeDtypeStruct((b, TOP_K, s), jnp.float32)],
        compiler_params=_params(("parallel", "parallel")),
        name="rmsnorm_adaln_router",
    )(x, g.reshape(1, d), sc.reshape(b, 1, d), sh.reshape(b, 1, d), rwt, rb)


def _weight_tile(w_ref, wbf_sc, first, scale, scale_tiles):
    @pl.when(first)
    def _():
        w = w_ref[0]
        if scale_tiles:
            w = w * jnp.where(pl.program_id(0) < scale_tiles, scale, 1.0)
        wbf_sc[...] = w.astype(jnp.bfloat16)
    return wbf_sc[...]


def _weight_spec(layer, col0, k, tn, rank):
    assert col0 % tn == 0
    if rank == 2:
        return pl.BlockSpec((1, k, tn), lambda j, i: (layer, 0, col0 // tn + j))
    return pl.BlockSpec((1, k, tn), lambda j, bi, i: (layer, 0, col0 // tn + j))


def _mm_kernel(a_ref, w_ref, o_ref, wbf_sc, *, act, scale, scale_tiles):
    w = _weight_tile(w_ref, wbf_sc, pl.program_id(1) == 0, scale, scale_tiles)
    acc = jnp.dot(a_ref[...], w, preferred_element_type=jnp.float32)
    if act == "sigmoid":
        acc = jax.nn.sigmoid(acc)
    o_ref[...] = acc.astype(o_ref.dtype)


def _matmul(a, w, layer, col0, n, out_dtype, act=None, scale=1.0, scale_cols=0, tm=512, tn=1024):
    m, k = a.shape
    tn = min(tn, n)
    tm = min(tm, m)
    assert scale_cols % tn == 0
    return pl.pallas_call(
        functools.partial(_mm_kernel, act=act, scale=scale, scale_tiles=scale_cols // tn),
        grid=(n // tn, m // tm),
        in_specs=[pl.BlockSpec((tm, k), lambda j, i: (i, 0)),
                  _weight_spec(layer, col0, k, tn, 2)],
        out_specs=pl.BlockSpec((tm, tn), lambda j, i: (i, j)),
        out_shape=jax.ShapeDtypeStruct((m, n), out_dtype),
        scratch_shapes=[pltpu.VMEM((k, tn), jnp.bfloat16)],
        compiler_params=_params(("arbitrary", "arbitrary")),
        name="matmul" if act is None else "matmul_" + act,
    )(a, w)


def _first_row_step():
    return (pl.program_id(1) == 0) & (pl.program_id(2) == 0)


def _mm_streams_kernel(a_ref, w_ref, *refs, dilations):
    out_refs, acc_sc, wbf_sc = refs[:-2], refs[-2], refs[-1]
    w = _weight_tile(w_ref, wbf_sc, _first_row_step(), 1.0, 0)
    acc = jnp.dot(a_ref[0], w, preferred_element_type=jnp.float32)
    _store_streams(acc, acc_sc, out_refs, dilations)


def _matmul_streams(a, w, layer, col0, n, dilations, tm=512, tn=1024):
    b, s, k = a.shape
    tn = min(tn, n)
    tm = min(tm, s)
    return pl.pallas_call(
        functools.partial(_mm_streams_kernel, dilations=dilations),
        grid=(n // tn, b, s // tm),
        in_specs=[pl.BlockSpec((1, tm, k), lambda j, bi, i: (bi, i, 0)),
                  _weight_spec(layer, col0, k, tn, 3)],
        out_specs=[pl.BlockSpec((1, r, tm // r, tn), lambda j, bi, i: (bi, 0, i, j)) for r in dilations],
        out_shape=[jax.ShapeDtypeStruct((b, r, s // r, n), jnp.bfloat16) for r in dilations],
        scratch_shapes=[pltpu.VMEM((tn // LANES, tm, LANES), jnp.float32), pltpu.VMEM((k, tn), jnp.bfloat16)],
        compiler_params=_params(("arbitrary", "arbitrary", "arbitrary")),
        name="matmul_streams",
    )(a, w)


def _mm_residual_kernel(a_ref, w_ref, x_ref, g_ref, o_ref, wbf_sc):
    w = _weight_tile(w_ref, wbf_sc, _first_row_step(), 1.0, 0)
    acc = jnp.dot(a_ref[0], w, preferred_element_type=jnp.float32)
    o_ref[0] = x_ref[0] + g_ref[0] * acc


def _matmul_residual(a, w, layer, x, gate, tm=512, tn=1024):
    b, s, k = a.shape
    n = w.shape[2]
    tn = min(tn, n)
    tm = min(tm, s)
    return pl.pallas_call(
        _mm_residual_kernel,
        grid=(n // tn, b, s // tm),
        in_specs=[pl.BlockSpec((1, tm, k), lambda j, bi, i: (bi, i, 0)),
                  _weight_spec(layer, 0, k, tn, 3),
                  pl.BlockSpec((1, tm, tn), lambda j, bi, i: (bi, i, j)),
                  pl.BlockSpec((1, 1, tn), lambda j, bi, i: (bi, 0, j))],
        out_specs=pl.BlockSpec((1, tm, tn), lambda j, bi, i: (bi, i, j)),
        out_shape=jax.ShapeDtypeStruct((b, s, n), jnp.float32),
        scratch_shapes=[pltpu.VMEM((k, tn), jnp.bfloat16)],
        compiler_params=_params(("arbitrary", "arbitrary", "arbitrary")),
        name="matmul_gated_residual",
    )(a, w, x, gate.reshape(b, 1, n))


def _cast_kernel(w_ref, o_ref):
    o_ref[...] = w_ref[...].astype(o_ref.dtype)


def _cast_bf16(w, rows=1024):
    c = w.shape[-1]
    w2 = w.reshape(-1, c)
    r = w2.shape[0]
    out = pl.pallas_call(
        _cast_kernel,
        grid=(r // rows,),
        in_specs=[pl.BlockSpec((rows, c), lambda i: (i, 0))],
        out_specs=pl.BlockSpec((rows, c), lambda i: (i, 0)),
        out_shape=jax.ShapeDtypeStruct((r, c), jnp.bfloat16),
        compiler_params=_params(("parallel",)),
        name="cast_bf16",
    )(w2)
    return out.reshape(w.shape)


def _gelu_tanh(x):
    return 0.5 * x * (1.0 + jnp.tanh(0.7978845608028654 * (x + 0.044715 * (x * x * x))))


def _compress_kernel(c_ref, pa_ref, pb_ref, wa_ref, wb_ref, w2_ref, o_ref):
    c = c_ref[0, 0, 0].astype(jnp.float32)
    lo = (c + pa_ref[0]).astype(jnp.bfloat16)
    hi = (c + pb_ref[0]).astype(jnp.bfloat16)
    ha = jnp.dot(lo, wa_ref[0], preferred_element_type=jnp.float32)
    hb = jnp.dot(hi, wb_ref[0], preferred_element_type=jnp.float32)
    n = ha.shape[0]
    hid = _gelu_tanh(ha + pltpu.roll(hb, n - 1, 0))
    out = jnp.dot(hid.astype(jnp.bfloat16), w2_ref[0], preferred_element_type=jnp.float32)
    row = lax.broadcasted_iota(jnp.int32, out.shape, 0)
    o_ref[0, 0, 0] = jnp.where(row < n - 1, out, 0.0).astype(o_ref.dtype)


def _compress(kv_chunks, w_phi1, w_phi2, phi_pos):
    b, two, hkv, nch, width = kv_chunks.shape
    dh = HEAD_DIM
    half = CMP_BLOCK // 2
    w1 = w_phi1.reshape(2, 2, half * dh, dh).astype(jnp.bfloat16)
    pos = phi_pos.reshape(2, 2, 1, half * dh)
    return pl.pallas_call(
        _compress_kernel,
        grid=(b, two, hkv),
        in_specs=[pl.BlockSpec((1, 1, 1, nch, width), lambda i, j, g: (i, j, g, 0, 0)),
                  pl.BlockSpec((1, 1, width), lambda i, j, g: (j, 0, 0)),
                  pl.BlockSpec((1, 1, width), lambda i, j, g: (j, 0, 0)),
                  pl.BlockSpec((1, width, dh), lambda i, j, g: (j, 0, 0)),
                  pl.BlockSpec((1, width, dh), lambda i, j, g: (j, 0, 0)),
                  pl.BlockSpec((1, dh, dh), lambda i, j, g: (j, 0, 0))],
        out_specs=pl.BlockSpec((1, 1, 1, nch, dh), lambda i, j, g: (i, j, g, 0, 0)),
        out_shape=jax.ShapeDtypeStruct((b, two, hkv, nch, dh), jnp.bfloat16),
        compiler_params=_params(("parallel", "parallel", "parallel")),
        name="nsa_compress",
    )(kv_chunks, pos[:, 0], pos[:, 1], w1[:, 0], w1[:, 1], w_phi2.astype(jnp.bfloat16))


MASK_BIG = 2.0 ** 100
AUX_SLOPE = 64
AUX_PAD = 70
SEL_CHUNK = 512
LOG2E = 1.4426950408889634


def _nsa_attn_kernel(sl_ref, q_ref, kc_ref, vc_ref, ks_ref, vs_ref, kw_ref, vw_ref, gt_ref, ovl_ref, qtab_ref,
                     kauxs_ref, kauxw_ref, o_ref, ocmp_sc, owin_sc, qaug_sc, qaugw_sc, p_sc, pw_sc,
                     kaug_s, vaug_s, kaug_w, vaug_w, *, n_cmp, n_blk, n_sel):
    t = ATT_TILE
    dh = HEAD_DIM
    s_len = ks_ref.shape[1]
    wpad = NSA_WINDOW
    g = pl.program_id(1)
    qt = pl.program_id(2)
    qstart = pl.multiple_of(qt * t, t)

    @pl.when(qt == 0)
    def _():
        ones = jnp.ones((s_len, dh), jnp.bfloat16)
        kaug_s[:, :dh] = ks_ref[0]
        kaug_s[:, dh:] = kauxs_ref[...]
        vaug_s[:, :dh] = vs_ref[0]
        vaug_s[:, dh:] = ones
        kaug_w[:wpad, :dh] = jnp.zeros((wpad, dh), jnp.bfloat16)
        kaug_w[wpad:, :dh] = kw_ref[0]
        kaug_w[:, dh:] = kauxw_ref[...]
        vaug_w[:wpad, :dh] = jnp.zeros((wpad, dh), jnp.bfloat16)
        vaug_w[wpad:, :dh] = vw_ref[0]
        vaug_w[:wpad, dh:] = jnp.ones((wpad, dh), jnp.bfloat16)
        vaug_w[wpad:, dh:] = ones

    slopes = [sl_ref[g * NSA_GROUP + r] for r in range(NSA_GROUP)]
    q = q_ref[0]
    q4 = jnp.concatenate([q[:, r * dh:(r + 1) * dh] for r in range(NSA_GROUP)], axis=0)
    row = lax.broadcasted_iota(jnp.int32, (t, LANES), 0)
    col = lax.broadcasted_iota(jnp.int32, (t, LANES), 1)
    tpos = qstart + row
    prows = [slice(pair * 2 * t, (pair + 1) * 2 * t) for pair in range(NSA_GROUP // 2)]
    qtab = qtab_ref[0]

    for r in range(NSA_GROUP):
        rows = slice(r * t, (r + 1) * t)
        qaugw_sc[rows, :dh] = q[:, r * dh:(r + 1) * dh]
        qaugw_sc[rows, dh:] = jnp.broadcast_to(qtab[r:r + 1, :], (t, LANES)).astype(jnp.bfloat16)
    wk = wpad + t
    nw = wk // LANES
    kw = kaug_w[pl.ds(qstart, wk), :]
    vw = vaug_w[pl.ds(qstart, wk), :]
    sws = [lax.dot_general(qaugw_sc[prow], kw, _NT, preferred_element_type=jnp.float32) for prow in prows]

    kc = kc_ref[0, 0, 0]
    vc = vc_ref[0, 0, 0]
    s = lax.dot_general(q4, kc, _NT, preferred_element_type=jnp.float32)
    visible = (col * CMP_STRIDE + (CMP_BLOCK - 1) <= tpos) & (col < n_cmp)
    dist_c = tpos.astype(jnp.float32) - (col.astype(jnp.float32) * CMP_STRIDE + (CMP_BLOCK - 1) / 2.0)
    psum = jnp.zeros((t, LANES), jnp.float32)
    for r in range(NSA_GROUP):
        rows = slice(r * t, (r + 1) * t)
        sr = jnp.where(visible, s[rows] - slopes[r] * dist_c, NEG_INF)
        e = jnp.where(visible, jnp.exp2(sr - jnp.max(sr, axis=-1, keepdims=True)), 0.0)
        p = e / jnp.maximum(jnp.sum(e, axis=-1, keepdims=True), 1e-30)
        psum = psum + p
        ocmp_sc[rows] = jnp.dot(p.astype(vc.dtype), vc, preferred_element_type=jnp.float32)

    nb = -(-n_blk // 8) * 8
    imp_t = lax.dot_general(ovl_ref[...], psum, _NT, precision=_HI, preferred_element_type=jnp.float32)[:nb]
    jrow = lax.broadcasted_iota(jnp.int32, (nb, t), 0)
    tpos_t = qstart + lax.broadcasted_iota(jnp.int32, (nb, t), 1)
    cur = jnp.right_shift(tpos_t, int(np.log2(SEL_BLOCK)))
    forced = (jrow == 0) | (jrow == cur) | (jrow == cur - 1)
    score = jnp.where(forced, FORCE_SCORE, jnp.where(jrow * SEL_BLOCK <= tpos_t, imp_t, -1.0))
    rank = jnp.zeros((nb, t), jnp.float32)
    for j in range(n_blk):
        cj = score[j:j + 1, :]
        ahead = (cj > score) | ((cj == score) & (jrow > j))
        rank = rank + jnp.where(ahead, 1.0, 0.0)
    mask_t = jnp.where((rank < n_sel) & (jrow < n_blk), 0.0, -MASK_BIG)
    mask_q = jnp.concatenate([mask_t, jnp.zeros((LANES - nb, t), jnp.float32)], axis=0).T

    for r in range(NSA_GROUP):
        rows = slice(r * t, (r + 1) * t)
        qaug_sc[rows, :dh] = q[:, r * dh:(r + 1) * dh]
        qaug_sc[rows, dh:] = (mask_q + qtab[r:r + 1, :]).astype(jnp.bfloat16)

    for pair, prow in enumerate(prows):
        sw = sws[pair]
        for r2 in range(2):
            rows = slice((2 * pair + r2) * t, (2 * pair + r2 + 1) * t)
            tiles = [sw[r2 * t:(r2 + 1) * t, j * LANES:(j + 1) * LANES] for j in range(nw)]
            tiles[0] = jnp.where(col > row, tiles[0], NEG_INF)
            tiles[-1] = jnp.where(col <= row, tiles[-1], NEG_INF)
            mx = functools.reduce(jnp.maximum, tiles)
            m = jnp.broadcast_to(jnp.max(mx, axis=-1, keepdims=True), (t, LANES))
            for j in range(nw):
                pw_sc[rows, j * LANES:(j + 1) * LANES] = jnp.exp2(tiles[j] - m).astype(jnp.bfloat16)
        ow = jnp.dot(pw_sc[prow], vw, preferred_element_type=jnp.float32)
        owin_sc[prow] = ow[:, :dh] / ow[:, dh:]

    ch = SEL_CHUNK
    nl = ch // LANES
    n_full = qt // (ch // t)
    gt = gt_ref[0]
    cmr = col - row

    def selected(k):
        nk = (k + 1) * ch
        kk = kaug_s[:nk, :]
        vv = vaug_s[:nk, :]
        sks = [lax.dot_general(qaug_sc[prow], kk, _NT, preferred_element_type=jnp.float32) for prow in prows]
        for pair, prow in enumerate(prows):
            sk = sks[pair]
            for r2 in range(2):
                rows = slice((2 * pair + r2) * t, (2 * pair + r2 + 1) * t)
                tiles = [sk[r2 * t:(r2 + 1) * t, j * LANES:(j + 1) * LANES] for j in range(nk // LANES)]
                for j in range(k * nl, (k + 1) * nl):
                    tiles[j] = jnp.where(cmr <= qstart - j * LANES, tiles[j], NEG_INF)
                mx = functools.reduce(jnp.maximum, tiles)
                m = jnp.broadcast_to(jnp.max(mx, axis=-1, keepdims=True), (t, LANES))
                for j in range(nk // LANES):
                    p_sc[rows, j * LANES:(j + 1) * LANES] = jnp.exp2(tiles[j] - m).astype(jnp.bfloat16)
            pv = jnp.dot(p_sc[prow, :nk], vv, preferred_element_type=jnp.float32)
            for r2 in range(2):
                r = 2 * pair + r2
                rows = slice(r * t, (r + 1) * t)
                orow = slice(r2 * t, (r2 + 1) * t)
                o = (gt[:, 3 * r:3 * r + 1] * ocmp_sc[rows]
                     + gt[:, 3 * r + 1:3 * r + 2] * (pv[orow, :dh] / pv[orow, dh:])
                     + gt[:, 3 * r + 2:3 * r + 3] * owin_sc[rows])
                o_ref[0, :, r * dh:(r + 1) * dh] = o.astype(o_ref.dtype)

    for k in range(s_len // ch):
        pl.when(n_full == k)(functools.partial(selected, k))


def _bf16_pieces(x):
    a0 = x.astype(jnp.bfloat16).astype(jnp.float32)
    a1 = (x - a0).astype(jnp.bfloat16).astype(jnp.float32)
    a2 = (x - a0 - a1).astype(jnp.bfloat16).astype(jnp.float32)
    return [a0, a1, a2]


def _nsa_tables(s, slopes2):
    n_cmp = s // CMP_STRIDE - CMP_BLOCK // CMP_STRIDE + 1
    n_blk = s // SEL_BLOCK
    assert n_cmp < LANES and n_blk <= AUX_SLOPE
    bj = np.arange(LANES)[:, None]
    cn = np.arange(LANES)[None, :]
    ovl_t = np.clip(np.minimum(cn * CMP_STRIDE + CMP_BLOCK, (bj + 1) * SEL_BLOCK)
                    - np.maximum(cn * CMP_STRIDE, bj * SEL_BLOCK), 0, None) / CMP_STRIDE
    ovl_t = np.where((cn < n_cmp) & (bj < n_blk), ovl_t, 0.0).astype(np.float32)
    pos = np.arange(s)
    kaux = np.zeros((s, LANES), np.float32)
    kaux[pos, pos // SEL_BLOCK] = 1.0
    kaux[:, AUX_SLOPE:AUX_SLOPE + 3] = (pos // LANES * LANES)[:, None]
    kaux[:, AUX_SLOPE + 3:AUX_SLOPE + 6] = (pos % LANES)[:, None]
    kaux_w = np.zeros((NSA_WINDOW + s, LANES), np.float32)
    kaux_w[NSA_WINDOW:, AUX_SLOPE:AUX_SLOPE + 6] = kaux[:, AUX_SLOPE:AUX_SLOPE + 6]
    kaux_w[:NSA_WINDOW, AUX_PAD] = 1.0
    pieces = jnp.stack(_bf16_pieces(slopes2) * 2, axis=-1)
    qtab = jnp.zeros((N_HEADS, LANES), jnp.float32)
    qtab = qtab.at[:, AUX_SLOPE:AUX_SLOPE + 6].set(pieces).at[:, AUX_PAD].set(-MASK_BIG)
    qtab = jnp.pad(qtab.reshape(NSA_KV_HEADS, NSA_GROUP, LANES), ((0, 0), (0, 8 - NSA_GROUP), (0, 0)))
    return (n_cmp, n_blk, jnp.asarray(ovl_t), qtab, jnp.asarray(kaux, dtype=jnp.bfloat16),
            jnp.asarray(kaux_w, dtype=jnp.bfloat16))


def _nsa_attention(proj, cmp_kv, gates, slopes2):
    b, s, _ = proj.shape
    t = ATT_TILE
    n_cmp, n_blk, ovl_t, qtab, kaux_s, kaux_w = _nsa_tables(s, slopes2)
    assert cmp_kv.shape[3] == LANES and s % SEL_CHUNK == 0 and NSA_WINDOW % t == 0
    qw = NSA_GROUP * HEAD_DIM
    rows = NSA_GROUP * t
    kv0 = ATT_WIDTH // HEAD_DIM

    def kv_spec(j):
        return pl.BlockSpec((1, s, HEAD_DIM), lambda bi, g, qi, j=j: (bi, 0, kv0 + j * NSA_KV_HEADS + g))

    def const(shape):
        return pl.BlockSpec(shape, lambda bi, g, qi: (0,) * len(shape))

    return pl.pallas_call(
        functools.partial(_nsa_attn_kernel, n_cmp=n_cmp, n_blk=n_blk, n_sel=min(N_SEL, n_blk)),
        grid=(b, NSA_KV_HEADS, s // t),
        in_specs=[pl.BlockSpec(memory_space=pltpu.SMEM),
                  pl.BlockSpec((1, t, qw), lambda bi, g, qi: (bi, qi, g)),
                  pl.BlockSpec((1, 1, 1, LANES, HEAD_DIM), lambda bi, g, qi: (bi, 0, g, 0, 0)),
                  pl.BlockSpec((1, 1, 1, LANES, HEAD_DIM), lambda bi, g, qi: (bi, 1, g, 0, 0)),
                  kv_spec(2), kv_spec(3), kv_spec(4), kv_spec(5),
                  pl.BlockSpec((1, t, LANES), lambda bi, g, qi: (bi, qi, g)),
                  const((LANES, LANES)),
                  pl.BlockSpec((1, 8, LANES), lambda bi, g, qi: (g, 0, 0)),
                  const((s, LANES)), const((NSA_WINDOW + s, LANES))],
        out_specs=pl.BlockSpec((1, t, qw), lambda bi, g, qi: (bi, qi, g)),
        out_shape=jax.ShapeDtypeStruct((b, s, ATT_WIDTH), jnp.bfloat16),
        scratch_shapes=[pltpu.VMEM((rows, HEAD_DIM), jnp.float32),
                        pltpu.VMEM((rows, HEAD_DIM), jnp.float32),
                        pltpu.VMEM((rows, 2 * HEAD_DIM), jnp.bfloat16),
                        pltpu.VMEM((rows, 2 * HEAD_DIM), jnp.bfloat16),
                        pltpu.VMEM((rows, s), jnp.bfloat16),
                        pltpu.VMEM((rows, NSA_WINDOW + t), jnp.bfloat16),
                        pltpu.VMEM((s, 2 * HEAD_DIM), jnp.bfloat16),
                        pltpu.VMEM((s, 2 * HEAD_DIM), jnp.bfloat16),
                        pltpu.VMEM((NSA_WINDOW + s, 2 * HEAD_DIM), jnp.bfloat16),
                        pltpu.VMEM((NSA_WINDOW + s, 2 * HEAD_DIM), jnp.bfloat16)],
        compiler_params=_params(("arbitrary", "arbitrary", "arbitrary")),
        name="nsa_attention",
    )(slopes2, proj, cmp_kv, cmp_kv, proj, proj, proj, proj, gates, ovl_t, qtab, kaux_s, kaux_w)


def _dil_attn_kernel(sl_ref, q_ref, kp_ref, kc_ref, vp_ref, vc_ref, o_ref, lse_ref, *, window, dist_scale):
    t = ATT_TILE
    i = pl.program_id(2)
    scale = HEAD_DIM ** -0.5
    row = lax.broadcasted_iota(jnp.int32, (t, 2 * t), 0)
    col = lax.broadcasted_iota(jnp.int32, (t, 2 * t), 1)
    dist = row - col + t
    valid = (dist >= 0) & (dist <= window) & (col >= jnp.where(i > 0, 0, t))
    distf = (dist * dist_scale).astype(jnp.float32)
    lane = lax.broadcasted_iota(jnp.int32, (t, LANES), 1)
    lse_all = jnp.zeros((t, LANES), jnp.float32)
    for h in range(N_HEADS):
        hs = slice(h * HEAD_DIM, (h + 1) * HEAD_DIM)
        q = q_ref[0, 0, :, hs]
        k = jnp.concatenate([kp_ref[0, 0, :, hs], kc_ref[0, 0, :, hs]], axis=0)
        v = jnp.concatenate([vp_ref[0, 0, :, hs], vc_ref[0, 0, :, hs]], axis=0)
        s = lax.dot_general(q, k, _NT, preferred_element_type=jnp.float32) * scale
        s = jnp.where(valid, s - sl_ref[h] * distf, NEG_INF)
        m = jnp.max(s, axis=-1, keepdims=True)
        e = jnp.exp(s - m)
        l = jnp.sum(e, axis=-1, keepdims=True)
        o = jnp.dot(e.astype(v.dtype), v, preferred_element_type=jnp.float32) / l
        o_ref[0, 0, :, hs] = o.astype(o_ref.dtype)
        lse_all = jnp.where(lane == h, m + jnp.log(l), lse_all)
    lse_ref[0, 0] = lse_all


def _dil_attention(q, kv, slopes, win, r):
    b, _, ln, _ = q.shape
    t = ATT_TILE

    def cur(c):
        return pl.BlockSpec((1, 1, t, ATT_WIDTH), lambda bi, rho, i, c=c: (bi, rho, i, c))

    def prev(c):
        return pl.BlockSpec((1, 1, t, ATT_WIDTH), lambda bi, rho, i, c=c: (bi, rho, jnp.maximum(i - 1, 0), c))

    return pl.pallas_call(
        functools.partial(_dil_attn_kernel, window=win // r, dist_scale=r),
        grid=(b, r, ln // t),
        in_specs=[pl.BlockSpec(memory_space=pltpu.SMEM), cur(0), prev(0), cur(0), prev(1), cur(1)],
        out_specs=[pl.BlockSpec((1, 1, t, ATT_WIDTH), lambda bi, rho, i: (bi, rho, i, 0)),
                   pl.BlockSpec((1, 1, t, LANES), lambda bi, rho, i: (bi, rho, i, 0))],
        out_shape=[jax.ShapeDtypeStruct((b, r, ln, ATT_WIDTH), jnp.bfloat16),
                   jax.ShapeDtypeStruct((b, r, ln, LANES), jnp.float32)],
        compiler_params=_params(("parallel", "parallel", "arbitrary")),
        name="dilated_attention_r%d" % r,
    )(slopes, q, kv, kv, kv, kv)


def _dil_merge_kernel(*refs, dilations):
    ng = len(dilations)
    o_refs, l_refs, out_ref, o_sc, l_sc = refs[:ng], refs[ng:2 * ng], refs[2 * ng], refs[2 * ng + 1], refs[2 * ng + 2]
    ts = out_ref.shape[1]
    ls = []
    for gi, r in enumerate(dilations):
        if r == 1:
            ls.append(l_refs[gi][0, 0])
            continue
        for rho in range(r):
            l_sc[gi, pl.ds(rho, ts // r, stride=r), :] = l_refs[gi][0, rho]
            for h in range(N_HEADS):
                o_sc[gi, h, pl.ds(rho, ts // r, stride=r), :] = (
                    o_refs[gi][0, rho, :, h * HEAD_DIM:(h + 1) * HEAD_DIM].astype(jnp.float32))
        ls.append(l_sc[gi])
    m = functools.reduce(jnp.maximum, ls)
    es = [jnp.exp(l - m) for l in ls]
    den = functools.reduce(lambda a, b: a + b, es)
    ws = [e / den for e in es]
    for h in range(N_HEADS):
        hs = slice(h * HEAD_DIM, (h + 1) * HEAD_DIM)
        acc = None
        for gi, r in enumerate(dilations):
            og = o_refs[gi][0, 0, :, hs].astype(jnp.float32) if r == 1 else o_sc[gi, h]
            term = ws[gi][:, h:h + 1] * og
            acc = term if acc is None else acc + term
        out_ref[0, :, hs] = acc.astype(out_ref.dtype)


def _dil_merge(outs, lses, dilations):
    b, _, _, w = outs[0].shape
    s = outs[0].shape[1] * outs[0].shape[2]
    ts = min(256, s)
    ng = len(dilations)

    def spec(r, width):
        return pl.BlockSpec((1, r, ts // r, width), lambda i, j: (i, 0, j, 0))

    return pl.pallas_call(
        functools.partial(_dil_merge_kernel, dilations=dilations),
        grid=(b, s // ts),
        in_specs=[spec(r, w) for r in dilations] + [spec(r, LANES) for r in dilations],
        out_specs=pl.BlockSpec((1, ts, w), lambda i, j: (i, j, 0)),
        out_shape=jax.ShapeDtypeStruct((b, s, w), jnp.bfloat16),
        scratch_shapes=[pltpu.VMEM((ng, w // HEAD_DIM, ts, HEAD_DIM), jnp.float32),
                        pltpu.VMEM((ng, ts, LANES), jnp.float32)],
        compiler_params=_params(("parallel", "parallel")),
        name="dilated_merge",
    )(*outs, *lses)


ROW_TILE = 8


def _gather_rows(idx_ref, src_hbm, dst, sem):
    def body(i, carry):
        for j in range(ROW_TILE):
            pltpu.make_async_copy(src_hbm.at[pl.ds(idx_ref[0, 0, i * ROW_TILE + j], 1)],
                                  dst.at[i, pl.ds(j, 1)], sem).start()
        return carry
    lax.fori_loop(0, dst.shape[0], body, 0)


def _wait_rows(dst, sem):
    pltpu.make_async_copy(dst, dst, sem).wait()


def _expert_kernel(be_ref, nu_ref, tokc_ref, tokn_ref, h_hbm, wg_ref, wu_ref, wd_ref, o_ref, xbuf, sem):
    del be_ref
    blk = pl.program_id(0)
    n_used = nu_ref[0]
    slot = blk % 2

    @pl.when(blk == 0)
    def _():
        _gather_rows(tokc_ref, h_hbm, xbuf.at[0], sem.at[0])

    @pl.when(blk + 1 < n_used)
    def _():
        _gather_rows(tokn_ref, h_hbm, xbuf.at[1 - slot], sem.at[1 - slot])

    @pl.when(blk < n_used)
    def _():
        _wait_rows(xbuf.at[slot], sem.at[slot])
        x = xbuf[slot].reshape(o_ref.shape).astype(jnp.bfloat16)
        gate = jnp.dot(x, wg_ref[0], preferred_element_type=jnp.float32)
        up = jnp.dot(x, wu_ref[0], preferred_element_type=jnp.float32)
        hid = (gate * jax.nn.sigmoid(gate) * up).astype(jnp.bfloat16)
        o_ref[...] = jnp.dot(hid, wd_ref[0], preferred_element_type=jnp.float32)

    @pl.when(blk >= n_used)
    def _():
        o_ref[...] = jnp.zeros(o_ref.shape, o_ref.dtype)


def _expert_blocks(h2d, slot_tok, blk_e, n_used, w_gate, w_up, w_down):
    t, d = h2d.shape
    nblk = blk_e.shape[0]
    de = w_gate.shape[2]
    tok = slot_tok.reshape(nblk, 1, MOE_BLOCK)
    grid_spec = pltpu.PrefetchScalarGridSpec(
        num_scalar_prefetch=2,
        grid=(nblk,),
        in_specs=[pl.BlockSpec((1, 1, MOE_BLOCK), lambda i, be, nu: (i, 0, 0), memory_space=pltpu.SMEM),
                  pl.BlockSpec((1, 1, MOE_BLOCK), lambda i, be, nu: (jnp.minimum(i + 1, nblk - 1), 0, 0),
                               memory_space=pltpu.SMEM),
                  pl.BlockSpec(memory_space=pl.ANY),
                  pl.BlockSpec((1, d, de), lambda i, be, nu: (be[i], 0, 0)),
                  pl.BlockSpec((1, d, de), lambda i, be, nu: (be[i], 0, 0)),
                  pl.BlockSpec((1, de, d), lambda i, be, nu: (be[i], 0, 0))],
        out_specs=pl.BlockSpec((MOE_BLOCK, d), lambda i, be, nu: (i, 0)),
        scratch_shapes=[pltpu.VMEM((2, MOE_BLOCK // ROW_TILE, ROW_TILE, d), jnp.float32),
                        pltpu.SemaphoreType.DMA((2,))],
    )
    return pl.pallas_call(
        _expert_kernel,
        grid_spec=grid_spec,
        out_shape=jax.ShapeDtypeStruct((nblk * MOE_BLOCK, d), jnp.float32),
        compiler_params=_params(("arbitrary",)),
        name="moe_expert_blocks",
    )(blk_e, n_used, tok, tok, h2d, w_gate, w_up, w_down)


def _combine_kernel(d0c_ref, d1c_ref, d0n_ref, d1n_ref, x_ref, w0_ref, w1_ref, g_ref, yo_hbm, o_ref, buf, sem):
    i = pl.program_id(0)
    n = pl.num_programs(0)
    slot = i % 2
    tt = x_ref.shape[0]

    @pl.when(i == 0)
    def _():
        _gather_rows(d0c_ref, yo_hbm, buf.at[0, 0], sem.at[0])
        _gather_rows(d1c_ref, yo_hbm, buf.at[0, 1], sem.at[0])

    @pl.when(i + 1 < n)
    def _():
        _gather_rows(d0n_ref, yo_hbm, buf.at[1 - slot, 0], sem.at[1 - slot])
        _gather_rows(d1n_ref, yo_hbm, buf.at[1 - slot, 1], sem.at[1 - slot])

    _wait_rows(buf.at[slot, 0], sem.at[slot])
    _wait_rows(buf.at[slot, 1], sem.at[slot])
    y = (w0_ref[...][:, 0:1] * buf[slot, 0].reshape(x_ref.shape)
         + w1_ref[...][:, 0:1] * buf[slot, 1].reshape(x_ref.shape))
    o_ref[...] = x_ref[...] + g_ref[0] * y


def _moe_combine(x2d, yo, dest, wts, gate, s):
    t, d = x2d.shape
    tt = min(256, s)
    nt = t // tt
    d0 = dest[:, 0].reshape(nt, 1, tt)
    d1 = dest[:, 1].reshape(nt, 1, tt)
    w0 = jnp.broadcast_to(wts[:, 0:1], (t, LANES))
    w1 = jnp.broadcast_to(wts[:, 1:2], (t, LANES))
    b = gate.shape[0]
    cur = pl.BlockSpec((1, 1, tt), lambda i: (i, 0, 0), memory_space=pltpu.SMEM)
    nxt = pl.BlockSpec((1, 1, tt), lambda i: (jnp.minimum(i + 1, nt - 1), 0, 0), memory_space=pltpu.SMEM)
    return pl.pallas_call(
        _combine_kernel,
        grid=(nt,),
        in_specs=[cur, cur, nxt, nxt,
                  pl.BlockSpec((tt, d), lambda i: (i, 0)),
                  pl.BlockSpec((tt, LANES), lambda i: (i, 0)),
                  pl.BlockSpec((tt, LANES), lambda i: (i, 0)),
                  pl.BlockSpec((1, 1, d), lambda i: (i * tt // s, 0, 0)),
                  pl.BlockSpec(memory_space=pl.ANY)],
        out_specs=pl.BlockSpec((tt, d), lambda i: (i, 0)),
        out_shape=jax.ShapeDtypeStruct((t, d), jnp.float32),
        scratch_shapes=[pltpu.VMEM((2, 2, tt // ROW_TILE, ROW_TILE, d), jnp.float32),
                        pltpu.SemaphoreType.DMA((2,))],
        compiler_params=_params(("arbitrary",)),
        name="moe_combine_residual",
    )(d0, d1, d0, d1, x2d, w0, w1, gate.reshape(b, 1, d), yo)


def _dispatch_tables(idx):
    t = idx.shape[0]
    a = t * TOP_K
    flat_e = idx.reshape(a)
    onehot = (flat_e[:, None] == jnp.arange(N_EXPERTS, dtype=jnp.int32)[None, :]).astype(jnp.int32)
    csum = jnp.cumsum(onehot, axis=0)
    rank = jnp.take_along_axis(csum, flat_e[:, None], axis=1)[:, 0] - 1
    counts = csum[-1]
    padded = (counts + MOE_BLOCK - 1) // MOE_BLOCK * MOE_BLOCK
    pad_end = jnp.cumsum(padded)
    pad_start = pad_end - padded
    dest = pad_start[flat_e] + rank
    nblk = -(-(a + N_EXPERTS * MOE_BLOCK) // MOE_BLOCK)
    cap = nblk * MOE_BLOCK
    slot_tok = jnp.zeros((cap,), jnp.int32).at[dest].set(jnp.arange(a, dtype=jnp.int32) // TOP_K)
    blk_start = jnp.arange(nblk, dtype=jnp.int32) * MOE_BLOCK
    blk_e = jnp.minimum(jnp.sum((pad_end[None, :] <= blk_start[:, None]).astype(jnp.int32), axis=1), N_EXPERTS - 1)
    n_used = (pad_end[-1:] // MOE_BLOCK).astype(jnp.int32)
    return dest.reshape(t, TOP_K).astype(jnp.int32), slot_tok, blk_e.astype(jnp.int32), n_used


def _moe_layer(x, g, sc, sh, gate, router_w, router_b, layer, w_gate, w_up, w_down):
    b, s, d = x.shape
    t = b * s
    h, idx, wts = _norm_route(x, g, sc, sh, router_w, router_b)
    idx = idx.transpose(0, 2, 1).reshape(t, TOP_K)
    wts = wts.transpose(0, 2, 1).reshape(t, TOP_K)
    dest, slot_tok, blk_e, n_used = _dispatch_tables(idx)
    yo = _expert_blocks(h.reshape(t, d), slot_tok, blk_e + layer * N_EXPERTS, n_used, w_gate, w_up, w_down)
    return _moe_combine(x.reshape(t, d), yo, dest, wts, gate, s).reshape(b, s, d)


def _nsa_layer(h, x, gate, slopes, j, w_in, w_phi1, w_phi2, phi_pos, w_out):
    b, s, d = h.shape
    t = b * s
    h2d = h.reshape(t, d)
    proj = _matmul(h2d, w_in, j, 0, NSA_QKV, jnp.bfloat16, scale=HEAD_DIM ** -0.5 * LOG2E, scale_cols=ATT_WIDTH)
    proj = proj.reshape(b, s, NSA_QKV)
    wg = w_in[j, :, NSA_QKV:].reshape(d, NSA_KV_HEADS, NSA_GATES)
    wg = jnp.pad(wg, ((0, 0), (0, 0), (0, LANES - NSA_GATES))).reshape(1, d, NSA_KV_HEADS * LANES)
    gates = _matmul(h2d, wg, 0, 0, NSA_KV_HEADS * LANES, jnp.float32, act="sigmoid")
    gates = gates.reshape(b, s, NSA_KV_HEADS * LANES)
    nch = s // CMP_STRIDE
    kv = proj[:, :, ATT_WIDTH:ATT_WIDTH + 2 * NSA_KV_WIDTH]
    kv = kv.reshape(b, nch, CMP_STRIDE, 2, NSA_KV_HEADS, HEAD_DIM).transpose(0, 3, 4, 1, 2, 5)
    cmp_kv = _compress(kv.reshape(b, 2, NSA_KV_HEADS, nch, CMP_STRIDE * HEAD_DIM), w_phi1, w_phi2, phi_pos)
    o = _nsa_attention(proj, cmp_kv, gates, slopes * LOG2E)
    return _matmul_residual(o, w_out, j, x, gate)


def _dil_layer(x, g, sc, sh, gate, slopes, j, w_in, w_out):
    b, s, d = x.shape
    dilations = tuple(r for _, r in DIL_PAIRS)
    assert dilations[0] == 1
    hs = _norm_mod_streams(x, g, sc, sh, dilations)
    qs = _matmul_streams(hs[0].reshape(b, s, d), w_in, j, 0, ATT_WIDTH, dilations)
    outs, lses = [], []
    for gidx, (win, r) in enumerate(DIL_PAIRS):
        off = ATT_WIDTH * (1 + 2 * gidx)
        kv = _matmul(hs[gidx].reshape(b * s, d), w_in, j, off, 2 * ATT_WIDTH, jnp.bfloat16)
        o, lse = _dil_attention(qs[gidx], kv.reshape(b, r, s // r, 2 * ATT_WIDTH), slopes, win, r)
        outs.append(o)
        lses.append(lse)
    o = _dil_merge(outs, lses, dilations)
    return _matmul_residual(o, w_out, j, x, gate)


def kernel(x, c, ada_w, ada_b, norm_mix, norm_ffn, norm_final, nsa_w_in, nsa_w_phi1, nsa_w_phi2, nsa_phi_pos,
           nsa_w_out, dil_w_in, dil_w_out, router_w, router_b, exp_w_gate, exp_w_up, exp_w_down):
    depth = ada_w.shape[0]
    d = x.shape[-1]
    mod = _modulation(c, ada_w, ada_b)
    slopes = 2.0 ** (-ALIBI_MAX_BIAS * jnp.arange(1, N_HEADS + 1, dtype=jnp.float32) / N_HEADS)
    e_gate, e_up, e_down = [_cast_bf16(w).reshape((-1,) + w.shape[2:]) for w in (exp_w_gate, exp_w_up, exp_w_down)]
    for i in range(depth):
        sh_m, sc_m, g_m, sh_f, sc_f, g_f = [mod[i, :, k * d:(k + 1) * d] for k in range(6)]
        j = i // 2
        if i % 2 == 0:
            h = _norm_mod(x, norm_mix[i], sc_m, sh_m, jnp.bfloat16)
            x = _nsa_layer(h, x, g_m, slopes, j, nsa_w_in, nsa_w_phi1[j], nsa_w_phi2[j], nsa_phi_pos[j], nsa_w_out)
        else:
            x = _dil_layer(x, norm_mix[i], sc_m, sh_m, g_m, slopes, j, dil_w_in, dil_w_out)
        x = _moe_layer(x, norm_ffn[i], sc_f, sh_f, g_f, router_w, router_b, i, e_gate, e_up, e_down)
    return _final_norm(x, norm_final)
```

```python
import functools

import jax
import jax.numpy as jnp
import numpy as np
from jax import lax
from jax.experimental import pallas as pl
from jax.experimental.pallas import tpu as pltpu

HEAD_DIM = 128
N_HEADS = 16
ATT_WIDTH = N_HEADS * HEAD_DIM
ALIBI_MAX_BIAS = 8.0

NSA_KV_HEADS = 4
NSA_GROUP = N_HEADS // NSA_KV_HEADS
NSA_KV_WIDTH = NSA_KV_HEADS * HEAD_DIM
CMP_BLOCK = 32
CMP_STRIDE = 16
SEL_BLOCK = 64
N_SEL = 16
NSA_WINDOW = 512
FORCE_SCORE = 1.0e6
NSA_QKV = ATT_WIDTH + 6 * NSA_KV_WIDTH
NSA_GATES = 3 * NSA_GROUP

DIL_PAIRS = ((128, 1), (512, 4), (2048, 16))
DIL_PROJ = ATT_WIDTH * (1 + 2 * len(DIL_PAIRS))

N_EXPERTS = 16
N_EXPERT_GROUPS = 4
EXPERTS_PER_GROUP = N_EXPERTS // N_EXPERT_GROUPS
TOP_K = 2
MOE_BLOCK = 256

RMS_EPS = 1e-6
NEG_INF = -1.0e30

LANES = 128
ATT_TILE = 128
VMEM_LIMIT = 56 * 1024 * 1024

_HI = lax.Precision.HIGHEST
_NT = (((1,), (1,)), ((), ()))


def _params(sem, vmem=VMEM_LIMIT):
    return pltpu.CompilerParams(dimension_semantics=sem, vmem_limit_bytes=vmem)


def _mod_kernel(c_ref, w_ref, b_ref, o_ref):
    c = c_ref[...]
    cond = c * jax.nn.sigmoid(c)
    o_ref[0] = jnp.dot(cond, w_ref[0], precision=_HI, preferred_element_type=jnp.float32) + b_ref[0]


def _modulation(c, ada_w, ada_b):
    depth, d, n = ada_w.shape
    b = c.shape[0]
    tn = 512
    return pl.pallas_call(
        _mod_kernel,
        grid=(depth, n // tn),
        in_specs=[pl.BlockSpec((b, d), lambda i, j: (0, 0)),
                  pl.BlockSpec((1, d, tn), lambda i, j: (i, 0, j)),
                  pl.BlockSpec((1, 1, tn), lambda i, j: (i, 0, j))],
        out_specs=pl.BlockSpec((1, b, tn), lambda i, j: (i, 0, j)),
        out_shape=jax.ShapeDtypeStruct((depth, b, n), jnp.float32),
        compiler_params=_params(("parallel", "parallel")),
        name="adaln_modulation",
    )(c, ada_w, ada_b.reshape(depth, 1, n))


def _rms(x, g):
    return x * lax.rsqrt(jnp.mean(x * x, axis=-1, keepdims=True) + RMS_EPS) * g


def _norm_mod_kernel(x_ref, g_ref, sc_ref, sh_ref, o_ref):
    h = _rms(x_ref[0], g_ref[...]) * (1.0 + sc_ref[0]) + sh_ref[0]
    o_ref[0] = h.astype(o_ref.dtype)


def _norm_mod(x, g, sc, sh, out_dtype):
    b, s, d = x.shape
    ts = min(256, s)
    return pl.pallas_call(
        _norm_mod_kernel,
        grid=(b, s // ts),
        in_specs=[pl.BlockSpec((1, ts, d), lambda i, j: (i, j, 0)),
                  pl.BlockSpec((1, d), lambda i, j: (0, 0)),
                  pl.BlockSpec((1, 1, d), lambda i, j: (i, 0, 0)),
                  pl.BlockSpec((1, 1, d), lambda i, j: (i, 0, 0))],
        out_specs=pl.BlockSpec((1, ts, d), lambda i, j: (i, j, 0)),
        out_shape=jax.ShapeDtypeStruct((b, s, d), out_dtype),
        compiler_params=_params(("parallel", "parallel")),
        name="rmsnorm_adaln",
    )(x, g.reshape(1, d), sc.reshape(b, 1, d), sh.reshape(b, 1, d))


def _store_streams(val, src_sc, out_refs, dilations):
    n = val.shape[0]
    for c in range(src_sc.shape[0]):
        src_sc[c] = val[:, c * LANES:(c + 1) * LANES]
    for r, ref in zip(dilations, out_refs):
        if r == 1:
            ref[0, 0] = val.astype(ref.dtype)
            continue
        for rho in range(r):
            for c in range(src_sc.shape[0]):
                ref[0, rho, :, c * LANES:(c + 1) * LANES] = (
                    src_sc[c, pl.ds(rho, n // r, stride=r), :].astype(ref.dtype))


def _norm_mod_streams_kernel(x_ref, g_ref, sc_ref, sh_ref, *refs, dilations):
    out_refs, h_sc = refs[:-1], refs[-1]
    h = _rms(x_ref[0], g_ref[...]) * (1.0 + sc_ref[0]) + sh_ref[0]
    _store_streams(h, h_sc, out_refs, dilations)


def _norm_mod_streams(x, g, sc, sh, dilations):
    b, s, d = x.shape
    ts = min(256, s)
    return pl.pallas_call(
        functools.partial(_norm_mod_streams_kernel, dilations=dilations),
        grid=(b, s // ts),
        in_specs=[pl.BlockSpec((1, ts, d), lambda i, j: (i, j, 0)),
                  pl.BlockSpec((1, d), lambda i, j: (0, 0)),
                  pl.BlockSpec((1, 1, d), lambda i, j: (i, 0, 0)),
                  pl.BlockSpec((1, 1, d), lambda i, j: (i, 0, 0))],
        out_specs=[pl.BlockSpec((1, r, ts // r, d), lambda i, j: (i, 0, j, 0)) for r in dilations],
        out_shape=[jax.ShapeDtypeStruct((b, r, s // r, d), jnp.bfloat16) for r in dilations],
        scratch_shapes=[pltpu.VMEM((d // LANES, ts, LANES), jnp.float32)],
        compiler_params=_params(("parallel", "parallel")),
        name="rmsnorm_adaln_streams",
    )(x, g.reshape(1, d), sc.reshape(b, 1, d), sh.reshape(b, 1, d))


def _pair_max(vals):
    out = None
    for a in range(len(vals)):
        for b in range(a + 1, len(vals)):
            s = vals[a] + vals[b]
            out = s if out is None else jnp.maximum(out, s)
    return out


def _norm_route_kernel(x_ref, g_ref, sc_ref, sh_ref, rw_ref, rb_ref, h_ref, idx_ref, wt_ref):
    h = _rms(x_ref[0], g_ref[...]) * (1.0 + sc_ref[0]) + sh_ref[0]
    h_ref[0] = h
    logits = lax.dot_general(rw_ref[...], h, _NT, precision=_HI, preferred_element_type=jnp.float32)
    scores = jax.nn.sigmoid(logits)
    biased = scores + rb_ref[...][:, 0:1]
    rows = [biased[e:e + 1, :] for e in range(N_EXPERTS)]
    srow = [scores[e:e + 1, :] for e in range(N_EXPERTS)]
    grp = [_pair_max(rows[q * EXPERTS_PER_GROUP:(q + 1) * EXPERTS_PER_GROUP]) for q in range(N_EXPERT_GROUPS)]
    best_v = grp[0]
    best = jnp.zeros(best_v.shape, jnp.int32)
    for q in range(1, N_EXPERT_GROUPS):
        take = grp[q] > best_v
        best = jnp.where(take, q, best)
        best_v = jnp.where(take, grp[q], best_v)
    v1 = jnp.full(best_v.shape, NEG_INF, jnp.float32)
    i1 = jnp.zeros(best_v.shape, jnp.int32)
    for e in range(N_EXPERTS):
        cand = jnp.where(best == e // EXPERTS_PER_GROUP, rows[e], NEG_INF)
        take = cand > v1
        i1 = jnp.where(take, e, i1)
        v1 = jnp.where(take, cand, v1)
    v2 = jnp.full(best_v.shape, NEG_INF, jnp.float32)
    i2 = jnp.zeros(best_v.shape, jnp.int32)
    for e in range(N_EXPERTS):
        cand = jnp.where(best == e // EXPERTS_PER_GROUP, jnp.where(i1 == e, NEG_INF, rows[e]), NEG_INF)
        take = cand > v2
        i2 = jnp.where(take, e, i2)
        v2 = jnp.where(take, cand, v2)
    w1 = jnp.zeros(best_v.shape, jnp.float32)
    w2 = jnp.zeros(best_v.shape, jnp.float32)
    for e in range(N_EXPERTS):
        w1 = jnp.where(i1 == e, srow[e], w1)
        w2 = jnp.where(i2 == e, srow[e], w2)
    tot = w1 + w2
    idx_ref[0] = jnp.concatenate([i1, i2], axis=0)
    wt_ref[0] = jnp.concatenate([w1 / tot, w2 / tot], axis=0)


def _norm_route(x, g, sc, sh, router_w, router_b):
    b, s, d = x.shape
    ts = min(256, s)
    rwt = router_w.T
    rb = jnp.broadcast_to(router_b.reshape(N_EXPERTS, 1), (N_EXPERTS, LANES))
    return pl.pallas_call(
        _norm_route_kernel,
        grid=(b, s // ts),
        in_specs=[pl.BlockSpec((1, ts, d), lambda i, j: (i, j, 0)),
                  pl.BlockSpec((1, d), lambda i, j: (0, 0)),
                  pl.BlockSpec((1, 1, d), lambda i, j: (i, 0, 0)),
                  pl.BlockSpec((1, 1, d), lambda i, j: (i, 0, 0)),
                  pl.BlockSpec((N_EXPERTS, d), lambda i, j: (0, 0)),
                  pl.BlockSpec((N_EXPERTS, LANES), lambda i, j: (0, 0))],
        out_specs=[pl.BlockSpec((1, ts, d), lambda i, j: (i, j, 0)),
                   pl.BlockSpec((1, TOP_K, ts), lambda i, j: (i, 0, j)),
                   pl.BlockSpec((1, TOP_K, ts), lambda i, j: (i, 0, j))],
        out_shape=[jax.ShapeDtypeStruct((b, s, d), jnp.float32),
                   jax.ShapeDtypeStruct((b, TOP_K, s), jnp.int32),
                   jax.ShapeDtypeStruct((b, TOP_K, s), jnp.float32)],
        compiler_params=_params(("parallel", "parallel")),
        name="rmsnorm_adaln_router",
    )(x, g.reshape(1, d), sc.reshape(b, 1, d), sh.reshape(b, 1, d), rwt, rb)


def _weight_tile(w_ref, wbf_sc, first, scale, scale_tiles):
    @pl.when(first)
    def _():
        w = w_ref[0]
        if scale_tiles:
            w = w * jnp.where(pl.program_id(0) < scale_tiles, scale, 1.0)
        wbf_sc[...] = w.astype(jnp.bfloat16)
    return wbf_sc[...]


def _weight_spec(layer, col0, k, tn, rank):
    assert col0 % tn == 0
    if rank == 2:
        return pl.BlockSpec((1, k, tn), lambda j, i: (layer, 0, col0 // tn + j))
    return pl.BlockSpec((1, k, tn), lambda j, bi, i: (layer, 0, col0 // tn + j))


def _mm_kernel(a_ref, w_ref, o_ref, wbf_sc, *, act, scale, scale_tiles):
    w = _weight_tile(w_ref, wbf_sc, pl.program_id(1) == 0, scale, scale_tiles)
    acc = jnp.dot(a_ref[...], w, preferred_element_type=jnp.float32)
    if act == "sigmoid":
        acc = jax.nn.sigmoid(acc)
    o_ref[...] = acc.astype(o_ref.dtype)


def _matmul(a, w, layer, col0, n, out_dtype, act=None, scale=1.0, scale_cols=0, tm=1024, tn=1024):
    m, k = a.shape
    tn = min(tn, n)
    tm = min(tm, m)
    assert scale_cols % tn == 0
    return pl.pallas_call(
        functools.partial(_mm_kernel, act=act, scale=scale, scale_tiles=scale_cols // tn),
        grid=(n // tn, m // tm),
        in_specs=[pl.BlockSpec((tm, k), lambda j, i: (i, 0)),
                  _weight_spec(layer, col0, k, tn, 2)],
        out_specs=pl.BlockSpec((tm, tn), lambda j, i: (i, j)),
        out_shape=jax.ShapeDtypeStruct((m, n), out_dtype),
        scratch_shapes=[pltpu.VMEM((k, tn), jnp.bfloat16)],
        compiler_params=_params(("arbitrary", "arbitrary")),
        name="matmul" if act is None else "matmul_" + act,
    )(a, w)


def _first_row_step():
    return (pl.program_id(1) == 0) & (pl.program_id(2) == 0)


def _mm_streams_kernel(a_ref, w_ref, *refs, dilations, scale, scale_tiles):
    out_refs, acc_sc, wbf_sc = refs[:-2], refs[-2], refs[-1]
    w = _weight_tile(w_ref, wbf_sc, _first_row_step(), scale, scale_tiles)
    acc = jnp.dot(a_ref[0], w, preferred_element_type=jnp.float32)
    _store_streams(acc, acc_sc, out_refs, dilations)


def _matmul_streams(a, w, layer, col0, n, dilations, scale=1.0, tm=512, tn=1024):
    b, s, k = a.shape
    tn = min(tn, n)
    tm = min(tm, s)
    return pl.pallas_call(
        functools.partial(_mm_streams_kernel, dilations=dilations, scale=scale,
                          scale_tiles=n // tn if scale != 1.0 else 0),
        grid=(n // tn, b, s // tm),
        in_specs=[pl.BlockSpec((1, tm, k), lambda j, bi, i: (bi, i, 0)),
                  _weight_spec(layer, col0, k, tn, 3)],
        out_specs=[pl.BlockSpec((1, r, tm // r, tn), lambda j, bi, i: (bi, 0, i, j)) for r in dilations],
        out_shape=[jax.ShapeDtypeStruct((b, r, s // r, n), jnp.bfloat16) for r in dilations],
        scratch_shapes=[pltpu.VMEM((tn // LANES, tm, LANES), jnp.float32), pltpu.VMEM((k, tn), jnp.bfloat16)],
        compiler_params=_params(("arbitrary", "arbitrary", "arbitrary")),
        name="matmul_streams",
    )(a, w)


def _mm_residual_kernel(a_ref, w_ref, x_ref, g_ref, o_ref, wbf_sc):
    w = _weight_tile(w_ref, wbf_sc, _first_row_step(), 1.0, 0)
    acc = jnp.dot(a_ref[0], w, preferred_element_type=jnp.float32)
    o_ref[0] = x_ref[0] + g_ref[0] * acc


def _matmul_residual(a, w, layer, x, gate, tm=512, tn=1024):
    b, s, k = a.shape
    n = w.shape[2]
    tn = min(tn, n)
    tm = min(tm, s)
    return pl.pallas_call(
        _mm_residual_kernel,
        grid=(n // tn, b, s // tm),
        in_specs=[pl.BlockSpec((1, tm, k), lambda j, bi, i: (bi, i, 0)),
                  _weight_spec(layer, 0, k, tn, 3),
                  pl.BlockSpec((1, tm, tn), lambda j, bi, i: (bi, i, j)),
                  pl.BlockSpec((1, 1, tn), lambda j, bi, i: (bi, 0, j))],
        out_specs=pl.BlockSpec((1, tm, tn), lambda j, bi, i: (bi, i, j)),
        out_shape=jax.ShapeDtypeStruct((b, s, n), jnp.float32),
        scratch_shapes=[pltpu.VMEM((k, tn), jnp.bfloat16)],
        compiler_params=_params(("arbitrary", "arbitrary", "arbitrary")),
        name="matmul_gated_residual",
    )(a, w, x, gate.reshape(b, 1, n))


def _cast_kernel(w_ref, o_ref):
    o_ref[...] = w_ref[...].astype(o_ref.dtype)


def _cast_bf16(w, rows=1024):
    c = w.shape[-1]
    w2 = w.reshape(-1, c)
    r = w2.shape[0]
    out = pl.pallas_call(
        _cast_kernel,
        grid=(r // rows,),
        in_specs=[pl.BlockSpec((rows, c), lambda i: (i, 0))],
        out_specs=pl.BlockSpec((rows, c), lambda i: (i, 0)),
        out_shape=jax.ShapeDtypeStruct((r, c), jnp.bfloat16),
        compiler_params=_params(("parallel",)),
        name="cast_bf16",
    )(w2)
    return out.reshape(w.shape)


def _gelu_tanh(x):
    return 0.5 * x * (1.0 + jnp.tanh(0.7978845608028654 * (x + 0.044715 * (x * x * x))))


def _compress_kernel(c_ref, pa_ref, pb_ref, wa_ref, wb_ref, w2_ref, o_ref):
    c = c_ref[0, 0, 0].astype(jnp.float32)
    lo = (c + pa_ref[0]).astype(jnp.bfloat16)
    hi = (c + pb_ref[0]).astype(jnp.bfloat16)
    ha = jnp.dot(lo, wa_ref[0], preferred_element_type=jnp.float32)
    hb = jnp.dot(hi, wb_ref[0], preferred_element_type=jnp.float32)
    n = ha.shape[0]
    hid = _gelu_tanh(ha + pltpu.roll(hb, n - 1, 0))
    out = jnp.dot(hid.astype(jnp.bfloat16), w2_ref[0], preferred_element_type=jnp.float32)
    row = lax.broadcasted_iota(jnp.int32, out.shape, 0)
    o_ref[0, 0, 0] = jnp.where(row < n - 1, out, 0.0).astype(o_ref.dtype)


def _compress(kv_chunks, w_phi1, w_phi2, phi_pos):
    b, two, hkv, nch, width = kv_chunks.shape
    dh = HEAD_DIM
    half = CMP_BLOCK // 2
    w1 = w_phi1.reshape(2, 2, half * dh, dh).astype(jnp.bfloat16)
    pos = phi_pos.reshape(2, 2, 1, half * dh)
    return pl.pallas_call(
        _compress_kernel,
        grid=(b, two, hkv),
        in_specs=[pl.BlockSpec((1, 1, 1, nch, width), lambda i, j, g: (i, j, g, 0, 0)),
                  pl.BlockSpec((1, 1, width), lambda i, j, g: (j, 0, 0)),
                  pl.BlockSpec((1, 1, width), lambda i, j, g: (j, 0, 0)),
                  pl.BlockSpec((1, width, dh), lambda i, j, g: (j, 0, 0)),
                  pl.BlockSpec((1, width, dh), lambda i, j, g: (j, 0, 0)),
                  pl.BlockSpec((1, dh, dh), lambda i, j, g: (j, 0, 0))],
        out_specs=pl.BlockSpec((1, 1, 1, nch, dh), lambda i, j, g: (i, j, g, 0, 0)),
        out_shape=jax.ShapeDtypeStruct((b, two, hkv, nch, dh), jnp.bfloat16),
        compiler_params=_params(("parallel", "parallel", "parallel")),
        name="nsa_compress",
    )(kv_chunks, pos[:, 0], pos[:, 1], w1[:, 0], w1[:, 1], w_phi2.astype(jnp.bfloat16))


MASK_BIG = 2.0 ** 100
AUX_SLOPE = 64
AUX_PAD = 70
SEL_CHUNK = 512
LOG2E = 1.4426950408889634


def _nsa_attn_kernel(sl_ref, q_ref, kc_ref, vc_ref, ks_ref, vs_ref, kw_ref, vw_ref, gt_ref, ovl_ref, qtab_ref,
                     kauxs_ref, kauxw_ref, o_ref, ocmp_sc, maskq_sc, owin_sc, qaug_sc, qaugw_sc, p_sc, pw_sc,
                     kaug_s, vaug_s, kaug_w, vaug_w, *, n_cmp, n_blk, n_sel):
    t = ATT_TILE
    dh = HEAD_DIM
    s_len = ks_ref.shape[1]
    wpad = NSA_WINDOW
    g = pl.program_id(1)
    qt = pl.program_id(2)
    qstart = pl.multiple_of(qt * t, t)

    @pl.when(qt == 0)
    def _():
        ones = jnp.ones((s_len, dh), jnp.bfloat16)
        kaug_s[:, :dh] = ks_ref[0]
        kaug_s[:, dh:] = kauxs_ref[...]
        vaug_s[:, :dh] = vs_ref[0]
        vaug_s[:, dh:] = ones
        kaug_w[:wpad, :dh] = jnp.zeros((wpad, dh), jnp.bfloat16)
        kaug_w[wpad:, :dh] = kw_ref[0]
        kaug_w[:, dh:] = kauxw_ref[...]
        vaug_w[:wpad, :dh] = jnp.zeros((wpad, dh), jnp.bfloat16)
        vaug_w[wpad:, :dh] = vw_ref[0]
        vaug_w[:wpad, dh:] = jnp.ones((wpad, dh), jnp.bfloat16)
        vaug_w[wpad:, dh:] = ones

    slopes = [sl_ref[g * NSA_GROUP + r] for r in range(NSA_GROUP)]
    row = lax.broadcasted_iota(jnp.int32, (t, LANES), 0)
    col = lax.broadcasted_iota(jnp.int32, (t, LANES), 1)
    prows = [slice(pair * 2 * t, (pair + 1) * 2 * t) for pair in range(NSA_GROUP // 2)]
    qtab = qtab_ref[0]

    def compress_and_select(qv, tile, par):
        qs = tile * t
        tpos = qs + row
        q4 = jnp.concatenate([qv[:, r * dh:(r + 1) * dh] for r in range(NSA_GROUP)], axis=0)
        kc = kc_ref[0, 0, 0]
        vc = vc_ref[0, 0, 0]
        s = lax.dot_general(q4, kc, _NT, preferred_element_type=jnp.float32)
        visible = (col * CMP_STRIDE + (CMP_BLOCK - 1) <= tpos) & (col < n_cmp)
        dist_c = tpos.astype(jnp.float32) - (col.astype(jnp.float32) * CMP_STRIDE + (CMP_BLOCK - 1) / 2.0)
        psum = jnp.zeros((t, LANES), jnp.float32)
        for r in range(NSA_GROUP):
            rows = slice(r * t, (r + 1) * t)
            sr = jnp.where(visible, s[rows] - slopes[r] * dist_c, NEG_INF)
            e = jnp.where(visible, jnp.exp2(sr - jnp.max(sr, axis=-1, keepdims=True)), 0.0)
            p = e / jnp.maximum(jnp.sum(e, axis=-1, keepdims=True), 1e-30)
            psum = psum + p
            ocmp_sc[par, rows] = jnp.dot(p.astype(vc.dtype), vc, preferred_element_type=jnp.float32)
        nb = -(-n_blk // 8) * 8
        imp_t = lax.dot_general(ovl_ref[...], psum, _NT, precision=_HI, preferred_element_type=jnp.float32)[:nb]
        jrow = lax.broadcasted_iota(jnp.int32, (nb, t), 0)
        tpos_t = qs + lax.broadcasted_iota(jnp.int32, (nb, t), 1)
        cur = jnp.right_shift(tpos_t, int(np.log2(SEL_BLOCK)))
        forced = (jrow == 0) | (jrow == cur) | (jrow == cur - 1)
        score = jnp.where(forced, FORCE_SCORE, jnp.where(jrow * SEL_BLOCK <= tpos_t, imp_t, -1.0))
        rank = jnp.zeros((nb, t), jnp.float32)
        for j in range(n_blk):
            cj = score[j:j + 1, :]
            ahead = (cj > score) | ((cj == score) & (jrow > j))
            rank = rank + jnp.where(ahead, 1.0, 0.0)
        mask_t = jnp.where((rank < n_sel) & (jrow < n_blk), 0.0, -MASK_BIG)
        maskq_sc[par] = jnp.concatenate([mask_t, jnp.zeros((LANES - nb, t), jnp.float32)], axis=0).T

    par = 0
    q = q_ref[0]

    for r in range(NSA_GROUP):
        rows = slice(r * t, (r + 1) * t)
        qaugw_sc[rows, :dh] = q[:, r * dh:(r + 1) * dh]
        qaugw_sc[rows, dh:] = jnp.broadcast_to(qtab[r:r + 1, :], (t, LANES)).astype(jnp.bfloat16)
    wk = wpad + t
    nw = wk // LANES
    kw = kaug_w[pl.ds(qstart, wk), :]
    vw = vaug_w[pl.ds(qstart, wk), :]
    sws = [lax.dot_general(qaugw_sc[prow], kw, _NT, preferred_element_type=jnp.float32) for prow in prows]

    compress_and_select(q, qt, par)

    mask_q = maskq_sc[par]
    for r in range(NSA_GROUP):
        rows = slice(r * t, (r + 1) * t)
        qaug_sc[rows, :dh] = q[:, r * dh:(r + 1) * dh]
        qaug_sc[rows, dh:] = (mask_q + qtab[r:r + 1, :]).astype(jnp.bfloat16)

    for pair, prow in enumerate(prows):
        sw = sws[pair]
        for r2 in range(2):
            rows = slice((2 * pair + r2) * t, (2 * pair + r2 + 1) * t)
            tiles = [sw[r2 * t:(r2 + 1) * t, j * LANES:(j + 1) * LANES] for j in range(nw)]
            tiles[0] = jnp.where(col > row, tiles[0], NEG_INF)
            tiles[-1] = jnp.where(col <= row, tiles[-1], NEG_INF)
            mx = functools.reduce(jnp.maximum, tiles)
            m = jnp.broadcast_to(jnp.max(mx, axis=-1, keepdims=True), (t, LANES))
            for j in range(nw):
                pw_sc[rows, j * LANES:(j + 1) * LANES] = jnp.exp2(tiles[j] - m).astype(jnp.bfloat16)
        ow = jnp.dot(pw_sc[prow], vw, preferred_element_type=jnp.float32)
        owin_sc[prow] = ow[:, :dh] / ow[:, dh:]

    ch = SEL_CHUNK
    nl = ch // LANES
    n_full = qt // (ch // t)
    gt = gt_ref[0]
    cmr = col - row

    def selected(k):
        nk = (k + 1) * ch
        kk = kaug_s[:nk, :]
        vv = vaug_s[:nk, :]
        sks = [lax.dot_general(qaug_sc[prow], kk, _NT, preferred_element_type=jnp.float32) for prow in prows]
        for pair, prow in enumerate(prows):
            sk = sks[pair]
            for r2 in range(2):
                rows = slice((2 * pair + r2) * t, (2 * pair + r2 + 1) * t)
                tiles = [sk[r2 * t:(r2 + 1) * t, j * LANES:(j + 1) * LANES] for j in range(nk // LANES)]
                for j in range(k * nl, (k + 1) * nl):
                    tiles[j] = jnp.where(cmr <= qstart - j * LANES, tiles[j], NEG_INF)
                mx = functools.reduce(jnp.maximum, tiles)
                m = jnp.broadcast_to(jnp.max(mx, axis=-1, keepdims=True), (t, LANES))
                for j in range(nk // LANES):
                    p_sc[rows, j * LANES:(j + 1) * LANES] = jnp.exp2(tiles[j] - m).astype(jnp.bfloat16)
            pv = jnp.dot(p_sc[prow, :nk], vv, preferred_element_type=jnp.float32)
            for r2 in range(2):
                r = 2 * pair + r2
                rows = slice(r * t, (r + 1) * t)
                orow = slice(r2 * t, (r2 + 1) * t)
                o = (gt[:, 3 * r:3 * r + 1] * ocmp_sc[par, rows]
                     + gt[:, 3 * r + 1:3 * r + 2] * (pv[orow, :dh] / pv[orow, dh:])
                     + gt[:, 3 * r + 2:3 * r + 3] * owin_sc[rows])
                o_ref[0, :, r * dh:(r + 1) * dh] = o.astype(o_ref.dtype)

    for k in range(s_len // ch):
        pl.when(n_full == k)(functools.partial(selected, k))


def _bf16_pieces(x):
    a0 = x.astype(jnp.bfloat16).astype(jnp.float32)
    a1 = (x - a0).astype(jnp.bfloat16).astype(jnp.float32)
    a2 = (x - a0 - a1).astype(jnp.bfloat16).astype(jnp.float32)
    return [a0, a1, a2]


def _nsa_tables(s, slopes2):
    n_cmp = s // CMP_STRIDE - CMP_BLOCK // CMP_STRIDE + 1
    n_blk = s // SEL_BLOCK
    assert n_cmp < LANES and n_blk <= AUX_SLOPE
    bj = np.arange(LANES)[:, None]
    cn = np.arange(LANES)[None, :]
    ovl_t = np.clip(np.minimum(cn * CMP_STRIDE + CMP_BLOCK, (bj + 1) * SEL_BLOCK)
                    - np.maximum(cn * CMP_STRIDE, bj * SEL_BLOCK), 0, None) / CMP_STRIDE
    ovl_t = np.where((cn < n_cmp) & (bj < n_blk), ovl_t, 0.0).astype(np.float32)
    pos = np.arange(s)
    kaux = np.zeros((s, LANES), np.float32)
    kaux[pos, pos // SEL_BLOCK] = 1.0
    kaux[:, AUX_SLOPE:AUX_SLOPE + 3] = (pos // LANES * LANES)[:, None]
    kaux[:, AUX_SLOPE + 3:AUX_SLOPE + 6] = (pos % LANES)[:, None]
    kaux_w = np.zeros((NSA_WINDOW + s, LANES), np.float32)
    kaux_w[NSA_WINDOW:, AUX_SLOPE:AUX_SLOPE + 6] = kaux[:, AUX_SLOPE:AUX_SLOPE + 6]
    kaux_w[:NSA_WINDOW, AUX_PAD] = 1.0
    pieces = jnp.stack(_bf16_pieces(slopes2) * 2, axis=-1)
    qtab = jnp.zeros((N_HEADS, LANES), jnp.float32)
    qtab = qtab.at[:, AUX_SLOPE:AUX_SLOPE + 6].set(pieces).at[:, AUX_PAD].set(-MASK_BIG)
    qtab = jnp.pad(qtab.reshape(NSA_KV_HEADS, NSA_GROUP, LANES), ((0, 0), (0, 8 - NSA_GROUP), (0, 0)))
    return (n_cmp, n_blk, jnp.asarray(ovl_t), qtab, jnp.asarray(kaux, dtype=jnp.bfloat16),
            jnp.asarray(kaux_w, dtype=jnp.bfloat16))


def _nsa_attention(proj, cmp_kv, gates, slopes2):
    b, s, _ = proj.shape
    t = ATT_TILE
    n_cmp, n_blk, ovl_t, qtab, kaux_s, kaux_w = _nsa_tables(s, slopes2)
    assert cmp_kv.shape[3] == LANES and s % SEL_CHUNK == 0 and NSA_WINDOW % t == 0
    qw = NSA_GROUP * HEAD_DIM
    rows = NSA_GROUP * t
    kv0 = ATT_WIDTH // HEAD_DIM

    def kv_spec(j):
        return pl.BlockSpec((1, s, HEAD_DIM), lambda bi, g, qi, j=j: (bi, 0, kv0 + j * NSA_KV_HEADS + g))

    def const(shape):
        return pl.BlockSpec(shape, lambda bi, g, qi: (0,) * len(shape))

    return pl.pallas_call(
        functools.partial(_nsa_attn_kernel, n_cmp=n_cmp, n_blk=n_blk, n_sel=min(N_SEL, n_blk)),
        grid=(b, NSA_KV_HEADS, s // t),
        in_specs=[pl.BlockSpec(memory_space=pltpu.SMEM),
                  pl.BlockSpec((1, t, qw), lambda bi, g, qi: (bi, qi, g)),
                  pl.BlockSpec((1, 1, 1, LANES, HEAD_DIM), lambda bi, g, qi: (bi, 0, g, 0, 0)),
                  pl.BlockSpec((1, 1, 1, LANES, HEAD_DIM), lambda bi, g, qi: (bi, 1, g, 0, 0)),
                  kv_spec(2), kv_spec(3), kv_spec(4), kv_spec(5),
                  pl.BlockSpec((1, t, LANES), lambda bi, g, qi: (bi, qi, g)),
                  const((LANES, LANES)),
                  pl.BlockSpec((1, 8, LANES), lambda bi, g, qi: (g, 0, 0)),
                  const((s, LANES)), const((NSA_WINDOW + s, LANES))],
        out_specs=pl.BlockSpec((1, t, qw), lambda bi, g, qi: (bi, qi, g)),
        out_shape=jax.ShapeDtypeStruct((b, s, ATT_WIDTH), jnp.bfloat16),
        scratch_shapes=[pltpu.VMEM((1, rows, HEAD_DIM), jnp.float32),
                        pltpu.VMEM((1, t, LANES), jnp.float32),
                        pltpu.VMEM((rows, HEAD_DIM), jnp.float32),
                        pltpu.VMEM((rows, 2 * HEAD_DIM), jnp.bfloat16),
                        pltpu.VMEM((rows, 2 * HEAD_DIM), jnp.bfloat16),
                        pltpu.VMEM((rows, s), jnp.bfloat16),
                        pltpu.VMEM((rows, NSA_WINDOW + t), jnp.bfloat16),
                        pltpu.VMEM((s, 2 * HEAD_DIM), jnp.bfloat16),
                        pltpu.VMEM((s, 2 * HEAD_DIM), jnp.bfloat16),
                        pltpu.VMEM((NSA_WINDOW + s, 2 * HEAD_DIM), jnp.bfloat16),
                        pltpu.VMEM((NSA_WINDOW + s, 2 * HEAD_DIM), jnp.bfloat16)],
        compiler_params=_params(("arbitrary", "arbitrary", "arbitrary")),
        name="nsa_attention",
    )(slopes2, proj, cmp_kv, cmp_kv, proj, proj, proj, proj, gates, ovl_t, qtab, kaux_s, kaux_w)


def _dil_attn_kernel(sl_ref, q_ref, kp_ref, kc_ref, vp_ref, vc_ref, o_ref, lse_ref, bias_sc, *, window):
    t = ATT_TILE
    i = pl.program_id(2)

    @pl.when((pl.program_id(0) == 0) & (pl.program_id(1) == 0) & (i == 0))
    def _():
        row = lax.broadcasted_iota(jnp.int32, (t, 2 * t), 0)
        col = lax.broadcasted_iota(jnp.int32, (t, 2 * t), 1)
        dist = row - col + t
        band = (dist >= 0) & (dist <= window)
        distf = dist.astype(jnp.float32)
        for h in range(N_HEADS):
            bias = sl_ref[h] * distf
            bias_sc[0, h] = jnp.where(band & (col >= t), bias, -NEG_INF)
            bias_sc[1, h] = jnp.where(band, bias, -NEG_INF)

    var = jnp.minimum(i, 1)
    lane = lax.broadcasted_iota(jnp.int32, (t, LANES), 1)
    lse_all = jnp.zeros((t, LANES), jnp.float32)
    for h in range(N_HEADS):
        hs = slice(h * HEAD_DIM, (h + 1) * HEAD_DIM)
        q = q_ref[0, 0, :, hs]
        k = jnp.concatenate([kp_ref[0, 0, :, hs], kc_ref[0, 0, :, hs]], axis=0)
        v = jnp.concatenate([vp_ref[0, 0, :, hs], vc_ref[0, 0, :, hs]], axis=0)
        s = lax.dot_general(q, k, _NT, preferred_element_type=jnp.float32) - bias_sc[var, h]
        m = jnp.max(s, axis=-1, keepdims=True)
        e = jnp.exp2(s - m)
        l = jnp.sum(e, axis=-1, keepdims=True)
        o = jnp.dot(e.astype(v.dtype), v, preferred_element_type=jnp.float32) / l
        o_ref[0, 0, :, hs] = o.astype(o_ref.dtype)
        lse_all = jnp.where(lane == h, m + jnp.log2(l), lse_all)
    lse_ref[0, 0] = lse_all


def _dil_attention(q, kv, slopes2, win, r):
    b, _, ln, _ = q.shape
    t = ATT_TILE

    def cur(c):
        return pl.BlockSpec((1, 1, t, ATT_WIDTH), lambda bi, rho, i, c=c: (bi, rho, i, c))

    def prev(c):
        return pl.BlockSpec((1, 1, t, ATT_WIDTH), lambda bi, rho, i, c=c: (bi, rho, jnp.maximum(i - 1, 0), c))

    return pl.pallas_call(
        functools.partial(_dil_attn_kernel, window=win // r),
        grid=(b, r, ln // t),
        in_specs=[pl.BlockSpec(memory_space=pltpu.SMEM), cur(0), prev(0), cur(0), prev(1), cur(1)],
        out_specs=[pl.BlockSpec((1, 1, t, ATT_WIDTH), lambda bi, rho, i: (bi, rho, i, 0)),
                   pl.BlockSpec((1, 1, t, LANES), lambda bi, rho, i: (bi, rho, i, 0))],
        out_shape=[jax.ShapeDtypeStruct((b, r, ln, ATT_WIDTH), jnp.bfloat16),
                   jax.ShapeDtypeStruct((b, r, ln, LANES), jnp.float32)],
        scratch_shapes=[pltpu.VMEM((2, N_HEADS, t, 2 * t), jnp.float32)],
        compiler_params=_params(("arbitrary", "arbitrary", "arbitrary")),
        name="dilated_attention_r%d" % r,
    )(slopes2 * r, q, kv, kv, kv, kv)


def _dil_merge_kernel(*refs, dilations):
    ng = len(dilations)
    o_refs, l_refs, out_ref, o_sc, l_sc = refs[:ng], refs[ng:2 * ng], refs[2 * ng], refs[2 * ng + 1], refs[2 * ng + 2]
    ts = out_ref.shape[1]
    ls = []
    for gi, r in enumerate(dilations):
        if r == 1:
            ls.append(l_refs[gi][0, 0])
            continue
        for rho in range(r):
            l_sc[gi, pl.ds(rho, ts // r, stride=r), :] = l_refs[gi][0, rho]
            for h in range(N_HEADS):
                o_sc[gi, h, pl.ds(rho, ts // r, stride=r), :] = (
                    o_refs[gi][0, rho, :, h * HEAD_DIM:(h + 1) * HEAD_DIM].astype(jnp.float32))
        ls.append(l_sc[gi])
    m = functools.reduce(jnp.maximum, ls)
    es = [jnp.exp2(l - m) for l in ls]
    den = functools.reduce(lambda a, b: a + b, es)
    ws = [e / den for e in es]
    for h in range(N_HEADS):
        hs = slice(h * HEAD_DIM, (h + 1) * HEAD_DIM)
        acc = None
        for gi, r in enumerate(dilations):
            og = o_refs[gi][0, 0, :, hs].astype(jnp.float32) if r == 1 else o_sc[gi, h]
            term = ws[gi][:, h:h + 1] * og
            acc = term if acc is None else acc + term
        out_ref[0, :, hs] = acc.astype(out_ref.dtype)


def _dil_merge(outs, lses, dilations):
    b, _, _, w = outs[0].shape
    s = outs[0].shape[1] * outs[0].shape[2]
    ts = min(256, s)
    ng = len(dilations)

    def spec(r, width):
        return pl.BlockSpec((1, r, ts // r, width), lambda i, j: (i, 0, j, 0))

    return pl.pallas_call(
        functools.partial(_dil_merge_kernel, dilations=dilations),
        grid=(b, s // ts),
        in_specs=[spec(r, w) for r in dilations] + [spec(r, LANES) for r in dilations],
        out_specs=pl.BlockSpec((1, ts, w), lambda i, j: (i, j, 0)),
        out_shape=jax.ShapeDtypeStruct((b, s, w), jnp.bfloat16),
        scratch_shapes=[pltpu.VMEM((ng, w // HEAD_DIM, ts, HEAD_DIM), jnp.float32),
                        pltpu.VMEM((ng, ts, LANES), jnp.float32)],
        compiler_params=_params(("parallel", "parallel")),
        name="dilated_merge",
    )(*outs, *lses)


ROW_TILE = 8


def _gather_rows(idx_ref, src_hbm, dst, sem):
    def body(i, carry):
        for j in range(ROW_TILE):
            pltpu.make_async_copy(src_hbm.at[pl.ds(idx_ref[0, 0, i * ROW_TILE + j], 1)],
                                  dst.at[i, pl.ds(j, 1)], sem).start()
        return carry
    lax.fori_loop(0, dst.shape[0], body, 0)


def _wait_rows(dst, sem):
    pltpu.make_async_copy(dst, dst, sem).wait()


def _expert_kernel(be_ref, nu_ref, tokc_ref, tokn_ref, h_hbm, wg_ref, wu_ref, wd_ref, o_ref, xbuf, sem):
    del be_ref
    blk = pl.program_id(0)
    n_used = nu_ref[0]
    slot = blk % 2

    @pl.when(blk == 0)
    def _():
        _gather_rows(tokc_ref, h_hbm, xbuf.at[0], sem.at[0])

    @pl.when(blk + 1 < n_used)
    def _():
        _gather_rows(tokn_ref, h_hbm, xbuf.at[1 - slot], sem.at[1 - slot])

    @pl.when(blk < n_used)
    def _():
        _wait_rows(xbuf.at[slot], sem.at[slot])
        x = xbuf[slot].reshape(o_ref.shape).astype(jnp.bfloat16)
        gate = jnp.dot(x, wg_ref[0], preferred_element_type=jnp.float32)
        up = jnp.dot(x, wu_ref[0], preferred_element_type=jnp.float32)
        hid = (gate * jax.nn.sigmoid(gate) * up).astype(jnp.bfloat16)
        o_ref[...] = jnp.dot(hid, wd_ref[0], preferred_element_type=jnp.float32)

    @pl.when(blk >= n_used)
    def _():
        o_ref[...] = jnp.zeros(o_ref.shape, o_ref.dtype)


def _expert_blocks(h2d, slot_tok, blk_e, n_used, w_gate, w_up, w_down):
    t, d = h2d.shape
    nblk = blk_e.shape[0]
    de = w_gate.shape[2]
    tok = slot_tok.reshape(nblk, 1, MOE_BLOCK)
    grid_spec = pltpu.PrefetchScalarGridSpec(
        num_scalar_prefetch=2,
        grid=(nblk,),
        in_specs=[pl.BlockSpec((1, 1, MOE_BLOCK), lambda i, be, nu: (i, 0, 0), memory_space=pltpu.SMEM),
                  pl.BlockSpec((1, 1, MOE_BLOCK), lambda i, be, nu: (jnp.minimum(i + 1, nblk - 1), 0, 0),
                               memory_space=pltpu.SMEM),
                  pl.BlockSpec(memory_space=pl.ANY),
                  pl.BlockSpec((1, d, de), lambda i, be, nu: (be[i], 0, 0)),
                  pl.BlockSpec((1, d, de), lambda i, be, nu: (be[i], 0, 0)),
                  pl.BlockSpec((1, de, d), lambda i, be, nu: (be[i], 0, 0))],
        out_specs=pl.BlockSpec((MOE_BLOCK, d), lambda i, be, nu: (i, 0)),
        scratch_shapes=[pltpu.VMEM((2, MOE_BLOCK // ROW_TILE, ROW_TILE, d), jnp.float32),
                        pltpu.SemaphoreType.DMA((2,))],
    )
    return pl.pallas_call(
        _expert_kernel,
        grid_spec=grid_spec,
        out_shape=jax.ShapeDtypeStruct((nblk * MOE_BLOCK, d), jnp.float32),
        compiler_params=_params(("arbitrary",)),
        name="moe_expert_blocks",
    )(blk_e, n_used, tok, tok, h2d, w_gate, w_up, w_down)


def _combine_kernel(d0c_ref, d1c_ref, d0n_ref, d1n_ref, x_ref, w0_ref, w1_ref, g_ref, gn_ref, yo_hbm, o_ref,
                    buf, sem, *, out_norm):
    i = pl.program_id(0)
    n = pl.num_programs(0)
    slot = i % 2
    tt = x_ref.shape[0]

    @pl.when(i == 0)
    def _():
        _gather_rows(d0c_ref, yo_hbm, buf.at[0, 0], sem.at[0])
        _gather_rows(d1c_ref, yo_hbm, buf.at[0, 1], sem.at[0])

    @pl.when(i + 1 < n)
    def _():
        _gather_rows(d0n_ref, yo_hbm, buf.at[1 - slot, 0], sem.at[1 - slot])
        _gather_rows(d1n_ref, yo_hbm, buf.at[1 - slot, 1], sem.at[1 - slot])

    _wait_rows(buf.at[slot, 0], sem.at[slot])
    _wait_rows(buf.at[slot, 1], sem.at[slot])
    y = (w0_ref[...][:, 0:1] * buf[slot, 0].reshape(x_ref.shape)
         + w1_ref[...][:, 0:1] * buf[slot, 1].reshape(x_ref.shape))
    out = x_ref[...] + g_ref[0] * y
    o_ref[...] = _rms(out, gn_ref[...]) if out_norm else out


def _moe_combine(x2d, yo, dest, wts, gate, s, out_norm_g=None):
    t, d = x2d.shape
    out_norm = out_norm_g is not None
    gn = (out_norm_g if out_norm else jnp.ones((d,), jnp.float32)).reshape(1, d)
    tt = min(256, s)
    nt = t // tt
    d0 = dest[:, 0].reshape(nt, 1, tt)
    d1 = dest[:, 1].reshape(nt, 1, tt)
    w0 = jnp.broadcast_to(wts[:, 0:1], (t, LANES))
    w1 = jnp.broadcast_to(wts[:, 1:2], (t, LANES))
    b = gate.shape[0]
    cur = pl.BlockSpec((1, 1, tt), lambda i: (i, 0, 0), memory_space=pltpu.SMEM)
    nxt = pl.BlockSpec((1, 1, tt), lambda i: (jnp.minimum(i + 1, nt - 1), 0, 0), memory_space=pltpu.SMEM)
    return pl.pallas_call(
        functools.partial(_combine_kernel, out_norm=out_norm),
        grid=(nt,),
        in_specs=[cur, cur, nxt, nxt,
                  pl.BlockSpec((tt, d), lambda i: (i, 0)),
                  pl.BlockSpec((tt, LANES), lambda i: (i, 0)),
                  pl.BlockSpec((tt, LANES), lambda i: (i, 0)),
                  pl.BlockSpec((1, 1, d), lambda i: (i * tt // s, 0, 0)),
                  pl.BlockSpec((1, d), lambda i: (0, 0)),
                  pl.BlockSpec(memory_space=pl.ANY)],
        out_specs=pl.BlockSpec((tt, d), lambda i: (i, 0)),
        out_shape=jax.ShapeDtypeStruct((t, d), jnp.float32),
        scratch_shapes=[pltpu.VMEM((2, 2, tt // ROW_TILE, ROW_TILE, d), jnp.float32),
                        pltpu.SemaphoreType.DMA((2,))],
        compiler_params=_params(("arbitrary",)),
        name="moe_combine_residual",
    )(d0, d1, d0, d1, x2d, w0, w1, gate.reshape(b, 1, d), gn, yo)


def _dispatch_tables(idx):
    t = idx.shape[0]
    a = t * TOP_K
    flat_e = idx.reshape(a)
    onehot = (flat_e[:, None] == jnp.arange(N_EXPERTS, dtype=jnp.int32)[None, :]).astype(jnp.int32)
    csum = jnp.cumsum(onehot, axis=0)
    rank = jnp.take_along_axis(csum, flat_e[:, None], axis=1)[:, 0] - 1
    counts = csum[-1]
    padded = (counts + MOE_BLOCK - 1) // MOE_BLOCK * MOE_BLOCK
    pad_end = jnp.cumsum(padded)
    pad_start = pad_end - padded
    dest = pad_start[flat_e] + rank
    nblk = -(-(a + N_EXPERTS * MOE_BLOCK) // MOE_BLOCK)
    cap = nblk * MOE_BLOCK
    slot_tok = jnp.zeros((cap,), jnp.int32).at[dest].set(jnp.arange(a, dtype=jnp.int32) // TOP_K,
                                                         unique_indices=True)
    blk_start = jnp.arange(nblk, dtype=jnp.int32) * MOE_BLOCK
    blk_e = jnp.minimum(jnp.sum((pad_end[None, :] <= blk_start[:, None]).astype(jnp.int32), axis=1), N_EXPERTS - 1)
    n_used = (pad_end[-1:] // MOE_BLOCK).astype(jnp.int32)
    return dest.reshape(t, TOP_K).astype(jnp.int32), slot_tok, blk_e.astype(jnp.int32), n_used


def _moe_layer(x, g, sc, sh, gate, router_w, router_b, layer, w_gate, w_up, w_down, out_norm_g=None):
    b, s, d = x.shape
    t = b * s
    h, idx, wts = _norm_route(x, g, sc, sh, router_w, router_b)
    idx = idx.transpose(0, 2, 1).reshape(t, TOP_K)
    wts = wts.transpose(0, 2, 1).reshape(t, TOP_K)
    dest, slot_tok, blk_e, n_used = _dispatch_tables(idx)
    yo = _expert_blocks(h.reshape(t, d), slot_tok, blk_e + layer * N_EXPERTS, n_used, w_gate, w_up, w_down)
    return _moe_combine(x.reshape(t, d), yo, dest, wts, gate, s, out_norm_g).reshape(b, s, d)


def _nsa_layer(h, x, gate, slopes, j, w_in, w_phi1, w_phi2, phi_pos, w_out):
    b, s, d = h.shape
    t = b * s
    h2d = h.reshape(t, d)
    proj = _matmul(h2d, w_in, j, 0, NSA_QKV, jnp.bfloat16, scale=HEAD_DIM ** -0.5 * LOG2E, scale_cols=ATT_WIDTH)
    proj = proj.reshape(b, s, NSA_QKV)
    wg = w_in[j, :, NSA_QKV:].reshape(d, NSA_KV_HEADS, NSA_GATES)
    wg = jnp.pad(wg, ((0, 0), (0, 0), (0, LANES - NSA_GATES))).reshape(1, d, NSA_KV_HEADS * LANES)
    gates = _matmul(h2d, wg, 0, 0, NSA_KV_HEADS * LANES, jnp.float32, act="sigmoid")
    gates = gates.reshape(b, s, NSA_KV_HEADS * LANES)
    nch = s // CMP_STRIDE
    kv = proj[:, :, ATT_WIDTH:ATT_WIDTH + 2 * NSA_KV_WIDTH]
    kv = kv.reshape(b, nch, CMP_STRIDE, 2, NSA_KV_HEADS, HEAD_DIM).transpose(0, 3, 4, 1, 2, 5)
    cmp_kv = _compress(kv.reshape(b, 2, NSA_KV_HEADS, nch, CMP_STRIDE * HEAD_DIM), w_phi1, w_phi2, phi_pos)
    o = _nsa_attention(proj, cmp_kv, gates, slopes * LOG2E)
    return _matmul_residual(o, w_out, j, x, gate)


def _dil_layer(x, g, sc, sh, gate, slopes, j, w_in, w_out):
    b, s, d = x.shape
    dilations = tuple(r for _, r in DIL_PAIRS)
    assert dilations[0] == 1
    hs = _norm_mod_streams(x, g, sc, sh, dilations)
    qs = _matmul_streams(hs[0].reshape(b, s, d), w_in, j, 0, ATT_WIDTH, dilations,
                         scale=HEAD_DIM ** -0.5 * LOG2E)
    outs, lses = [], []
    for gidx, (win, r) in enumerate(DIL_PAIRS):
        off = ATT_WIDTH * (1 + 2 * gidx)
        kv = _matmul(hs[gidx].reshape(b * s, d), w_in, j, off, 2 * ATT_WIDTH, jnp.bfloat16)
        o, lse = _dil_attention(qs[gidx], kv.reshape(b, r, s // r, 2 * ATT_WIDTH), slopes * LOG2E, win, r)
        outs.append(o)
        lses.append(lse)
    o = _dil_merge(outs, lses, dilations)
    return _matmul_residual(o, w_out, j, x, gate)


def kernel(x, c, ada_w, ada_b, norm_mix, norm_ffn, norm_final, nsa_w_in, nsa_w_phi1, nsa_w_phi2, nsa_phi_pos,
           nsa_w_out, dil_w_in, dil_w_out, router_w, router_b, exp_w_gate, exp_w_up, exp_w_down):
    depth = ada_w.shape[0]
    d = x.shape[-1]
    mod = _modulation(c, ada_w, ada_b)
    slopes = 2.0 ** (-ALIBI_MAX_BIAS * jnp.arange(1, N_HEADS + 1, dtype=jnp.float32) / N_HEADS)
    e_gate, e_up, e_down = [_cast_bf16(w).reshape((-1,) + w.shape[2:]) for w in (exp_w_gate, exp_w_up, exp_w_down)]
    for i in range(depth):
        sh_m, sc_m, g_m, sh_f, sc_f, g_f = [mod[i, :, k * d:(k + 1) * d] for k in range(6)]
        j = i // 2
        if i % 2 == 0:
            h = _norm_mod(x, norm_mix[i], sc_m, sh_m, jnp.bfloat16)
            x = _nsa_layer(h, x, g_m, slopes, j, nsa_w_in, nsa_w_phi1[j], nsa_w_phi2[j], nsa_phi_pos[j], nsa_w_out)
        else:
            x = _dil_layer(x, norm_mix[i], sc_m, sh_m, g_m, slopes, j, dil_w_in, dil_w_out)
        x = _moe_layer(x, norm_ffn[i], sc_f, sh_f, g_f, router_w, router_b, i, e_gate, e_up, e_down,
                       norm_final if i == depth - 1 else None)
    return x
```

```python
import functools

import jax
import jax.numpy as jnp
import numpy as np
from jax import lax
from jax.experimental import pallas as pl
from jax.experimental.pallas import tpu as pltpu

HEAD_DIM = 128
N_HEADS = 16
ATT_WIDTH = N_HEADS * HEAD_DIM
ALIBI_MAX_BIAS = 8.0

NSA_KV_HEADS = 4
NSA_GROUP = N_HEADS // NSA_KV_HEADS
NSA_KV_WIDTH = NSA_KV_HEADS * HEAD_DIM
CMP_BLOCK = 32
CMP_STRIDE = 16
SEL_BLOCK = 64
N_SEL = 16
NSA_WINDOW = 512
FORCE_SCORE = 1.0e6
NSA_QKV = ATT_WIDTH + 6 * NSA_KV_WIDTH
NSA_GATES = 3 * NSA_GROUP

DIL_PAIRS = ((128, 1), (512, 4), (2048, 16))
DIL_PROJ = ATT_WIDTH * (1 + 2 * len(DIL_PAIRS))

N_EXPERTS = 16
N_EXPERT_GROUPS = 4
EXPERTS_PER_GROUP = N_EXPERTS // N_EXPERT_GROUPS
TOP_K = 2
MOE_BLOCK = 256

RMS_EPS = 1e-6
NEG_INF = -1.0e30

LANES = 128
ATT_TILE = 128
VMEM_LIMIT = 56 * 1024 * 1024

_HI = lax.Precision.HIGHEST
_NT = (((1,), (1,)), ((), ()))


def _params(sem, vmem=VMEM_LIMIT):
    return pltpu.CompilerParams(dimension_semantics=sem, vmem_limit_bytes=vmem)


def _mod_kernel(c_ref, w_ref, b_ref, o_ref):
    c = c_ref[...]
    cond = c * jax.nn.sigmoid(c)
    o_ref[0] = jnp.dot(cond, w_ref[0], precision=_HI, preferred_element_type=jnp.float32) + b_ref[0]


def _modulation(c, ada_w, ada_b):
    depth, d, n = ada_w.shape
    b = c.shape[0]
    tn = 512
    return pl.pallas_call(
        _mod_kernel,
        grid=(depth, n // tn),
        in_specs=[pl.BlockSpec((b, d), lambda i, j: (0, 0)),
                  pl.BlockSpec((1, d, tn), lambda i, j: (i, 0, j)),
                  pl.BlockSpec((1, 1, tn), lambda i, j: (i, 0, j))],
        out_specs=pl.BlockSpec((1, b, tn), lambda i, j: (i, 0, j)),
        out_shape=jax.ShapeDtypeStruct((depth, b, n), jnp.float32),
        compiler_params=_params(("parallel", "parallel")),
        name="adaln_modulation",
    )(c, ada_w, ada_b.reshape(depth, 1, n))


def _rms(x, g):
    return x * lax.rsqrt(jnp.mean(x * x, axis=-1, keepdims=True) + RMS_EPS) * g


def _norm_mod_kernel(x_ref, g_ref, sc_ref, sh_ref, o_ref):
    h = _rms(x_ref[0], g_ref[...]) * (1.0 + sc_ref[0]) + sh_ref[0]
    o_ref[0] = h.astype(o_ref.dtype)


def _norm_mod(x, g, sc, sh, out_dtype):
    b, s, d = x.shape
    ts = min(256, s)
    return pl.pallas_call(
        _norm_mod_kernel,
        grid=(b, s // ts),
        in_specs=[pl.BlockSpec((1, ts, d), lambda i, j: (i, j, 0)),
                  pl.BlockSpec((1, d), lambda i, j: (0, 0)),
                  pl.BlockSpec((1, 1, d), lambda i, j: (i, 0, 0)),
                  pl.BlockSpec((1, 1, d), lambda i, j: (i, 0, 0))],
        out_specs=pl.BlockSpec((1, ts, d), lambda i, j: (i, j, 0)),
        out_shape=jax.ShapeDtypeStruct((b, s, d), out_dtype),
        compiler_params=_params(("parallel", "parallel")),
        name="rmsnorm_adaln",
    )(x, g.reshape(1, d), sc.reshape(b, 1, d), sh.reshape(b, 1, d))


def _store_streams(val, src_sc, out_refs, dilations):
    n = val.shape[0]
    for c in range(src_sc.shape[0]):
        src_sc[c] = val[:, c * LANES:(c + 1) * LANES]
    for r, ref in zip(dilations, out_refs):
        if r == 1:
            ref[0, 0] = val.astype(ref.dtype)
            continue
        for rho in range(r):
            for c in range(src_sc.shape[0]):
                ref[0, rho, :, c * LANES:(c + 1) * LANES] = (
                    src_sc[c, pl.ds(rho, n // r, stride=r), :].astype(ref.dtype))


def _norm_mod_streams_kernel(x_ref, g_ref, sc_ref, sh_ref, *refs, dilations):
    out_refs, h_sc = refs[:-1], refs[-1]
    h = _rms(x_ref[0], g_ref[...]) * (1.0 + sc_ref[0]) + sh_ref[0]
    _store_streams(h, h_sc, out_refs, dilations)


def _norm_mod_streams(x, g, sc, sh, dilations):
    b, s, d = x.shape
    ts = min(256, s)
    return pl.pallas_call(
        functools.partial(_norm_mod_streams_kernel, dilations=dilations),
        grid=(b, s // ts),
        in_specs=[pl.BlockSpec((1, ts, d), lambda i, j: (i, j, 0)),
                  pl.BlockSpec((1, d), lambda i, j: (0, 0)),
                  pl.BlockSpec((1, 1, d), lambda i, j: (i, 0, 0)),
                  pl.BlockSpec((1, 1, d), lambda i, j: (i, 0, 0))],
        out_specs=[pl.BlockSpec((1, r, ts // r, d), lambda i, j: (i, 0, j, 0)) for r in dilations],
        out_shape=[jax.ShapeDtypeStruct((b, r, s // r, d), jnp.bfloat16) for r in dilations],
        scratch_shapes=[pltpu.VMEM((d // LANES, ts, LANES), jnp.float32)],
        compiler_params=_params(("parallel", "parallel")),
        name="rmsnorm_adaln_streams",
    )(x, g.reshape(1, d), sc.reshape(b, 1, d), sh.reshape(b, 1, d))


def _pair_max(vals):
    out = None
    for a in range(len(vals)):
        for b in range(a + 1, len(vals)):
            s = vals[a] + vals[b]
            out = s if out is None else jnp.maximum(out, s)
    return out


def _store_row_major(ref, val):
    rows = val.shape[0]
    c_tiles = val.shape[1] // LANES
    for c in range(c_tiles):
        ref[pl.ds(c, rows, stride=c_tiles), :] = val[:, c * LANES:(c + 1) * LANES]


def _load_row_tiles(buf):
    n, c_tiles = buf.shape[0], buf.shape[1]
    return jnp.concatenate([buf[:, c].reshape(n * ROW_TILE, LANES) for c in range(c_tiles)], axis=1)


def _norm_route_kernel(x_ref, g_ref, sc_ref, sh_ref, rw_ref, rb_ref, h_ref, idx_ref, wt_ref):
    h = _rms(x_ref[0], g_ref[...]) * (1.0 + sc_ref[0]) + sh_ref[0]
    _store_row_major(h_ref, h)
    logits = lax.dot_general(rw_ref[...], h, _NT, precision=_HI, preferred_element_type=jnp.float32)
    scores = jax.nn.sigmoid(logits)
    biased = scores + rb_ref[...][:, 0:1]
    rows = [biased[e:e + 1, :] for e in range(N_EXPERTS)]
    srow = [scores[e:e + 1, :] for e in range(N_EXPERTS)]
    grp = [_pair_max(rows[q * EXPERTS_PER_GROUP:(q + 1) * EXPERTS_PER_GROUP]) for q in range(N_EXPERT_GROUPS)]
    best_v = grp[0]
    best = jnp.zeros(best_v.shape, jnp.int32)
    for q in range(1, N_EXPERT_GROUPS):
        take = grp[q] > best_v
        best = jnp.where(take, q, best)
        best_v = jnp.where(take, grp[q], best_v)
    v1 = jnp.full(best_v.shape, NEG_INF, jnp.float32)
    i1 = jnp.zeros(best_v.shape, jnp.int32)
    for e in range(N_EXPERTS):
        cand = jnp.where(best == e // EXPERTS_PER_GROUP, rows[e], NEG_INF)
        take = cand > v1
        i1 = jnp.where(take, e, i1)
        v1 = jnp.where(take, cand, v1)
    v2 = jnp.full(best_v.shape, NEG_INF, jnp.float32)
    i2 = jnp.zeros(best_v.shape, jnp.int32)
    for e in range(N_EXPERTS):
        cand = jnp.where(best == e // EXPERTS_PER_GROUP, jnp.where(i1 == e, NEG_INF, rows[e]), NEG_INF)
        take = cand > v2
        i2 = jnp.where(take, e, i2)
        v2 = jnp.where(take, cand, v2)
    w1 = jnp.zeros(best_v.shape, jnp.float32)
    w2 = jnp.zeros(best_v.shape, jnp.float32)
    for e in range(N_EXPERTS):
        w1 = jnp.where(i1 == e, srow[e], w1)
        w2 = jnp.where(i2 == e, srow[e], w2)
    tot = w1 + w2
    idx_ref[0] = jnp.concatenate([i1, i2], axis=0)
    wt_ref[0] = jnp.concatenate([w1 / tot, w2 / tot], axis=0)


def _norm_route(x, g, sc, sh, router_w, router_b):
    b, s, d = x.shape
    ts = min(256, s)
    ct = d // LANES
    rwt = router_w.T
    rb = jnp.broadcast_to(router_b.reshape(N_EXPERTS, 1), (N_EXPERTS, LANES))
    h, idx, wts = pl.pallas_call(
        _norm_route_kernel,
        grid=(b, s // ts),
        in_specs=[pl.BlockSpec((1, ts, d), lambda i, j: (i, j, 0)),
                  pl.BlockSpec((1, d), lambda i, j: (0, 0)),
                  pl.BlockSpec((1, 1, d), lambda i, j: (i, 0, 0)),
                  pl.BlockSpec((1, 1, d), lambda i, j: (i, 0, 0)),
                  pl.BlockSpec((N_EXPERTS, d), lambda i, j: (0, 0)),
                  pl.BlockSpec((N_EXPERTS, LANES), lambda i, j: (0, 0))],
        out_specs=[pl.BlockSpec((ts * ct, LANES), lambda i, j: (i * (s // ts) + j, 0)),
                   pl.BlockSpec((1, TOP_K, ts), lambda i, j: (i, 0, j)),
                   pl.BlockSpec((1, TOP_K, ts), lambda i, j: (i, 0, j))],
        out_shape=[jax.ShapeDtypeStruct((b * s * ct, LANES), jnp.float32),
                   jax.ShapeDtypeStruct((b, TOP_K, s), jnp.int32),
                   jax.ShapeDtypeStruct((b, TOP_K, s), jnp.float32)],
        compiler_params=_params(("parallel", "parallel")),
        name="rmsnorm_adaln_router",
    )(x, g.reshape(1, d), sc.reshape(b, 1, d), sh.reshape(b, 1, d), rwt, rb)
    return h.reshape(b * s, ct, LANES), idx, wts


def _weight_tile(w_ref, wbf_sc, first, scale, scale_tiles):
    @pl.when(first)
    def _():
        w = w_ref[0]
        if scale_tiles:
            w = w * jnp.where(pl.program_id(0) < scale_tiles, scale, 1.0)
        wbf_sc[...] = w.astype(jnp.bfloat16)
    return wbf_sc[...]


def _weight_spec(layer, col0, k, tn, rank):
    assert col0 % tn == 0
    if rank == 2:
        return pl.BlockSpec((1, k, tn), lambda j, i: (layer, 0, col0 // tn + j))
    return pl.BlockSpec((1, k, tn), lambda j, bi, i: (layer, 0, col0 // tn + j))


def _mm_kernel(a_ref, w_ref, o_ref, wbf_sc, *, act, scale, scale_tiles):
    w = _weight_tile(w_ref, wbf_sc, pl.program_id(1) == 0, scale, scale_tiles)
    acc = jnp.dot(a_ref[...], w, preferred_element_type=jnp.float32)
    if act == "sigmoid":
        acc = jax.nn.sigmoid(acc)
    o_ref[...] = acc.astype(o_ref.dtype)


def _matmul(a, w, layer, col0, n, out_dtype, act=None, scale=1.0, scale_cols=0, tm=2048, tn=1024):
    m, k = a.shape
    tn = min(tn, n)
    tm = min(tm, m)
    assert scale_cols % tn == 0
    return pl.pallas_call(
        functools.partial(_mm_kernel, act=act, scale=scale, scale_tiles=scale_cols // tn),
        grid=(n // tn, m // tm),
        in_specs=[pl.BlockSpec((tm, k), lambda j, i: (i, 0)),
                  _weight_spec(layer, col0, k, tn, 2)],
        out_specs=pl.BlockSpec((tm, tn), lambda j, i: (i, j)),
        out_shape=jax.ShapeDtypeStruct((m, n), out_dtype),
        scratch_shapes=[pltpu.VMEM((k, tn), jnp.bfloat16)],
        compiler_params=_params(("arbitrary", "arbitrary")),
        name="matmul" if act is None else "matmul_" + act,
    )(a, w)


def _first_row_step():
    return (pl.program_id(1) == 0) & (pl.program_id(2) == 0)


def _mm_streams_kernel(a_ref, w_ref, *refs, dilations, scale, scale_tiles):
    out_refs, acc_sc, wbf_sc = refs[:-2], refs[-2], refs[-1]
    w = _weight_tile(w_ref, wbf_sc, _first_row_step(), scale, scale_tiles)
    acc = jnp.dot(a_ref[0], w, preferred_element_type=jnp.float32)
    _store_streams(acc, acc_sc, out_refs, dilations)


def _matmul_streams(a, w, layer, col0, n, dilations, scale=1.0, tm=512, tn=1024):
    b, s, k = a.shape
    tn = min(tn, n)
    tm = min(tm, s)
    return pl.pallas_call(
        functools.partial(_mm_streams_kernel, dilations=dilations, scale=scale,
                          scale_tiles=n // tn if scale != 1.0 else 0),
        grid=(n // tn, b, s // tm),
        in_specs=[pl.BlockSpec((1, tm, k), lambda j, bi, i: (bi, i, 0)),
                  _weight_spec(layer, col0, k, tn, 3)],
        out_specs=[pl.BlockSpec((1, r, tm // r, tn), lambda j, bi, i: (bi, 0, i, j)) for r in dilations],
        out_shape=[jax.ShapeDtypeStruct((b, r, s // r, n), jnp.bfloat16) for r in dilations],
        scratch_shapes=[pltpu.VMEM((tn // LANES, tm, LANES), jnp.float32), pltpu.VMEM((k, tn), jnp.bfloat16)],
        compiler_params=_params(("arbitrary", "arbitrary", "arbitrary")),
        name="matmul_streams",
    )(a, w)


def _mm_residual_kernel(a_ref, w_ref, x_ref, g_ref, o_ref, wbf_sc):
    w = _weight_tile(w_ref, wbf_sc, _first_row_step(), 1.0, 0)
    acc = jnp.dot(a_ref[0], w, preferred_element_type=jnp.float32)
    o_ref[0] = x_ref[0] + g_ref[0] * acc


def _matmul_residual(a, w, layer, x, gate, tm=512, tn=1024):
    b, s, k = a.shape
    n = w.shape[2]
    tn = min(tn, n)
    tm = min(tm, s)
    return pl.pallas_call(
        _mm_residual_kernel,
        grid=(n // tn, b, s // tm),
        in_specs=[pl.BlockSpec((1, tm, k), lambda j, bi, i: (bi, i, 0)),
                  _weight_spec(layer, 0, k, tn, 3),
                  pl.BlockSpec((1, tm, tn), lambda j, bi, i: (bi, i, j)),
                  pl.BlockSpec((1, 1, tn), lambda j, bi, i: (bi, 0, j))],
        out_specs=pl.BlockSpec((1, tm, tn), lambda j, bi, i: (bi, i, j)),
        out_shape=jax.ShapeDtypeStruct((b, s, n), jnp.float32),
        scratch_shapes=[pltpu.VMEM((k, tn), jnp.bfloat16)],
        compiler_params=_params(("arbitrary", "arbitrary", "arbitrary")),
        name="matmul_gated_residual",
    )(a, w, x, gate.reshape(b, 1, n))


def _cast_kernel(w_ref, o_ref):
    o_ref[...] = w_ref[...].astype(o_ref.dtype)


def _cast_bf16(w, rows=1024):
    c = w.shape[-1]
    w2 = w.reshape(-1, c)
    r = w2.shape[0]
    out = pl.pallas_call(
        _cast_kernel,
        grid=(r // rows,),
        in_specs=[pl.BlockSpec((rows, c), lambda i: (i, 0))],
        out_specs=pl.BlockSpec((rows, c), lambda i: (i, 0)),
        out_shape=jax.ShapeDtypeStruct((r, c), jnp.bfloat16),
        compiler_params=_params(("parallel",)),
        name="cast_bf16",
    )(w2)
    return out.reshape(w.shape)


def _gelu_tanh(x):
    return 0.5 * x * (1.0 + jnp.tanh(0.7978845608028654 * (x + 0.044715 * (x * x * x))))


def _compress_kernel(c_ref, pa_ref, pb_ref, wa_ref, wb_ref, w2_ref, o_ref):
    c = c_ref[0, 0, 0].astype(jnp.float32)
    lo = (c + pa_ref[0]).astype(jnp.bfloat16)
    hi = (c + pb_ref[0]).astype(jnp.bfloat16)
    ha = jnp.dot(lo, wa_ref[0], preferred_element_type=jnp.float32)
    hb = jnp.dot(hi, wb_ref[0], preferred_element_type=jnp.float32)
    n = ha.shape[0]
    hid = _gelu_tanh(ha + pltpu.roll(hb, n - 1, 0))
    out = jnp.dot(hid.astype(jnp.bfloat16), w2_ref[0], preferred_element_type=jnp.float32)
    row = lax.broadcasted_iota(jnp.int32, out.shape, 0)
    o_ref[0, 0, 0] = jnp.where(row < n - 1, out, 0.0).astype(o_ref.dtype)


def _compress(kv_chunks, w_phi1, w_phi2, phi_pos):
    b, two, hkv, nch, width = kv_chunks.shape
    dh = HEAD_DIM
    half = CMP_BLOCK // 2
    w1 = w_phi1.reshape(2, 2, half * dh, dh).astype(jnp.bfloat16)
    pos = phi_pos.reshape(2, 2, 1, half * dh)
    return pl.pallas_call(
        _compress_kernel,
        grid=(b, two, hkv),
        in_specs=[pl.BlockSpec((1, 1, 1, nch, width), lambda i, j, g: (i, j, g, 0, 0)),
                  pl.BlockSpec((1, 1, width), lambda i, j, g: (j, 0, 0)),
                  pl.BlockSpec((1, 1, width), lambda i, j, g: (j, 0, 0)),
                  pl.BlockSpec((1, width, dh), lambda i, j, g: (j, 0, 0)),
                  pl.BlockSpec((1, width, dh), lambda i, j, g: (j, 0, 0)),
                  pl.BlockSpec((1, dh, dh), lambda i, j, g: (j, 0, 0))],
        out_specs=pl.BlockSpec((1, 1, 1, nch, dh), lambda i, j, g: (i, j, g, 0, 0)),
        out_shape=jax.ShapeDtypeStruct((b, two, hkv, nch, dh), jnp.bfloat16),
        compiler_params=_params(("parallel", "parallel", "parallel")),
        name="nsa_compress",
    )(kv_chunks, pos[:, 0], pos[:, 1], w1[:, 0], w1[:, 1], w_phi2.astype(jnp.bfloat16))


MASK_BIG = 2.0 ** 100
AUX_SLOPE = 64
AUX_PAD = 70
SEL_CHUNK = 512
LOG2E = 1.4426950408889634


def _nsa_attn_kernel(sl_ref, q_ref, kc_ref, vc_ref, ks_ref, vs_ref, kw_ref, vw_ref, gt_ref, ovl_ref, qtab_ref,
                     kauxs_ref, kauxw_ref, o_ref, ocmp_sc, maskq_sc, owin_sc, qaug_sc, qaugw_sc, p_sc, pw_sc,
                     kaug_s, vaug_s, kaug_w, vaug_w, *, n_cmp, n_blk, n_sel):
    t = ATT_TILE
    dh = HEAD_DIM
    s_len = ks_ref.shape[1]
    wpad = NSA_WINDOW
    g = pl.program_id(1)
    qt = pl.program_id(2)
    qstart = pl.multiple_of(qt * t, t)

    @pl.when(qt == 0)
    def _():
        ones = jnp.ones((s_len, dh), jnp.bfloat16)
        kaug_s[:, :dh] = ks_ref[0]
        kaug_s[:, dh:] = kauxs_ref[...]
        vaug_s[:, :dh] = vs_ref[0]
        vaug_s[:, dh:] = ones
        kaug_w[:wpad, :dh] = jnp.zeros((wpad, dh), jnp.bfloat16)
        kaug_w[wpad:, :dh] = kw_ref[0]
        kaug_w[:, dh:] = kauxw_ref[...]
        vaug_w[:wpad, :dh] = jnp.zeros((wpad, dh), jnp.bfloat16)
        vaug_w[wpad:, :dh] = vw_ref[0]
        vaug_w[:wpad, dh:] = jnp.ones((wpad, dh), jnp.bfloat16)
        vaug_w[wpad:, dh:] = ones

    slopes = [sl_ref[g * NSA_GROUP + r] for r in range(NSA_GROUP)]
    row = lax.broadcasted_iota(jnp.int32, (t, LANES), 0)
    col = lax.broadcasted_iota(jnp.int32, (t, LANES), 1)
    prows = [slice(pair * 2 * t, (pair + 1) * 2 * t) for pair in range(NSA_GROUP // 2)]
    qtab = qtab_ref[0]

    def compress_and_select(qv, tile, par):
        qs = tile * t
        tpos = qs + row
        q4 = jnp.concatenate([qv[:, r * dh:(r + 1) * dh] for r in range(NSA_GROUP)], axis=0)
        kc = kc_ref[0, 0, 0]
        vc = vc_ref[0, 0, 0]
        s = lax.dot_general(q4, kc, _NT, preferred_element_type=jnp.float32)
        visible = (col * CMP_STRIDE + (CMP_BLOCK - 1) <= tpos) & (col < n_cmp)
        dist_c = tpos.astype(jnp.float32) - (col.astype(jnp.float32) * CMP_STRIDE + (CMP_BLOCK - 1) / 2.0)
        psum = jnp.zeros((t, LANES), jnp.float32)
        for r in range(NSA_GROUP):
            rows = slice(r * t, (r + 1) * t)
            sr = jnp.where(visible, s[rows] - slopes[r] * dist_c, NEG_INF)
            e = jnp.where(visible, jnp.exp2(sr - jnp.max(sr, axis=-1, keepdims=True)), 0.0)
            p = e / jnp.maximum(jnp.sum(e, axis=-1, keepdims=True), 1e-30)
            psum = psum + p
            ocmp_sc[par, rows] = jnp.dot(p.astype(vc.dtype), vc, preferred_element_type=jnp.float32)
        nb = -(-n_blk // 8) * 8
        imp_t = lax.dot_general(ovl_ref[...], psum, _NT, precision=_HI, preferred_element_type=jnp.float32)[:nb]
        jrow = lax.broadcasted_iota(jnp.int32, (nb, t), 0)
        tpos_t = qs + lax.broadcasted_iota(jnp.int32, (nb, t), 1)
        cur = jnp.right_shift(tpos_t, int(np.log2(SEL_BLOCK)))
        forced = (jrow == 0) | (jrow == cur) | (jrow == cur - 1)
        score = jnp.where(forced, FORCE_SCORE, jnp.where(jrow * SEL_BLOCK <= tpos_t, imp_t, -1.0))
        rank = jnp.zeros((nb, t), jnp.float32)
        for j in range(n_blk):
            cj = score[j:j + 1, :]
            ahead = (cj > score) | ((cj == score) & (jrow > j))
            rank = rank + jnp.where(ahead, 1.0, 0.0)
        mask_t = jnp.where((rank < n_sel) & (jrow < n_blk), 0.0, -MASK_BIG)
        maskq_sc[par] = jnp.concatenate([mask_t, jnp.zeros((LANES - nb, t), jnp.float32)], axis=0).T

    par = 0
    q = q_ref[0]

    for r in range(NSA_GROUP):
        rows = slice(r * t, (r + 1) * t)
        qaugw_sc[rows, :dh] = q[:, r * dh:(r + 1) * dh]
        qaugw_sc[rows, dh:] = jnp.broadcast_to(qtab[r:r + 1, :], (t, LANES)).astype(jnp.bfloat16)
    wk = wpad + t
    nw = wk // LANES
    kw = kaug_w[pl.ds(qstart, wk), :]
    vw = vaug_w[pl.ds(qstart, wk), :]
    sws = [lax.dot_general(qaugw_sc[prow], kw, _NT, preferred_element_type=jnp.float32) for prow in prows]

    compress_and_select(q, qt, par)

    mask_q = maskq_sc[par]
    for r in range(NSA_GROUP):
        rows = slice(r * t, (r + 1) * t)
        qaug_sc[rows, :dh] = q[:, r * dh:(r + 1) * dh]
        qaug_sc[rows, dh:] = (mask_q + qtab[r:r + 1, :]).astype(jnp.bfloat16)

    for pair, prow in enumerate(prows):
        sw = sws[pair]
        for r2 in range(2):
            rows = slice((2 * pair + r2) * t, (2 * pair + r2 + 1) * t)
            tiles = [sw[r2 * t:(r2 + 1) * t, j * LANES:(j + 1) * LANES] for j in range(nw)]
            tiles[0] = jnp.where(col > row, tiles[0], NEG_INF)
            tiles[-1] = jnp.where(col <= row, tiles[-1], NEG_INF)
            mx = functools.reduce(jnp.maximum, tiles)
            m = jnp.broadcast_to(jnp.max(mx, axis=-1, keepdims=True), (t, LANES))
            for j in range(nw):
                pw_sc[rows, j * LANES:(j + 1) * LANES] = jnp.exp2(tiles[j] - m).astype(jnp.bfloat16)
        ow = jnp.dot(pw_sc[prow], vw, preferred_element_type=jnp.float32)
        owin_sc[prow] = ow[:, :dh] / ow[:, dh:]

    ch = SEL_CHUNK
    nl = ch // LANES
    n_full = qt // (ch // t)
    gt = gt_ref[0]
    cmr = col - row

    def selected(k):
        nk = (k + 1) * ch
        kk = kaug_s[:nk, :]
        vv = vaug_s[:nk, :]
        sks = [lax.dot_general(qaug_sc[prow], kk, _NT, preferred_element_type=jnp.float32) for prow in prows]
        for pair, prow in enumerate(prows):
            sk = sks[pair]
            for r2 in range(2):
                rows = slice((2 * pair + r2) * t, (2 * pair + r2 + 1) * t)
                tiles = [sk[r2 * t:(r2 + 1) * t, j * LANES:(j + 1) * LANES] for j in range(nk // LANES)]
                for j in range(k * nl, (k + 1) * nl):
                    tiles[j] = jnp.where(cmr <= qstart - j * LANES, tiles[j], NEG_INF)
                mx = functools.reduce(jnp.maximum, tiles)
                m = jnp.broadcast_to(jnp.max(mx, axis=-1, keepdims=True), (t, LANES))
                for j in range(nk // LANES):
                    p_sc[rows, j * LANES:(j + 1) * LANES] = jnp.exp2(tiles[j] - m).astype(jnp.bfloat16)
            pv = jnp.dot(p_sc[prow, :nk], vv, preferred_element_type=jnp.float32)
            for r2 in range(2):
                r = 2 * pair + r2
                rows = slice(r * t, (r + 1) * t)
                orow = slice(r2 * t, (r2 + 1) * t)
                o = (gt[:, 3 * r:3 * r + 1] * ocmp_sc[par, rows]
                     + gt[:, 3 * r + 1:3 * r + 2] * (pv[orow, :dh] / pv[orow, dh:])
                     + gt[:, 3 * r + 2:3 * r + 3] * owin_sc[rows])
                o_ref[0, :, r * dh:(r + 1) * dh] = o.astype(o_ref.dtype)

    for k in range(s_len // ch):
        pl.when(n_full == k)(functools.partial(selected, k))


def _bf16_pieces(x):
    a0 = x.astype(jnp.bfloat16).astype(jnp.float32)
    a1 = (x - a0).astype(jnp.bfloat16).astype(jnp.float32)
    a2 = (x - a0 - a1).astype(jnp.bfloat16).astype(jnp.float32)
    return [a0, a1, a2]


def _nsa_tables(s, slopes2):
    n_cmp = s // CMP_STRIDE - CMP_BLOCK // CMP_STRIDE + 1
    n_blk = s // SEL_BLOCK
    assert n_cmp < LANES and n_blk <= AUX_SLOPE
    bj = np.arange(LANES)[:, None]
    cn = np.arange(LANES)[None, :]
    ovl_t = np.clip(np.minimum(cn * CMP_STRIDE + CMP_BLOCK, (bj + 1) * SEL_BLOCK)
                    - np.maximum(cn * CMP_STRIDE, bj * SEL_BLOCK), 0, None) / CMP_STRIDE
    ovl_t = np.where((cn < n_cmp) & (bj < n_blk), ovl_t, 0.0).astype(np.float32)
    pos = np.arange(s)
    kaux = np.zeros((s, LANES), np.float32)
    kaux[pos, pos // SEL_BLOCK] = 1.0
    kaux[:, AUX_SLOPE:AUX_SLOPE + 3] = (pos // LANES * LANES)[:, None]
    kaux[:, AUX_SLOPE + 3:AUX_SLOPE + 6] = (pos % LANES)[:, None]
    kaux_w = np.zeros((NSA_WINDOW + s, LANES), np.float32)
    kaux_w[NSA_WINDOW:, AUX_SLOPE:AUX_SLOPE + 6] = kaux[:, AUX_SLOPE:AUX_SLOPE + 6]
    kaux_w[:NSA_WINDOW, AUX_PAD] = 1.0
    pieces = jnp.stack(_bf16_pieces(slopes2) * 2, axis=-1)
    qtab = jnp.zeros((N_HEADS, LANES), jnp.float32)
    qtab = qtab.at[:, AUX_SLOPE:AUX_SLOPE + 6].set(pieces).at[:, AUX_PAD].set(-MASK_BIG)
    qtab = jnp.pad(qtab.reshape(NSA_KV_HEADS, NSA_GROUP, LANES), ((0, 0), (0, 8 - NSA_GROUP), (0, 0)))
    return (n_cmp, n_blk, jnp.asarray(ovl_t), qtab, jnp.asarray(kaux, dtype=jnp.bfloat16),
            jnp.asarray(kaux_w, dtype=jnp.bfloat16))


def _nsa_attention(proj, cmp_kv, gates, slopes2):
    b, s, _ = proj.shape
    t = ATT_TILE
    n_cmp, n_blk, ovl_t, qtab, kaux_s, kaux_w = _nsa_tables(s, slopes2)
    assert cmp_kv.shape[3] == LANES and s % SEL_CHUNK == 0 and NSA_WINDOW % t == 0
    qw = NSA_GROUP * HEAD_DIM
    rows = NSA_GROUP * t
    kv0 = ATT_WIDTH // HEAD_DIM

    def kv_spec(j):
        return pl.BlockSpec((1, s, HEAD_DIM), lambda bi, g, qi, j=j: (bi, 0, kv0 + j * NSA_KV_HEADS + g))

    def const(shape):
        return pl.BlockSpec(shape, lambda bi, g, qi: (0,) * len(shape))

    return pl.pallas_call(
        functools.partial(_nsa_attn_kernel, n_cmp=n_cmp, n_blk=n_blk, n_sel=min(N_SEL, n_blk)),
        grid=(b, NSA_KV_HEADS, s // t),
        in_specs=[pl.BlockSpec(memory_space=pltpu.SMEM),
                  pl.BlockSpec((1, t, qw), lambda bi, g, qi: (bi, qi, g)),
                  pl.BlockSpec((1, 1, 1, LANES, HEAD_DIM), lambda bi, g, qi: (bi, 0, g, 0, 0)),
                  pl.BlockSpec((1, 1, 1, LANES, HEAD_DIM), lambda bi, g, qi: (bi, 1, g, 0, 0)),
                  kv_spec(2), kv_spec(3), kv_spec(4), kv_spec(5),
                  pl.BlockSpec((1, t, LANES), lambda bi, g, qi: (bi, qi, g)),
                  const((LANES, LANES)),
                  pl.BlockSpec((1, 8, LANES), lambda bi, g, qi: (g, 0, 0)),
                  const((s, LANES)), const((NSA_WINDOW + s, LANES))],
        out_specs=pl.BlockSpec((1, t, qw), lambda bi, g, qi: (bi, qi, g)),
        out_shape=jax.ShapeDtypeStruct((b, s, ATT_WIDTH), jnp.bfloat16),
        scratch_shapes=[pltpu.VMEM((1, rows, HEAD_DIM), jnp.float32),
                        pltpu.VMEM((1, t, LANES), jnp.float32),
                        pltpu.VMEM((rows, HEAD_DIM), jnp.float32),
                        pltpu.VMEM((rows, 2 * HEAD_DIM), jnp.bfloat16),
                        pltpu.VMEM((rows, 2 * HEAD_DIM), jnp.bfloat16),
                        pltpu.VMEM((rows, s), jnp.bfloat16),
                        pltpu.VMEM((rows, NSA_WINDOW + t), jnp.bfloat16),
                        pltpu.VMEM((s, 2 * HEAD_DIM), jnp.bfloat16),
                        pltpu.VMEM((s, 2 * HEAD_DIM), jnp.bfloat16),
                        pltpu.VMEM((NSA_WINDOW + s, 2 * HEAD_DIM), jnp.bfloat16),
                        pltpu.VMEM((NSA_WINDOW + s, 2 * HEAD_DIM), jnp.bfloat16)],
        compiler_params=_params(("arbitrary", "arbitrary", "arbitrary")),
        name="nsa_attention",
    )(slopes2, proj, cmp_kv, cmp_kv, proj, proj, proj, proj, gates, ovl_t, qtab, kaux_s, kaux_w)


def _dil_attn_kernel(sl_ref, q_ref, *refs, window, has_prev):
    if has_prev:
        kp_ref, kc_ref, vp_ref, vc_ref, o_ref, lse_ref, bias_sc = refs
    else:
        kc_ref, vc_ref, o_ref, lse_ref, bias_sc = refs
    t = ATT_TILE
    i = pl.program_id(2)

    @pl.when((pl.program_id(0) == 0) & (pl.program_id(1) == 0) & (i == 0))
    def _():
        row = lax.broadcasted_iota(jnp.int32, (t, 2 * t), 0)
        col = lax.broadcasted_iota(jnp.int32, (t, 2 * t), 1)
        dist = row - col + t
        band = (dist >= 0) & (dist <= window)
        distf = dist.astype(jnp.float32)
        for h in range(N_HEADS):
            bias = sl_ref[h] * distf
            bias_sc[0, h] = jnp.where(band & (col >= t), bias, -NEG_INF)
            bias_sc[1, h] = jnp.where(band, bias, -NEG_INF)

    var = jnp.minimum(i, 1)
    lane = lax.broadcasted_iota(jnp.int32, (t, LANES), 1)
    lse_all = jnp.zeros((t, LANES), jnp.float32)
    for h in range(N_HEADS):
        hs = slice(h * HEAD_DIM, (h + 1) * HEAD_DIM)
        q = q_ref[0, 0, :, hs]
        if has_prev:
            k = jnp.concatenate([kp_ref[0, 0, :, hs], kc_ref[0, 0, :, hs]], axis=0)
            v = jnp.concatenate([vp_ref[0, 0, :, hs], vc_ref[0, 0, :, hs]], axis=0)
            bias = bias_sc[var, h]
        else:
            k = kc_ref[0, 0, :, hs]
            v = vc_ref[0, 0, :, hs]
            bias = bias_sc[0, h, :, t:]
        s = lax.dot_general(q, k, _NT, preferred_element_type=jnp.float32) - bias
        m = jnp.max(s, axis=-1, keepdims=True)
        e = jnp.exp2(s - m)
        l = jnp.sum(e, axis=-1, keepdims=True)
        o = jnp.dot(e.astype(v.dtype), v, preferred_element_type=jnp.float32) / l
        o_ref[0, 0, :, hs] = o.astype(o_ref.dtype)
        lse_all = jnp.where(lane == h, m + jnp.log2(l), lse_all)
    lse_ref[0, 0] = lse_all


def _dil_attention(q, kv, slopes2, win, r):
    b, _, ln, _ = q.shape
    t = ATT_TILE

    def cur(c):
        return pl.BlockSpec((1, 1, t, ATT_WIDTH), lambda bi, rho, i, c=c: (bi, rho, i, c))

    def prev(c):
        return pl.BlockSpec((1, 1, t, ATT_WIDTH), lambda bi, rho, i, c=c: (bi, rho, jnp.maximum(i - 1, 0), c))

    has_prev = ln > t
    kv_specs = [prev(0), cur(0), prev(1), cur(1)] if has_prev else [cur(0), cur(1)]
    return pl.pallas_call(
        functools.partial(_dil_attn_kernel, window=win // r, has_prev=has_prev),
        grid=(b, r, ln // t),
        in_specs=[pl.BlockSpec(memory_space=pltpu.SMEM), cur(0)] + kv_specs,
        out_specs=[pl.BlockSpec((1, 1, t, ATT_WIDTH), lambda bi, rho, i: (bi, rho, i, 0)),
                   pl.BlockSpec((1, 1, t, LANES), lambda bi, rho, i: (bi, rho, i, 0))],
        out_shape=[jax.ShapeDtypeStruct((b, r, ln, ATT_WIDTH), jnp.bfloat16),
                   jax.ShapeDtypeStruct((b, r, ln, LANES), jnp.float32)],
        scratch_shapes=[pltpu.VMEM((2, N_HEADS, t, 2 * t), jnp.float32)],
        compiler_params=_params(("arbitrary", "arbitrary", "arbitrary")),
        name="dilated_attention_r%d" % r,
    )(slopes2 * r, q, *([kv] * len(kv_specs)))


def _dil_merge_kernel(*refs, dilations):
    ng = len(dilations)
    o_refs, l_refs, out_ref, o_sc, l_sc = refs[:ng], refs[ng:2 * ng], refs[2 * ng], refs[2 * ng + 1], refs[2 * ng + 2]
    ts = out_ref.shape[1]
    ls = []
    for gi, r in enumerate(dilations):
        if r == 1:
            ls.append(l_refs[gi][0, 0])
            continue
        for rho in range(r):
            l_sc[gi, pl.ds(rho, ts // r, stride=r), :] = l_refs[gi][0, rho]
            for h in range(N_HEADS):
                o_sc[gi, h, pl.ds(rho, ts // r, stride=r), :] = (
                    o_refs[gi][0, rho, :, h * HEAD_DIM:(h + 1) * HEAD_DIM].astype(jnp.float32))
        ls.append(l_sc[gi])
    m = functools.reduce(jnp.maximum, ls)
    es = [jnp.exp2(l - m) for l in ls]
    den = functools.reduce(lambda a, b: a + b, es)
    ws = [e / den for e in es]
    for h in range(N_HEADS):
        hs = slice(h * HEAD_DIM, (h + 1) * HEAD_DIM)
        acc = None
        for gi, r in enumerate(dilations):
            og = o_refs[gi][0, 0, :, hs].astype(jnp.float32) if r == 1 else o_sc[gi, h]
            term = ws[gi][:, h:h + 1] * og
            acc = term if acc is None else acc + term
        out_ref[0, :, hs] = acc.astype(out_ref.dtype)


def _dil_merge(outs, lses, dilations):
    b, _, _, w = outs[0].shape
    s = outs[0].shape[1] * outs[0].shape[2]
    ts = min(256, s)
    ng = len(dilations)

    def spec(r, width):
        return pl.BlockSpec((1, r, ts // r, width), lambda i, j: (i, 0, j, 0))

    return pl.pallas_call(
        functools.partial(_dil_merge_kernel, dilations=dilations),
        grid=(b, s // ts),
        in_specs=[spec(r, w) for r in dilations] + [spec(r, LANES) for r in dilations],
        out_specs=pl.BlockSpec((1, ts, w), lambda i, j: (i, j, 0)),
        out_shape=jax.ShapeDtypeStruct((b, s, w), jnp.bfloat16),
        scratch_shapes=[pltpu.VMEM((ng, w // HEAD_DIM, ts, HEAD_DIM), jnp.float32),
                        pltpu.VMEM((ng, ts, LANES), jnp.float32)],
        compiler_params=_params(("parallel", "parallel")),
        name="dilated_merge",
    )(*outs, *lses)


ROW_TILE = 8


def _gather_rows(idx_ref, src_hbm, dst, sem):
    def body(i, carry):
        for j in range(ROW_TILE):
            pltpu.make_async_copy(src_hbm.at[idx_ref[0, 0, i * ROW_TILE + j]], dst.at[i, :, j], sem).start()
        return carry
    lax.fori_loop(0, dst.shape[0], body, 0)


def _wait_rows(dst, sem):
    pltpu.make_async_copy(dst, dst, sem).wait()


def _expert_kernel(be_ref, nu_ref, tokc_ref, tokn_ref, h_hbm, wg_ref, wu_ref, wd_ref, o_ref, xbuf, sem):
    del be_ref
    blk = pl.program_id(0)
    n_used = nu_ref[0]
    slot = blk % 2

    @pl.when(blk == 0)
    def _():
        _gather_rows(tokc_ref, h_hbm, xbuf.at[0], sem.at[0])

    @pl.when(blk + 1 < n_used)
    def _():
        _gather_rows(tokn_ref, h_hbm, xbuf.at[1 - slot], sem.at[1 - slot])

    @pl.when(blk < n_used)
    def _():
        _wait_rows(xbuf.at[slot], sem.at[slot])
        x = _load_row_tiles(xbuf.at[slot]).astype(jnp.bfloat16)
        gate = jnp.dot(x, wg_ref[0], preferred_element_type=jnp.float32)
        up = jnp.dot(x, wu_ref[0], preferred_element_type=jnp.float32)
        hid = (gate * jax.nn.sigmoid(gate) * up).astype(jnp.bfloat16)
        _store_row_major(o_ref, jnp.dot(hid, wd_ref[0], preferred_element_type=jnp.float32))

    @pl.when(blk >= n_used)
    def _():
        o_ref[...] = jnp.zeros(o_ref.shape, o_ref.dtype)


def _expert_blocks(h3, slot_tok, blk_e, n_used, w_gate, w_up, w_down):
    t, ct, _ = h3.shape
    d = ct * LANES
    nblk = blk_e.shape[0]
    de = w_gate.shape[2]
    tok = slot_tok.reshape(nblk, 1, MOE_BLOCK)
    grid_spec = pltpu.PrefetchScalarGridSpec(
        num_scalar_prefetch=2,
        grid=(nblk,),
        in_specs=[pl.BlockSpec((1, 1, MOE_BLOCK), lambda i, be, nu: (i, 0, 0), memory_space=pltpu.SMEM),
                  pl.BlockSpec((1, 1, MOE_BLOCK), lambda i, be, nu: (jnp.minimum(i + 1, nblk - 1), 0, 0),
                               memory_space=pltpu.SMEM),
                  pl.BlockSpec(memory_space=pl.ANY),
                  pl.BlockSpec((1, d, de), lambda i, be, nu: (be[i], 0, 0)),
                  pl.BlockSpec((1, d, de), lambda i, be, nu: (be[i], 0, 0)),
                  pl.BlockSpec((1, de, d), lambda i, be, nu: (be[i], 0, 0))],
        out_specs=pl.BlockSpec((MOE_BLOCK * ct, LANES), lambda i, be, nu: (i, 0)),
        scratch_shapes=[pltpu.VMEM((2, MOE_BLOCK // ROW_TILE, ct, ROW_TILE, LANES), jnp.float32),
                        pltpu.SemaphoreType.DMA((2,))],
    )
    yo = pl.pallas_call(
        _expert_kernel,
        grid_spec=grid_spec,
        out_shape=jax.ShapeDtypeStruct((nblk * MOE_BLOCK * ct, LANES), jnp.float32),
        compiler_params=_params(("arbitrary",)),
        name="moe_expert_blocks",
    )(blk_e, n_used, tok, tok, h3, w_gate, w_up, w_down)
    return yo.reshape(nblk * MOE_BLOCK, ct, LANES)


def _combine_kernel(d0c_ref, d1c_ref, d0n_ref, d1n_ref, x_ref, w0_ref, w1_ref, g_ref, gn_ref, yo_hbm, o_ref,
                    buf, sem, *, out_norm):
    i = pl.program_id(0)
    n = pl.num_programs(0)
    slot = i % 2
    tt = x_ref.shape[0]

    @pl.when(i == 0)
    def _():
        _gather_rows(d0c_ref, yo_hbm, buf.at[0, 0], sem.at[0])
        _gather_rows(d1c_ref, yo_hbm, buf.at[0, 1], sem.at[0])

    @pl.when(i + 1 < n)
    def _():
        _gather_rows(d0n_ref, yo_hbm, buf.at[1 - slot, 0], sem.at[1 - slot])
        _gather_rows(d1n_ref, yo_hbm, buf.at[1 - slot, 1], sem.at[1 - slot])

    _wait_rows(buf.at[slot, 0], sem.at[slot])
    _wait_rows(buf.at[slot, 1], sem.at[slot])
    y = (w0_ref[...][:, 0:1] * _load_row_tiles(buf.at[slot, 0])
         + w1_ref[...][:, 0:1] * _load_row_tiles(buf.at[slot, 1]))
    out = x_ref[...] + g_ref[0] * y
    o_ref[...] = _rms(out, gn_ref[...]) if out_norm else out


def _moe_combine(x2d, yo, dest, wts, gate, s, out_norm_g=None):
    t, d = x2d.shape
    out_norm = out_norm_g is not None
    gn = (out_norm_g if out_norm else jnp.ones((d,), jnp.float32)).reshape(1, d)
    tt = min(256, s)
    nt = t // tt
    d0 = dest[:, 0].reshape(nt, 1, tt)
    d1 = dest[:, 1].reshape(nt, 1, tt)
    w0 = jnp.broadcast_to(wts[:, 0:1], (t, LANES))
    w1 = jnp.broadcast_to(wts[:, 1:2], (t, LANES))
    b = gate.shape[0]
    cur = pl.BlockSpec((1, 1, tt), lambda i: (i, 0, 0), memory_space=pltpu.SMEM)
    nxt = pl.BlockSpec((1, 1, tt), lambda i: (jnp.minimum(i + 1, nt - 1), 0, 0), memory_space=pltpu.SMEM)
    return pl.pallas_call(
        functools.partial(_combine_kernel, out_norm=out_norm),
        grid=(nt,),
        in_specs=[cur, cur, nxt, nxt,
                  pl.BlockSpec((tt, d), lambda i: (i, 0)),
                  pl.BlockSpec((tt, LANES), lambda i: (i, 0)),
                  pl.BlockSpec((tt, LANES), lambda i: (i, 0)),
                  pl.BlockSpec((1, 1, d), lambda i: (i * tt // s, 0, 0)),
                  pl.BlockSpec((1, d), lambda i: (0, 0)),
                  pl.BlockSpec(memory_space=pl.ANY)],
        out_specs=pl.BlockSpec((tt, d), lambda i: (i, 0)),
        out_shape=jax.ShapeDtypeStruct((t, d), jnp.float32),
        scratch_shapes=[pltpu.VMEM((2, 2, tt // ROW_TILE, d // LANES, ROW_TILE, LANES), jnp.float32),
                        pltpu.SemaphoreType.DMA((2,))],
        compiler_params=_params(("arbitrary",)),
        name="moe_combine_residual",
    )(d0, d1, d0, d1, x2d, w0, w1, gate.reshape(b, 1, d), gn, yo)


def _dispatch_tables(idx):
    t = idx.shape[0]
    a = t * TOP_K
    flat_e = idx.reshape(a)
    onehot = (flat_e[:, None] == jnp.arange(N_EXPERTS, dtype=jnp.int32)[None, :]).astype(jnp.int32)
    csum = jnp.cumsum(onehot, axis=0)
    rank = jnp.take_along_axis(csum, flat_e[:, None], axis=1)[:, 0] - 1
    counts = csum[-1]
    padded = (counts + MOE_BLOCK - 1) // MOE_BLOCK * MOE_BLOCK
    pad_end = jnp.cumsum(padded)
    pad_start = pad_end - padded
    dest = pad_start[flat_e] + rank
    nblk = -(-(a + N_EXPERTS * MOE_BLOCK) // MOE_BLOCK)
    cap = nblk * MOE_BLOCK
    slot_tok = jnp.zeros((cap,), jnp.int32).at[dest].set(jnp.arange(a, dtype=jnp.int32) // TOP_K,
                                                         unique_indices=True)
    blk_start = jnp.arange(nblk, dtype=jnp.int32) * MOE_BLOCK
    blk_e = jnp.minimum(jnp.sum((pad_end[None, :] <= blk_start[:, None]).astype(jnp.int32), axis=1), N_EXPERTS - 1)
    n_used = (pad_end[-1:] // MOE_BLOCK).astype(jnp.int32)
    return dest.reshape(t, TOP_K).astype(jnp.int32), slot_tok, blk_e.astype(jnp.int32), n_used


def _moe_layer(x, g, sc, sh, gate, router_w, router_b, layer, w_gate, w_up, w_down, out_norm_g=None):
    b, s, d = x.shape
    t = b * s
    h, idx, wts = _norm_route(x, g, sc, sh, router_w, router_b)
    idx = idx.transpose(0, 2, 1).reshape(t, TOP_K)
    wts = wts.transpose(0, 2, 1).reshape(t, TOP_K)
    dest, slot_tok, blk_e, n_used = _dispatch_tables(idx)
    yo = _expert_blocks(h, slot_tok, blk_e + layer * N_EXPERTS, n_used, w_gate, w_up, w_down)
    return _moe_combine(x.reshape(t, d), yo, dest, wts, gate, s, out_norm_g).reshape(b, s, d)


def _nsa_layer(h, x, gate, slopes, j, w_in, w_phi1, w_phi2, phi_pos, w_out):
    b, s, d = h.shape
    t = b * s
    h2d = h.reshape(t, d)
    proj = _matmul(h2d, w_in, j, 0, NSA_QKV, jnp.bfloat16, scale=HEAD_DIM ** -0.5 * LOG2E, scale_cols=ATT_WIDTH)
    proj = proj.reshape(b, s, NSA_QKV)
    wg = w_in[j, :, NSA_QKV:].reshape(d, NSA_KV_HEADS, NSA_GATES)
    wg = jnp.pad(wg, ((0, 0), (0, 0), (0, LANES - NSA_GATES))).reshape(1, d, NSA_KV_HEADS * LANES)
    gates = _matmul(h2d, wg, 0, 0, NSA_KV_HEADS * LANES, jnp.float32, act="sigmoid")
    gates = gates.reshape(b, s, NSA_KV_HEADS * LANES)
    nch = s // CMP_STRIDE
    kv = proj[:, :, ATT_WIDTH:ATT_WIDTH + 2 * NSA_KV_WIDTH]
    kv = kv.reshape(b, nch, CMP_STRIDE, 2, NSA_KV_HEADS, HEAD_DIM).transpose(0, 3, 4, 1, 2, 5)
    cmp_kv = _compress(kv.reshape(b, 2, NSA_KV_HEADS, nch, CMP_STRIDE * HEAD_DIM), w_phi1, w_phi2, phi_pos)
    o = _nsa_attention(proj, cmp_kv, gates, slopes * LOG2E)
    return _matmul_residual(o, w_out, j, x, gate)


def _dil_layer(x, g, sc, sh, gate, slopes, j, w_in, w_out):
    b, s, d = x.shape
    dilations = tuple(r for _, r in DIL_PAIRS)
    assert dilations[0] == 1
    hs = _norm_mod_streams(x, g, sc, sh, dilations)
    qs = _matmul_streams(hs[0].reshape(b, s, d), w_in, j, 0, ATT_WIDTH, dilations,
                         scale=HEAD_DIM ** -0.5 * LOG2E)
    outs, lses = [], []
    for gidx, (win, r) in enumerate(DIL_PAIRS):
        off = ATT_WIDTH * (1 + 2 * gidx)
        kv = _matmul(hs[gidx].reshape(b * s, d), w_in, j, off, 2 * ATT_WIDTH, jnp.bfloat16)
        o, lse = _dil_attention(qs[gidx], kv.reshape(b, r, s // r, 2 * ATT_WIDTH), slopes * LOG2E, win, r)
        outs.append(o)
        lses.append(lse)
    o = _dil_merge(outs, lses, dilations)
    return _matmul_residual(o, w_out, j, x, gate)


def kernel(x, c, ada_w, ada_b, norm_mix, norm_ffn, norm_final, nsa_w_in, nsa_w_phi1, nsa_w_phi2, nsa_phi_pos,
           nsa_w_out, dil_w_in, dil_w_out, router_w, router_b, exp_w_gate, exp_w_up, exp_w_down):
    depth = ada_w.shape[0]
    d = x.shape[-1]
    mod = _modulation(c, ada_w, ada_b)
    slopes = 2.0 ** (-ALIBI_MAX_BIAS * jnp.arange(1, N_HEADS + 1, dtype=jnp.float32) / N_HEADS)
    e_gate, e_up, e_down = [_cast_bf16(w).reshape((-1,) + w.shape[2:]) for w in (exp_w_gate, exp_w_up, exp_w_down)]
    for i in range(depth):
        sh_m, sc_m, g_m, sh_f, sc_f, g_f = [mod[i, :, k * d:(k + 1) * d] for k in range(6)]
        j = i // 2
        if i % 2 == 0:
            h = _norm_mod(x, norm_mix[i], sc_m, sh_m, jnp.bfloat16)
            x = _nsa_layer(h, x, g_m, slopes, j, nsa_w_in, nsa_w_phi1[j], nsa_w_phi2[j], nsa_phi_pos[j], nsa_w_out)
        else:
            x = _dil_layer(x, norm_mix[i], sc_m, sh_m, g_m, slopes, j, dil_w_in, dil_w_out)
        x = _moe_layer(x, norm_ffn[i], sc_f, sh_f, g_f, router_w, router_b, i, e_gate, e_up, e_down,
                       norm_final if i == depth - 1 else None)
    return x
```

```python
import functools

import jax
import jax.numpy as jnp
import numpy as np
from jax import lax
from jax.experimental import pallas as pl
from jax.experimental.pallas import tpu as pltpu

HEAD_DIM = 128
N_HEADS = 16
ATT_WIDTH = N_HEADS * HEAD_DIM
ALIBI_MAX_BIAS = 8.0

NSA_KV_HEADS = 4
NSA_GROUP = N_HEADS // NSA_KV_HEADS
NSA_KV_WIDTH = NSA_KV_HEADS * HEAD_DIM
CMP_BLOCK = 32
CMP_STRIDE = 16
SEL_BLOCK = 64
N_SEL = 16
NSA_WINDOW = 512
FORCE_SCORE = 1.0e6
NSA_QKV = ATT_WIDTH + 6 * NSA_KV_WIDTH
NSA_GATES = 3 * NSA_GROUP

DIL_PAIRS = ((128, 1), (512, 4), (2048, 16))
DIL_PROJ = ATT_WIDTH * (1 + 2 * len(DIL_PAIRS))

N_EXPERTS = 16
N_EXPERT_GROUPS = 4
EXPERTS_PER_GROUP = N_EXPERTS // N_EXPERT_GROUPS
TOP_K = 2
MOE_BLOCK = 256

RMS_EPS = 1e-6
NEG_INF = -1.0e30

LANES = 128
ATT_TILE = 128
VMEM_LIMIT = 56 * 1024 * 1024

_HI = lax.Precision.HIGHEST
_NT = (((1,), (1,)), ((), ()))


def _params(sem, vmem=VMEM_LIMIT):
    return pltpu.CompilerParams(dimension_semantics=sem, vmem_limit_bytes=vmem)


def _mod_kernel(c_ref, w_ref, b_ref, o_ref):
    c = c_ref[...]
    cond = c * jax.nn.sigmoid(c)
    o_ref[0] = jnp.dot(cond, w_ref[0], precision=_HI, preferred_element_type=jnp.float32) + b_ref[0]


def _modulation(c, ada_w, ada_b):
    depth, d, n = ada_w.shape
    b = c.shape[0]
    tn = 512
    return pl.pallas_call(
        _mod_kernel,
        grid=(depth, n // tn),
        in_specs=[pl.BlockSpec((b, d), lambda i, j: (0, 0)),
                  pl.BlockSpec((1, d, tn), lambda i, j: (i, 0, j)),
                  pl.BlockSpec((1, 1, tn), lambda i, j: (i, 0, j))],
        out_specs=pl.BlockSpec((1, b, tn), lambda i, j: (i, 0, j)),
        out_shape=jax.ShapeDtypeStruct((depth, b, n), jnp.float32),
        compiler_params=_params(("parallel", "parallel")),
        name="adaln_modulation",
    )(c, ada_w, ada_b.reshape(depth, 1, n))


def _rms(x, g):
    return x * lax.rsqrt(jnp.mean(x * x, axis=-1, keepdims=True) + RMS_EPS) * g


def _norm_mod_kernel(x_ref, g_ref, sc_ref, sh_ref, o_ref):
    h = _rms(x_ref[0], g_ref[...]) * (1.0 + sc_ref[0]) + sh_ref[0]
    o_ref[0] = h.astype(o_ref.dtype)


def _norm_mod(x, g, sc, sh, out_dtype):
    b, s, d = x.shape
    ts = min(256, s)
    return pl.pallas_call(
        _norm_mod_kernel,
        grid=(b, s // ts),
        in_specs=[pl.BlockSpec((1, ts, d), lambda i, j: (i, j, 0)),
                  pl.BlockSpec((1, d), lambda i, j: (0, 0)),
                  pl.BlockSpec((1, 1, d), lambda i, j: (i, 0, 0)),
                  pl.BlockSpec((1, 1, d), lambda i, j: (i, 0, 0))],
        out_specs=pl.BlockSpec((1, ts, d), lambda i, j: (i, j, 0)),
        out_shape=jax.ShapeDtypeStruct((b, s, d), out_dtype),
        compiler_params=_params(("parallel", "parallel")),
        name="rmsnorm_adaln",
    )(x, g.reshape(1, d), sc.reshape(b, 1, d), sh.reshape(b, 1, d))


def _store_streams(val, src_sc, out_refs, dilations):
    n = val.shape[0]
    for c in range(src_sc.shape[0]):
        src_sc[c] = val[:, c * LANES:(c + 1) * LANES]
    for r, ref in zip(dilations, out_refs):
        if r == 1:
            ref[0, 0] = val.astype(ref.dtype)
            continue
        for rho in range(r):
            for c in range(src_sc.shape[0]):
                ref[0, rho, :, c * LANES:(c + 1) * LANES] = (
                    src_sc[c, pl.ds(rho, n // r, stride=r), :].astype(ref.dtype))


def _norm_mod_streams_kernel(x_ref, g_ref, sc_ref, sh_ref, *refs, dilations):
    out_refs, h_sc = refs[:-1], refs[-1]
    h = _rms(x_ref[0], g_ref[...]) * (1.0 + sc_ref[0]) + sh_ref[0]
    _store_streams(h, h_sc, out_refs, dilations)


def _norm_mod_streams(x, g, sc, sh, dilations):
    b, s, d = x.shape
    ts = min(256, s)
    return pl.pallas_call(
        functools.partial(_norm_mod_streams_kernel, dilations=dilations),
        grid=(b, s // ts),
        in_specs=[pl.BlockSpec((1, ts, d), lambda i, j: (i, j, 0)),
                  pl.BlockSpec((1, d), lambda i, j: (0, 0)),
                  pl.BlockSpec((1, 1, d), lambda i, j: (i, 0, 0)),
                  pl.BlockSpec((1, 1, d), lambda i, j: (i, 0, 0))],
        out_specs=[pl.BlockSpec((1, r, ts // r, d), lambda i, j: (i, 0, j, 0)) for r in dilations],
        out_shape=[jax.ShapeDtypeStruct((b, r, s // r, d), jnp.bfloat16) for r in dilations],
        scratch_shapes=[pltpu.VMEM((d // LANES, ts, LANES), jnp.float32)],
        compiler_params=_params(("parallel", "parallel")),
        name="rmsnorm_adaln_streams",
    )(x, g.reshape(1, d), sc.reshape(b, 1, d), sh.reshape(b, 1, d))


def _pair_max(vals):
    out = None
    for a in range(len(vals)):
        for b in range(a + 1, len(vals)):
            s = vals[a] + vals[b]
            out = s if out is None else jnp.maximum(out, s)
    return out


def _store_row_major(ref, val):
    rows = val.shape[0]
    c_tiles = val.shape[1] // LANES
    for c in range(c_tiles):
        ref[pl.ds(c, rows, stride=c_tiles), :] = val[:, c * LANES:(c + 1) * LANES]


def _load_row_tiles(buf):
    n, c_tiles = buf.shape[0], buf.shape[1]
    return jnp.concatenate([buf[:, c].reshape(n * ROW_TILE, LANES) for c in range(c_tiles)], axis=1)


def _norm_route_kernel(x_ref, g_ref, sc_ref, sh_ref, rw_ref, rb_ref, h_ref, idx_ref, wt_ref):
    h = _rms(x_ref[0], g_ref[...]) * (1.0 + sc_ref[0]) + sh_ref[0]
    _store_row_major(h_ref, h)
    logits = lax.dot_general(rw_ref[...], h, _NT, precision=_HI, preferred_element_type=jnp.float32)
    scores = jax.nn.sigmoid(logits)
    biased = scores + rb_ref[...][:, 0:1]
    rows = [biased[e:e + 1, :] for e in range(N_EXPERTS)]
    srow = [scores[e:e + 1, :] for e in range(N_EXPERTS)]
    grp = [_pair_max(rows[q * EXPERTS_PER_GROUP:(q + 1) * EXPERTS_PER_GROUP]) for q in range(N_EXPERT_GROUPS)]
    best_v = grp[0]
    best = jnp.zeros(best_v.shape, jnp.int32)
    for q in range(1, N_EXPERT_GROUPS):
        take = grp[q] > best_v
        best = jnp.where(take, q, best)
        best_v = jnp.where(take, grp[q], best_v)
    v1 = jnp.full(best_v.shape, NEG_INF, jnp.float32)
    i1 = jnp.zeros(best_v.shape, jnp.int32)
    for e in range(N_EXPERTS):
        cand = jnp.where(best == e // EXPERTS_PER_GROUP, rows[e], NEG_INF)
        take = cand > v1
        i1 = jnp.where(take, e, i1)
        v1 = jnp.where(take, cand, v1)
    v2 = jnp.full(best_v.shape, NEG_INF, jnp.float32)
    i2 = jnp.zeros(best_v.shape, jnp.int32)
    for e in range(N_EXPERTS):
        cand = jnp.where(best == e // EXPERTS_PER_GROUP, jnp.where(i1 == e, NEG_INF, rows[e]), NEG_INF)
        take = cand > v2
        i2 = jnp.where(take, e, i2)
        v2 = jnp.where(take, cand, v2)
    w1 = jnp.zeros(best_v.shape, jnp.float32)
    w2 = jnp.zeros(best_v.shape, jnp.float32)
    for e in range(N_EXPERTS):
        w1 = jnp.where(i1 == e, srow[e], w1)
        w2 = jnp.where(i2 == e, srow[e], w2)
    tot = w1 + w2
    idx_ref[0] = jnp.concatenate([i1, i2], axis=0)
    wt_ref[0] = jnp.concatenate([w1 / tot, w2 / tot], axis=0)


def _norm_route(x, g, sc, sh, router_w, router_b):
    b, s, d = x.shape
    ts = min(256, s)
    ct = d // LANES
    rwt = router_w.T
    rb = jnp.broadcast_to(router_b.reshape(N_EXPERTS, 1), (N_EXPERTS, LANES))
    h, idx, wts = pl.pallas_call(
        _norm_route_kernel,
        grid=(b, s // ts),
        in_specs=[pl.BlockSpec((1, ts, d), lambda i, j: (i, j, 0)),
                  pl.BlockSpec((1, d), lambda i, j: (0, 0)),
                  pl.BlockSpec((1, 1, d), lambda i, j: (i, 0, 0)),
                  pl.BlockSpec((1, 1, d), lambda i, j: (i, 0, 0)),
                  pl.BlockSpec((N_EXPERTS, d), lambda i, j: (0, 0)),
                  pl.BlockSpec((N_EXPERTS, LANES), lambda i, j: (0, 0))],
        out_specs=[pl.BlockSpec((ts * ct, LANES), lambda i, j: (i * (s // ts) + j, 0)),
                   pl.BlockSpec((1, TOP_K, ts), lambda i, j: (i, 0, j)),
                   pl.BlockSpec((1, TOP_K, ts), lambda i, j: (i, 0, j))],
        out_shape=[jax.ShapeDtypeStruct((b * s * ct, LANES), jnp.float32),
                   jax.ShapeDtypeStruct((b, TOP_K, s), jnp.int32),
                   jax.ShapeDtypeStruct((b, TOP_K, s), jnp.float32)],
        compiler_params=_params(("parallel", "parallel")),
        name="rmsnorm_adaln_router",
    )(x, g.reshape(1, d), sc.reshape(b, 1, d), sh.reshape(b, 1, d), rwt, rb)
    return h.reshape(b * s, ct, LANES), idx, wts


def _weight_tile(w_ref, wbf_sc, first, scale, scale_tiles):
    @pl.when(first)
    def _():
        w = w_ref[0]
        if scale_tiles:
            w = w * jnp.where(pl.program_id(0) < scale_tiles, scale, 1.0)
        wbf_sc[...] = w.astype(jnp.bfloat16)
    return wbf_sc[...]


def _weight_spec(layer, col0, k, tn, rank):
    assert col0 % tn == 0
    if rank == 2:
        return pl.BlockSpec((1, k, tn), lambda j, i: (layer, 0, col0 // tn + j))
    return pl.BlockSpec((1, k, tn), lambda j, bi, i: (layer, 0, col0 // tn + j))


def _mm_kernel(a_ref, w_ref, o_ref, wbf_sc, *, act, scale, scale_tiles):
    w = _weight_tile(w_ref, wbf_sc, pl.program_id(1) == 0, scale, scale_tiles)
    acc = jnp.dot(a_ref[...], w, preferred_element_type=jnp.float32)
    if act == "sigmoid":
        acc = jax.nn.sigmoid(acc)
    o_ref[...] = acc.astype(o_ref.dtype)


def _matmul(a, w, layer, col0, n, out_dtype, act=None, scale=1.0, scale_cols=0, tm=2048, tn=1024):
    m, k = a.shape
    tn = min(tn, n)
    tm = min(tm, m)
    assert scale_cols % tn == 0
    return pl.pallas_call(
        functools.partial(_mm_kernel, act=act, scale=scale, scale_tiles=scale_cols // tn),
        grid=(n // tn, m // tm),
        in_specs=[pl.BlockSpec((tm, k), lambda j, i: (i, 0)),
                  _weight_spec(layer, col0, k, tn, 2)],
        out_specs=pl.BlockSpec((tm, tn), lambda j, i: (i, j)),
        out_shape=jax.ShapeDtypeStruct((m, n), out_dtype),
        scratch_shapes=[pltpu.VMEM((k, tn), jnp.bfloat16)],
        compiler_params=_params(("arbitrary", "arbitrary")),
        name="matmul" if act is None else "matmul_" + act,
    )(a, w)


def _first_row_step():
    return (pl.program_id(1) == 0) & (pl.program_id(2) == 0)


def _mm_streams_kernel(a_ref, w_ref, *refs, dilations, scale, scale_tiles):
    out_refs, acc_sc, wbf_sc = refs[:-2], refs[-2], refs[-1]
    w = _weight_tile(w_ref, wbf_sc, _first_row_step(), scale, scale_tiles)
    acc = jnp.dot(a_ref[0], w, preferred_element_type=jnp.float32)
    _store_streams(acc, acc_sc, out_refs, dilations)


def _matmul_streams(a, w, layer, col0, n, dilations, scale=1.0, tm=512, tn=1024):
    b, s, k = a.shape
    tn = min(tn, n)
    tm = min(tm, s)
    return pl.pallas_call(
        functools.partial(_mm_streams_kernel, dilations=dilations, scale=scale,
                          scale_tiles=n // tn if scale != 1.0 else 0),
        grid=(n // tn, b, s // tm),
        in_specs=[pl.BlockSpec((1, tm, k), lambda j, bi, i: (bi, i, 0)),
                  _weight_spec(layer, col0, k, tn, 3)],
        out_specs=[pl.BlockSpec((1, r, tm // r, tn), lambda j, bi, i: (bi, 0, i, j)) for r in dilations],
        out_shape=[jax.ShapeDtypeStruct((b, r, s // r, n), jnp.bfloat16) for r in dilations],
        scratch_shapes=[pltpu.VMEM((tn // LANES, tm, LANES), jnp.float32), pltpu.VMEM((k, tn), jnp.bfloat16)],
        compiler_params=_params(("arbitrary", "arbitrary", "arbitrary")),
        name="matmul_streams",
    )(a, w)


def _mm_residual_kernel(a_ref, w_ref, x_ref, g_ref, o_ref, wbf_sc):
    w = _weight_tile(w_ref, wbf_sc, _first_row_step(), 1.0, 0)
    acc = jnp.dot(a_ref[0], w, preferred_element_type=jnp.float32)
    o_ref[0] = x_ref[0] + g_ref[0] * acc


def _matmul_residual(a, w, layer, x, gate, tm=512, tn=1024):
    b, s, k = a.shape
    n = w.shape[2]
    tn = min(tn, n)
    tm = min(tm, s)
    return pl.pallas_call(
        _mm_residual_kernel,
        grid=(n // tn, b, s // tm),
        in_specs=[pl.BlockSpec((1, tm, k), lambda j, bi, i: (bi, i, 0)),
                  _weight_spec(layer, 0, k, tn, 3),
                  pl.BlockSpec((1, tm, tn), lambda j, bi, i: (bi, i, j)),
                  pl.BlockSpec((1, 1, tn), lambda j, bi, i: (bi, 0, j))],
        out_specs=pl.BlockSpec((1, tm, tn), lambda j, bi, i: (bi, i, j)),
        out_shape=jax.ShapeDtypeStruct((b, s, n), jnp.float32),
        scratch_shapes=[pltpu.VMEM((k, tn), jnp.bfloat16)],
        compiler_params=_params(("arbitrary", "arbitrary", "arbitrary")),
        name="matmul_gated_residual",
    )(a, w, x, gate.reshape(b, 1, n))


def _gelu_tanh(x):
    return 0.5 * x * (1.0 + jnp.tanh(0.7978845608028654 * (x + 0.044715 * (x * x * x))))


def _compress_kernel(c_ref, pa_ref, pb_ref, wa_ref, wb_ref, w2_ref, o_ref):
    c = c_ref[0, 0, 0].astype(jnp.float32)
    lo = (c + pa_ref[0]).astype(jnp.bfloat16)
    hi = (c + pb_ref[0]).astype(jnp.bfloat16)
    ha = jnp.dot(lo, wa_ref[0], preferred_element_type=jnp.float32)
    hb = jnp.dot(hi, wb_ref[0], preferred_element_type=jnp.float32)
    n = ha.shape[0]
    hid = _gelu_tanh(ha + pltpu.roll(hb, n - 1, 0))
    out = jnp.dot(hid.astype(jnp.bfloat16), w2_ref[0], preferred_element_type=jnp.float32)
    row = lax.broadcasted_iota(jnp.int32, out.shape, 0)
    o_ref[0, 0, 0] = jnp.where(row < n - 1, out, 0.0).astype(o_ref.dtype)


def _compress(kv_chunks, w_phi1, w_phi2, phi_pos):
    b, two, hkv, nch, width = kv_chunks.shape
    dh = HEAD_DIM
    half = CMP_BLOCK // 2
    w1 = w_phi1.reshape(2, 2, half * dh, dh).astype(jnp.bfloat16)
    pos = phi_pos.reshape(2, 2, 1, half * dh)
    return pl.pallas_call(
        _compress_kernel,
        grid=(b, two, hkv),
        in_specs=[pl.BlockSpec((1, 1, 1, nch, width), lambda i, j, g: (i, j, g, 0, 0)),
                  pl.BlockSpec((1, 1, width), lambda i, j, g: (j, 0, 0)),
                  pl.BlockSpec((1, 1, width), lambda i, j, g: (j, 0, 0)),
                  pl.BlockSpec((1, width, dh), lambda i, j, g: (j, 0, 0)),
                  pl.BlockSpec((1, width, dh), lambda i, j, g: (j, 0, 0)),
                  pl.BlockSpec((1, dh, dh), lambda i, j, g: (j, 0, 0))],
        out_specs=pl.BlockSpec((1, 1, 1, nch, dh), lambda i, j, g: (i, j, g, 0, 0)),
        out_shape=jax.ShapeDtypeStruct((b, two, hkv, nch, dh), jnp.bfloat16),
        compiler_params=_params(("parallel", "parallel", "parallel")),
        name="nsa_compress",
    )(kv_chunks, pos[:, 0], pos[:, 1], w1[:, 0], w1[:, 1], w_phi2.astype(jnp.bfloat16))


MASK_BIG = 2.0 ** 100
AUX_SLOPE = 64
AUX_PAD = 70
SEL_CHUNK = 512
LOG2E = 1.4426950408889634


def _nsa_attn_kernel(sl_ref, q_ref, kc_ref, vc_ref, ks_ref, vs_ref, kw_ref, vw_ref, gt_ref, ovl_ref, qtab_ref,
                     kauxs_ref, kauxw_ref, *refs, n_cmp, n_blk, n_sel, n_cast):
    cast_src, refs = refs[:n_cast], refs[n_cast:]
    o_ref, refs = refs[0], refs[1:]
    cast_dst, refs = refs[:n_cast], refs[n_cast:]
    (ocmp_sc, maskq_sc, owin_sc, qaug_sc, qaugw_sc, p_sc, pw_sc, kaug_s, vaug_s, kaug_w, vaug_w), refs = refs[:11], refs[11:]
    cast_in, cast_out, (cast_in_sem, cast_out_sem) = refs[:n_cast], refs[n_cast:2 * n_cast], refs[2 * n_cast:]
    t = ATT_TILE
    dh = HEAD_DIM
    s_len = ks_ref.shape[1]
    wpad = NSA_WINDOW
    g = pl.program_id(1)
    qt = pl.program_id(2)
    qstart = pl.multiple_of(qt * t, t)

    if n_cast:
        step = (pl.program_id(0) * pl.num_programs(1) + g) * pl.num_programs(2) + qt
        n_steps = pl.num_programs(0) * pl.num_programs(1) * pl.num_programs(2)
        _cast_side_job(step, n_steps, cast_src, cast_dst, cast_in, cast_out, cast_in_sem, cast_out_sem)

    @pl.when(qt == 0)
    def _():
        ones = jnp.ones((s_len, dh), jnp.bfloat16)
        kaug_s[:, :dh] = ks_ref[0]
        kaug_s[:, dh:] = kauxs_ref[...]
        vaug_s[:, :dh] = vs_ref[0]
        vaug_s[:, dh:] = ones
        kaug_w[:wpad, :dh] = jnp.zeros((wpad, dh), jnp.bfloat16)
        kaug_w[wpad:, :dh] = kw_ref[0]
        kaug_w[:, dh:] = kauxw_ref[...]
        vaug_w[:wpad, :dh] = jnp.zeros((wpad, dh), jnp.bfloat16)
        vaug_w[wpad:, :dh] = vw_ref[0]
        vaug_w[:wpad, dh:] = jnp.ones((wpad, dh), jnp.bfloat16)
        vaug_w[wpad:, dh:] = ones

    slopes = [sl_ref[g * NSA_GROUP + r] for r in range(NSA_GROUP)]
    row = lax.broadcasted_iota(jnp.int32, (t, LANES), 0)
    col = lax.broadcasted_iota(jnp.int32, (t, LANES), 1)
    prows = [slice(pair * 2 * t, (pair + 1) * 2 * t) for pair in range(NSA_GROUP // 2)]
    qtab = qtab_ref[0]

    def compress_and_select(qv, tile, par):
        qs = tile * t
        tpos = qs + row
        q4 = jnp.concatenate([qv[:, r * dh:(r + 1) * dh] for r in range(NSA_GROUP)], axis=0)
        kc = kc_ref[0, 0, 0]
        vc = vc_ref[0, 0, 0]
        s = lax.dot_general(q4, kc, _NT, preferred_element_type=jnp.float32)
        visible = (col * CMP_STRIDE + (CMP_BLOCK - 1) <= tpos) & (col < n_cmp)
        dist_c = tpos.astype(jnp.float32) - (col.astype(jnp.float32) * CMP_STRIDE + (CMP_BLOCK - 1) / 2.0)
        psum = jnp.zeros((t, LANES), jnp.float32)
        for r in range(NSA_GROUP):
            rows = slice(r * t, (r + 1) * t)
            sr = jnp.where(visible, s[rows] - slopes[r] * dist_c, NEG_INF)
            e = jnp.where(visible, jnp.exp2(sr - jnp.max(sr, axis=-1, keepdims=True)), 0.0)
            p = e / jnp.maximum(jnp.sum(e, axis=-1, keepdims=True), 1e-30)
            psum = psum + p
            ocmp_sc[par, rows] = jnp.dot(p.astype(vc.dtype), vc, preferred_element_type=jnp.float32)
        nb = -(-n_blk // 8) * 8
        imp_t = lax.dot_general(ovl_ref[...], psum, _NT, precision=_HI, preferred_element_type=jnp.float32)[:nb]
        jrow = lax.broadcasted_iota(jnp.int32, (nb, t), 0)
        tpos_t = qs + lax.broadcasted_iota(jnp.int32, (nb, t), 1)
        cur = jnp.right_shift(tpos_t, int(np.log2(SEL_BLOCK)))
        forced = (jrow == 0) | (jrow == cur) | (jrow == cur - 1)
        score = jnp.where(forced, FORCE_SCORE, jnp.where(jrow * SEL_BLOCK <= tpos_t, imp_t, -1.0))
        rank = jnp.zeros((nb, t), jnp.float32)
        for j in range(n_blk):
            cj = score[j:j + 1, :]
            ahead = (cj > score) | ((cj == score) & (jrow > j))
            rank = rank + jnp.where(ahead, 1.0, 0.0)
        mask_t = jnp.where((rank < n_sel) & (jrow < n_blk), 0.0, -MASK_BIG)
        maskq_sc[par] = jnp.concatenate([mask_t, jnp.zeros((LANES - nb, t), jnp.float32)], axis=0).T

    par = 0
    q = q_ref[0]

    for r in range(NSA_GROUP):
        rows = slice(r * t, (r + 1) * t)
        qaugw_sc[rows, :dh] = q[:, r * dh:(r + 1) * dh]
        qaugw_sc[rows, dh:] = jnp.broadcast_to(qtab[r:r + 1, :], (t, LANES)).astype(jnp.bfloat16)
    wk = wpad + t
    nw = wk // LANES
    kw = kaug_w[pl.ds(qstart, wk), :]
    vw = vaug_w[pl.ds(qstart, wk), :]
    sws = [lax.dot_general(qaugw_sc[prow], kw, _NT, preferred_element_type=jnp.float32) for prow in prows]

    compress_and_select(q, qt, par)

    mask_q = maskq_sc[par]
    for r in range(NSA_GROUP):
        rows = slice(r * t, (r + 1) * t)
        qaug_sc[rows, :dh] = q[:, r * dh:(r + 1) * dh]
        qaug_sc[rows, dh:] = (mask_q + qtab[r:r + 1, :]).astype(jnp.bfloat16)

    for pair, prow in enumerate(prows):
        sw = sws[pair]
        for r2 in range(2):
            rows = slice((2 * pair + r2) * t, (2 * pair + r2 + 1) * t)
            tiles = [sw[r2 * t:(r2 + 1) * t, j * LANES:(j + 1) * LANES] for j in range(nw)]
            tiles[0] = jnp.where(col > row, tiles[0], NEG_INF)
            tiles[-1] = jnp.where(col <= row, tiles[-1], NEG_INF)
            mx = functools.reduce(jnp.maximum, tiles)
            m = jnp.broadcast_to(jnp.max(mx, axis=-1, keepdims=True), (t, LANES))
            for j in range(nw):
                pw_sc[rows, j * LANES:(j + 1) * LANES] = jnp.exp2(tiles[j] - m).astype(jnp.bfloat16)
        ow = jnp.dot(pw_sc[prow], vw, preferred_element_type=jnp.float32)
        owin_sc[prow] = ow[:, :dh] / ow[:, dh:]

    ch = SEL_CHUNK
    nl = ch // LANES
    n_full = qt // (ch // t)
    gt = gt_ref[0]
    cmr = col - row

    def selected(k):
        nk = (k + 1) * ch
        kk = kaug_s[:nk, :]
        vv = vaug_s[:nk, :]
        sks = [lax.dot_general(qaug_sc[prow], kk, _NT, preferred_element_type=jnp.float32) for prow in prows]
        for pair, prow in enumerate(prows):
            sk = sks[pair]
            for r2 in range(2):
                rows = slice((2 * pair + r2) * t, (2 * pair + r2 + 1) * t)
                tiles = [sk[r2 * t:(r2 + 1) * t, j * LANES:(j + 1) * LANES] for j in range(nk // LANES)]
                for j in range(k * nl, (k + 1) * nl):
                    tiles[j] = jnp.where(cmr <= qstart - j * LANES, tiles[j], NEG_INF)
                mx = functools.reduce(jnp.maximum, tiles)
                m = jnp.broadcast_to(jnp.max(mx, axis=-1, keepdims=True), (t, LANES))
                for j in range(nk // LANES):
                    p_sc[rows, j * LANES:(j + 1) * LANES] = jnp.exp2(tiles[j] - m).astype(jnp.bfloat16)
            pv = jnp.dot(p_sc[prow, :nk], vv, preferred_element_type=jnp.float32)
            for r2 in range(2):
                r = 2 * pair + r2
                rows = slice(r * t, (r + 1) * t)
                orow = slice(r2 * t, (r2 + 1) * t)
                o = (gt[:, 3 * r:3 * r + 1] * ocmp_sc[par, rows]
                     + gt[:, 3 * r + 1:3 * r + 2] * (pv[orow, :dh] / pv[orow, dh:])
                     + gt[:, 3 * r + 2:3 * r + 3] * owin_sc[rows])
                o_ref[0, :, r * dh:(r + 1) * dh] = o.astype(o_ref.dtype)

    for k in range(s_len // ch):
        pl.when(n_full == k)(functools.partial(selected, k))


def _cast_chunks(shape, n_steps):
    r, c = shape
    for n_c in (1, 2, 4, 8, 16):
        if n_steps % n_c == 0 and c % (n_c * LANES) == 0 and r % (n_steps // n_c) == 0:
            rows = r // (n_steps // n_c)
            if rows % 16 == 0:
                return rows, c // n_c
    raise ValueError("no tile-aligned split of %s into %d chunks" % (shape, n_steps))


def _cast_side_job(n, n_steps, srcs, dsts, inbufs, outbufs, in_sem, out_sem):
    slot = n % 2

    def chunk(k, step):
        rows, cols = inbufs[k].shape[1:]
        n_c = srcs[k].shape[1] // cols
        return pl.ds(pl.multiple_of((step // n_c) * rows, 16), rows), pl.ds(pl.multiple_of((step % n_c) * cols, LANES), cols)

    def in_copy(k, step, sl):
        return pltpu.make_async_copy(srcs[k].at[chunk(k, step)], inbufs[k].at[sl], in_sem.at[sl])

    def out_copy(k, step, sl):
        return pltpu.make_async_copy(outbufs[k].at[sl], dsts[k].at[chunk(k, step)], out_sem.at[sl])

    ks = range(len(srcs))

    @pl.when(n == 0)
    def _():
        for k in ks:
            in_copy(k, 0, 0).start()
            outbufs[k][...] = jnp.zeros(outbufs[k].shape, outbufs[k].dtype)
            out_copy(k, 0, 0).start()
            out_copy(k, 1, 1).start()

    for k in ks:
        in_copy(k, n, slot).wait()
    nxt = jnp.minimum(n + 1, n_steps - 1)
    for k in ks:
        in_copy(k, nxt, 1 - slot).start()
    for k in ks:
        out_copy(k, n, slot).wait()
    for k in ks:
        outbufs[k][slot] = inbufs[k][slot].astype(outbufs[k].dtype)
    for k in ks:
        out_copy(k, n, slot).start()

    @pl.when(n == n_steps - 1)
    def _():
        for k in ks:
            in_copy(k, n, 1 - slot).wait()
            out_copy(k, n, 1 - slot).wait()
            out_copy(k, n, slot).wait()


def _bf16_pieces(x):
    a0 = x.astype(jnp.bfloat16).astype(jnp.float32)
    a1 = (x - a0).astype(jnp.bfloat16).astype(jnp.float32)
    a2 = (x - a0 - a1).astype(jnp.bfloat16).astype(jnp.float32)
    return [a0, a1, a2]


def _nsa_tables(s, slopes2):
    n_cmp = s // CMP_STRIDE - CMP_BLOCK // CMP_STRIDE + 1
    n_blk = s // SEL_BLOCK
    assert n_cmp < LANES and n_blk <= AUX_SLOPE
    bj = np.arange(LANES)[:, None]
    cn = np.arange(LANES)[None, :]
    ovl_t = np.clip(np.minimum(cn * CMP_STRIDE + CMP_BLOCK, (bj + 1) * SEL_BLOCK)
                    - np.maximum(cn * CMP_STRIDE, bj * SEL_BLOCK), 0, None) / CMP_STRIDE
    ovl_t = np.where((cn < n_cmp) & (bj < n_blk), ovl_t, 0.0).astype(np.float32)
    pos = np.arange(s)
    kaux = np.zeros((s, LANES), np.float32)
    kaux[pos, pos // SEL_BLOCK] = 1.0
    kaux[:, AUX_SLOPE:AUX_SLOPE + 3] = (pos // LANES * LANES)[:, None]
    kaux[:, AUX_SLOPE + 3:AUX_SLOPE + 6] = (pos % LANES)[:, None]
    kaux_w = np.zeros((NSA_WINDOW + s, LANES), np.float32)
    kaux_w[NSA_WINDOW:, AUX_SLOPE:AUX_SLOPE + 6] = kaux[:, AUX_SLOPE:AUX_SLOPE + 6]
    kaux_w[:NSA_WINDOW, AUX_PAD] = 1.0
    pieces = jnp.stack(_bf16_pieces(slopes2) * 2, axis=-1)
    qtab = jnp.zeros((N_HEADS, LANES), jnp.float32)
    qtab = qtab.at[:, AUX_SLOPE:AUX_SLOPE + 6].set(pieces).at[:, AUX_PAD].set(-MASK_BIG)
    qtab = jnp.pad(qtab.reshape(NSA_KV_HEADS, NSA_GROUP, LANES), ((0, 0), (0, 8 - NSA_GROUP), (0, 0)))
    return (n_cmp, n_blk, jnp.asarray(ovl_t), qtab, jnp.asarray(kaux, dtype=jnp.bfloat16),
            jnp.asarray(kaux_w, dtype=jnp.bfloat16))


def _nsa_attention(proj, cmp_kv, gates, slopes2, cast_ws=()):
    b, s, _ = proj.shape
    t = ATT_TILE
    n_steps = b * NSA_KV_HEADS * (s // t)
    chunks = [_cast_chunks(w.shape, n_steps) for w in cast_ws]
    assert n_steps >= 2
    any_spec = pl.BlockSpec(memory_space=pl.ANY)
    n_cmp, n_blk, ovl_t, qtab, kaux_s, kaux_w = _nsa_tables(s, slopes2)
    assert cmp_kv.shape[3] == LANES and s % SEL_CHUNK == 0 and NSA_WINDOW % t == 0
    qw = NSA_GROUP * HEAD_DIM
    rows = NSA_GROUP * t
    kv0 = ATT_WIDTH // HEAD_DIM

    def kv_spec(j):
        return pl.BlockSpec((1, s, HEAD_DIM), lambda bi, g, qi, j=j: (bi, 0, kv0 + j * NSA_KV_HEADS + g))

    def const(shape):
        return pl.BlockSpec(shape, lambda bi, g, qi: (0,) * len(shape))

    outs = pl.pallas_call(
        functools.partial(_nsa_attn_kernel, n_cmp=n_cmp, n_blk=n_blk, n_sel=min(N_SEL, n_blk),
                          n_cast=len(cast_ws)),
        grid=(b, NSA_KV_HEADS, s // t),
        in_specs=[pl.BlockSpec(memory_space=pltpu.SMEM),
                  pl.BlockSpec((1, t, qw), lambda bi, g, qi: (bi, qi, g)),
                  pl.BlockSpec((1, 1, 1, LANES, HEAD_DIM), lambda bi, g, qi: (bi, 0, g, 0, 0)),
                  pl.BlockSpec((1, 1, 1, LANES, HEAD_DIM), lambda bi, g, qi: (bi, 1, g, 0, 0)),
                  kv_spec(2), kv_spec(3), kv_spec(4), kv_spec(5),
                  pl.BlockSpec((1, t, LANES), lambda bi, g, qi: (bi, qi, g)),
                  const((LANES, LANES)),
                  pl.BlockSpec((1, 8, LANES), lambda bi, g, qi: (g, 0, 0)),
                  const((s, LANES)), const((NSA_WINDOW + s, LANES))] + [any_spec] * len(cast_ws),
        out_specs=[pl.BlockSpec((1, t, qw), lambda bi, g, qi: (bi, qi, g))] + [any_spec] * len(cast_ws),
        out_shape=[jax.ShapeDtypeStruct((b, s, ATT_WIDTH), jnp.bfloat16)]
        + [jax.ShapeDtypeStruct(w.shape, jnp.bfloat16) for w in cast_ws],
        scratch_shapes=[pltpu.VMEM((1, rows, HEAD_DIM), jnp.float32),
                        pltpu.VMEM((1, t, LANES), jnp.float32),
                        pltpu.VMEM((rows, HEAD_DIM), jnp.float32),
                        pltpu.VMEM((rows, 2 * HEAD_DIM), jnp.bfloat16),
                        pltpu.VMEM((rows, 2 * HEAD_DIM), jnp.bfloat16),
                        pltpu.VMEM((rows, s), jnp.bfloat16),
                        pltpu.VMEM((rows, NSA_WINDOW + t), jnp.bfloat16),
                        pltpu.VMEM((s, 2 * HEAD_DIM), jnp.bfloat16),
                        pltpu.VMEM((s, 2 * HEAD_DIM), jnp.bfloat16),
                        pltpu.VMEM((NSA_WINDOW + s, 2 * HEAD_DIM), jnp.bfloat16),
                        pltpu.VMEM((NSA_WINDOW + s, 2 * HEAD_DIM), jnp.bfloat16)]
        + [pltpu.VMEM((2,) + c, jnp.float32) for c in chunks]
        + [pltpu.VMEM((2,) + c, jnp.bfloat16) for c in chunks]
        + [pltpu.SemaphoreType.DMA((2,)), pltpu.SemaphoreType.DMA((2,))],
        compiler_params=_params(("arbitrary", "arbitrary", "arbitrary")),
        name="nsa_attention",
    )(slopes2, proj, cmp_kv, cmp_kv, proj, proj, proj, proj, gates, ovl_t, qtab, kaux_s, kaux_w, *cast_ws)
    return outs[0], list(outs[1:])


def _dil_attn_kernel(sl_ref, q_ref, *refs, window, has_prev):
    if has_prev:
        kp_ref, kc_ref, vp_ref, vc_ref, o_ref, lse_ref, bias_sc = refs
    else:
        kc_ref, vc_ref, o_ref, lse_ref, bias_sc = refs
        kp_ref, vp_ref = kc_ref, vc_ref
    t = ATT_TILE
    i = pl.program_id(2)

    @pl.when((pl.program_id(0) == 0) & (pl.program_id(1) == 0) & (i == 0))
    def _():
        row = lax.broadcasted_iota(jnp.int32, (t, 2 * t), 0)
        col = lax.broadcasted_iota(jnp.int32, (t, 2 * t), 1)
        dist = row - col + t
        band = (dist >= 0) & (dist <= window)
        distf = dist.astype(jnp.float32)
        for h in range(N_HEADS):
            bias = sl_ref[h] * distf
            bias_sc[0, h] = jnp.where(band & (col >= t), bias, -NEG_INF)
            bias_sc[1, h] = jnp.where(band, bias, -NEG_INF)

    var = jnp.minimum(i, 1)
    lane = lax.broadcasted_iota(jnp.int32, (t, LANES), 1)
    lse_all = jnp.zeros((t, LANES), jnp.float32)
    for h in range(N_HEADS):
        hs = slice(h * HEAD_DIM, (h + 1) * HEAD_DIM)
        q = q_ref[0, 0, :, hs]
        k = jnp.concatenate([kp_ref[0, 0, :, hs], kc_ref[0, 0, :, hs]], axis=0)
        v = jnp.concatenate([vp_ref[0, 0, :, hs], vc_ref[0, 0, :, hs]], axis=0)
        s = lax.dot_general(q, k, _NT, preferred_element_type=jnp.float32) - bias_sc[var, h]
        m = jnp.max(s, axis=-1, keepdims=True)
        e = jnp.exp2(s - m)
        l = jnp.sum(e, axis=-1, keepdims=True)
        o = jnp.dot(e.astype(v.dtype), v, preferred_element_type=jnp.float32) / l
        o_ref[0, 0, :, hs] = o.astype(o_ref.dtype)
        lse_all = jnp.where(lane == h, m + jnp.log2(l), lse_all)
    lse_ref[0, 0] = lse_all


def _dil_attention(q, kv, slopes2, win, r):
    b, _, ln, _ = q.shape
    t = ATT_TILE

    def cur(c):
        return pl.BlockSpec((1, 1, t, ATT_WIDTH), lambda bi, rho, i, c=c: (bi, rho, i, c))

    def prev(c):
        return pl.BlockSpec((1, 1, t, ATT_WIDTH), lambda bi, rho, i, c=c: (bi, rho, jnp.maximum(i - 1, 0), c))

    has_prev = ln > t
    kv_specs = [prev(0), cur(0), prev(1), cur(1)] if has_prev else [cur(0), cur(1)]
    return pl.pallas_call(
        functools.partial(_dil_attn_kernel, window=win // r, has_prev=has_prev),
        grid=(b, r, ln // t),
        in_specs=[pl.BlockSpec(memory_space=pltpu.SMEM), cur(0)] + kv_specs,
        out_specs=[pl.BlockSpec((1, 1, t, ATT_WIDTH), lambda bi, rho, i: (bi, rho, i, 0)),
                   pl.BlockSpec((1, 1, t, LANES), lambda bi, rho, i: (bi, rho, i, 0))],
        out_shape=[jax.ShapeDtypeStruct((b, r, ln, ATT_WIDTH), jnp.bfloat16),
                   jax.ShapeDtypeStruct((b, r, ln, LANES), jnp.float32)],
        scratch_shapes=[pltpu.VMEM((2, N_HEADS, t, 2 * t), jnp.float32)],
        compiler_params=_params(("arbitrary", "arbitrary", "arbitrary")),
        name="dilated_attention_r%d" % r,
    )(slopes2 * r, q, *([kv] * len(kv_specs)))


def _dil_merge_kernel(*refs, dilations):
    ng = len(dilations)
    o_refs, l_refs, out_ref, o_sc, l_sc = refs[:ng], refs[ng:2 * ng], refs[2 * ng], refs[2 * ng + 1], refs[2 * ng + 2]
    ts = out_ref.shape[1]
    ls = []
    for gi, r in enumerate(dilations):
        if r == 1:
            ls.append(l_refs[gi][0, 0])
            continue
        for rho in range(r):
            l_sc[gi, pl.ds(rho, ts // r, stride=r), :] = l_refs[gi][0, rho]
            for h in range(N_HEADS):
                o_sc[gi, h, pl.ds(rho, ts // r, stride=r), :] = (
                    o_refs[gi][0, rho, :, h * HEAD_DIM:(h + 1) * HEAD_DIM].astype(jnp.float32))
        ls.append(l_sc[gi])
    m = functools.reduce(jnp.maximum, ls)
    es = [jnp.exp2(l - m) for l in ls]
    den = functools.reduce(lambda a, b: a + b, es)
    ws = [e / den for e in es]
    for h in range(N_HEADS):
        hs = slice(h * HEAD_DIM, (h + 1) * HEAD_DIM)
        acc = None
        for gi, r in enumerate(dilations):
            og = o_refs[gi][0, 0, :, hs].astype(jnp.float32) if r == 1 else o_sc[gi, h]
            term = ws[gi][:, h:h + 1] * og
            acc = term if acc is None else acc + term
        out_ref[0, :, hs] = acc.astype(out_ref.dtype)


def _dil_merge(outs, lses, dilations):
    b, _, _, w = outs[0].shape
    s = outs[0].shape[1] * outs[0].shape[2]
    ts = min(256, s)
    ng = len(dilations)

    def spec(r, width):
        return pl.BlockSpec((1, r, ts // r, width), lambda i, j: (i, 0, j, 0))

    return pl.pallas_call(
        functools.partial(_dil_merge_kernel, dilations=dilations),
        grid=(b, s // ts),
        in_specs=[spec(r, w) for r in dilations] + [spec(r, LANES) for r in dilations],
        out_specs=pl.BlockSpec((1, ts, w), lambda i, j: (i, j, 0)),
        out_shape=jax.ShapeDtypeStruct((b, s, w), jnp.bfloat16),
        scratch_shapes=[pltpu.VMEM((ng, w // HEAD_DIM, ts, HEAD_DIM), jnp.float32),
                        pltpu.VMEM((ng, ts, LANES), jnp.float32)],
        compiler_params=_params(("parallel", "parallel")),
        name="dilated_merge",
    )(*outs, *lses)


ROW_TILE = 8


def _gather_rows(idx_ref, src_hbm, dst, sem):
    def body(i, carry):
        for j in range(ROW_TILE):
            pltpu.make_async_copy(src_hbm.at[idx_ref[0, 0, i * ROW_TILE + j]], dst.at[i, :, j], sem).start()
        return carry
    lax.fori_loop(0, dst.shape[0], body, 0)


def _gather_tiled_rows(idx_ref, src_hbm, dst, sem):
    def body(i, carry):
        for j in range(ROW_TILE):
            pltpu.make_async_copy(src_hbm.at[pl.ds(idx_ref[0, 0, i * ROW_TILE + j], 1)],
                                  dst.at[i, pl.ds(j, 1)], sem).start()
        return carry
    lax.fori_loop(0, dst.shape[0], body, 0)


def _wait_rows(dst, sem):
    pltpu.make_async_copy(dst, dst, sem).wait()


def _expert_kernel(be_ref, nu_ref, tokc_ref, tokn_ref, h_hbm, wg_ref, wu_ref, wd_ref, o_ref, xbuf, sem):
    del be_ref
    blk = pl.program_id(0)
    n_used = nu_ref[0]
    slot = blk % 2

    @pl.when(blk == 0)
    def _():
        _gather_rows(tokc_ref, h_hbm, xbuf.at[0], sem.at[0])

    @pl.when(blk + 1 < n_used)
    def _():
        _gather_rows(tokn_ref, h_hbm, xbuf.at[1 - slot], sem.at[1 - slot])

    @pl.when(blk < n_used)
    def _():
        _wait_rows(xbuf.at[slot], sem.at[slot])
        x = _load_row_tiles(xbuf.at[slot]).astype(jnp.bfloat16)
        gate = jnp.dot(x, wg_ref[0], preferred_element_type=jnp.float32)
        up = jnp.dot(x, wu_ref[0], preferred_element_type=jnp.float32)
        hid = (gate * jax.nn.sigmoid(gate) * up).astype(jnp.bfloat16)
        o_ref[...] = jnp.dot(hid, wd_ref[0], preferred_element_type=jnp.float32)

    @pl.when(blk >= n_used)
    def _():
        o_ref[...] = jnp.zeros(o_ref.shape, o_ref.dtype)


def _expert_blocks(h3, slot_tok, blk_e, n_used, w_gate, w_up, w_down):
    t, ct, _ = h3.shape
    d = ct * LANES
    nblk = blk_e.shape[0]
    de = w_gate.shape[2]
    tok = slot_tok.reshape(nblk, 1, MOE_BLOCK)
    grid_spec = pltpu.PrefetchScalarGridSpec(
        num_scalar_prefetch=2,
        grid=(nblk,),
        in_specs=[pl.BlockSpec((1, 1, MOE_BLOCK), lambda i, be, nu: (i, 0, 0), memory_space=pltpu.SMEM),
                  pl.BlockSpec((1, 1, MOE_BLOCK), lambda i, be, nu: (jnp.minimum(i + 1, nblk - 1), 0, 0),
                               memory_space=pltpu.SMEM),
                  pl.BlockSpec(memory_space=pl.ANY),
                  pl.BlockSpec((1, d, de), lambda i, be, nu: (be[i], 0, 0)),
                  pl.BlockSpec((1, d, de), lambda i, be, nu: (be[i], 0, 0)),
                  pl.BlockSpec((1, de, d), lambda i, be, nu: (be[i], 0, 0))],
        out_specs=pl.BlockSpec((MOE_BLOCK, d), lambda i, be, nu: (i, 0)),
        scratch_shapes=[pltpu.VMEM((2, MOE_BLOCK // ROW_TILE, ct, ROW_TILE, LANES), jnp.float32),
                        pltpu.SemaphoreType.DMA((2,))],
    )
    return pl.pallas_call(
        _expert_kernel,
        grid_spec=grid_spec,
        out_shape=jax.ShapeDtypeStruct((nblk * MOE_BLOCK, d), jnp.float32),
        compiler_params=_params(("arbitrary",)),
        name="moe_expert_blocks",
    )(blk_e, n_used, tok, tok, h3, w_gate, w_up, w_down)


def _combine_kernel(d0c_ref, d1c_ref, d0n_ref, d1n_ref, x_ref, w0_ref, w1_ref, g_ref, gn_ref, yo_hbm, o_ref,
                    buf, sem, *, out_norm):
    i = pl.program_id(0)
    n = pl.num_programs(0)
    slot = i % 2
    tt = x_ref.shape[0]

    @pl.when(i == 0)
    def _():
        _gather_tiled_rows(d0c_ref, yo_hbm, buf.at[0, 0], sem.at[0])
        _gather_tiled_rows(d1c_ref, yo_hbm, buf.at[0, 1], sem.at[0])

    @pl.when(i + 1 < n)
    def _():
        _gather_tiled_rows(d0n_ref, yo_hbm, buf.at[1 - slot, 0], sem.at[1 - slot])
        _gather_tiled_rows(d1n_ref, yo_hbm, buf.at[1 - slot, 1], sem.at[1 - slot])

    _wait_rows(buf.at[slot, 0], sem.at[slot])
    _wait_rows(buf.at[slot, 1], sem.at[slot])
    y = (w0_ref[...][:, 0:1] * buf[slot, 0].reshape(x_ref.shape)
         + w1_ref[...][:, 0:1] * buf[slot, 1].reshape(x_ref.shape))
    out = x_ref[...] + g_ref[0] * y
    o_ref[...] = _rms(out, gn_ref[...]) if out_norm else out


def _moe_combine(x2d, yo, dest, wts, gate, s, out_norm_g=None):
    t, d = x2d.shape
    out_norm = out_norm_g is not None
    gn = (out_norm_g if out_norm else jnp.ones((d,), jnp.float32)).reshape(1, d)
    tt = min(256, s)
    nt = t // tt
    d0 = dest[:, 0].reshape(nt, 1, tt)
    d1 = dest[:, 1].reshape(nt, 1, tt)
    w0 = jnp.broadcast_to(wts[:, 0:1], (t, LANES))
    w1 = jnp.broadcast_to(wts[:, 1:2], (t, LANES))
    b = gate.shape[0]
    cur = pl.BlockSpec((1, 1, tt), lambda i: (i, 0, 0), memory_space=pltpu.SMEM)
    nxt = pl.BlockSpec((1, 1, tt), lambda i: (jnp.minimum(i + 1, nt - 1), 0, 0), memory_space=pltpu.SMEM)
    return pl.pallas_call(
        functools.partial(_combine_kernel, out_norm=out_norm),
        grid=(nt,),
        in_specs=[cur, cur, nxt, nxt,
                  pl.BlockSpec((tt, d), lambda i: (i, 0)),
                  pl.BlockSpec((tt, LANES), lambda i: (i, 0)),
                  pl.BlockSpec((tt, LANES), lambda i: (i, 0)),
                  pl.BlockSpec((1, 1, d), lambda i: (i * tt // s, 0, 0)),
                  pl.BlockSpec((1, d), lambda i: (0, 0)),
                  pl.BlockSpec(memory_space=pl.ANY)],
        out_specs=pl.BlockSpec((tt, d), lambda i: (i, 0)),
        out_shape=jax.ShapeDtypeStruct((t, d), jnp.float32),
        scratch_shapes=[pltpu.VMEM((2, 2, tt // ROW_TILE, ROW_TILE, d), jnp.float32),
                        pltpu.SemaphoreType.DMA((2,))],
        compiler_params=_params(("arbitrary",)),
        name="moe_combine_residual",
    )(d0, d1, d0, d1, x2d, w0, w1, gate.reshape(b, 1, d), gn, yo)


def _dispatch_tables(idx):
    t = idx.shape[0]
    a = t * TOP_K
    flat_e = idx.reshape(a)
    onehot = (flat_e[:, None] == jnp.arange(N_EXPERTS, dtype=jnp.int32)[None, :]).astype(jnp.int32)
    csum = jnp.cumsum(onehot, axis=0)
    rank = jnp.take_along_axis(csum, flat_e[:, None], axis=1)[:, 0] - 1
    counts = csum[-1]
    padded = (counts + MOE_BLOCK - 1) // MOE_BLOCK * MOE_BLOCK
    pad_end = jnp.cumsum(padded)
    pad_start = pad_end - padded
    dest = pad_start[flat_e] + rank
    nblk = -(-(a + N_EXPERTS * MOE_BLOCK) // MOE_BLOCK)
    cap = nblk * MOE_BLOCK
    slot_tok = jnp.zeros((cap,), jnp.int32).at[dest].set(jnp.arange(a, dtype=jnp.int32) // TOP_K,
                                                         unique_indices=True)
    blk_start = jnp.arange(nblk, dtype=jnp.int32) * MOE_BLOCK
    blk_e = jnp.minimum(jnp.sum((pad_end[None, :] <= blk_start[:, None]).astype(jnp.int32), axis=1), N_EXPERTS - 1)
    n_used = (pad_end[-1:] // MOE_BLOCK).astype(jnp.int32)
    return dest.reshape(t, TOP_K).astype(jnp.int32), slot_tok, blk_e.astype(jnp.int32), n_used


def _moe_layer(x, g, sc, sh, gate, router_w, router_b, layer, w_gate, w_up, w_down, out_norm_g=None):
    b, s, d = x.shape
    t = b * s
    h, idx, wts = _norm_route(x, g, sc, sh, router_w, router_b)
    idx = idx.transpose(0, 2, 1).reshape(t, TOP_K)
    wts = wts.transpose(0, 2, 1).reshape(t, TOP_K)
    dest, slot_tok, blk_e, n_used = _dispatch_tables(idx)
    yo = _expert_blocks(h, slot_tok, blk_e + layer * N_EXPERTS, n_used, w_gate, w_up, w_down)
    return _moe_combine(x.reshape(t, d), yo, dest, wts, gate, s, out_norm_g).reshape(b, s, d)


def _nsa_layer(h, x, gate, slopes, j, w_in, w_phi1, w_phi2, phi_pos, w_out, cast_ws=()):
    b, s, d = h.shape
    t = b * s
    h2d = h.reshape(t, d)
    proj = _matmul(h2d, w_in, j, 0, NSA_QKV, jnp.bfloat16, scale=HEAD_DIM ** -0.5 * LOG2E, scale_cols=ATT_WIDTH)
    proj = proj.reshape(b, s, NSA_QKV)
    wg = w_in[j, :, NSA_QKV:].reshape(d, NSA_KV_HEADS, NSA_GATES)
    wg = jnp.pad(wg, ((0, 0), (0, 0), (0, LANES - NSA_GATES))).reshape(1, d, NSA_KV_HEADS * LANES)
    gates = _matmul(h2d, wg, 0, 0, NSA_KV_HEADS * LANES, jnp.float32, act="sigmoid")
    gates = gates.reshape(b, s, NSA_KV_HEADS * LANES)
    nch = s // CMP_STRIDE
    kv = proj[:, :, ATT_WIDTH:ATT_WIDTH + 2 * NSA_KV_WIDTH]
    kv = kv.reshape(b, nch, CMP_STRIDE, 2, NSA_KV_HEADS, HEAD_DIM).transpose(0, 3, 4, 1, 2, 5)
    cmp_kv = _compress(kv.reshape(b, 2, NSA_KV_HEADS, nch, CMP_STRIDE * HEAD_DIM), w_phi1, w_phi2, phi_pos)
    o, casted = _nsa_attention(proj, cmp_kv, gates, slopes * LOG2E, cast_ws)
    return _matmul_residual(o, w_out, j, x, gate), casted


def _dil_layer(x, g, sc, sh, gate, slopes, j, w_in, w_out):
    b, s, d = x.shape
    dilations = tuple(r for _, r in DIL_PAIRS)
    assert dilations[0] == 1
    hs = _norm_mod_streams(x, g, sc, sh, dilations)
    qs = _matmul_streams(hs[0].reshape(b, s, d), w_in, j, 0, ATT_WIDTH, dilations,
                         scale=HEAD_DIM ** -0.5 * LOG2E)
    outs, lses = [], []
    for gidx, (win, r) in enumerate(DIL_PAIRS):
        off = ATT_WIDTH * (1 + 2 * gidx)
        kv = _matmul(hs[gidx].reshape(b * s, d), w_in, j, off, 2 * ATT_WIDTH, jnp.bfloat16)
        o, lse = _dil_attention(qs[gidx], kv.reshape(b, r, s // r, 2 * ATT_WIDTH), slopes * LOG2E, win, r)
        outs.append(o)
        lses.append(lse)
    o = _dil_merge(outs, lses, dilations)
    return _matmul_residual(o, w_out, j, x, gate)


def kernel(x, c, ada_w, ada_b, norm_mix, norm_ffn, norm_final, nsa_w_in, nsa_w_phi1, nsa_w_phi2, nsa_phi_pos,
           nsa_w_out, dil_w_in, dil_w_out, router_w, router_b, exp_w_gate, exp_w_up, exp_w_down):
    depth = ada_w.shape[0]
    d = x.shape[-1]
    mod = _modulation(c, ada_w, ada_b)
    slopes = 2.0 ** (-ALIBI_MAX_BIAS * jnp.arange(1, N_HEADS + 1, dtype=jnp.float32) / N_HEADS)
    experts = (exp_w_gate, exp_w_up, exp_w_down)
    for i in range(depth):
        sh_m, sc_m, g_m, sh_f, sc_f, g_f = [mod[i, :, k * d:(k + 1) * d] for k in range(6)]
        j = i // 2
        if i % 2 == 0:
            h = _norm_mod(x, norm_mix[i], sc_m, sh_m, jnp.bfloat16)
            cast_ws = [w.reshape(-1, w.shape[-1]) for w in experts] if i == 0 else ()
            x, casted = _nsa_layer(h, x, g_m, slopes, j, nsa_w_in, nsa_w_phi1[j], nsa_w_phi2[j], nsa_phi_pos[j],
                                   nsa_w_out, cast_ws)
            if i == 0:
                e_gate, e_up, e_down = [c.reshape((-1,) + w.shape[2:]) for c, w in zip(casted, experts)]
        else:
            x = _dil_layer(x, norm_mix[i], sc_m, sh_m, g_m, slopes, j, dil_w_in, dil_w_out)
        x = _moe_layer(x, norm_ffn[i], sc_f, sh_f, g_f, router_w, router_b, i, e_gate, e_up, e_down,
                       norm_final if i == depth - 1 else None)
    return x
```

```python
import functools

import jax
import jax.numpy as jnp
import numpy as np
from jax import lax
from jax.experimental import pallas as pl
from jax.experimental.pallas import tpu as pltpu

HEAD_DIM = 128
N_HEADS = 16
ATT_WIDTH = N_HEADS * HEAD_DIM
ALIBI_MAX_BIAS = 8.0

NSA_KV_HEADS = 4
NSA_GROUP = N_HEADS // NSA_KV_HEADS
NSA_KV_WIDTH = NSA_KV_HEADS * HEAD_DIM
CMP_BLOCK = 32
CMP_STRIDE = 16
SEL_BLOCK = 64
N_SEL = 16
NSA_WINDOW = 512
FORCE_SCORE = 1.0e6
NSA_QKV = ATT_WIDTH + 6 * NSA_KV_WIDTH
NSA_GATES = 3 * NSA_GROUP

DIL_PAIRS = ((128, 1), (512, 4), (2048, 16))
DIL_PROJ = ATT_WIDTH * (1 + 2 * len(DIL_PAIRS))

N_EXPERTS = 16
N_EXPERT_GROUPS = 4
EXPERTS_PER_GROUP = N_EXPERTS // N_EXPERT_GROUPS
TOP_K = 2
MOE_BLOCK = 256

RMS_EPS = 1e-6
NEG_INF = -1.0e30

LANES = 128
ATT_TILE = 128
VMEM_LIMIT = 56 * 1024 * 1024

_HI = lax.Precision.HIGHEST
_NT = (((1,), (1,)), ((), ()))


def _params(sem, vmem=VMEM_LIMIT):
    return pltpu.CompilerParams(dimension_semantics=sem, vmem_limit_bytes=vmem)


def _mod_kernel(c_ref, w_ref, b_ref, o_ref):
    c = c_ref[...]
    cond = c * jax.nn.sigmoid(c)
    o_ref[0] = jnp.dot(cond, w_ref[0], precision=_HI, preferred_element_type=jnp.float32) + b_ref[0]


def _modulation(c, ada_w, ada_b):
    depth, d, n = ada_w.shape
    b = c.shape[0]
    tn = 512
    return pl.pallas_call(
        _mod_kernel,
        grid=(depth, n // tn),
        in_specs=[pl.BlockSpec((b, d), lambda i, j: (0, 0)),
                  pl.BlockSpec((1, d, tn), lambda i, j: (i, 0, j)),
                  pl.BlockSpec((1, 1, tn), lambda i, j: (i, 0, j))],
        out_specs=pl.BlockSpec((1, b, tn), lambda i, j: (i, 0, j)),
        out_shape=jax.ShapeDtypeStruct((depth, b, n), jnp.float32),
        compiler_params=_params(("parallel", "parallel")),
        name="adaln_modulation",
    )(c, ada_w, ada_b.reshape(depth, 1, n))


def _rms(x, g):
    return x * lax.rsqrt(jnp.mean(x * x, axis=-1, keepdims=True) + RMS_EPS) * g


def _norm_mod_kernel(x_ref, g_ref, sc_ref, sh_ref, o_ref):
    h = _rms(x_ref[0], g_ref[...]) * (1.0 + sc_ref[0]) + sh_ref[0]
    o_ref[0] = h.astype(o_ref.dtype)


def _norm_mod(x, g, sc, sh, out_dtype):
    b, s, d = x.shape
    ts = min(256, s)
    return pl.pallas_call(
        _norm_mod_kernel,
        grid=(b, s // ts),
        in_specs=[pl.BlockSpec((1, ts, d), lambda i, j: (i, j, 0)),
                  pl.BlockSpec((1, d), lambda i, j: (0, 0)),
                  pl.BlockSpec((1, 1, d), lambda i, j: (i, 0, 0)),
                  pl.BlockSpec((1, 1, d), lambda i, j: (i, 0, 0))],
        out_specs=pl.BlockSpec((1, ts, d), lambda i, j: (i, j, 0)),
        out_shape=jax.ShapeDtypeStruct((b, s, d), out_dtype),
        compiler_params=_params(("parallel", "parallel")),
        name="rmsnorm_adaln",
    )(x, g.reshape(1, d), sc.reshape(b, 1, d), sh.reshape(b, 1, d))


def _store_streams(val, src_sc, out_refs, dilations):
    n = val.shape[0]
    for c in range(src_sc.shape[0]):
        src_sc[c] = val[:, c * LANES:(c + 1) * LANES]
    for r, ref in zip(dilations, out_refs):
        if r == 1:
            ref[0, 0] = val.astype(ref.dtype)
            continue
        for rho in range(r):
            for c in range(src_sc.shape[0]):
                ref[0, rho, :, c * LANES:(c + 1) * LANES] = (
                    src_sc[c, pl.ds(rho, n // r, stride=r), :].astype(ref.dtype))


def _norm_mod_streams_kernel(x_ref, g_ref, sc_ref, sh_ref, *refs, dilations):
    out_refs, h_sc = refs[:-1], refs[-1]
    h = _rms(x_ref[0], g_ref[...]) * (1.0 + sc_ref[0]) + sh_ref[0]
    _store_streams(h, h_sc, out_refs, dilations)


def _norm_mod_streams(x, g, sc, sh, dilations):
    b, s, d = x.shape
    ts = min(256, s)
    return pl.pallas_call(
        functools.partial(_norm_mod_streams_kernel, dilations=dilations),
        grid=(b, s // ts),
        in_specs=[pl.BlockSpec((1, ts, d), lambda i, j: (i, j, 0)),
                  pl.BlockSpec((1, d), lambda i, j: (0, 0)),
                  pl.BlockSpec((1, 1, d), lambda i, j: (i, 0, 0)),
                  pl.BlockSpec((1, 1, d), lambda i, j: (i, 0, 0))],
        out_specs=[pl.BlockSpec((1, r, ts // r, d), lambda i, j: (i, 0, j, 0)) for r in dilations],
        out_shape=[jax.ShapeDtypeStruct((b, r, s // r, d), jnp.bfloat16) for r in dilations],
        scratch_shapes=[pltpu.VMEM((d // LANES, ts, LANES), jnp.float32)],
        compiler_params=_params(("parallel", "parallel")),
        name="rmsnorm_adaln_streams",
    )(x, g.reshape(1, d), sc.reshape(b, 1, d), sh.reshape(b, 1, d))


def _pair_max(vals):
    out = None
    for a in range(len(vals)):
        for b in range(a + 1, len(vals)):
            s = vals[a] + vals[b]
            out = s if out is None else jnp.maximum(out, s)
    return out


def _store_row_major(ref, val):
    rows = val.shape[0]
    c_tiles = val.shape[1] // LANES
    for c in range(c_tiles):
        ref[pl.ds(c, rows, stride=c_tiles), :] = val[:, c * LANES:(c + 1) * LANES]


def _load_row_tiles(buf):
    n, c_tiles = buf.shape[0], buf.shape[1]
    return jnp.concatenate([buf[:, c].reshape(n * ROW_TILE, LANES) for c in range(c_tiles)], axis=1)


def _norm_route_kernel(x_ref, g_ref, sc_ref, sh_ref, rw_ref, rb_ref, h_ref, idx_ref, wt_ref):
    h = _rms(x_ref[0], g_ref[...]) * (1.0 + sc_ref[0]) + sh_ref[0]
    _store_row_major(h_ref, h)

    logits = lax.dot_general(rw_ref[...], h, _NT, precision=_HI, preferred_element_type=jnp.float32)
    scores = jax.nn.sigmoid(logits)
    biased = scores + rb_ref[...][:, 0:1]
    rows = [biased[e:e + 1, :] for e in range(N_EXPERTS)]
    srow = [scores[e:e + 1, :] for e in range(N_EXPERTS)]
    grp = [_pair_max(rows[q * EXPERTS_PER_GROUP:(q + 1) * EXPERTS_PER_GROUP]) for q in range(N_EXPERT_GROUPS)]
    best_v = grp[0]
    best = jnp.zeros(best_v.shape, jnp.int32)
    for q in range(1, N_EXPERT_GROUPS):
        take = grp[q] > best_v
        best = jnp.where(take, q, best)
        best_v = jnp.where(take, grp[q], best_v)
    v1 = jnp.full(best_v.shape, NEG_INF, jnp.float32)
    i1 = jnp.zeros(best_v.shape, jnp.int32)
    for e in range(N_EXPERTS):
        cand = jnp.where(best == e // EXPERTS_PER_GROUP, rows[e], NEG_INF)
        take = cand > v1
        i1 = jnp.where(take, e, i1)
        v1 = jnp.where(take, cand, v1)
    v2 = jnp.full(best_v.shape, NEG_INF, jnp.float32)
    i2 = jnp.zeros(best_v.shape, jnp.int32)
    for e in range(N_EXPERTS):
        cand = jnp.where(best == e // EXPERTS_PER_GROUP, jnp.where(i1 == e, NEG_INF, rows[e]), NEG_INF)
        take = cand > v2
        i2 = jnp.where(take, e, i2)
        v2 = jnp.where(take, cand, v2)
    w1 = jnp.zeros(best_v.shape, jnp.float32)
    w2 = jnp.zeros(best_v.shape, jnp.float32)
    for e in range(N_EXPERTS):
        w1 = jnp.where(i1 == e, srow[e], w1)
        w2 = jnp.where(i2 == e, srow[e], w2)
    tot = w1 + w2
    idx_ref[0] = jnp.concatenate([i1, i2], axis=0)
    wt_ref[0] = jnp.concatenate([w1 / tot, w2 / tot], axis=0)


def _norm_route(x, g, sc, sh, router_w, router_b):
    b, s, d = x.shape
    ts = min(256, s)
    ct = d // LANES
    rwt = router_w.T
    rb = jnp.broadcast_to(router_b.reshape(N_EXPERTS, 1), (N_EXPERTS, LANES))
    h, idx, wts = pl.pallas_call(
        _norm_route_kernel,
        grid=(b, s // ts),
        in_specs=[pl.BlockSpec((1, ts, d), lambda i, j: (i, j, 0)),
                  pl.BlockSpec((1, d), lambda i, j: (0, 0)),
                  pl.BlockSpec((1, 1, d), lambda i, j: (i, 0, 0)),
                  pl.BlockSpec((1, 1, d), lambda i, j: (i, 0, 0)),
                  pl.BlockSpec((N_EXPERTS, d), lambda i, j: (0, 0)),
                  pl.BlockSpec((N_EXPERTS, LANES), lambda i, j: (0, 0))],
        out_specs=[pl.BlockSpec((ts * ct, LANES), lambda i, j: (i * (s // ts) + j, 0)),
                   pl.BlockSpec((1, TOP_K, ts), lambda i, j: (i, 0, j)),
                   pl.BlockSpec((1, TOP_K, ts), lambda i, j: (i, 0, j))],
        out_shape=[jax.ShapeDtypeStruct((b * s * ct, LANES), jnp.float32),
                   jax.ShapeDtypeStruct((b, TOP_K, s), jnp.int32),
                   jax.ShapeDtypeStruct((b, TOP_K, s), jnp.float32)],
        compiler_params=_params(("parallel", "parallel")),
        name="rmsnorm_adaln_router",
    )(x, g.reshape(1, d), sc.reshape(b, 1, d), sh.reshape(b, 1, d), rwt, rb)
    return h.reshape(b * s, ct, LANES), idx, wts


def _weight_tile(w_ref, wbf_sc, first, scale, scale_tiles):
    @pl.when(first)
    def _():
        w = w_ref[0]
        if scale_tiles:
            w = w * jnp.where(pl.program_id(0) < scale_tiles, scale, 1.0)
        wbf_sc[...] = w.astype(jnp.bfloat16)
    return wbf_sc[...]


def _weight_spec(layer, col0, k, tn, rank):
    assert col0 % tn == 0
    if rank == 2:
        return pl.BlockSpec((1, k, tn), lambda j, i: (layer, 0, col0 // tn + j))
    return pl.BlockSpec((1, k, tn), lambda j, bi, i: (layer, 0, col0 // tn + j))


def _mm_kernel(a_ref, w_ref, o_ref, wbf_sc, *, act, scale, scale_tiles):
    w = _weight_tile(w_ref, wbf_sc, pl.program_id(1) == 0, scale, scale_tiles)
    acc = jnp.dot(a_ref[...], w, preferred_element_type=jnp.float32)
    if act == "sigmoid":
        acc = jax.nn.sigmoid(acc)
    o_ref[...] = acc.astype(o_ref.dtype)


def _matmul(a, w, layer, col0, n, out_dtype, act=None, scale=1.0, scale_cols=0, tm=2048, tn=1024):
    m, k = a.shape
    tn = min(tn, n)
    tm = min(tm, m)
    assert scale_cols % tn == 0
    return pl.pallas_call(
        functools.partial(_mm_kernel, act=act, scale=scale, scale_tiles=scale_cols // tn),
        grid=(n // tn, m // tm),
        in_specs=[pl.BlockSpec((tm, k), lambda j, i: (i, 0)),
                  _weight_spec(layer, col0, k, tn, 2)],
        out_specs=pl.BlockSpec((tm, tn), lambda j, i: (i, j)),
        out_shape=jax.ShapeDtypeStruct((m, n), out_dtype),
        scratch_shapes=[pltpu.VMEM((k, tn), jnp.bfloat16)],
        compiler_params=_params(("arbitrary", "arbitrary")),
        name="matmul" if act is None else "matmul_" + act,
    )(a, w)


def _first_row_step():
    return (pl.program_id(1) == 0) & (pl.program_id(2) == 0)


def _mm_streams_kernel(a_ref, w_ref, *refs, dilations, scale, scale_tiles):
    out_refs, acc_sc, wbf_sc = refs[:-2], refs[-2], refs[-1]
    w = _weight_tile(w_ref, wbf_sc, _first_row_step(), scale, scale_tiles)
    acc = jnp.dot(a_ref[0], w, preferred_element_type=jnp.float32)
    _store_streams(acc, acc_sc, out_refs, dilations)


def _matmul_streams(a, w, layer, col0, n, dilations, scale=1.0, tm=512, tn=1024):
    b, s, k = a.shape
    tn = min(tn, n)
    tm = min(tm, s)
    return pl.pallas_call(
        functools.partial(_mm_streams_kernel, dilations=dilations, scale=scale,
                          scale_tiles=n // tn if scale != 1.0 else 0),
        grid=(n // tn, b, s // tm),
        in_specs=[pl.BlockSpec((1, tm, k), lambda j, bi, i: (bi, i, 0)),
                  _weight_spec(layer, col0, k, tn, 3)],
        out_specs=[pl.BlockSpec((1, r, tm // r, tn), lambda j, bi, i: (bi, 0, i, j)) for r in dilations],
        out_shape=[jax.ShapeDtypeStruct((b, r, s // r, n), jnp.bfloat16) for r in dilations],
        scratch_shapes=[pltpu.VMEM((tn // LANES, tm, LANES), jnp.float32), pltpu.VMEM((k, tn), jnp.bfloat16)],
        compiler_params=_params(("arbitrary", "arbitrary", "arbitrary")),
        name="matmul_streams",
    )(a, w)


def _mm_residual_kernel(a_ref, w_ref, x_ref, g_ref, o_ref, wbf_sc):
    w = _weight_tile(w_ref, wbf_sc, _first_row_step(), 1.0, 0)
    acc = jnp.dot(a_ref[0], w, preferred_element_type=jnp.float32)
    o_ref[0] = x_ref[0] + g_ref[0] * acc


def _matmul_residual(a, w, layer, x, gate, tm=512, tn=1024):
    b, s, k = a.shape
    n = w.shape[2]
    tn = min(tn, n)
    tm = min(tm, s)
    return pl.pallas_call(
        _mm_residual_kernel,
        grid=(n // tn, b, s // tm),
        in_specs=[pl.BlockSpec((1, tm, k), lambda j, bi, i: (bi, i, 0)),
                  _weight_spec(layer, 0, k, tn, 3),
                  pl.BlockSpec((1, tm, tn), lambda j, bi, i: (bi, i, j)),
                  pl.BlockSpec((1, 1, tn), lambda j, bi, i: (bi, 0, j))],
        out_specs=pl.BlockSpec((1, tm, tn), lambda j, bi, i: (bi, i, j)),
        out_shape=jax.ShapeDtypeStruct((b, s, n), jnp.float32),
        scratch_shapes=[pltpu.VMEM((k, tn), jnp.bfloat16)],
        compiler_params=_params(("arbitrary", "arbitrary", "arbitrary")),
        name="matmul_gated_residual",
    )(a, w, x, gate.reshape(b, 1, n))


def _gelu_tanh(x):
    return 0.5 * x * (1.0 + jnp.tanh(0.7978845608028654 * (x + 0.044715 * (x * x * x))))


def _compress_kernel(c_ref, pa_ref, pb_ref, wa_ref, wb_ref, w2_ref, o_ref):
    c = c_ref[0, 0, 0].astype(jnp.float32)
    lo = (c + pa_ref[0]).astype(jnp.bfloat16)
    hi = (c + pb_ref[0]).astype(jnp.bfloat16)
    ha = jnp.dot(lo, wa_ref[0], preferred_element_type=jnp.float32)
    hb = jnp.dot(hi, wb_ref[0], preferred_element_type=jnp.float32)
    n = ha.shape[0]
    hid = _gelu_tanh(ha + pltpu.roll(hb, n - 1, 0))
    out = jnp.dot(hid.astype(jnp.bfloat16), w2_ref[0], preferred_element_type=jnp.float32)
    row = lax.broadcasted_iota(jnp.int32, out.shape, 0)
    o_ref[0, 0, 0] = jnp.where(row < n - 1, out, 0.0).astype(o_ref.dtype)


def _compress(kv_chunks, w_phi1, w_phi2, phi_pos):
    b, two, hkv, nch, width = kv_chunks.shape
    dh = HEAD_DIM
    half = CMP_BLOCK // 2
    w1 = w_phi1.reshape(2, 2, half * dh, dh).astype(jnp.bfloat16)
    pos = phi_pos.reshape(2, 2, 1, half * dh)
    return pl.pallas_call(
        _compress_kernel,
        grid=(b, two, hkv),
        in_specs=[pl.BlockSpec((1, 1, 1, nch, width), lambda i, j, g: (i, j, g, 0, 0)),
                  pl.BlockSpec((1, 1, width), lambda i, j, g: (j, 0, 0)),
                  pl.BlockSpec((1, 1, width), lambda i, j, g: (j, 0, 0)),
                  pl.BlockSpec((1, width, dh), lambda i, j, g: (j, 0, 0)),
                  pl.BlockSpec((1, width, dh), lambda i, j, g: (j, 0, 0)),
                  pl.BlockSpec((1, dh, dh), lambda i, j, g: (j, 0, 0))],
        out_specs=pl.BlockSpec((1, 1, 1, nch, dh), lambda i, j, g: (i, j, g, 0, 0)),
        out_shape=jax.ShapeDtypeStruct((b, two, hkv, nch, dh), jnp.bfloat16),
        compiler_params=_params(("parallel", "parallel", "parallel")),
        name="nsa_compress",
    )(kv_chunks, pos[:, 0], pos[:, 1], w1[:, 0], w1[:, 1], w_phi2.astype(jnp.bfloat16))


MASK_BIG = 2.0 ** 100
AUX_SLOPE = 64
AUX_PAD = 70
SEL_CHUNK = 512
LOG2E = 1.4426950408889634


def _nsa_attn_kernel(sl_ref, q_ref, kc_ref, vc_ref, ks_ref, vs_ref, kw_ref, vw_ref, gt_ref, ovl_ref, qtab_ref,
                     kauxs_ref, kauxw_ref, *refs, n_cmp, n_blk, n_sel, cast_place, n_cast_dst):
    n_cast = len(cast_place)
    cast_src, refs = refs[:n_cast], refs[n_cast:]
    o_ref, refs = refs[0], refs[1:]
    cast_dst, refs = refs[:n_cast_dst], refs[n_cast_dst:]
    (ocmp_sc, maskq_sc, owin_sc, qaug_sc, qaugw_sc, p_sc, pw_sc, kaug_s, vaug_s, kaug_w, vaug_w), refs = refs[:11], refs[11:]
    cast_in, cast_out, (cast_in_sem, cast_out_sem) = refs[:n_cast], refs[n_cast:2 * n_cast], refs[2 * n_cast:]
    t = ATT_TILE
    dh = HEAD_DIM
    s_len = ks_ref.shape[1]
    wpad = NSA_WINDOW
    g = pl.program_id(1)
    qt = pl.program_id(2)
    qstart = pl.multiple_of(qt * t, t)

    if n_cast:
        step = (pl.program_id(0) * pl.num_programs(1) + g) * pl.num_programs(2) + qt
        n_steps = pl.num_programs(0) * pl.num_programs(1) * pl.num_programs(2)
        drain_cast = _cast_side_job(step, n_steps, cast_src, cast_dst, cast_place, cast_in, cast_out, cast_in_sem,
                                    cast_out_sem)

    @pl.when(qt == 0)
    def _():
        ones = jnp.ones((s_len, dh), jnp.bfloat16)
        kaug_s[:, :dh] = ks_ref[0]
        kaug_s[:, dh:] = kauxs_ref[...]
        vaug_s[:, :dh] = vs_ref[0]
        vaug_s[:, dh:] = ones
        kaug_w[:wpad, :dh] = jnp.zeros((wpad, dh), jnp.bfloat16)
        kaug_w[wpad:, :dh] = kw_ref[0]
        kaug_w[:, dh:] = kauxw_ref[...]
        vaug_w[:wpad, :dh] = jnp.zeros((wpad, dh), jnp.bfloat16)
        vaug_w[wpad:, :dh] = vw_ref[0]
        vaug_w[:wpad, dh:] = jnp.ones((wpad, dh), jnp.bfloat16)
        vaug_w[wpad:, dh:] = ones

    slopes = [sl_ref[g * NSA_GROUP + r] for r in range(NSA_GROUP)]
    row = lax.broadcasted_iota(jnp.int32, (t, LANES), 0)
    col = lax.broadcasted_iota(jnp.int32, (t, LANES), 1)
    prows = [slice(pair * 2 * t, (pair + 1) * 2 * t) for pair in range(NSA_GROUP // 2)]
    qtab = qtab_ref[0]

    def compress_and_select(qv, tile, par):
        qs = tile * t
        tpos = qs + row
        q4 = jnp.concatenate([qv[:, r * dh:(r + 1) * dh] for r in range(NSA_GROUP)], axis=0)
        kc = kc_ref[0, 0, 0]
        vc = vc_ref[0, 0, 0]
        s = lax.dot_general(q4, kc, _NT, preferred_element_type=jnp.float32)
        visible = (col * CMP_STRIDE + (CMP_BLOCK - 1) <= tpos) & (col < n_cmp)
        dist_c = tpos.astype(jnp.float32) - (col.astype(jnp.float32) * CMP_STRIDE + (CMP_BLOCK - 1) / 2.0)
        psum = jnp.zeros((t, LANES), jnp.float32)
        for r in range(NSA_GROUP):
            rows = slice(r * t, (r + 1) * t)
            sr = jnp.where(visible, s[rows] - slopes[r] * dist_c, NEG_INF)
            e = jnp.where(visible, jnp.exp2(sr - jnp.max(sr, axis=-1, keepdims=True)), 0.0)
            p = e / jnp.maximum(jnp.sum(e, axis=-1, keepdims=True), 1e-30)
            psum = psum + p
            ocmp_sc[par, rows] = jnp.dot(p.astype(vc.dtype), vc, preferred_element_type=jnp.float32)
        nb = -(-n_blk // 8) * 8
        imp_t = lax.dot_general(ovl_ref[...], psum, _NT, precision=_HI, preferred_element_type=jnp.float32)[:nb]
        jrow = lax.broadcasted_iota(jnp.int32, (nb, t), 0)
        tpos_t = qs + lax.broadcasted_iota(jnp.int32, (nb, t), 1)
        cur = jnp.right_shift(tpos_t, int(np.log2(SEL_BLOCK)))
        forced = (jrow == 0) | (jrow == cur) | (jrow == cur - 1)
        score = jnp.where(forced, FORCE_SCORE, jnp.where(jrow * SEL_BLOCK <= tpos_t, imp_t, -1.0))
        rank = jnp.zeros((nb, t), jnp.float32)
        for j in range(n_blk):
            cj = score[j:j + 1, :]
            ahead = (cj > score) | ((cj == score) & (jrow > j))
            rank = rank + jnp.where(ahead, 1.0, 0.0)
        mask_t = jnp.where((rank < n_sel) & (jrow < n_blk), 0.0, -MASK_BIG)
        maskq_sc[par] = jnp.concatenate([mask_t, jnp.zeros((LANES - nb, t), jnp.float32)], axis=0).T

    par = 0
    q = q_ref[0]

    for r in range(NSA_GROUP):
        rows = slice(r * t, (r + 1) * t)
        qaugw_sc[rows, :dh] = q[:, r * dh:(r + 1) * dh]
        qaugw_sc[rows, dh:] = jnp.broadcast_to(qtab[r:r + 1, :], (t, LANES)).astype(jnp.bfloat16)
    wk = wpad + t
    nw = wk // LANES
    kw = kaug_w[pl.ds(qstart, wk), :]
    vw = vaug_w[pl.ds(qstart, wk), :]
    sws = [lax.dot_general(qaugw_sc[prow], kw, _NT, preferred_element_type=jnp.float32) for prow in prows]

    compress_and_select(q, qt, par)

    mask_q = maskq_sc[par]
    for r in range(NSA_GROUP):
        rows = slice(r * t, (r + 1) * t)
        qaug_sc[rows, :dh] = q[:, r * dh:(r + 1) * dh]
        qaug_sc[rows, dh:] = (mask_q + qtab[r:r + 1, :]).astype(jnp.bfloat16)

    for pair, prow in enumerate(prows):
        sw = sws[pair]
        for r2 in range(2):
            rows = slice((2 * pair + r2) * t, (2 * pair + r2 + 1) * t)
            tiles = [sw[r2 * t:(r2 + 1) * t, j * LANES:(j + 1) * LANES] for j in range(nw)]
            tiles[0] = jnp.where(col > row, tiles[0], NEG_INF)
            tiles[-1] = jnp.where(col <= row, tiles[-1], NEG_INF)
            mx = functools.reduce(jnp.maximum, tiles)
            m = jnp.broadcast_to(jnp.max(mx, axis=-1, keepdims=True), (t, LANES))
            for j in range(nw):
                pw_sc[rows, j * LANES:(j + 1) * LANES] = jnp.exp2(tiles[j] - m).astype(jnp.bfloat16)
        ow = jnp.dot(pw_sc[prow], vw, preferred_element_type=jnp.float32)
        owin_sc[prow] = ow[:, :dh] / ow[:, dh:]

    ch = SEL_CHUNK
    nl = ch // LANES
    n_full = qt // (ch // t)
    gt = gt_ref[0]
    cmr = col - row

    def selected(k):
        nk = (k + 1) * ch
        kk = kaug_s[:nk, :]
        vv = vaug_s[:nk, :]
        sks = [lax.dot_general(qaug_sc[prow], kk, _NT, preferred_element_type=jnp.float32) for prow in prows]
        for pair, prow in enumerate(prows):
            sk = sks[pair]
            for r2 in range(2):
                rows = slice((2 * pair + r2) * t, (2 * pair + r2 + 1) * t)
                tiles = [sk[r2 * t:(r2 + 1) * t, j * LANES:(j + 1) * LANES] for j in range(nk // LANES)]
                for j in range(k * nl, (k + 1) * nl):
                    tiles[j] = jnp.where(cmr <= qstart - j * LANES, tiles[j], NEG_INF)
                mx = functools.reduce(jnp.maximum, tiles)
                m = jnp.broadcast_to(jnp.max(mx, axis=-1, keepdims=True), (t, LANES))
                for j in range(nk // LANES):
                    p_sc[rows, j * LANES:(j + 1) * LANES] = jnp.exp2(tiles[j] - m).astype(jnp.bfloat16)
            pv = jnp.dot(p_sc[prow, :nk], vv, preferred_element_type=jnp.float32)
            for r2 in range(2):
                r = 2 * pair + r2
                rows = slice(r * t, (r + 1) * t)
                orow = slice(r2 * t, (r2 + 1) * t)
                o = (gt[:, 3 * r:3 * r + 1] * ocmp_sc[par, rows]
                     + gt[:, 3 * r + 1:3 * r + 2] * (pv[orow, :dh] / pv[orow, dh:])
                     + gt[:, 3 * r + 2:3 * r + 3] * owin_sc[rows])
                o_ref[0, :, r * dh:(r + 1) * dh] = o.astype(o_ref.dtype)

    for k in range(s_len // ch):
        pl.when(n_full == k)(functools.partial(selected, k))
    if cast_place:
        drain_cast()


def _cast_chunks(shape, n_steps):
    r, c = shape
    for n_c in (1, 2, 4, 8, 16):
        if n_steps % n_c == 0 and c % (n_c * LANES) == 0 and r % (n_steps // n_c) == 0:
            rows = r // (n_steps // n_c)
            if rows % 16 == 0:
                return rows, c // n_c
    raise ValueError("no tile-aligned split of %s into %d chunks" % (shape, n_steps))


def _cast_side_job(n, n_steps, srcs, dsts, place, inbufs, outbufs, in_sem, out_sem):
    slot = n % 2

    def chunk(k, step, col0=0):
        rows, cols = inbufs[k].shape[1:]
        n_c = srcs[k].shape[1] // cols
        return (pl.ds(pl.multiple_of((step // n_c) * rows, 16), rows),
                pl.ds(pl.multiple_of(col0 + (step % n_c) * cols, LANES), cols))

    def in_copy(k, step, sl):
        return pltpu.make_async_copy(srcs[k].at[chunk(k, step)], inbufs[k].at[sl], in_sem.at[sl])

    def out_copy(k, step, sl):
        dst, col0 = place[k]
        return pltpu.make_async_copy(outbufs[k].at[sl], dsts[dst].at[chunk(k, step, col0)], out_sem.at[sl])

    ks = range(len(srcs))

    @pl.when(n == 0)
    def _():
        for k in ks:
            in_copy(k, 0, 0).start()
            outbufs[k][...] = jnp.zeros(outbufs[k].shape, outbufs[k].dtype)
            out_copy(k, 0, 0).start()
            out_copy(k, 1, 1).start()

    for k in ks:
        in_copy(k, n, slot).wait()
    nxt = jnp.minimum(n + 1, n_steps - 1)
    for k in ks:
        in_copy(k, nxt, 1 - slot).start()
    for k in ks:
        out_copy(k, n, slot).wait()

    for k in ks:
        outbufs[k][slot] = inbufs[k][slot].astype(outbufs[k].dtype)
    for k in ks:
        out_copy(k, n, slot).start()

    def drain():
        @pl.when(n == n_steps - 1)
        def _():
            for k in ks:
                in_copy(k, n, 1 - slot).wait()
                out_copy(k, n, 1 - slot).wait()
                out_copy(k, n, slot).wait()

    return drain


def _bf16_pieces(x):
    a0 = x.astype(jnp.bfloat16).astype(jnp.float32)
    a1 = (x - a0).astype(jnp.bfloat16).astype(jnp.float32)
    a2 = (x - a0 - a1).astype(jnp.bfloat16).astype(jnp.float32)
    return [a0, a1, a2]


def _nsa_tables(s, slopes2):
    n_cmp = s // CMP_STRIDE - CMP_BLOCK // CMP_STRIDE + 1
    n_blk = s // SEL_BLOCK
    assert n_cmp < LANES and n_blk <= AUX_SLOPE
    bj = np.arange(LANES)[:, None]
    cn = np.arange(LANES)[None, :]
    ovl_t = np.clip(np.minimum(cn * CMP_STRIDE + CMP_BLOCK, (bj + 1) * SEL_BLOCK)
                    - np.maximum(cn * CMP_STRIDE, bj * SEL_BLOCK), 0, None) / CMP_STRIDE
    ovl_t = np.where((cn < n_cmp) & (bj < n_blk), ovl_t, 0.0).astype(np.float32)
    pos = np.arange(s)
    kaux = np.zeros((s, LANES), np.float32)
    kaux[pos, pos // SEL_BLOCK] = 1.0
    kaux[:, AUX_SLOPE:AUX_SLOPE + 3] = (pos // LANES * LANES)[:, None]
    kaux[:, AUX_SLOPE + 3:AUX_SLOPE + 6] = (pos % LANES)[:, None]
    kaux_w = np.zeros((NSA_WINDOW + s, LANES), np.float32)
    kaux_w[NSA_WINDOW:, AUX_SLOPE:AUX_SLOPE + 6] = kaux[:, AUX_SLOPE:AUX_SLOPE + 6]
    kaux_w[:NSA_WINDOW, AUX_PAD] = 1.0
    pieces = jnp.stack(_bf16_pieces(slopes2) * 2, axis=-1)
    qtab = jnp.zeros((N_HEADS, LANES), jnp.float32)
    qtab = qtab.at[:, AUX_SLOPE:AUX_SLOPE + 6].set(pieces).at[:, AUX_PAD].set(-MASK_BIG)
    qtab = jnp.pad(qtab.reshape(NSA_KV_HEADS, NSA_GROUP, LANES), ((0, 0), (0, 8 - NSA_GROUP), (0, 0)))
    return (n_cmp, n_blk, jnp.asarray(ovl_t), qtab, jnp.asarray(kaux, dtype=jnp.bfloat16),
            jnp.asarray(kaux_w, dtype=jnp.bfloat16))


def _nsa_attention(proj, cmp_kv, gates, slopes2, cast_ws=(), cast_place=(), cast_shapes=()):
    b, s, _ = proj.shape
    t = ATT_TILE
    n_steps = b * NSA_KV_HEADS * (s // t)
    chunks = [_cast_chunks(w.shape, n_steps) for w in cast_ws]
    assert n_steps >= 2
    any_spec = pl.BlockSpec(memory_space=pl.ANY)
    n_cmp, n_blk, ovl_t, qtab, kaux_s, kaux_w = _nsa_tables(s, slopes2)
    assert cmp_kv.shape[3] == LANES and s % SEL_CHUNK == 0 and NSA_WINDOW % t == 0
    qw = NSA_GROUP * HEAD_DIM
    rows = NSA_GROUP * t
    kv0 = ATT_WIDTH // HEAD_DIM

    def kv_spec(j):
        return pl.BlockSpec((1, s, HEAD_DIM), lambda bi, g, qi, j=j: (bi, 0, kv0 + j * NSA_KV_HEADS + g))

    def const(shape):
        return pl.BlockSpec(shape, lambda bi, g, qi: (0,) * len(shape))

    outs = pl.pallas_call(
        functools.partial(_nsa_attn_kernel, n_cmp=n_cmp, n_blk=n_blk, n_sel=min(N_SEL, n_blk),
                          cast_place=tuple(cast_place), n_cast_dst=len(cast_shapes)),
        grid=(b, NSA_KV_HEADS, s // t),
        in_specs=[pl.BlockSpec(memory_space=pltpu.SMEM),
                  pl.BlockSpec((1, t, qw), lambda bi, g, qi: (bi, qi, g)),
                  pl.BlockSpec((1, 1, 1, LANES, HEAD_DIM), lambda bi, g, qi: (bi, 0, g, 0, 0)),
                  pl.BlockSpec((1, 1, 1, LANES, HEAD_DIM), lambda bi, g, qi: (bi, 1, g, 0, 0)),
                  kv_spec(2), kv_spec(3), kv_spec(4), kv_spec(5),
                  pl.BlockSpec((1, t, LANES), lambda bi, g, qi: (bi, qi, g)),
                  const((LANES, LANES)),
                  pl.BlockSpec((1, 8, LANES), lambda bi, g, qi: (g, 0, 0)),
                  const((s, LANES)), const((NSA_WINDOW + s, LANES))] + [any_spec] * len(cast_ws),
        out_specs=[pl.BlockSpec((1, t, qw), lambda bi, g, qi: (bi, qi, g))] + [any_spec] * len(cast_shapes),
        out_shape=[jax.ShapeDtypeStruct((b, s, ATT_WIDTH), jnp.bfloat16)]
        + [jax.ShapeDtypeStruct(shape, jnp.bfloat16) for shape in cast_shapes],
        scratch_shapes=[pltpu.VMEM((1, rows, HEAD_DIM), jnp.float32),
                        pltpu.VMEM((1, t, LANES), jnp.float32),
                        pltpu.VMEM((rows, HEAD_DIM), jnp.float32),
                        pltpu.VMEM((rows, 2 * HEAD_DIM), jnp.bfloat16),
                        pltpu.VMEM((rows, 2 * HEAD_DIM), jnp.bfloat16),
                        pltpu.VMEM((rows, s), jnp.bfloat16),
                        pltpu.VMEM((rows, NSA_WINDOW + t), jnp.bfloat16),
                        pltpu.VMEM((s, 2 * HEAD_DIM), jnp.bfloat16),
                        pltpu.VMEM((s, 2 * HEAD_DIM), jnp.bfloat16),
                        pltpu.VMEM((NSA_WINDOW + s, 2 * HEAD_DIM), jnp.bfloat16),
                        pltpu.VMEM((NSA_WINDOW + s, 2 * HEAD_DIM), jnp.bfloat16)]
        + [pltpu.VMEM((2,) + c, jnp.float32) for c in chunks]
        + [pltpu.VMEM((2,) + c, jnp.bfloat16) for c in chunks]
        + [pltpu.SemaphoreType.DMA((2,)), pltpu.SemaphoreType.DMA((2,))],
        compiler_params=_params(("arbitrary", "arbitrary", "arbitrary")),
        name="nsa_attention",
    )(slopes2, proj, cmp_kv, cmp_kv, proj, proj, proj, proj, gates, ovl_t, qtab, kaux_s, kaux_w, *cast_ws)
    return outs[0], list(outs[1:])


def _dil_attn_kernel(sl_ref, q_ref, *refs, window, has_prev):
    if has_prev:
        kp_ref, kc_ref, vp_ref, vc_ref, o_ref, lse_ref, bias_sc = refs
    else:
        kc_ref, vc_ref, o_ref, lse_ref, bias_sc = refs
        kp_ref, vp_ref = kc_ref, vc_ref
    t = ATT_TILE
    i = pl.program_id(2)

    @pl.when((pl.program_id(0) == 0) & (pl.program_id(1) == 0) & (i == 0))
    def _():
        row = lax.broadcasted_iota(jnp.int32, (t, 2 * t), 0)
        col = lax.broadcasted_iota(jnp.int32, (t, 2 * t), 1)
        dist = row - col + t
        band = (dist >= 0) & (dist <= window)
        distf = dist.astype(jnp.float32)
        for h in range(N_HEADS):
            bias = sl_ref[h] * distf
            bias_sc[0, h] = jnp.where(band & (col >= t), bias, -NEG_INF)
            bias_sc[1, h] = jnp.where(band, bias, -NEG_INF)

    var = jnp.minimum(i, 1)
    lane = lax.broadcasted_iota(jnp.int32, (t, LANES), 1)
    lse_all = jnp.zeros((t, LANES), jnp.float32)
    for h in range(N_HEADS):
        hs = slice(h * HEAD_DIM, (h + 1) * HEAD_DIM)
        q = q_ref[0, 0, :, hs]
        k = jnp.concatenate([kp_ref[0, 0, :, hs], kc_ref[0, 0, :, hs]], axis=0)
        v = jnp.concatenate([vp_ref[0, 0, :, hs], vc_ref[0, 0, :, hs]], axis=0)
        s = lax.dot_general(q, k, _NT, preferred_element_type=jnp.float32) - bias_sc[var, h]
        m = jnp.max(s, axis=-1, keepdims=True)
        e = jnp.exp2(s - m)
        l = jnp.sum(e, axis=-1, keepdims=True)
        o = jnp.dot(e.astype(v.dtype), v, preferred_element_type=jnp.float32) / l
        o_ref[0, 0, :, hs] = o.astype(o_ref.dtype)
        lse_all = jnp.where(lane == h, m + jnp.log2(l), lse_all)
    lse_ref[0, 0] = lse_all


def _dil_attention(q, kv, slopes2, win, r):
    b, _, ln, _ = q.shape
    t = ATT_TILE

    def cur(c):
        return pl.BlockSpec((1, 1, t, ATT_WIDTH), lambda bi, rho, i, c=c: (bi, rho, i, c))

    def prev(c):
        return pl.BlockSpec((1, 1, t, ATT_WIDTH), lambda bi, rho, i, c=c: (bi, rho, jnp.maximum(i - 1, 0), c))

    has_prev = ln > t
    kv_specs = [prev(0), cur(0), prev(1), cur(1)] if has_prev else [cur(0), cur(1)]
    return pl.pallas_call(
        functools.partial(_dil_attn_kernel, window=win // r, has_prev=has_prev),
        grid=(b, r, ln // t),
        in_specs=[pl.BlockSpec(memory_space=pltpu.SMEM), cur(0)] + kv_specs,
        out_specs=[pl.BlockSpec((1, 1, t, ATT_WIDTH), lambda bi, rho, i: (bi, rho, i, 0)),
                   pl.BlockSpec((1, 1, t, LANES), lambda bi, rho, i: (bi, rho, i, 0))],
        out_shape=[jax.ShapeDtypeStruct((b, r, ln, ATT_WIDTH), jnp.bfloat16),
                   jax.ShapeDtypeStruct((b, r, ln, LANES), jnp.float32)],
        scratch_shapes=[pltpu.VMEM((2, N_HEADS, t, 2 * t), jnp.float32)],
        compiler_params=_params(("arbitrary", "arbitrary", "arbitrary")),
        name="dilated_attention_r%d" % r,
    )(slopes2 * r, q, *([kv] * len(kv_specs)))


def _dil_merge_kernel(*refs, dilations):
    ng = len(dilations)
    o_refs, l_refs, out_ref, o_sc, l_sc = refs[:ng], refs[ng:2 * ng], refs[2 * ng], refs[2 * ng + 1], refs[2 * ng + 2]
    ts = out_ref.shape[1]
    ls = []
    for gi, r in enumerate(dilations):
        if r == 1:
            ls.append(l_refs[gi][0, 0])
            continue
        for rho in range(r):
            l_sc[gi, pl.ds(rho, ts // r, stride=r), :] = l_refs[gi][0, rho]
            for h in range(N_HEADS):
                o_sc[gi, h, pl.ds(rho, ts // r, stride=r), :] = (
                    o_refs[gi][0, rho, :, h * HEAD_DIM:(h + 1) * HEAD_DIM].astype(jnp.float32))
        ls.append(l_sc[gi])
    m = functools.reduce(jnp.maximum, ls)
    es = [jnp.exp2(l - m) for l in ls]
    den = functools.reduce(lambda a, b: a + b, es)
    ws = [e / den for e in es]
    for h in range(N_HEADS):
        hs = slice(h * HEAD_DIM, (h + 1) * HEAD_DIM)
        acc = None
        for gi, r in enumerate(dilations):
            og = o_refs[gi][0, 0, :, hs].astype(jnp.float32) if r == 1 else o_sc[gi, h]
            term = ws[gi][:, h:h + 1] * og
            acc = term if acc is None else acc + term
        out_ref[0, :, hs] = acc.astype(out_ref.dtype)


def _dil_merge(outs, lses, dilations):
    b, _, _, w = outs[0].shape
    s = outs[0].shape[1] * outs[0].shape[2]
    ts = min(256, s)
    ng = len(dilations)

    def spec(r, width):
        return pl.BlockSpec((1, r, ts // r, width), lambda i, j: (i, 0, j, 0))

    return pl.pallas_call(
        functools.partial(_dil_merge_kernel, dilations=dilations),
        grid=(b, s // ts),
        in_specs=[spec(r, w) for r in dilations] + [spec(r, LANES) for r in dilations],
        out_specs=pl.BlockSpec((1, ts, w), lambda i, j: (i, j, 0)),
        out_shape=jax.ShapeDtypeStruct((b, s, w), jnp.bfloat16),
        scratch_shapes=[pltpu.VMEM((ng, w // HEAD_DIM, ts, HEAD_DIM), jnp.float32),
                        pltpu.VMEM((ng, ts, LANES), jnp.float32)],
        compiler_params=_params(("parallel", "parallel")),
        name="dilated_merge",
    )(*outs, *lses)


ROW_TILE = 8


def _gather_rows(idx_ref, src_hbm, dst, sem):
    def body(i, carry):
        for j in range(ROW_TILE):
            pltpu.make_async_copy(src_hbm.at[idx_ref[0, 0, i * ROW_TILE + j]], dst.at[i, :, j], sem).start()
        return carry
    lax.fori_loop(0, dst.shape[0], body, 0)


def _gather_tiled_rows(idx_ref, src_hbm, dst, sem):
    def body(i, carry):
        for j in range(ROW_TILE):
            pltpu.make_async_copy(src_hbm.at[pl.ds(idx_ref[0, 0, i * ROW_TILE + j], 1)],
                                  dst.at[i, pl.ds(j, 1)], sem).start()
        return carry
    lax.fori_loop(0, dst.shape[0], body, 0)


def _wait_rows(dst, sem):
    pltpu.make_async_copy(dst, dst, sem).wait()


def _expert_kernel(be_ref, nu_ref, tokc_ref, tokn_ref, h_hbm, wgu_ref, wd_ref, o_ref, xbuf, sem):
    del be_ref
    de = wd_ref.shape[1]
    blk = pl.program_id(0)
    n_used = nu_ref[0]
    slot = blk % 2

    @pl.when(blk == 0)
    def _():
        _gather_rows(tokc_ref, h_hbm, xbuf.at[0], sem.at[0])

    @pl.when(blk + 1 < n_used)
    def _():
        _gather_rows(tokn_ref, h_hbm, xbuf.at[1 - slot], sem.at[1 - slot])

    @pl.when(blk < n_used)
    def _():
        _wait_rows(xbuf.at[slot], sem.at[slot])
        x = _load_row_tiles(xbuf.at[slot]).astype(jnp.bfloat16)
        gate = jnp.dot(x, wgu_ref[0, :, :de], preferred_element_type=jnp.float32)
        up = jnp.dot(x, wgu_ref[0, :, de:], preferred_element_type=jnp.float32)
        hid = (gate * jax.nn.sigmoid(gate) * up).astype(jnp.bfloat16)
        o_ref[...] = jnp.dot(hid, wd_ref[0], preferred_element_type=jnp.float32)

    @pl.when(blk >= n_used)
    def _():
        o_ref[...] = jnp.zeros(o_ref.shape, o_ref.dtype)


def _expert_blocks(h3, slot_tok, blk_e, n_used, w_gate_up, w_down):
    t, ct, _ = h3.shape
    d = ct * LANES
    nblk = blk_e.shape[0]
    de = w_down.shape[1]
    assert de % LANES == 0
    tok = slot_tok.reshape(nblk, 1, MOE_BLOCK)
    grid_spec = pltpu.PrefetchScalarGridSpec(
        num_scalar_prefetch=2,
        grid=(nblk,),
        in_specs=[pl.BlockSpec((1, 1, MOE_BLOCK), lambda i, be, nu: (i, 0, 0), memory_space=pltpu.SMEM),
                  pl.BlockSpec((1, 1, MOE_BLOCK), lambda i, be, nu: (jnp.minimum(i + 1, nblk - 1), 0, 0),
                               memory_space=pltpu.SMEM),
                  pl.BlockSpec(memory_space=pl.ANY),
                  pl.BlockSpec((1, d, 2 * de), lambda i, be, nu: (be[i], 0, 0)),
                  pl.BlockSpec((1, de, d), lambda i, be, nu: (be[i], 0, 0))],
        out_specs=pl.BlockSpec((MOE_BLOCK, d), lambda i, be, nu: (i, 0)),
        scratch_shapes=[pltpu.VMEM((2, MOE_BLOCK // ROW_TILE, ct, ROW_TILE, LANES), jnp.float32),
                        pltpu.SemaphoreType.DMA((2,))],
    )
    return pl.pallas_call(
        _expert_kernel,
        grid_spec=grid_spec,
        out_shape=jax.ShapeDtypeStruct((nblk * MOE_BLOCK, d), jnp.float32),
        compiler_params=_params(("arbitrary",)),
        name="moe_expert_blocks",
    )(blk_e, n_used, tok, tok, h3, w_gate_up, w_down)


def _combine_kernel(d0c_ref, d1c_ref, d0n_ref, d1n_ref, x_ref, w0_ref, w1_ref, g_ref, gn_ref, yo_hbm, o_ref,
                    buf, sem, *, out_norm):
    i = pl.program_id(0)
    n = pl.num_programs(0)
    slot = i % 2
    tt = x_ref.shape[0]

    @pl.when(i == 0)
    def _():
        _gather_tiled_rows(d0c_ref, yo_hbm, buf.at[0, 0], sem.at[0])
        _gather_tiled_rows(d1c_ref, yo_hbm, buf.at[0, 1], sem.at[0])

    @pl.when(i + 1 < n)
    def _():
        _gather_tiled_rows(d0n_ref, yo_hbm, buf.at[1 - slot, 0], sem.at[1 - slot])
        _gather_tiled_rows(d1n_ref, yo_hbm, buf.at[1 - slot, 1], sem.at[1 - slot])

    _wait_rows(buf.at[slot, 0], sem.at[slot])
    _wait_rows(buf.at[slot, 1], sem.at[slot])
    y = (w0_ref[...][:, 0:1] * buf[slot, 0].reshape(x_ref.shape)
         + w1_ref[...][:, 0:1] * buf[slot, 1].reshape(x_ref.shape))
    out = x_ref[...] + g_ref[0] * y
    o_ref[...] = _rms(out, gn_ref[...]) if out_norm else out


def _moe_combine(x2d, yo, dest, wts, gate, s, out_norm_g=None):
    t, d = x2d.shape
    out_norm = out_norm_g is not None
    gn = (out_norm_g if out_norm else jnp.ones((d,), jnp.float32)).reshape(1, d)
    tt = min(256, s)
    nt = t // tt
    d0 = dest[:, 0].reshape(nt, 1, tt)
    d1 = dest[:, 1].reshape(nt, 1, tt)
    w0 = jnp.broadcast_to(wts[:, 0:1], (t, LANES))
    w1 = jnp.broadcast_to(wts[:, 1:2], (t, LANES))
    b = gate.shape[0]
    cur = pl.BlockSpec((1, 1, tt), lambda i: (i, 0, 0), memory_space=pltpu.SMEM)
    nxt = pl.BlockSpec((1, 1, tt), lambda i: (jnp.minimum(i + 1, nt - 1), 0, 0), memory_space=pltpu.SMEM)
    return pl.pallas_call(
        functools.partial(_combine_kernel, out_norm=out_norm),
        grid=(nt,),
        in_specs=[cur, cur, nxt, nxt,
                  pl.BlockSpec((tt, d), lambda i: (i, 0)),
                  pl.BlockSpec((tt, LANES), lambda i: (i, 0)),
                  pl.BlockSpec((tt, LANES), lambda i: (i, 0)),
                  pl.BlockSpec((1, 1, d), lambda i: (i * tt // s, 0, 0)),
                  pl.BlockSpec((1, d), lambda i: (0, 0)),
                  pl.BlockSpec(memory_space=pl.ANY)],
        out_specs=pl.BlockSpec((tt, d), lambda i: (i, 0)),
        out_shape=jax.ShapeDtypeStruct((t, d), jnp.float32),
        scratch_shapes=[pltpu.VMEM((2, 2, tt // ROW_TILE, ROW_TILE, d), jnp.float32),
                        pltpu.SemaphoreType.DMA((2,))],
        compiler_params=_params(("arbitrary",)),
        name="moe_combine_residual",
    )(d0, d1, d0, d1, x2d, w0, w1, gate.reshape(b, 1, d), gn, yo)


def _dispatch_tables(idx):
    t = idx.shape[0]
    a = t * TOP_K
    flat_e = idx.reshape(a)
    onehot = (flat_e[:, None] == jnp.arange(N_EXPERTS, dtype=jnp.int32)[None, :]).astype(jnp.int32)
    csum = jnp.cumsum(onehot, axis=0)
    rank = jnp.take_along_axis(csum, flat_e[:, None], axis=1)[:, 0] - 1
    counts = csum[-1]
    padded = (counts + MOE_BLOCK - 1) // MOE_BLOCK * MOE_BLOCK
    pad_end = jnp.cumsum(padded)
    pad_start = pad_end - padded
    dest = pad_start[flat_e] + rank
    nblk = -(-(a + N_EXPERTS * MOE_BLOCK) // MOE_BLOCK)
    cap = nblk * MOE_BLOCK
    slot_tok = jnp.zeros((cap,), jnp.int32).at[dest].set(jnp.arange(a, dtype=jnp.int32) // TOP_K,
                                                         unique_indices=True)
    blk_start = jnp.arange(nblk, dtype=jnp.int32) * MOE_BLOCK
    blk_e = jnp.minimum(jnp.sum((pad_end[None, :] <= blk_start[:, None]).astype(jnp.int32), axis=1), N_EXPERTS - 1)
    n_used = (pad_end[-1:] // MOE_BLOCK).astype(jnp.int32)
    return dest.reshape(t, TOP_K).astype(jnp.int32), slot_tok, blk_e.astype(jnp.int32), n_used


def _moe_layer(x, g, sc, sh, gate, router_w, router_b, layer, w_gate_up, w_down, out_norm_g=None):
    b, s, d = x.shape
    t = b * s
    h, idx, wts = _norm_route(x, g, sc, sh, router_w, router_b)
    idx = idx.transpose(0, 2, 1).reshape(t, TOP_K)
    wts = wts.transpose(0, 2, 1).reshape(t, TOP_K)
    dest, slot_tok, blk_e, n_used = _dispatch_tables(idx)
    yo = _expert_blocks(h, slot_tok, blk_e + layer * N_EXPERTS, n_used, w_gate_up, w_down)
    return _moe_combine(x.reshape(t, d), yo, dest, wts, gate, s, out_norm_g).reshape(b, s, d)


def _nsa_layer(h, x, gate, slopes, j, w_in, w_phi1, w_phi2, phi_pos, w_out, cast_job=((), (), ())):
    b, s, d = h.shape
    t = b * s
    h2d = h.reshape(t, d)
    proj = _matmul(h2d, w_in, j, 0, NSA_QKV, jnp.bfloat16, scale=HEAD_DIM ** -0.5 * LOG2E, scale_cols=ATT_WIDTH)
    proj = proj.reshape(b, s, NSA_QKV)
    wg = w_in[j, :, NSA_QKV:].reshape(d, NSA_KV_HEADS, NSA_GATES)
    wg = jnp.pad(wg, ((0, 0), (0, 0), (0, LANES - NSA_GATES))).reshape(1, d, NSA_KV_HEADS * LANES)
    gates = _matmul(h2d, wg, 0, 0, NSA_KV_HEADS * LANES, jnp.float32, act="sigmoid")
    gates = gates.reshape(b, s, NSA_KV_HEADS * LANES)
    nch = s // CMP_STRIDE
    kv = proj[:, :, ATT_WIDTH:ATT_WIDTH + 2 * NSA_KV_WIDTH]
    kv = kv.reshape(b, nch, CMP_STRIDE, 2, NSA_KV_HEADS, HEAD_DIM).transpose(0, 3, 4, 1, 2, 5)
    cmp_kv = _compress(kv.reshape(b, 2, NSA_KV_HEADS, nch, CMP_STRIDE * HEAD_DIM), w_phi1, w_phi2, phi_pos)
    o, casted = _nsa_attention(proj, cmp_kv, gates, slopes * LOG2E, *cast_job)
    return _matmul_residual(o, w_out, j, x, gate), casted


def _dil_layer(x, g, sc, sh, gate, slopes, j, w_in, w_out):
    b, s, d = x.shape
    dilations = tuple(r for _, r in DIL_PAIRS)
    assert dilations[0] == 1
    hs = _norm_mod_streams(x, g, sc, sh, dilations)
    qs = _matmul_streams(hs[0].reshape(b, s, d), w_in, j, 0, ATT_WIDTH, dilations,
                         scale=HEAD_DIM ** -0.5 * LOG2E)
    outs, lses = [], []
    for gidx, (win, r) in enumerate(DIL_PAIRS):
        off = ATT_WIDTH * (1 + 2 * gidx)
        kv = _matmul(hs[gidx].reshape(b * s, d), w_in, j, off, 2 * ATT_WIDTH, jnp.bfloat16)
        o, lse = _dil_attention(qs[gidx], kv.reshape(b, r, s // r, 2 * ATT_WIDTH), slopes * LOG2E, win, r)
        outs.append(o)
        lses.append(lse)
    o = _dil_merge(outs, lses, dilations)
    return _matmul_residual(o, w_out, j, x, gate)


def kernel(x, c, ada_w, ada_b, norm_mix, norm_ffn, norm_final, nsa_w_in, nsa_w_phi1, nsa_w_phi2, nsa_phi_pos,
           nsa_w_out, dil_w_in, dil_w_out, router_w, router_b, exp_w_gate, exp_w_up, exp_w_down):
    depth = ada_w.shape[0]
    d = x.shape[-1]
    mod = _modulation(c, ada_w, ada_b)
    slopes = 2.0 ** (-ALIBI_MAX_BIAS * jnp.arange(1, N_HEADS + 1, dtype=jnp.float32) / N_HEADS)
    n_exp, _, de = exp_w_gate.shape[1:]
    for i in range(depth):
        sh_m, sc_m, g_m, sh_f, sc_f, g_f = [mod[i, :, k * d:(k + 1) * d] for k in range(6)]
        j = i // 2
        if i % 2 == 0:
            h = _norm_mod(x, norm_mix[i], sc_m, sh_m, jnp.bfloat16)
            cast_job = ((), (), ())
            if i == 0:
                srcs = [w.reshape(-1, w.shape[-1]) for w in (exp_w_gate, exp_w_up, exp_w_down)]
                cast_job = (srcs, ((0, 0), (0, de), (1, 0)), ((srcs[0].shape[0], 2 * de), srcs[2].shape))
            x, casted = _nsa_layer(h, x, g_m, slopes, j, nsa_w_in, nsa_w_phi1[j], nsa_w_phi2[j], nsa_phi_pos[j],
                                   nsa_w_out, cast_job)
            if i == 0:
                e_gate_up = casted[0].reshape(depth * n_exp, d, 2 * de)
                e_down = casted[1].reshape(depth * n_exp, de, d)
        else:
            x = _dil_layer(x, norm_mix[i], sc_m, sh_m, g_m, slopes, j, dil_w_in, dil_w_out)
        x = _moe_layer(x, norm_ffn[i], sc_f, sh_f, g_f, router_w, router_b, i, e_gate_up, e_down,
                       norm_final if i == depth - 1 else None)
    return x
```

```python
import functools

import jax
import jax.numpy as jnp
import numpy as np
from jax import lax
from jax.experimental import pallas as pl
from jax.experimental.pallas import tpu as pltpu

HEAD_DIM = 128
N_HEADS = 16
ATT_WIDTH = N_HEADS * HEAD_DIM
ALIBI_MAX_BIAS = 8.0

NSA_KV_HEADS = 4
NSA_GROUP = N_HEADS // NSA_KV_HEADS
NSA_KV_WIDTH = NSA_KV_HEADS * HEAD_DIM
CMP_BLOCK = 32
CMP_STRIDE = 16
SEL_BLOCK = 64
N_SEL = 16
NSA_WINDOW = 512
FORCE_SCORE = 1.0e6
NSA_QKV = ATT_WIDTH + 6 * NSA_KV_WIDTH
NSA_GATES = 3 * NSA_GROUP

DIL_PAIRS = ((128, 1), (512, 4), (2048, 16))
DIL_PROJ = ATT_WIDTH * (1 + 2 * len(DIL_PAIRS))

N_EXPERTS = 16
N_EXPERT_GROUPS = 4
EXPERTS_PER_GROUP = N_EXPERTS // N_EXPERT_GROUPS
TOP_K = 2
MOE_BLOCK = 256

RMS_EPS = 1e-6
NEG_INF = -1.0e30

LANES = 128
ATT_TILE = 128
VMEM_LIMIT = 56 * 1024 * 1024

_HI = lax.Precision.HIGHEST
_NT = (((1,), (1,)), ((), ()))


def _params(sem, vmem=VMEM_LIMIT):
    return pltpu.CompilerParams(dimension_semantics=sem, vmem_limit_bytes=vmem)


def _mod_kernel(c_ref, w_ref, b_ref, o_ref):
    c = c_ref[...]
    cond = c * jax.nn.sigmoid(c)
    o_ref[0] = jnp.dot(cond, w_ref[0], precision=_HI, preferred_element_type=jnp.float32) + b_ref[0]


def _modulation(c, ada_w, ada_b):
    depth, d, n = ada_w.shape
    b = c.shape[0]
    tn = 512
    return pl.pallas_call(
        _mod_kernel,
        grid=(depth, n // tn),
        in_specs=[pl.BlockSpec((b, d), lambda i, j: (0, 0)),
                  pl.BlockSpec((1, d, tn), lambda i, j: (i, 0, j)),
                  pl.BlockSpec((1, 1, tn), lambda i, j: (i, 0, j))],
        out_specs=pl.BlockSpec((1, b, tn), lambda i, j: (i, 0, j)),
        out_shape=jax.ShapeDtypeStruct((depth, b, n), jnp.float32),
        compiler_params=_params(("parallel", "parallel")),
        name="adaln_modulation",
    )(c, ada_w, ada_b.reshape(depth, 1, n))


def _rms(x, g):
    return x * lax.rsqrt(jnp.mean(x * x, axis=-1, keepdims=True) + RMS_EPS) * g


def _norm_mod_kernel(x_ref, g_ref, sc_ref, sh_ref, o_ref):
    h = _rms(x_ref[0], g_ref[...]) * (1.0 + sc_ref[0]) + sh_ref[0]
    o_ref[0] = h.astype(o_ref.dtype)


def _norm_mod(x, g, sc, sh, out_dtype):
    b, s, d = x.shape
    ts = min(256, s)
    return pl.pallas_call(
        _norm_mod_kernel,
        grid=(b, s // ts),
        in_specs=[pl.BlockSpec((1, ts, d), lambda i, j: (i, j, 0)),
                  pl.BlockSpec((1, d), lambda i, j: (0, 0)),
                  pl.BlockSpec((1, 1, d), lambda i, j: (i, 0, 0)),
                  pl.BlockSpec((1, 1, d), lambda i, j: (i, 0, 0))],
        out_specs=pl.BlockSpec((1, ts, d), lambda i, j: (i, j, 0)),
        out_shape=jax.ShapeDtypeStruct((b, s, d), out_dtype),
        compiler_params=_params(("parallel", "parallel")),
        name="rmsnorm_adaln",
    )(x, g.reshape(1, d), sc.reshape(b, 1, d), sh.reshape(b, 1, d))


def _store_streams(val, src_sc, out_refs, dilations):
    n = val.shape[0]
    for c in range(src_sc.shape[0]):
        src_sc[c] = val[:, c * LANES:(c + 1) * LANES]
    for r, ref in zip(dilations, out_refs):
        if r == 1:
            ref[0, 0] = val.astype(ref.dtype)
            continue
        for rho in range(r):
            for c in range(src_sc.shape[0]):
                ref[0, rho, :, c * LANES:(c + 1) * LANES] = (
                    src_sc[c, pl.ds(rho, n // r, stride=r), :].astype(ref.dtype))


def _norm_mod_streams_kernel(x_ref, g_ref, sc_ref, sh_ref, *refs, dilations):
    out_refs, h_sc = refs[:-1], refs[-1]
    h = _rms(x_ref[0], g_ref[...]) * (1.0 + sc_ref[0]) + sh_ref[0]
    _store_streams(h, h_sc, out_refs, dilations)


def _norm_mod_streams(x, g, sc, sh, dilations):
    b, s, d = x.shape
    ts = min(256, s)
    return pl.pallas_call(
        functools.partial(_norm_mod_streams_kernel, dilations=dilations),
        grid=(b, s // ts),
        in_specs=[pl.BlockSpec((1, ts, d), lambda i, j: (i, j, 0)),
                  pl.BlockSpec((1, d), lambda i, j: (0, 0)),
                  pl.BlockSpec((1, 1, d), lambda i, j: (i, 0, 0)),
                  pl.BlockSpec((1, 1, d), lambda i, j: (i, 0, 0))],
        out_specs=[pl.BlockSpec((1, r, ts // r, d), lambda i, j: (i, 0, j, 0)) for r in dilations],
        out_shape=[jax.ShapeDtypeStruct((b, r, s // r, d), jnp.bfloat16) for r in dilations],
        scratch_shapes=[pltpu.VMEM((d // LANES, ts, LANES), jnp.float32)],
        compiler_params=_params(("parallel", "parallel")),
        name="rmsnorm_adaln_streams",
    )(x, g.reshape(1, d), sc.reshape(b, 1, d), sh.reshape(b, 1, d))


def _pair_max(vals):
    out = None
    for a in range(len(vals)):
        for b in range(a + 1, len(vals)):
            s = vals[a] + vals[b]
            out = s if out is None else jnp.maximum(out, s)
    return out


def _store_row_major(ref, val):
    rows = val.shape[0]
    c_tiles = val.shape[1] // LANES
    for c in range(c_tiles):
        ref[pl.ds(c, rows, stride=c_tiles), :] = val[:, c * LANES:(c + 1) * LANES]


def _load_row_tiles(buf):
    n, c_tiles = buf.shape[0], buf.shape[1]
    return jnp.concatenate([buf[:, c].reshape(n * ROW_TILE, LANES) for c in range(c_tiles)], axis=1)


def _norm_route_kernel(x_ref, g_ref, sc_ref, sh_ref, rw_ref, rb_ref, h_ref, idx_ref, wt_ref):
    h = _rms(x_ref[0], g_ref[...]) * (1.0 + sc_ref[0]) + sh_ref[0]
    _store_row_major(h_ref, h)

    logits = lax.dot_general(rw_ref[...], h, _NT, precision=_HI, preferred_element_type=jnp.float32)
    scores = jax.nn.sigmoid(logits)
    biased = scores + rb_ref[...][:, 0:1]
    rows = [biased[e:e + 1, :] for e in range(N_EXPERTS)]
    srow = [scores[e:e + 1, :] for e in range(N_EXPERTS)]
    grp = [_pair_max(rows[q * EXPERTS_PER_GROUP:(q + 1) * EXPERTS_PER_GROUP]) for q in range(N_EXPERT_GROUPS)]
    best_v = grp[0]
    best = jnp.zeros(best_v.shape, jnp.int32)
    for q in range(1, N_EXPERT_GROUPS):
        take = grp[q] > best_v
        best = jnp.where(take, q, best)
        best_v = jnp.where(take, grp[q], best_v)
    v1 = jnp.full(best_v.shape, NEG_INF, jnp.float32)
    i1 = jnp.zeros(best_v.shape, jnp.int32)
    for e in range(N_EXPERTS):
        cand = jnp.where(best == e // EXPERTS_PER_GROUP, rows[e], NEG_INF)
        take = cand > v1
        i1 = jnp.where(take, e, i1)
        v1 = jnp.where(take, cand, v1)
    v2 = jnp.full(best_v.shape, NEG_INF, jnp.float32)
    i2 = jnp.zeros(best_v.shape, jnp.int32)
    for e in range(N_EXPERTS):
        cand = jnp.where(best == e // EXPERTS_PER_GROUP, jnp.where(i1 == e, NEG_INF, rows[e]), NEG_INF)
        take = cand > v2
        i2 = jnp.where(take, e, i2)
        v2 = jnp.where(take, cand, v2)
    w1 = jnp.zeros(best_v.shape, jnp.float32)
    w2 = jnp.zeros(best_v.shape, jnp.float32)
    for e in range(N_EXPERTS):
        w1 = jnp.where(i1 == e, srow[e], w1)
        w2 = jnp.where(i2 == e, srow[e], w2)
    tot = w1 + w2
    idx_ref[0] = jnp.concatenate([i1, i2], axis=0)
    wt_ref[0] = jnp.concatenate([w1 / tot, w2 / tot], axis=0)


def _norm_route(x, g, sc, sh, router_w, router_b):
    b, s, d = x.shape
    ts = min(256, s)
    ct = d // LANES
    rwt = router_w.T
    rb = jnp.broadcast_to(router_b.reshape(N_EXPERTS, 1), (N_EXPERTS, LANES))
    h, idx, wts = pl.pallas_call(
        _norm_route_kernel,
        grid=(b, s // ts),
        in_specs=[pl.BlockSpec((1, ts, d), lambda i, j: (i, j, 0)),
                  pl.BlockSpec((1, d), lambda i, j: (0, 0)),
                  pl.BlockSpec((1, 1, d), lambda i, j: (i, 0, 0)),
                  pl.BlockSpec((1, 1, d), lambda i, j: (i, 0, 0)),
                  pl.BlockSpec((N_EXPERTS, d), lambda i, j: (0, 0)),
                  pl.BlockSpec((N_EXPERTS, LANES), lambda i, j: (0, 0))],
        out_specs=[pl.BlockSpec((ts * ct, LANES), lambda i, j: (i * (s // ts) + j, 0)),
                   pl.BlockSpec((1, TOP_K, ts), lambda i, j: (i, 0, j)),
                   pl.BlockSpec((1, TOP_K, ts), lambda i, j: (i, 0, j))],
        out_shape=[jax.ShapeDtypeStruct((b * s * ct, LANES), jnp.float32),
                   jax.ShapeDtypeStruct((b, TOP_K, s), jnp.int32),
                   jax.ShapeDtypeStruct((b, TOP_K, s), jnp.float32)],
        compiler_params=_params(("parallel", "parallel")),
        name="rmsnorm_adaln_router",
    )(x, g.reshape(1, d), sc.reshape(b, 1, d), sh.reshape(b, 1, d), rwt, rb)
    return h.reshape(b * s, ct, LANES), idx, wts


def _weight_tile(w_ref, wbf_sc, first, scale, scale_tiles):
    @pl.when(first)
    def _():
        w = w_ref[0]
        if scale_tiles:
            w = w * jnp.where(pl.program_id(0) < scale_tiles, scale, 1.0)
        wbf_sc[...] = w.astype(jnp.bfloat16)
    return wbf_sc[...]


def _weight_spec(layer, col0, k, tn, rank):
    assert col0 % tn == 0
    if rank == 2:
        return pl.BlockSpec((1, k, tn), lambda j, i: (layer, 0, col0 // tn + j))
    return pl.BlockSpec((1, k, tn), lambda j, bi, i: (layer, 0, col0 // tn + j))


def _mm_kernel(a_ref, w_ref, o_ref, wbf_sc, *, act, scale, scale_tiles):
    w = _weight_tile(w_ref, wbf_sc, pl.program_id(1) == 0, scale, scale_tiles)
    acc = jnp.dot(a_ref[...], w, preferred_element_type=jnp.float32)
    if act == "sigmoid":
        acc = jax.nn.sigmoid(acc)
    o_ref[...] = acc.astype(o_ref.dtype)


def _matmul(a, w, layer, col0, n, out_dtype, act=None, scale=1.0, scale_cols=0, tm=2048, tn=1024):
    m, k = a.shape
    tn = min(tn, n)
    tm = min(tm, m)
    assert scale_cols % tn == 0
    return pl.pallas_call(
        functools.partial(_mm_kernel, act=act, scale=scale, scale_tiles=scale_cols // tn),
        grid=(n // tn, m // tm),
        in_specs=[pl.BlockSpec((tm, k), lambda j, i: (i, 0)),
                  _weight_spec(layer, col0, k, tn, 2)],
        out_specs=pl.BlockSpec((tm, tn), lambda j, i: (i, j)),
        out_shape=jax.ShapeDtypeStruct((m, n), out_dtype),
        scratch_shapes=[pltpu.VMEM((k, tn), jnp.bfloat16)],
        compiler_params=_params(("arbitrary", "arbitrary")),
        name="matmul" if act is None else "matmul_" + act,
    )(a, w)


def _nsa_proj_kernel(a_ref, w_ref, o_ref, ch_ref, wbf_sc, stage_sc, *, scale, scale_tiles, kv_tile, n_tiles):
    tile = (pl.program_id(0) + kv_tile + 1) % n_tiles

    @pl.when(pl.program_id(1) == 0)
    def _():
        w = w_ref[0]
        if scale_tiles:
            w = w * jnp.where(tile < scale_tiles, scale, 1.0)
        wbf_sc[...] = w.astype(jnp.bfloat16)

    acc = jnp.dot(a_ref[...], wbf_sc[...], preferred_element_type=jnp.float32)
    o_ref[...] = acc.astype(o_ref.dtype)

    @pl.when(pl.program_id(0) == n_tiles - 1)
    def _():
        nch = ch_ref.shape[3]
        for c in range(acc.shape[1] // HEAD_DIM):
            kv, g = divmod(c, NSA_KV_HEADS)
            stage_sc[...] = acc[:, c * HEAD_DIM:(c + 1) * HEAD_DIM]
            for l in range(CMP_STRIDE):
                ch_ref[0, kv, g, :, l * HEAD_DIM:(l + 1) * HEAD_DIM] = (
                    stage_sc[pl.ds(l, nch, stride=CMP_STRIDE), :].astype(ch_ref.dtype))


def _nsa_projection(a, w, layer, s, scale, tm=1024, tn=1024):
    m, k = a.shape
    b = m // s
    n_tiles = NSA_QKV // tn
    kv_tile = ATT_WIDTH // tn
    assert ATT_WIDTH % tn == 0 and 2 * NSA_KV_WIDTH == tn and s % tm == 0 and HEAD_DIM == LANES
    per_b = s // tm
    nch = tm // CMP_STRIDE

    def tile_of(j):
        return (j + kv_tile + 1) % n_tiles

    def chunk_block(j, i):
        last = j == n_tiles - 1
        return (jnp.where(last, i // per_b, 0), 0, 0, jnp.where(last, i % per_b, 0), 0)

    return pl.pallas_call(
        functools.partial(_nsa_proj_kernel, scale=scale, scale_tiles=ATT_WIDTH // tn, kv_tile=kv_tile,
                          n_tiles=n_tiles),
        grid=(n_tiles, m // tm),
        in_specs=[pl.BlockSpec((tm, k), lambda j, i: (i, 0)),
                  pl.BlockSpec((1, k, tn), lambda j, i: (layer, 0, tile_of(j)))],
        out_specs=[pl.BlockSpec((tm, tn), lambda j, i: (i, tile_of(j))),
                   pl.BlockSpec((1, 2, NSA_KV_HEADS, nch, CMP_STRIDE * HEAD_DIM), chunk_block)],
        out_shape=[jax.ShapeDtypeStruct((m, NSA_QKV), jnp.bfloat16),
                   jax.ShapeDtypeStruct((b, 2, NSA_KV_HEADS, s // CMP_STRIDE, CMP_STRIDE * HEAD_DIM), jnp.bfloat16)],
        scratch_shapes=[pltpu.VMEM((k, tn), jnp.bfloat16), pltpu.VMEM((tm, HEAD_DIM), jnp.float32)],
        compiler_params=_params(("arbitrary", "arbitrary")),
        name="matmul_nsa_proj",
    )(a, w)


def _first_row_step():
    return (pl.program_id(1) == 0) & (pl.program_id(2) == 0)


def _mm_streams_kernel(a_ref, w_ref, *refs, dilations, scale, scale_tiles):
    out_refs, acc_sc, wbf_sc = refs[:-2], refs[-2], refs[-1]
    w = _weight_tile(w_ref, wbf_sc, _first_row_step(), scale, scale_tiles)
    acc = jnp.dot(a_ref[0], w, preferred_element_type=jnp.float32)
    _store_streams(acc, acc_sc, out_refs, dilations)


def _matmul_streams(a, w, layer, col0, n, dilations, scale=1.0, tm=512, tn=1024):
    b, s, k = a.shape
    tn = min(tn, n)
    tm = min(tm, s)
    return pl.pallas_call(
        functools.partial(_mm_streams_kernel, dilations=dilations, scale=scale,
                          scale_tiles=n // tn if scale != 1.0 else 0),
        grid=(n // tn, b, s // tm),
        in_specs=[pl.BlockSpec((1, tm, k), lambda j, bi, i: (bi, i, 0)),
                  _weight_spec(layer, col0, k, tn, 3)],
        out_specs=[pl.BlockSpec((1, r, tm // r, tn), lambda j, bi, i: (bi, 0, i, j)) for r in dilations],
        out_shape=[jax.ShapeDtypeStruct((b, r, s // r, n), jnp.bfloat16) for r in dilations],
        scratch_shapes=[pltpu.VMEM((tn // LANES, tm, LANES), jnp.float32), pltpu.VMEM((k, tn), jnp.bfloat16)],
        compiler_params=_params(("arbitrary", "arbitrary", "arbitrary")),
        name="matmul_streams",
    )(a, w)


def _mm_residual_kernel(a_ref, w_ref, x_ref, g_ref, o_ref, wbf_sc):
    w = _weight_tile(w_ref, wbf_sc, _first_row_step(), 1.0, 0)
    acc = jnp.dot(a_ref[0], w, preferred_element_type=jnp.float32)
    o_ref[0] = x_ref[0] + g_ref[0] * acc


def _matmul_residual(a, w, layer, x, gate, tm=512, tn=1024):
    b, s, k = a.shape
    n = w.shape[2]
    tn = min(tn, n)
    tm = min(tm, s)
    return pl.pallas_call(
        _mm_residual_kernel,
        grid=(n // tn, b, s // tm),
        in_specs=[pl.BlockSpec((1, tm, k), lambda j, bi, i: (bi, i, 0)),
                  _weight_spec(layer, 0, k, tn, 3),
                  pl.BlockSpec((1, tm, tn), lambda j, bi, i: (bi, i, j)),
                  pl.BlockSpec((1, 1, tn), lambda j, bi, i: (bi, 0, j))],
        out_specs=pl.BlockSpec((1, tm, tn), lambda j, bi, i: (bi, i, j)),
        out_shape=jax.ShapeDtypeStruct((b, s, n), jnp.float32),
        scratch_shapes=[pltpu.VMEM((k, tn), jnp.bfloat16)],
        compiler_params=_params(("arbitrary", "arbitrary", "arbitrary")),
        name="matmul_gated_residual",
    )(a, w, x, gate.reshape(b, 1, n))


def _gelu_tanh(x):
    return 0.5 * x * (1.0 + jnp.tanh(0.7978845608028654 * (x + 0.044715 * (x * x * x))))


def _compress_kernel(c_ref, pa_ref, pb_ref, wa_ref, wb_ref, w2_ref, o_ref):
    c = c_ref[0, 0, 0].astype(jnp.float32)
    lo = (c + pa_ref[0]).astype(jnp.bfloat16)
    hi = (c + pb_ref[0]).astype(jnp.bfloat16)
    ha = jnp.dot(lo, wa_ref[0], preferred_element_type=jnp.float32)
    hb = jnp.dot(hi, wb_ref[0], preferred_element_type=jnp.float32)
    n = ha.shape[0]
    hid = _gelu_tanh(ha + pltpu.roll(hb, n - 1, 0))
    out = jnp.dot(hid.astype(jnp.bfloat16), w2_ref[0], preferred_element_type=jnp.float32)
    row = lax.broadcasted_iota(jnp.int32, out.shape, 0)
    o_ref[0, 0, 0] = jnp.where(row < n - 1, out, 0.0).astype(o_ref.dtype)


def _compress(kv_chunks, w_phi1, w_phi2, phi_pos):
    b, two, hkv, nch, width = kv_chunks.shape
    dh = HEAD_DIM
    half = CMP_BLOCK // 2
    w1 = w_phi1.reshape(2, 2, half * dh, dh).astype(jnp.bfloat16)
    pos = phi_pos.reshape(2, 2, 1, half * dh)
    return pl.pallas_call(
        _compress_kernel,
        grid=(b, two, hkv),
        in_specs=[pl.BlockSpec((1, 1, 1, nch, width), lambda i, j, g: (i, j, g, 0, 0)),
                  pl.BlockSpec((1, 1, width), lambda i, j, g: (j, 0, 0)),
                  pl.BlockSpec((1, 1, width), lambda i, j, g: (j, 0, 0)),
                  pl.BlockSpec((1, width, dh), lambda i, j, g: (j, 0, 0)),
                  pl.BlockSpec((1, width, dh), lambda i, j, g: (j, 0, 0)),
                  pl.BlockSpec((1, dh, dh), lambda i, j, g: (j, 0, 0))],
        out_specs=pl.BlockSpec((1, 1, 1, nch, dh), lambda i, j, g: (i, j, g, 0, 0)),
        out_shape=jax.ShapeDtypeStruct((b, two, hkv, nch, dh), jnp.bfloat16),
        compiler_params=_params(("parallel", "parallel", "parallel")),
        name="nsa_compress",
    )(kv_chunks, pos[:, 0], pos[:, 1], w1[:, 0], w1[:, 1], w_phi2.astype(jnp.bfloat16))


MASK_BIG = 2.0 ** 100
AUX_SLOPE = 64
AUX_PAD = 70
SEL_CHUNK = 512
LOG2E = 1.4426950408889634


def _nsa_attn_kernel(sl_ref, q_ref, kc_ref, vc_ref, ks_ref, vs_ref, kw_ref, vw_ref, gt_ref, ovl_ref, qtab_ref,
                     kauxs_ref, kauxw_ref, *refs, n_cmp, n_blk, n_sel, cast_place, n_cast_dst):
    n_cast = len(cast_place)
    cast_src, refs = refs[:n_cast], refs[n_cast:]
    o_ref, refs = refs[0], refs[1:]
    cast_dst, refs = refs[:n_cast_dst], refs[n_cast_dst:]
    (ocmp_sc, maskq_sc, owin_sc, qaug_sc, qaugw_sc, p_sc, pw_sc, kaug_s, vaug_s, kaug_w, vaug_w), refs = refs[:11], refs[11:]
    cast_in, cast_out, (cast_in_sem, cast_out_sem) = refs[:n_cast], refs[n_cast:2 * n_cast], refs[2 * n_cast:]
    t = ATT_TILE
    dh = HEAD_DIM
    s_len = ks_ref.shape[1]
    wpad = NSA_WINDOW
    g = pl.program_id(1)
    qt = pl.program_id(2)
    qstart = pl.multiple_of(qt * t, t)

    if n_cast:
        step = (pl.program_id(0) * pl.num_programs(1) + g) * pl.num_programs(2) + qt
        n_steps = pl.num_programs(0) * pl.num_programs(1) * pl.num_programs(2)
        drain_cast = _cast_side_job(step, n_steps, cast_src, cast_dst, cast_place, cast_in, cast_out, cast_in_sem,
                                    cast_out_sem)

    @pl.when(qt == 0)
    def _():
        ones = jnp.ones((s_len, dh), jnp.bfloat16)
        kaug_s[:, :dh] = ks_ref[0]
        kaug_s[:, dh:] = kauxs_ref[...]
        vaug_s[:, :dh] = vs_ref[0]
        vaug_s[:, dh:] = ones
        kaug_w[:wpad, :dh] = jnp.zeros((wpad, dh), jnp.bfloat16)
        kaug_w[wpad:, :dh] = kw_ref[0]
        kaug_w[:, dh:] = kauxw_ref[...]
        vaug_w[:wpad, :dh] = jnp.zeros((wpad, dh), jnp.bfloat16)
        vaug_w[wpad:, :dh] = vw_ref[0]
        vaug_w[:wpad, dh:] = jnp.ones((wpad, dh), jnp.bfloat16)
        vaug_w[wpad:, dh:] = ones

    slopes = [sl_ref[g * NSA_GROUP + r] for r in range(NSA_GROUP)]
    row = lax.broadcasted_iota(jnp.int32, (t, LANES), 0)
    col = lax.broadcasted_iota(jnp.int32, (t, LANES), 1)
    prows = [slice(pair * 2 * t, (pair + 1) * 2 * t) for pair in range(NSA_GROUP // 2)]
    qtab = qtab_ref[0]

    def compress_and_select(qv, tile, par):
        qs = tile * t
        tpos = qs + row
        q4 = jnp.concatenate([qv[:, r * dh:(r + 1) * dh] for r in range(NSA_GROUP)], axis=0)
        kc = kc_ref[0, 0, 0]
        vc = vc_ref[0, 0, 0]
        s = lax.dot_general(q4, kc, _NT, preferred_element_type=jnp.float32)
        visible = (col * CMP_STRIDE + (CMP_BLOCK - 1) <= tpos) & (col < n_cmp)
        dist_c = tpos.astype(jnp.float32) - (col.astype(jnp.float32) * CMP_STRIDE + (CMP_BLOCK - 1) / 2.0)
        psum = jnp.zeros((t, LANES), jnp.float32)
        for r in range(NSA_GROUP):
            rows = slice(r * t, (r + 1) * t)
            sr = jnp.where(visible, s[rows] - slopes[r] * dist_c, NEG_INF)
            e = jnp.where(visible, jnp.exp2(sr - jnp.max(sr, axis=-1, keepdims=True)), 0.0)
            p = e / jnp.maximum(jnp.sum(e, axis=-1, keepdims=True), 1e-30)
            psum = psum + p
            ocmp_sc[par, rows] = jnp.dot(p.astype(vc.dtype), vc, preferred_element_type=jnp.float32)
        nb = -(-n_blk // 8) * 8
        imp_t = lax.dot_general(ovl_ref[...], psum, _NT, precision=_HI, preferred_element_type=jnp.float32)[:nb]
        jrow = lax.broadcasted_iota(jnp.int32, (nb, t), 0)
        tpos_t = qs + lax.broadcasted_iota(jnp.int32, (nb, t), 1)
        cur = jnp.right_shift(tpos_t, int(np.log2(SEL_BLOCK)))
        forced = (jrow == 0) | (jrow == cur) | (jrow == cur - 1)
        score = jnp.where(forced, FORCE_SCORE, jnp.where(jrow * SEL_BLOCK <= tpos_t, imp_t, -1.0))
        rank = jnp.zeros((nb, t), jnp.float32)
        for j in range(n_blk):
            cj = score[j:j + 1, :]
            ahead = (cj > score) | ((cj == score) & (jrow > j))
            rank = rank + jnp.where(ahead, 1.0, 0.0)
        mask_t = jnp.where((rank < n_sel) & (jrow < n_blk), 0.0, -MASK_BIG)
        maskq_sc[par] = jnp.concatenate([mask_t, jnp.zeros((LANES - nb, t), jnp.float32)], axis=0).T

    par = 0
    q = q_ref[0]

    for r in range(NSA_GROUP):
        rows = slice(r * t, (r + 1) * t)
        qaugw_sc[rows, :dh] = q[:, r * dh:(r + 1) * dh]
        qaugw_sc[rows, dh:] = jnp.broadcast_to(qtab[r:r + 1, :], (t, LANES)).astype(jnp.bfloat16)
    wk = wpad + t
    nw = wk // LANES
    kw = kaug_w[pl.ds(qstart, wk), :]
    vw = vaug_w[pl.ds(qstart, wk), :]
    sws = [lax.dot_general(qaugw_sc[prow], kw, _NT, preferred_element_type=jnp.float32) for prow in prows]

    compress_and_select(q, qt, par)

    mask_q = maskq_sc[par]
    for r in range(NSA_GROUP):
        rows = slice(r * t, (r + 1) * t)
        qaug_sc[rows, :dh] = q[:, r * dh:(r + 1) * dh]
        qaug_sc[rows, dh:] = (mask_q + qtab[r:r + 1, :]).astype(jnp.bfloat16)

    for pair, prow in enumerate(prows):
        sw = sws[pair]
        for r2 in range(2):
            rows = slice((2 * pair + r2) * t, (2 * pair + r2 + 1) * t)
            tiles = [sw[r2 * t:(r2 + 1) * t, j * LANES:(j + 1) * LANES] for j in range(nw)]
            tiles[0] = jnp.where(col > row, tiles[0], NEG_INF)
            tiles[-1] = jnp.where(col <= row, tiles[-1], NEG_INF)
            mx = functools.reduce(jnp.maximum, tiles)
            m = jnp.broadcast_to(jnp.max(mx, axis=-1, keepdims=True), (t, LANES))
            for j in range(nw):
                pw_sc[rows, j * LANES:(j + 1) * LANES] = jnp.exp2(tiles[j] - m).astype(jnp.bfloat16)
        ow = jnp.dot(pw_sc[prow], vw, preferred_element_type=jnp.float32)
        owin_sc[prow] = ow[:, :dh] / ow[:, dh:]

    ch = SEL_CHUNK
    nl = ch // LANES
    n_full = qt // (ch // t)
    gt = gt_ref[0]
    cmr = col - row

    def selected(k):
        nk = (k + 1) * ch
        kk = kaug_s[:nk, :]
        vv = vaug_s[:nk, :]
        sks = [lax.dot_general(qaug_sc[prow], kk, _NT, preferred_element_type=jnp.float32) for prow in prows]
        for pair, prow in enumerate(prows):
            sk = sks[pair]
            for r2 in range(2):
                rows = slice((2 * pair + r2) * t, (2 * pair + r2 + 1) * t)
                tiles = [sk[r2 * t:(r2 + 1) * t, j * LANES:(j + 1) * LANES] for j in range(nk // LANES)]
                for j in range(k * nl, (k + 1) * nl):
                    tiles[j] = jnp.where(cmr <= qstart - j * LANES, tiles[j], NEG_INF)
                mx = functools.reduce(jnp.maximum, tiles)
                m = jnp.broadcast_to(jnp.max(mx, axis=-1, keepdims=True), (t, LANES))
                for j in range(nk // LANES):
                    p_sc[rows, j * LANES:(j + 1) * LANES] = jnp.exp2(tiles[j] - m).astype(jnp.bfloat16)
            pv = jnp.dot(p_sc[prow, :nk], vv, preferred_element_type=jnp.float32)
            for r2 in range(2):
                r = 2 * pair + r2
                rows = slice(r * t, (r + 1) * t)
                orow = slice(r2 * t, (r2 + 1) * t)
                o = (gt[:, 3 * r:3 * r + 1] * ocmp_sc[par, rows]
                     + gt[:, 3 * r + 1:3 * r + 2] * (pv[orow, :dh] / pv[orow, dh:])
                     + gt[:, 3 * r + 2:3 * r + 3] * owin_sc[rows])
                o_ref[0, :, r * dh:(r + 1) * dh] = o.astype(o_ref.dtype)

    for k in range(s_len // ch):
        pl.when(n_full == k)(functools.partial(selected, k))
    if cast_place:
        drain_cast()


def _cast_chunks(shape, n_steps):
    r, c = shape
    for n_c in (1, 2, 4, 8, 16):
        if n_steps % n_c == 0 and c % (n_c * LANES) == 0 and r % (n_steps // n_c) == 0:
            rows = r // (n_steps // n_c)
            if rows % 16 == 0:
                return rows, c // n_c
    raise ValueError("no tile-aligned split of %s into %d chunks" % (shape, n_steps))


def _cast_side_job(n, n_steps, srcs, dsts, place, inbufs, outbufs, in_sem, out_sem):
    slot = n % 2

    def chunk(k, step, col0=0):
        rows, cols = inbufs[k].shape[1:]
        n_c = srcs[k].shape[1] // cols
        return (pl.ds(pl.multiple_of((step // n_c) * rows, 16), rows),
                pl.ds(pl.multiple_of(col0 + (step % n_c) * cols, LANES), cols))

    def in_copy(k, step, sl):
        return pltpu.make_async_copy(srcs[k].at[chunk(k, step)], inbufs[k].at[sl], in_sem.at[sl])

    def out_copy(k, step, sl):
        dst, col0 = place[k]
        return pltpu.make_async_copy(outbufs[k].at[sl], dsts[dst].at[chunk(k, step, col0)], out_sem.at[sl])

    ks = range(len(srcs))

    @pl.when(n == 0)
    def _():
        for k in ks:
            in_copy(k, 0, 0).start()
            outbufs[k][...] = jnp.zeros(outbufs[k].shape, outbufs[k].dtype)
            out_copy(k, 0, 0).start()
            out_copy(k, 1, 1).start()

    for k in ks:
        in_copy(k, n, slot).wait()
    nxt = jnp.minimum(n + 1, n_steps - 1)
    for k in ks:
        in_copy(k, nxt, 1 - slot).start()
    for k in ks:
        out_copy(k, n, slot).wait()

    for k in ks:
        outbufs[k][slot] = inbufs[k][slot].astype(outbufs[k].dtype)
    for k in ks:
        out_copy(k, n, slot).start()

    def drain():
        @pl.when(n == n_steps - 1)
        def _():
            for k in ks:
                in_copy(k, n, 1 - slot).wait()
                out_copy(k, n, 1 - slot).wait()
                out_copy(k, n, slot).wait()

    return drain


def _bf16_pieces(x):
    a0 = x.astype(jnp.bfloat16).astype(jnp.float32)
    a1 = (x - a0).astype(jnp.bfloat16).astype(jnp.float32)
    a2 = (x - a0 - a1).astype(jnp.bfloat16).astype(jnp.float32)
    return [a0, a1, a2]


def _nsa_tables(s, slopes2):
    n_cmp = s // CMP_STRIDE - CMP_BLOCK // CMP_STRIDE + 1
    n_blk = s // SEL_BLOCK
    assert n_cmp < LANES and n_blk <= AUX_SLOPE
    bj = np.arange(LANES)[:, None]
    cn = np.arange(LANES)[None, :]
    ovl_t = np.clip(np.minimum(cn * CMP_STRIDE + CMP_BLOCK, (bj + 1) * SEL_BLOCK)
                    - np.maximum(cn * CMP_STRIDE, bj * SEL_BLOCK), 0, None) / CMP_STRIDE
    ovl_t = np.where((cn < n_cmp) & (bj < n_blk), ovl_t, 0.0).astype(np.float32)
    pos = np.arange(s)
    kaux = np.zeros((s, LANES), np.float32)
    kaux[pos, pos // SEL_BLOCK] = 1.0
    kaux[:, AUX_SLOPE:AUX_SLOPE + 3] = (pos // LANES * LANES)[:, None]
    kaux[:, AUX_SLOPE + 3:AUX_SLOPE + 6] = (pos % LANES)[:, None]
    kaux_w = np.zeros((NSA_WINDOW + s, LANES), np.float32)
    kaux_w[NSA_WINDOW:, AUX_SLOPE:AUX_SLOPE + 6] = kaux[:, AUX_SLOPE:AUX_SLOPE + 6]
    kaux_w[:NSA_WINDOW, AUX_PAD] = 1.0
    pieces = jnp.stack(_bf16_pieces(slopes2) * 2, axis=-1)
    qtab = jnp.zeros((N_HEADS, LANES), jnp.float32)
    qtab = qtab.at[:, AUX_SLOPE:AUX_SLOPE + 6].set(pieces).at[:, AUX_PAD].set(-MASK_BIG)
    qtab = jnp.pad(qtab.reshape(NSA_KV_HEADS, NSA_GROUP, LANES), ((0, 0), (0, 8 - NSA_GROUP), (0, 0)))
    return (n_cmp, n_blk, jnp.asarray(ovl_t), qtab, jnp.asarray(kaux, dtype=jnp.bfloat16),
            jnp.asarray(kaux_w, dtype=jnp.bfloat16))


def _nsa_attention(proj, cmp_kv, gates, slopes2, cast_ws=(), cast_place=(), cast_shapes=()):
    b, s, _ = proj.shape
    t = ATT_TILE
    n_steps = b * NSA_KV_HEADS * (s // t)
    chunks = [_cast_chunks(w.shape, n_steps) for w in cast_ws]
    assert n_steps >= 2
    any_spec = pl.BlockSpec(memory_space=pl.ANY)
    n_cmp, n_blk, ovl_t, qtab, kaux_s, kaux_w = _nsa_tables(s, slopes2)
    assert cmp_kv.shape[3] == LANES and s % SEL_CHUNK == 0 and NSA_WINDOW % t == 0
    qw = NSA_GROUP * HEAD_DIM
    rows = NSA_GROUP * t
    kv0 = ATT_WIDTH // HEAD_DIM

    def kv_spec(j):
        return pl.BlockSpec((1, s, HEAD_DIM), lambda bi, g, qi, j=j: (bi, 0, kv0 + j * NSA_KV_HEADS + g))

    def const(shape):
        return pl.BlockSpec(shape, lambda bi, g, qi: (0,) * len(shape))

    outs = pl.pallas_call(
        functools.partial(_nsa_attn_kernel, n_cmp=n_cmp, n_blk=n_blk, n_sel=min(N_SEL, n_blk),
                          cast_place=tuple(cast_place), n_cast_dst=len(cast_shapes)),
        grid=(b, NSA_KV_HEADS, s // t),
        in_specs=[pl.BlockSpec(memory_space=pltpu.SMEM),
                  pl.BlockSpec((1, t, qw), lambda bi, g, qi: (bi, qi, g)),
                  pl.BlockSpec((1, 1, 1, LANES, HEAD_DIM), lambda bi, g, qi: (bi, 0, g, 0, 0)),
                  pl.BlockSpec((1, 1, 1, LANES, HEAD_DIM), lambda bi, g, qi: (bi, 1, g, 0, 0)),
                  kv_spec(2), kv_spec(3), kv_spec(4), kv_spec(5),
                  pl.BlockSpec((1, t, LANES), lambda bi, g, qi: (bi, qi, g)),
                  const((LANES, LANES)),
                  pl.BlockSpec((1, 8, LANES), lambda bi, g, qi: (g, 0, 0)),
                  const((s, LANES)), const((NSA_WINDOW + s, LANES))] + [any_spec] * len(cast_ws),
        out_specs=[pl.BlockSpec((1, t, qw), lambda bi, g, qi: (bi, qi, g))] + [any_spec] * len(cast_shapes),
        out_shape=[jax.ShapeDtypeStruct((b, s, ATT_WIDTH), jnp.bfloat16)]
        + [jax.ShapeDtypeStruct(shape, jnp.bfloat16) for shape in cast_shapes],
        scratch_shapes=[pltpu.VMEM((1, rows, HEAD_DIM), jnp.float32),
                        pltpu.VMEM((1, t, LANES), jnp.float32),
                        pltpu.VMEM((rows, HEAD_DIM), jnp.float32),
                        pltpu.VMEM((rows, 2 * HEAD_DIM), jnp.bfloat16),
                        pltpu.VMEM((rows, 2 * HEAD_DIM), jnp.bfloat16),
                        pltpu.VMEM((rows, s), jnp.bfloat16),
                        pltpu.VMEM((rows, NSA_WINDOW + t), jnp.bfloat16),
                        pltpu.VMEM((s, 2 * HEAD_DIM), jnp.bfloat16),
                        pltpu.VMEM((s, 2 * HEAD_DIM), jnp.bfloat16),
                        pltpu.VMEM((NSA_WINDOW + s, 2 * HEAD_DIM), jnp.bfloat16),
                        pltpu.VMEM((NSA_WINDOW + s, 2 * HEAD_DIM), jnp.bfloat16)]
        + [pltpu.VMEM((2,) + c, jnp.float32) for c in chunks]
        + [pltpu.VMEM((2,) + c, jnp.bfloat16) for c in chunks]
        + [pltpu.SemaphoreType.DMA((2,)), pltpu.SemaphoreType.DMA((2,))],
        compiler_params=_params(("arbitrary", "arbitrary", "arbitrary")),
        name="nsa_attention",
    )(slopes2, proj, cmp_kv, cmp_kv, proj, proj, proj, proj, gates, ovl_t, qtab, kaux_s, kaux_w, *cast_ws)
    return outs[0], list(outs[1:])


def _dil_attn_kernel(sl_ref, q_ref, *refs, window, has_prev):
    if has_prev:
        kp_ref, kc_ref, vp_ref, vc_ref, o_ref, lse_ref, bias_sc = refs
    else:
        kc_ref, vc_ref, o_ref, lse_ref, bias_sc = refs
        kp_ref, vp_ref = kc_ref, vc_ref
    t = ATT_TILE
    i = pl.program_id(2)

    @pl.when((pl.program_id(0) == 0) & (pl.program_id(1) == 0) & (i == 0))
    def _():
        row = lax.broadcasted_iota(jnp.int32, (t, 2 * t), 0)
        col = lax.broadcasted_iota(jnp.int32, (t, 2 * t), 1)
        dist = row - col + t
        band = (dist >= 0) & (dist <= window)
        distf = dist.astype(jnp.float32)
        for h in range(N_HEADS):
            bias = sl_ref[h] * distf
            bias_sc[0, h] = jnp.where(band & (col >= t), bias, -NEG_INF)
            bias_sc[1, h] = jnp.where(band, bias, -NEG_INF)

    var = jnp.minimum(i, 1)
    lane = lax.broadcasted_iota(jnp.int32, (t, LANES), 1)
    lse_all = jnp.zeros((t, LANES), jnp.float32)
    for h in range(N_HEADS):
        hs = slice(h * HEAD_DIM, (h + 1) * HEAD_DIM)
        q = q_ref[0, 0, :, hs]
        k = jnp.concatenate([kp_ref[0, 0, :, hs], kc_ref[0, 0, :, hs]], axis=0)
        v = jnp.concatenate([vp_ref[0, 0, :, hs], vc_ref[0, 0, :, hs]], axis=0)
        s = lax.dot_general(q, k, _NT, preferred_element_type=jnp.float32) - bias_sc[var, h]
        m = jnp.max(s, axis=-1, keepdims=True)
        e = jnp.exp2(s - m)
        l = jnp.sum(e, axis=-1, keepdims=True)
        o = jnp.dot(e.astype(v.dtype), v, preferred_element_type=jnp.float32) / l
        o_ref[0, 0, :, hs] = o.astype(o_ref.dtype)
        lse_all = jnp.where(lane == h, m + jnp.log2(l), lse_all)
    lse_ref[0, 0] = lse_all


def _dil_attention(q, kv, slopes2, win, r):
    b, _, ln, _ = q.shape
    t = ATT_TILE

    def cur(c):
        return pl.BlockSpec((1, 1, t, ATT_WIDTH), lambda bi, rho, i, c=c: (bi, rho, i, c))

    def prev(c):
        return pl.BlockSpec((1, 1, t, ATT_WIDTH), lambda bi, rho, i, c=c: (bi, rho, jnp.maximum(i - 1, 0), c))

    has_prev = ln > t
    kv_specs = [prev(0), cur(0), prev(1), cur(1)] if has_prev else [cur(0), cur(1)]
    return pl.pallas_call(
        functools.partial(_dil_attn_kernel, window=win // r, has_prev=has_prev),
        grid=(b, r, ln // t),
        in_specs=[pl.BlockSpec(memory_space=pltpu.SMEM), cur(0)] + kv_specs,
        out_specs=[pl.BlockSpec((1, 1, t, ATT_WIDTH), lambda bi, rho, i: (bi, rho, i, 0)),
                   pl.BlockSpec((1, 1, t, LANES), lambda bi, rho, i: (bi, rho, i, 0))],
        out_shape=[jax.ShapeDtypeStruct((b, r, ln, ATT_WIDTH), jnp.bfloat16),
                   jax.ShapeDtypeStruct((b, r, ln, LANES), jnp.float32)],
        scratch_shapes=[pltpu.VMEM((2, N_HEADS, t, 2 * t), jnp.float32)],
        compiler_params=_params(("arbitrary", "arbitrary", "arbitrary")),
        name="dilated_attention_r%d" % r,
    )(slopes2 * r, q, *([kv] * len(kv_specs)))


def _dil_merge_kernel(*refs, dilations):
    ng = len(dilations)
    o_refs, l_refs, out_ref, o_sc, l_sc = refs[:ng], refs[ng:2 * ng], refs[2 * ng], refs[2 * ng + 1], refs[2 * ng + 2]
    ts = out_ref.shape[1]
    ls = []
    for gi, r in enumerate(dilations):
        if r == 1:
            ls.append(l_refs[gi][0, 0])
            continue
        for rho in range(r):
            l_sc[gi, pl.ds(rho, ts // r, stride=r), :] = l_refs[gi][0, rho]
            for h in range(N_HEADS):
                o_sc[gi, h, pl.ds(rho, ts // r, stride=r), :] = (
                    o_refs[gi][0, rho, :, h * HEAD_DIM:(h + 1) * HEAD_DIM].astype(jnp.float32))
        ls.append(l_sc[gi])
    m = functools.reduce(jnp.maximum, ls)
    es = [jnp.exp2(l - m) for l in ls]
    den = functools.reduce(lambda a, b: a + b, es)
    ws = [e / den for e in es]
    for h in range(N_HEADS):
        hs = slice(h * HEAD_DIM, (h + 1) * HEAD_DIM)
        acc = None
        for gi, r in enumerate(dilations):
            og = o_refs[gi][0, 0, :, hs].astype(jnp.float32) if r == 1 else o_sc[gi, h]
            term = ws[gi][:, h:h + 1] * og
            acc = term if acc is None else acc + term
        out_ref[0, :, hs] = acc.astype(out_ref.dtype)


def _dil_merge(outs, lses, dilations):
    b, _, _, w = outs[0].shape
    s = outs[0].shape[1] * outs[0].shape[2]
    ts = min(256, s)
    ng = len(dilations)

    def spec(r, width):
        return pl.BlockSpec((1, r, ts // r, width), lambda i, j: (i, 0, j, 0))

    return pl.pallas_call(
        functools.partial(_dil_merge_kernel, dilations=dilations),
        grid=(b, s // ts),
        in_specs=[spec(r, w) for r in dilations] + [spec(r, LANES) for r in dilations],
        out_specs=pl.BlockSpec((1, ts, w), lambda i, j: (i, j, 0)),
        out_shape=jax.ShapeDtypeStruct((b, s, w), jnp.bfloat16),
        scratch_shapes=[pltpu.VMEM((ng, w // HEAD_DIM, ts, HEAD_DIM), jnp.float32),
                        pltpu.VMEM((ng, ts, LANES), jnp.float32)],
        compiler_params=_params(("parallel", "parallel")),
        name="dilated_merge",
    )(*outs, *lses)


ROW_TILE = 8


def _gather_rows(idx_ref, src_hbm, dst, sem):
    def body(i, carry):
        for j in range(ROW_TILE):
            pltpu.make_async_copy(src_hbm.at[idx_ref[0, 0, i * ROW_TILE + j]], dst.at[i, :, j], sem).start()
        return carry
    lax.fori_loop(0, dst.shape[0], body, 0)


def _gather_tiled_rows(idx_ref, src_hbm, dst, sem):
    def body(i, carry):
        for j in range(ROW_TILE):
            pltpu.make_async_copy(src_hbm.at[pl.ds(idx_ref[0, 0, i * ROW_TILE + j], 1)],
                                  dst.at[i, pl.ds(j, 1)], sem).start()
        return carry
    lax.fori_loop(0, dst.shape[0], body, 0)


def _wait_rows(dst, sem):
    pltpu.make_async_copy(dst, dst, sem).wait()


def _expert_kernel(be_ref, nu_ref, tokc_ref, tokn_ref, h_hbm, wgu_ref, wd_ref, o_ref, xbuf, sem):
    del be_ref
    de = wd_ref.shape[1]
    blk = pl.program_id(0)
    n_used = nu_ref[0]
    slot = blk % 2

    @pl.when(blk == 0)
    def _():
        _gather_rows(tokc_ref, h_hbm, xbuf.at[0], sem.at[0])

    @pl.when(blk + 1 < n_used)
    def _():
        _gather_rows(tokn_ref, h_hbm, xbuf.at[1 - slot], sem.at[1 - slot])

    @pl.when(blk < n_used)
    def _():
        _wait_rows(xbuf.at[slot], sem.at[slot])
        x = _load_row_tiles(xbuf.at[slot]).astype(jnp.bfloat16)
        gate = jnp.dot(x, wgu_ref[0, :, :de], preferred_element_type=jnp.float32)
        up = jnp.dot(x, wgu_ref[0, :, de:], preferred_element_type=jnp.float32)
        hid = (gate * jax.nn.sigmoid(gate) * up).astype(jnp.bfloat16)
        o_ref[...] = jnp.dot(hid, wd_ref[0], preferred_element_type=jnp.float32)

    @pl.when(blk >= n_used)
    def _():
        o_ref[...] = jnp.zeros(o_ref.shape, o_ref.dtype)


def _expert_blocks(h3, slot_tok, blk_e, n_used, w_gate_up, w_down):
    t, ct, _ = h3.shape
    d = ct * LANES
    nblk = blk_e.shape[0]
    de = w_down.shape[1]
    assert de % LANES == 0
    tok = slot_tok.reshape(nblk, 1, MOE_BLOCK)
    grid_spec = pltpu.PrefetchScalarGridSpec(
        num_scalar_prefetch=2,
        grid=(nblk,),
        in_specs=[pl.BlockSpec((1, 1, MOE_BLOCK), lambda i, be, nu: (i, 0, 0), memory_space=pltpu.SMEM),
                  pl.BlockSpec((1, 1, MOE_BLOCK), lambda i, be, nu: (jnp.minimum(i + 1, nblk - 1), 0, 0),
                               memory_space=pltpu.SMEM),
                  pl.BlockSpec(memory_space=pl.ANY),
                  pl.BlockSpec((1, d, 2 * de), lambda i, be, nu: (be[i], 0, 0)),
                  pl.BlockSpec((1, de, d), lambda i, be, nu: (be[i], 0, 0))],
        out_specs=pl.BlockSpec((MOE_BLOCK, d), lambda i, be, nu: (i, 0)),
        scratch_shapes=[pltpu.VMEM((2, MOE_BLOCK // ROW_TILE, ct, ROW_TILE, LANES), jnp.float32),
                        pltpu.SemaphoreType.DMA((2,))],
    )
    return pl.pallas_call(
        _expert_kernel,
        grid_spec=grid_spec,
        out_shape=jax.ShapeDtypeStruct((nblk * MOE_BLOCK, d), jnp.float32),
        compiler_params=_params(("arbitrary",)),
        name="moe_expert_blocks",
    )(blk_e, n_used, tok, tok, h3, w_gate_up, w_down)


def _combine_kernel(d0c_ref, d1c_ref, d0n_ref, d1n_ref, x_ref, w0_ref, w1_ref, g_ref, gn_ref, yo_hbm, o_ref,
                    buf, sem, *, out_norm):
    i = pl.program_id(0)
    n = pl.num_programs(0)
    slot = i % 2
    tt = x_ref.shape[0]

    @pl.when(i == 0)
    def _():
        _gather_tiled_rows(d0c_ref, yo_hbm, buf.at[0, 0], sem.at[0])
        _gather_tiled_rows(d1c_ref, yo_hbm, buf.at[0, 1], sem.at[0])

    @pl.when(i + 1 < n)
    def _():
        _gather_tiled_rows(d0n_ref, yo_hbm, buf.at[1 - slot, 0], sem.at[1 - slot])
        _gather_tiled_rows(d1n_ref, yo_hbm, buf.at[1 - slot, 1], sem.at[1 - slot])

    _wait_rows(buf.at[slot, 0], sem.at[slot])
    _wait_rows(buf.at[slot, 1], sem.at[slot])
    y = (w0_ref[...][:, 0:1] * buf[slot, 0].reshape(x_ref.shape)
         + w1_ref[...][:, 0:1] * buf[slot, 1].reshape(x_ref.shape))
    out = x_ref[...] + g_ref[0] * y
    o_ref[...] = _rms(out, gn_ref[...]) if out_norm else out


def _moe_combine(x2d, yo, dest, wts, gate, s, out_norm_g=None):
    t, d = x2d.shape
    out_norm = out_norm_g is not None
    gn = (out_norm_g if out_norm else jnp.ones((d,), jnp.float32)).reshape(1, d)
    tt = min(256, s)
    nt = t // tt
    d0 = dest[:, 0].reshape(nt, 1, tt)
    d1 = dest[:, 1].reshape(nt, 1, tt)
    w0 = jnp.broadcast_to(wts[:, 0:1], (t, LANES))
    w1 = jnp.broadcast_to(wts[:, 1:2], (t, LANES))
    b = gate.shape[0]
    cur = pl.BlockSpec((1, 1, tt), lambda i: (i, 0, 0), memory_space=pltpu.SMEM)
    nxt = pl.BlockSpec((1, 1, tt), lambda i: (jnp.minimum(i + 1, nt - 1), 0, 0), memory_space=pltpu.SMEM)
    return pl.pallas_call(
        functools.partial(_combine_kernel, out_norm=out_norm),
        grid=(nt,),
        in_specs=[cur, cur, nxt, nxt,
                  pl.BlockSpec((tt, d), lambda i: (i, 0)),
                  pl.BlockSpec((tt, LANES), lambda i: (i, 0)),
                  pl.BlockSpec((tt, LANES), lambda i: (i, 0)),
                  pl.BlockSpec((1, 1, d), lambda i: (i * tt // s, 0, 0)),
                  pl.BlockSpec((1, d), lambda i: (0, 0)),
                  pl.BlockSpec(memory_space=pl.ANY)],
        out_specs=pl.BlockSpec((tt, d), lambda i: (i, 0)),
        out_shape=jax.ShapeDtypeStruct((t, d), jnp.float32),
        scratch_shapes=[pltpu.VMEM((2, 2, tt // ROW_TILE, ROW_TILE, d), jnp.float32),
                        pltpu.SemaphoreType.DMA((2,))],
        compiler_params=_params(("arbitrary",)),
        name="moe_combine_residual",
    )(d0, d1, d0, d1, x2d, w0, w1, gate.reshape(b, 1, d), gn, yo)


def _dispatch_tables(idx):
    t = idx.shape[0]
    a = t * TOP_K
    flat_e = idx.reshape(a)
    onehot = (flat_e[:, None] == jnp.arange(N_EXPERTS, dtype=jnp.int32)[None, :]).astype(jnp.int32)
    csum = jnp.cumsum(onehot, axis=0)
    rank = jnp.take_along_axis(csum, flat_e[:, None], axis=1)[:, 0] - 1
    counts = csum[-1]
    padded = (counts + MOE_BLOCK - 1) // MOE_BLOCK * MOE_BLOCK
    pad_end = jnp.cumsum(padded)
    pad_start = pad_end - padded
    dest = pad_start[flat_e] + rank
    nblk = -(-(a + N_EXPERTS * MOE_BLOCK) // MOE_BLOCK)
    cap = nblk * MOE_BLOCK
    slot_tok = jnp.zeros((cap,), jnp.int32).at[dest].set(jnp.arange(a, dtype=jnp.int32) // TOP_K,
                                                         unique_indices=True)
    blk_start = jnp.arange(nblk, dtype=jnp.int32) * MOE_BLOCK
    blk_e = jnp.minimum(jnp.sum((pad_end[None, :] <= blk_start[:, None]).astype(jnp.int32), axis=1), N_EXPERTS - 1)
    n_used = (pad_end[-1:] // MOE_BLOCK).astype(jnp.int32)
    return dest.reshape(t, TOP_K).astype(jnp.int32), slot_tok, blk_e.astype(jnp.int32), n_used


def _moe_layer(x, g, sc, sh, gate, router_w, router_b, layer, w_gate_up, w_down, out_norm_g=None):
    b, s, d = x.shape
    t = b * s
    h, idx, wts = _norm_route(x, g, sc, sh, router_w, router_b)
    idx = idx.transpose(0, 2, 1).reshape(t, TOP_K)
    wts = wts.transpose(0, 2, 1).reshape(t, TOP_K)
    dest, slot_tok, blk_e, n_used = _dispatch_tables(idx)
    yo = _expert_blocks(h, slot_tok, blk_e + layer * N_EXPERTS, n_used, w_gate_up, w_down)
    return _moe_combine(x.reshape(t, d), yo, dest, wts, gate, s, out_norm_g).reshape(b, s, d)


def _nsa_layer(h, x, gate, slopes, j, w_in, w_phi1, w_phi2, phi_pos, w_out, cast_job=((), (), ())):
    b, s, d = h.shape
    t = b * s
    h2d = h.reshape(t, d)
    proj, kv_chunks = _nsa_projection(h2d, w_in, j, s, HEAD_DIM ** -0.5 * LOG2E)
    proj = proj.reshape(b, s, NSA_QKV)
    wg = w_in[j, :, NSA_QKV:].reshape(d, NSA_KV_HEADS, NSA_GATES)
    wg = jnp.pad(wg, ((0, 0), (0, 0), (0, LANES - NSA_GATES))).reshape(1, d, NSA_KV_HEADS * LANES)
    gates = _matmul(h2d, wg, 0, 0, NSA_KV_HEADS * LANES, jnp.float32, act="sigmoid")
    gates = gates.reshape(b, s, NSA_KV_HEADS * LANES)
    cmp_kv = _compress(kv_chunks, w_phi1, w_phi2, phi_pos)
    o, casted = _nsa_attention(proj, cmp_kv, gates, slopes * LOG2E, *cast_job)
    return _matmul_residual(o, w_out, j, x, gate), casted


def _dil_layer(x, g, sc, sh, gate, slopes, j, w_in, w_out):
    b, s, d = x.shape
    dilations = tuple(r for _, r in DIL_PAIRS)
    assert dilations[0] == 1
    hs = _norm_mod_streams(x, g, sc, sh, dilations)
    qs = _matmul_streams(hs[0].reshape(b, s, d), w_in, j, 0, ATT_WIDTH, dilations,
                         scale=HEAD_DIM ** -0.5 * LOG2E)
    outs, lses = [], []
    for gidx, (win, r) in enumerate(DIL_PAIRS):
        off = ATT_WIDTH * (1 + 2 * gidx)
        kv = _matmul(hs[gidx].reshape(b * s, d), w_in, j, off, 2 * ATT_WIDTH, jnp.bfloat16)
        o, lse = _dil_attention(qs[gidx], kv.reshape(b, r, s // r, 2 * ATT_WIDTH), slopes * LOG2E, win, r)
        outs.append(o)
        lses.append(lse)
    o = _dil_merge(outs, lses, dilations)
    return _matmul_residual(o, w_out, j, x, gate)


def kernel(x, c, ada_w, ada_b, norm_mix, norm_ffn, norm_final, nsa_w_in, nsa_w_phi1, nsa_w_phi2, nsa_phi_pos,
           nsa_w_out, dil_w_in, dil_w_out, router_w, router_b, exp_w_gate, exp_w_up, exp_w_down):
    depth = ada_w.shape[0]
    d = x.shape[-1]
    mod = _modulation(c, ada_w, ada_b)
    slopes = 2.0 ** (-ALIBI_MAX_BIAS * jnp.arange(1, N_HEADS + 1, dtype=jnp.float32) / N_HEADS)
    n_exp, _, de = exp_w_gate.shape[1:]
    for i in range(depth):
        sh_m, sc_m, g_m, sh_f, sc_f, g_f = [mod[i, :, k * d:(k + 1) * d] for k in range(6)]
        j = i // 2
        if i % 2 == 0:
            h = _norm_mod(x, norm_mix[i], sc_m, sh_m, jnp.bfloat16)
            cast_job = ((), (), ())
            if i == 0:
                srcs = [w.reshape(-1, w.shape[-1]) for w in (exp_w_gate, exp_w_up, exp_w_down)]
                cast_job = (srcs, ((0, 0), (0, de), (1, 0)), ((srcs[0].shape[0], 2 * de), srcs[2].shape))
            x, casted = _nsa_layer(h, x, g_m, slopes, j, nsa_w_in, nsa_w_phi1[j], nsa_w_phi2[j], nsa_phi_pos[j],
                                   nsa_w_out, cast_job)
            if i == 0:
                e_gate_up = casted[0].reshape(depth * n_exp, d, 2 * de)
                e_down = casted[1].reshape(depth * n_exp, de, d)
        else:
            x = _dil_layer(x, norm_mix[i], sc_m, sh_m, g_m, slopes, j, dil_w_in, dil_w_out)
        x = _moe_layer(x, norm_ffn[i], sc_f, sh_f, g_f, router_w, router_b, i, e_gate_up, e_down,
                       norm_final if i == depth - 1 else None)
    return x
```

```python
import functools

import jax
import jax.numpy as jnp
import numpy as np
from jax import lax
from jax.experimental import pallas as pl
from jax.experimental.pallas import tpu as pltpu

HEAD_DIM = 128
N_HEADS = 16
ATT_WIDTH = N_HEADS * HEAD_DIM
ALIBI_MAX_BIAS = 8.0

NSA_KV_HEADS = 4
NSA_GROUP = N_HEADS // NSA_KV_HEADS
NSA_KV_WIDTH = NSA_KV_HEADS * HEAD_DIM
CMP_BLOCK = 32
CMP_STRIDE = 16
SEL_BLOCK = 64
N_SEL = 16
NSA_WINDOW = 512
FORCE_SCORE = 1.0e6
NSA_QKV = ATT_WIDTH + 6 * NSA_KV_WIDTH
NSA_GATES = 3 * NSA_GROUP

DIL_PAIRS = ((128, 1), (512, 4), (2048, 16))
DIL_PROJ = ATT_WIDTH * (1 + 2 * len(DIL_PAIRS))

N_EXPERTS = 16
N_EXPERT_GROUPS = 4
EXPERTS_PER_GROUP = N_EXPERTS // N_EXPERT_GROUPS
TOP_K = 2
MOE_BLOCK = 256

RMS_EPS = 1e-6
NEG_INF = -1.0e30

LANES = 128
ATT_TILE = 128
ROW_BLOCK = 512
VMEM_LIMIT = 56 * 1024 * 1024

_HI = lax.Precision.HIGHEST
_NT = (((1,), (1,)), ((), ()))


def _params(sem, vmem=VMEM_LIMIT):
    return pltpu.CompilerParams(dimension_semantics=sem, vmem_limit_bytes=vmem)


def _mod_kernel(c_ref, w_ref, b_ref, o_ref):
    c = c_ref[...]
    cond = c * jax.nn.sigmoid(c)
    o_ref[0] = jnp.dot(cond, w_ref[0], precision=_HI, preferred_element_type=jnp.float32) + b_ref[0]


def _modulation(c, ada_w, ada_b):
    depth, d, n = ada_w.shape
    b = c.shape[0]
    tn = 512
    return pl.pallas_call(
        _mod_kernel,
        grid=(depth, n // tn),
        in_specs=[pl.BlockSpec((b, d), lambda i, j: (0, 0)),
                  pl.BlockSpec((1, d, tn), lambda i, j: (i, 0, j)),
                  pl.BlockSpec((1, 1, tn), lambda i, j: (i, 0, j))],
        out_specs=pl.BlockSpec((1, b, tn), lambda i, j: (i, 0, j)),
        out_shape=jax.ShapeDtypeStruct((depth, b, n), jnp.float32),
        compiler_params=_params(("parallel", "parallel")),
        name="adaln_modulation",
    )(c, ada_w, ada_b.reshape(depth, 1, n))


def _rms(x, g):
    return x * lax.rsqrt(jnp.mean(x * x, axis=-1, keepdims=True) + RMS_EPS) * g


def _norm_mod_kernel(x_ref, g_ref, sc_ref, sh_ref, o_ref):
    h = _rms(x_ref[0], g_ref[...]) * (1.0 + sc_ref[0]) + sh_ref[0]
    o_ref[0] = h.astype(o_ref.dtype)


def _norm_mod(x, g, sc, sh, out_dtype):
    b, s, d = x.shape
    ts = min(ROW_BLOCK, s)
    return pl.pallas_call(
        _norm_mod_kernel,
        grid=(b, s // ts),
        in_specs=[pl.BlockSpec((1, ts, d), lambda i, j: (i, j, 0)),
                  pl.BlockSpec((1, d), lambda i, j: (0, 0)),
                  pl.BlockSpec((1, 1, d), lambda i, j: (i, 0, 0)),
                  pl.BlockSpec((1, 1, d), lambda i, j: (i, 0, 0))],
        out_specs=pl.BlockSpec((1, ts, d), lambda i, j: (i, j, 0)),
        out_shape=jax.ShapeDtypeStruct((b, s, d), out_dtype),
        compiler_params=_params(("parallel", "parallel")),
        name="rmsnorm_adaln",
    )(x, g.reshape(1, d), sc.reshape(b, 1, d), sh.reshape(b, 1, d))


def _store_streams(val, src_sc, out_refs, dilations):
    n = val.shape[0]
    for c in range(src_sc.shape[0]):
        src_sc[c] = val[:, c * LANES:(c + 1) * LANES]
    for r, ref in zip(dilations, out_refs):
        if r == 1:
            ref[0, 0] = val.astype(ref.dtype)
            continue
        for rho in range(r):
            for c in range(src_sc.shape[0]):
                ref[0, rho, :, c * LANES:(c + 1) * LANES] = (
                    src_sc[c, pl.ds(rho, n // r, stride=r), :].astype(ref.dtype))


def _norm_mod_streams_kernel(x_ref, g_ref, sc_ref, sh_ref, *refs, dilations):
    out_refs, h_sc = refs[:-1], refs[-1]
    h = _rms(x_ref[0], g_ref[...]) * (1.0 + sc_ref[0]) + sh_ref[0]
    _store_streams(h, h_sc, out_refs, dilations)


def _norm_mod_streams(x, g, sc, sh, dilations):
    b, s, d = x.shape
    ts = min(ROW_BLOCK, s)
    return pl.pallas_call(
        functools.partial(_norm_mod_streams_kernel, dilations=dilations),
        grid=(b, s // ts),
        in_specs=[pl.BlockSpec((1, ts, d), lambda i, j: (i, j, 0)),
                  pl.BlockSpec((1, d), lambda i, j: (0, 0)),
                  pl.BlockSpec((1, 1, d), lambda i, j: (i, 0, 0)),
                  pl.BlockSpec((1, 1, d), lambda i, j: (i, 0, 0))],
        out_specs=[pl.BlockSpec((1, r, ts // r, d), lambda i, j: (i, 0, j, 0)) for r in dilations],
        out_shape=[jax.ShapeDtypeStruct((b, r, s // r, d), jnp.bfloat16) for r in dilations],
        scratch_shapes=[pltpu.VMEM((d // LANES, ts, LANES), jnp.float32)],
        compiler_params=_params(("parallel", "parallel")),
        name="rmsnorm_adaln_streams",
    )(x, g.reshape(1, d), sc.reshape(b, 1, d), sh.reshape(b, 1, d))


def _pair_max(vals):
    out = None
    for a in range(len(vals)):
        for b in range(a + 1, len(vals)):
            s = vals[a] + vals[b]
            out = s if out is None else jnp.maximum(out, s)
    return out


def _store_row_major(ref, val):
    rows = val.shape[0]
    c_tiles = val.shape[1] // LANES
    for c in range(c_tiles):
        ref[pl.ds(c, rows, stride=c_tiles), :] = val[:, c * LANES:(c + 1) * LANES]


def _load_row_tiles(buf):
    n, c_tiles = buf.shape[0], buf.shape[1]
    return jnp.concatenate([buf[:, c].reshape(n * ROW_TILE, LANES) for c in range(c_tiles)], axis=1)


def _norm_route_kernel(x_ref, g_ref, sc_ref, sh_ref, rw_ref, rb_ref, h_ref, idx_ref, wt_ref):
    h = _rms(x_ref[0], g_ref[...]) * (1.0 + sc_ref[0]) + sh_ref[0]
    _store_row_major(h_ref, h)

    logits = lax.dot_general(rw_ref[...], h, _NT, precision=_HI, preferred_element_type=jnp.float32)
    scores = jax.nn.sigmoid(logits)
    biased = scores + rb_ref[...][:, 0:1]
    rows = [biased[e:e + 1, :] for e in range(N_EXPERTS)]
    srow = [scores[e:e + 1, :] for e in range(N_EXPERTS)]
    grp = [_pair_max(rows[q * EXPERTS_PER_GROUP:(q + 1) * EXPERTS_PER_GROUP]) for q in range(N_EXPERT_GROUPS)]
    best_v = grp[0]
    best = jnp.zeros(best_v.shape, jnp.int32)
    for q in range(1, N_EXPERT_GROUPS):
        take = grp[q] > best_v
        best = jnp.where(take, q, best)
        best_v = jnp.where(take, grp[q], best_v)
    v1 = jnp.full(best_v.shape, NEG_INF, jnp.float32)
    i1 = jnp.zeros(best_v.shape, jnp.int32)
    for e in range(N_EXPERTS):
        cand = jnp.where(best == e // EXPERTS_PER_GROUP, rows[e], NEG_INF)
        take = cand > v1
        i1 = jnp.where(take, e, i1)
        v1 = jnp.where(take, cand, v1)
    v2 = jnp.full(best_v.shape, NEG_INF, jnp.float32)
    i2 = jnp.zeros(best_v.shape, jnp.int32)
    for e in range(N_EXPERTS):
        cand = jnp.where(best == e // EXPERTS_PER_GROUP, jnp.where(i1 == e, NEG_INF, rows[e]), NEG_INF)
        take = cand > v2
        i2 = jnp.where(take, e, i2)
        v2 = jnp.where(take, cand, v2)
    w1 = jnp.zeros(best_v.shape, jnp.float32)
    w2 = jnp.zeros(best_v.shape, jnp.float32)
    for e in range(N_EXPERTS):
        w1 = jnp.where(i1 == e, srow[e], w1)
        w2 = jnp.where(i2 == e, srow[e], w2)
    tot = w1 + w2
    idx_ref[0] = jnp.concatenate([i1, i2], axis=0)
    wt_ref[0] = jnp.concatenate([w1 / tot, w2 / tot], axis=0)


def _norm_route(x, g, sc, sh, router_w, router_b):
    b, s, d = x.shape
    ts = min(ROW_BLOCK, s)
    ct = d // LANES
    rwt = router_w.T
    rb = jnp.broadcast_to(router_b.reshape(N_EXPERTS, 1), (N_EXPERTS, LANES))
    h, idx, wts = pl.pallas_call(
        _norm_route_kernel,
        grid=(b, s // ts),
        in_specs=[pl.BlockSpec((1, ts, d), lambda i, j: (i, j, 0)),
                  pl.BlockSpec((1, d), lambda i, j: (0, 0)),
                  pl.BlockSpec((1, 1, d), lambda i, j: (i, 0, 0)),
                  pl.BlockSpec((1, 1, d), lambda i, j: (i, 0, 0)),
                  pl.BlockSpec((N_EXPERTS, d), lambda i, j: (0, 0)),
                  pl.BlockSpec((N_EXPERTS, LANES), lambda i, j: (0, 0))],
        out_specs=[pl.BlockSpec((ts * ct, LANES), lambda i, j: (i * (s // ts) + j, 0)),
                   pl.BlockSpec((1, TOP_K, ts), lambda i, j: (i, 0, j)),
                   pl.BlockSpec((1, TOP_K, ts), lambda i, j: (i, 0, j))],
        out_shape=[jax.ShapeDtypeStruct((b * s * ct, LANES), jnp.float32),
                   jax.ShapeDtypeStruct((b, TOP_K, s), jnp.int32),
                   jax.ShapeDtypeStruct((b, TOP_K, s), jnp.float32)],
        compiler_params=_params(("parallel", "parallel")),
        name="rmsnorm_adaln_router",
    )(x, g.reshape(1, d), sc.reshape(b, 1, d), sh.reshape(b, 1, d), rwt, rb)
    return h.reshape(b * s, ct, LANES), idx, wts


def _weight_tile(w_ref, wbf_sc, first, scale, scale_tiles):
    @pl.when(first)
    def _():
        w = w_ref[0]
        if scale_tiles:
            w = w * jnp.where(pl.program_id(0) < scale_tiles, scale, 1.0)
        wbf_sc[...] = w.astype(jnp.bfloat16)
    return wbf_sc[...]


def _weight_spec(layer, col0, k, tn, rank):
    assert col0 % tn == 0
    if rank == 2:
        return pl.BlockSpec((1, k, tn), lambda j, i: (layer, 0, col0 // tn + j))
    return pl.BlockSpec((1, k, tn), lambda j, bi, i: (layer, 0, col0 // tn + j))


def _mm_kernel(a_ref, w_ref, o_ref, wbf_sc, *, act, scale, scale_tiles):
    w = _weight_tile(w_ref, wbf_sc, pl.program_id(1) == 0, scale, scale_tiles)
    acc = jnp.dot(a_ref[...], w, preferred_element_type=jnp.float32)
    if act == "sigmoid":
        acc = jax.nn.sigmoid(acc)
    o_ref[...] = acc.astype(o_ref.dtype)


def _matmul(a, w, layer, col0, n, out_dtype, act=None, scale=1.0, scale_cols=0, tm=2048, tn=1024):
    m, k = a.shape
    tn = min(tn, n)
    tm = min(tm, m)
    assert scale_cols % tn == 0
    return pl.pallas_call(
        functools.partial(_mm_kernel, act=act, scale=scale, scale_tiles=scale_cols // tn),
        grid=(n // tn, m // tm),
        in_specs=[pl.BlockSpec((tm, k), lambda j, i: (i, 0)),
                  _weight_spec(layer, col0, k, tn, 2)],
        out_specs=pl.BlockSpec((tm, tn), lambda j, i: (i, j)),
        out_shape=jax.ShapeDtypeStruct((m, n), out_dtype),
        scratch_shapes=[pltpu.VMEM((k, tn), jnp.bfloat16)],
        compiler_params=_params(("arbitrary", "arbitrary")),
        name="matmul" if act is None else "matmul_" + act,
    )(a, w)


def _nsa_proj_kernel(a_ref, w_ref, o_ref, ch_ref, wbf_sc, stage_sc, *, scale, scale_tiles, kv_tile, n_tiles):
    tile = (pl.program_id(0) + kv_tile + 1) % n_tiles

    @pl.when(pl.program_id(1) == 0)
    def _():
        w = w_ref[0]
        if scale_tiles:
            w = w * jnp.where(tile < scale_tiles, scale, 1.0)
        wbf_sc[...] = w.astype(jnp.bfloat16)

    acc = jnp.dot(a_ref[...], wbf_sc[...], preferred_element_type=jnp.float32)
    o_ref[...] = acc.astype(o_ref.dtype)

    @pl.when(pl.program_id(0) == n_tiles - 1)
    def _():
        nch = ch_ref.shape[3]
        for c in range(acc.shape[1] // HEAD_DIM):
            kv, g = divmod(c, NSA_KV_HEADS)
            stage_sc[...] = acc[:, c * HEAD_DIM:(c + 1) * HEAD_DIM]
            for l in range(CMP_STRIDE):
                ch_ref[0, kv, g, :, l * HEAD_DIM:(l + 1) * HEAD_DIM] = (
                    stage_sc[pl.ds(l, nch, stride=CMP_STRIDE), :].astype(ch_ref.dtype))


def _nsa_projection(a, w, layer, s, scale, tm=1024, tn=1024):
    m, k = a.shape
    b = m // s
    n_tiles = NSA_QKV // tn
    kv_tile = ATT_WIDTH // tn
    assert ATT_WIDTH % tn == 0 and 2 * NSA_KV_WIDTH == tn and s % tm == 0 and HEAD_DIM == LANES
    per_b = s // tm
    nch = tm // CMP_STRIDE

    def tile_of(j):
        return (j + kv_tile + 1) % n_tiles

    def chunk_block(j, i):
        last = j == n_tiles - 1
        return (jnp.where(last, i // per_b, 0), 0, 0, jnp.where(last, i % per_b, 0), 0)

    return pl.pallas_call(
        functools.partial(_nsa_proj_kernel, scale=scale, scale_tiles=ATT_WIDTH // tn, kv_tile=kv_tile,
                          n_tiles=n_tiles),
        grid=(n_tiles, m // tm),
        in_specs=[pl.BlockSpec((tm, k), lambda j, i: (i, 0)),
                  pl.BlockSpec((1, k, tn), lambda j, i: (layer, 0, tile_of(j)))],
        out_specs=[pl.BlockSpec((tm, tn), lambda j, i: (i, tile_of(j))),
                   pl.BlockSpec((1, 2, NSA_KV_HEADS, nch, CMP_STRIDE * HEAD_DIM), chunk_block)],
        out_shape=[jax.ShapeDtypeStruct((m, NSA_QKV), jnp.bfloat16),
                   jax.ShapeDtypeStruct((b, 2, NSA_KV_HEADS, s // CMP_STRIDE, CMP_STRIDE * HEAD_DIM), jnp.bfloat16)],
        scratch_shapes=[pltpu.VMEM((k, tn), jnp.bfloat16), pltpu.VMEM((tm, HEAD_DIM), jnp.float32)],
        compiler_params=_params(("arbitrary", "arbitrary")),
        name="matmul_nsa_proj",
    )(a, w)


def _first_row_step():
    return (pl.program_id(1) == 0) & (pl.program_id(2) == 0)


def _mm_streams_kernel(a_ref, w_ref, *refs, dilations, scale, scale_tiles):
    out_refs, acc_sc, wbf_sc = refs[:-2], refs[-2], refs[-1]
    w = _weight_tile(w_ref, wbf_sc, _first_row_step(), scale, scale_tiles)
    acc = jnp.dot(a_ref[0], w, preferred_element_type=jnp.float32)
    _store_streams(acc, acc_sc, out_refs, dilations)


def _matmul_streams(a, w, layer, col0, n, dilations, scale=1.0, tm=1024, tn=1024):
    b, s, k = a.shape
    tn = min(tn, n)
    tm = min(tm, s)
    return pl.pallas_call(
        functools.partial(_mm_streams_kernel, dilations=dilations, scale=scale,
                          scale_tiles=n // tn if scale != 1.0 else 0),
        grid=(n // tn, b, s // tm),
        in_specs=[pl.BlockSpec((1, tm, k), lambda j, bi, i: (bi, i, 0)),
                  _weight_spec(layer, col0, k, tn, 3)],
        out_specs=[pl.BlockSpec((1, r, tm // r, tn), lambda j, bi, i: (bi, 0, i, j)) for r in dilations],
        out_shape=[jax.ShapeDtypeStruct((b, r, s // r, n), jnp.bfloat16) for r in dilations],
        scratch_shapes=[pltpu.VMEM((tn // LANES, tm, LANES), jnp.float32), pltpu.VMEM((k, tn), jnp.bfloat16)],
        compiler_params=_params(("arbitrary", "arbitrary", "arbitrary")),
        name="matmul_streams",
    )(a, w)


def _mm_residual_kernel(a_ref, w_ref, x_ref, g_ref, o_ref, wbf_sc):
    w = _weight_tile(w_ref, wbf_sc, _first_row_step(), 1.0, 0)
    acc = jnp.dot(a_ref[0], w, preferred_element_type=jnp.float32)
    o_ref[0] = x_ref[0] + g_ref[0] * acc


def _matmul_residual(a, w, layer, x, gate, tm=1024, tn=1024):
    b, s, k = a.shape
    n = w.shape[2]
    tn = min(tn, n)
    tm = min(tm, s)
    return pl.pallas_call(
        _mm_residual_kernel,
        grid=(n // tn, b, s // tm),
        in_specs=[pl.BlockSpec((1, tm, k), lambda j, bi, i: (bi, i, 0)),
                  _weight_spec(layer, 0, k, tn, 3),
                  pl.BlockSpec((1, tm, tn), lambda j, bi, i: (bi, i, j)),
                  pl.BlockSpec((1, 1, tn), lambda j, bi, i: (bi, 0, j))],
        out_specs=pl.BlockSpec((1, tm, tn), lambda j, bi, i: (bi, i, j)),
        out_shape=jax.ShapeDtypeStruct((b, s, n), jnp.float32),
        scratch_shapes=[pltpu.VMEM((k, tn), jnp.bfloat16)],
        compiler_params=_params(("arbitrary", "arbitrary", "arbitrary")),
        name="matmul_gated_residual",
    )(a, w, x, gate.reshape(b, 1, n))


def _gelu_tanh(x):
    return 0.5 * x * (1.0 + jnp.tanh(0.7978845608028654 * (x + 0.044715 * (x * x * x))))


def _compress_kernel(c_ref, pa_ref, pb_ref, wa_ref, wb_ref, w2_ref, o_ref):
    c = c_ref[0, 0, 0].astype(jnp.float32)
    lo = (c + pa_ref[0]).astype(jnp.bfloat16)
    hi = (c + pb_ref[0]).astype(jnp.bfloat16)
    ha = jnp.dot(lo, wa_ref[0], preferred_element_type=jnp.float32)
    hb = jnp.dot(hi, wb_ref[0], preferred_element_type=jnp.float32)
    n = ha.shape[0]
    hid = _gelu_tanh(ha + pltpu.roll(hb, n - 1, 0))
    out = jnp.dot(hid.astype(jnp.bfloat16), w2_ref[0], preferred_element_type=jnp.float32)
    row = lax.broadcasted_iota(jnp.int32, out.shape, 0)
    o_ref[0, 0, 0] = jnp.where(row < n - 1, out, 0.0).astype(o_ref.dtype)


def _compress(kv_chunks, w_phi1, w_phi2, phi_pos):
    b, two, hkv, nch, width = kv_chunks.shape
    dh = HEAD_DIM
    half = CMP_BLOCK // 2
    w1 = w_phi1.reshape(2, 2, half * dh, dh).astype(jnp.bfloat16)
    pos = phi_pos.reshape(2, 2, 1, half * dh)
    return pl.pallas_call(
        _compress_kernel,
        grid=(b, two, hkv),
        in_specs=[pl.BlockSpec((1, 1, 1, nch, width), lambda i, j, g: (i, j, g, 0, 0)),
                  pl.BlockSpec((1, 1, width), lambda i, j, g: (j, 0, 0)),
                  pl.BlockSpec((1, 1, width), lambda i, j, g: (j, 0, 0)),
                  pl.BlockSpec((1, width, dh), lambda i, j, g: (j, 0, 0)),
                  pl.BlockSpec((1, width, dh), lambda i, j, g: (j, 0, 0)),
                  pl.BlockSpec((1, dh, dh), lambda i, j, g: (j, 0, 0))],
        out_specs=pl.BlockSpec((1, 1, 1, nch, dh), lambda i, j, g: (i, j, g, 0, 0)),
        out_shape=jax.ShapeDtypeStruct((b, two, hkv, nch, dh), jnp.bfloat16),
        compiler_params=_params(("parallel", "parallel", "parallel")),
        name="nsa_compress",
    )(kv_chunks, pos[:, 0], pos[:, 1], w1[:, 0], w1[:, 1], w_phi2.astype(jnp.bfloat16))


MASK_BIG = 2.0 ** 100
AUX_SLOPE = 64
AUX_PAD = 70
SEL_CHUNK = 512
LOG2E = 1.4426950408889634


def _nsa_attn_kernel(sl_ref, q_ref, kc_ref, vc_ref, ks_ref, vs_ref, kw_ref, vw_ref, gt_ref, ovl_ref, qtab_ref,
                     kauxs_ref, kauxw_ref, *refs, n_cmp, n_blk, n_sel, cast_place, n_cast_dst):
    n_cast = len(cast_place)
    cast_src, refs = refs[:n_cast], refs[n_cast:]
    o_ref, refs = refs[0], refs[1:]
    cast_dst, refs = refs[:n_cast_dst], refs[n_cast_dst:]
    (ocmp_sc, maskq_sc, owin_sc, qaug_sc, qaugw_sc, p_sc, pw_sc, kaug_s, vaug_s, kaug_w, vaug_w), refs = refs[:11], refs[11:]
    cast_in, cast_out, (cast_in_sem, cast_out_sem) = refs[:n_cast], refs[n_cast:2 * n_cast], refs[2 * n_cast:]
    t = ATT_TILE
    dh = HEAD_DIM
    s_len = ks_ref.shape[1]
    wpad = NSA_WINDOW
    g = pl.program_id(1)
    qt = pl.program_id(2)
    qstart = pl.multiple_of(qt * t, t)

    if n_cast:
        step = (pl.program_id(0) * pl.num_programs(1) + g) * pl.num_programs(2) + qt
        n_steps = pl.num_programs(0) * pl.num_programs(1) * pl.num_programs(2)
        drain_cast = _cast_side_job(step, n_steps, cast_src, cast_dst, cast_place, cast_in, cast_out, cast_in_sem,
                                    cast_out_sem)

    @pl.when(qt == 0)
    def _():
        ones = jnp.ones((s_len, dh), jnp.bfloat16)
        kaug_s[:, :dh] = ks_ref[0]
        kaug_s[:, dh:] = kauxs_ref[...]
        vaug_s[:, :dh] = vs_ref[0]
        vaug_s[:, dh:] = ones
        kaug_w[:wpad, :dh] = jnp.zeros((wpad, dh), jnp.bfloat16)
        kaug_w[wpad:, :dh] = kw_ref[0]
        kaug_w[:, dh:] = kauxw_ref[...]
        vaug_w[:wpad, :dh] = jnp.zeros((wpad, dh), jnp.bfloat16)
        vaug_w[wpad:, :dh] = vw_ref[0]
        vaug_w[:wpad, dh:] = jnp.ones((wpad, dh), jnp.bfloat16)
        vaug_w[wpad:, dh:] = ones

    slopes = [sl_ref[g * NSA_GROUP + r] for r in range(NSA_GROUP)]
    row = lax.broadcasted_iota(jnp.int32, (t, LANES), 0)
    col = lax.broadcasted_iota(jnp.int32, (t, LANES), 1)
    prows = [slice(pair * 2 * t, (pair + 1) * 2 * t) for pair in range(NSA_GROUP // 2)]
    qtab = qtab_ref[0]

    def compress_and_select(qv, tile, par):
        qs = tile * t
        tpos = qs + row
        q4 = jnp.concatenate([qv[:, r * dh:(r + 1) * dh] for r in range(NSA_GROUP)], axis=0)
        kc = kc_ref[0, 0, 0]
        vc = vc_ref[0, 0, 0]
        s = lax.dot_general(q4, kc, _NT, preferred_element_type=jnp.float32)
        visible = (col * CMP_STRIDE + (CMP_BLOCK - 1) <= tpos) & (col < n_cmp)
        dist_c = tpos.astype(jnp.float32) - (col.astype(jnp.float32) * CMP_STRIDE + (CMP_BLOCK - 1) / 2.0)
        psum = jnp.zeros((t, LANES), jnp.float32)
        for r in range(NSA_GROUP):
            rows = slice(r * t, (r + 1) * t)
            sr = jnp.where(visible, s[rows] - slopes[r] * dist_c, NEG_INF)
            e = jnp.where(visible, jnp.exp2(sr - jnp.max(sr, axis=-1, keepdims=True)), 0.0)
            p = e / jnp.maximum(jnp.sum(e, axis=-1, keepdims=True), 1e-30)
            psum = psum + p
            ocmp_sc[par, rows] = jnp.dot(p.astype(vc.dtype), vc, preferred_element_type=jnp.float32)
        nb = -(-n_blk // 8) * 8
        imp_t = lax.dot_general(ovl_ref[...], psum, _NT, precision=_HI, preferred_element_type=jnp.float32)[:nb]
        jrow = lax.broadcasted_iota(jnp.int32, (nb, t), 0)
        tpos_t = qs + lax.broadcasted_iota(jnp.int32, (nb, t), 1)
        cur = jnp.right_shift(tpos_t, int(np.log2(SEL_BLOCK)))
        forced = (jrow == 0) | (jrow == cur) | (jrow == cur - 1)
        score = jnp.where(forced, FORCE_SCORE, jnp.where(jrow * SEL_BLOCK <= tpos_t, imp_t, -1.0))
        rank = jnp.zeros((nb, t), jnp.float32)
        for j in range(n_blk):
            cj = score[j:j + 1, :]
            ahead = (cj > score) | ((cj == score) & (jrow > j))
            rank = rank + jnp.where(ahead, 1.0, 0.0)
        mask_t = jnp.where((rank < n_sel) & (jrow < n_blk), 0.0, -MASK_BIG)
        maskq_sc[par] = jnp.concatenate([mask_t, jnp.zeros((LANES - nb, t), jnp.float32)], axis=0).T

    par = 0
    q = q_ref[0]

    for r in range(NSA_GROUP):
        rows = slice(r * t, (r + 1) * t)
        qaugw_sc[rows, :dh] = q[:, r * dh:(r + 1) * dh]
        qaugw_sc[rows, dh:] = jnp.broadcast_to(qtab[r:r + 1, :], (t, LANES)).astype(jnp.bfloat16)
    wk = wpad + t
    nw = wk // LANES
    kw = kaug_w[pl.ds(qstart, wk), :]
    vw = vaug_w[pl.ds(qstart, wk), :]
    sws = [lax.dot_general(qaugw_sc[prow], kw, _NT, preferred_element_type=jnp.float32) for prow in prows]

    compress_and_select(q, qt, par)

    mask_q = maskq_sc[par]
    for r in range(NSA_GROUP):
        rows = slice(r * t, (r + 1) * t)
        qaug_sc[rows, :dh] = q[:, r * dh:(r + 1) * dh]
        qaug_sc[rows, dh:] = (mask_q + qtab[r:r + 1, :]).astype(jnp.bfloat16)

    for pair, prow in enumerate(prows):
        sw = sws[pair]
        for r2 in range(2):
            rows = slice((2 * pair + r2) * t, (2 * pair + r2 + 1) * t)
            tiles = [sw[r2 * t:(r2 + 1) * t, j * LANES:(j + 1) * LANES] for j in range(nw)]
            tiles[0] = jnp.where(col > row, tiles[0], NEG_INF)
            tiles[-1] = jnp.where(col <= row, tiles[-1], NEG_INF)
            mx = functools.reduce(jnp.maximum, tiles)
            m = jnp.broadcast_to(jnp.max(mx, axis=-1, keepdims=True), (t, LANES))
            for j in range(nw):
                pw_sc[rows, j * LANES:(j + 1) * LANES] = jnp.exp2(tiles[j] - m).astype(jnp.bfloat16)
        ow = jnp.dot(pw_sc[prow], vw, preferred_element_type=jnp.float32)
        owin_sc[prow] = ow[:, :dh] / ow[:, dh:]

    ch = SEL_CHUNK
    nl = ch // LANES
    n_full = qt // (ch // t)
    gt = gt_ref[0]
    cmr = col - row

    def selected(k):
        nk = (k + 1) * ch
        kk = kaug_s[:nk, :]
        vv = vaug_s[:nk, :]
        sks = [lax.dot_general(qaug_sc[prow], kk, _NT, preferred_element_type=jnp.float32) for prow in prows]
        for pair, prow in enumerate(prows):
            sk = sks[pair]
            for r2 in range(2):
                rows = slice((2 * pair + r2) * t, (2 * pair + r2 + 1) * t)
                tiles = [sk[r2 * t:(r2 + 1) * t, j * LANES:(j + 1) * LANES] for j in range(nk // LANES)]
                for j in range(k * nl, (k + 1) * nl):
                    tiles[j] = jnp.where(cmr <= qstart - j * LANES, tiles[j], NEG_INF)
                mx = functools.reduce(jnp.maximum, tiles)
                m = jnp.broadcast_to(jnp.max(mx, axis=-1, keepdims=True), (t, LANES))
                for j in range(nk // LANES):
                    p_sc[rows, j * LANES:(j + 1) * LANES] = jnp.exp2(tiles[j] - m).astype(jnp.bfloat16)
            pv = jnp.dot(p_sc[prow, :nk], vv, preferred_element_type=jnp.float32)
            for r2 in range(2):
                r = 2 * pair + r2
                rows = slice(r * t, (r + 1) * t)
                orow = slice(r2 * t, (r2 + 1) * t)
                o = (gt[:, 3 * r:3 * r + 1] * ocmp_sc[par, rows]
                     + gt[:, 3 * r + 1:3 * r + 2] * (pv[orow, :dh] / pv[orow, dh:])
                     + gt[:, 3 * r + 2:3 * r + 3] * owin_sc[rows])
                o_ref[0, :, r * dh:(r + 1) * dh] = o.astype(o_ref.dtype)

    for k in range(s_len // ch):
        pl.when(n_full == k)(functools.partial(selected, k))
    if cast_place:
        drain_cast()


def _cast_chunks(shape, n_steps):
    r, c = shape
    for n_c in (1, 2, 4, 8, 16):
        if n_steps % n_c == 0 and c % (n_c * LANES) == 0 and r % (n_steps // n_c) == 0:
            rows = r // (n_steps // n_c)
            if rows % 16 == 0:
                return rows, c // n_c
    raise ValueError("no tile-aligned split of %s into %d chunks" % (shape, n_steps))


def _cast_side_job(n, n_steps, srcs, dsts, place, inbufs, outbufs, in_sem, out_sem):
    slot = n % 2

    def chunk(k, step, col0=0):
        rows, cols = inbufs[k].shape[1:]
        n_c = srcs[k].shape[1] // cols
        return (pl.ds(pl.multiple_of((step // n_c) * rows, 16), rows),
                pl.ds(pl.multiple_of(col0 + (step % n_c) * cols, LANES), cols))

    def in_copy(k, step, sl):
        return pltpu.make_async_copy(srcs[k].at[chunk(k, step)], inbufs[k].at[sl], in_sem.at[sl])

    def out_copy(k, step, sl):
        dst, col0 = place[k]
        return pltpu.make_async_copy(outbufs[k].at[sl], dsts[dst].at[chunk(k, step, col0)], out_sem.at[sl])

    ks = range(len(srcs))

    @pl.when(n == 0)
    def _():
        for k in ks:
            in_copy(k, 0, 0).start()
            outbufs[k][...] = jnp.zeros(outbufs[k].shape, outbufs[k].dtype)
            out_copy(k, 0, 0).start()
            out_copy(k, 1, 1).start()

    for k in ks:
        in_copy(k, n, slot).wait()
    nxt = jnp.minimum(n + 1, n_steps - 1)
    for k in ks:
        in_copy(k, nxt, 1 - slot).start()
    for k in ks:
        out_copy(k, n, slot).wait()

    for k in ks:
        outbufs[k][slot] = inbufs[k][slot].astype(outbufs[k].dtype)
    for k in ks:
        out_copy(k, n, slot).start()

    def drain():
        @pl.when(n == n_steps - 1)
        def _():
            for k in ks:
                in_copy(k, n, 1 - slot).wait()
                out_copy(k, n, 1 - slot).wait()
                out_copy(k, n, slot).wait()

    return drain


def _bf16_pieces(x):
    a0 = x.astype(jnp.bfloat16).astype(jnp.float32)
    a1 = (x - a0).astype(jnp.bfloat16).astype(jnp.float32)
    a2 = (x - a0 - a1).astype(jnp.bfloat16).astype(jnp.float32)
    return [a0, a1, a2]


def _nsa_tables(s, slopes2):
    n_cmp = s // CMP_STRIDE - CMP_BLOCK // CMP_STRIDE + 1
    n_blk = s // SEL_BLOCK
    assert n_cmp < LANES and n_blk <= AUX_SLOPE
    bj = np.arange(LANES)[:, None]
    cn = np.arange(LANES)[None, :]
    ovl_t = np.clip(np.minimum(cn * CMP_STRIDE + CMP_BLOCK, (bj + 1) * SEL_BLOCK)
                    - np.maximum(cn * CMP_STRIDE, bj * SEL_BLOCK), 0, None) / CMP_STRIDE
    ovl_t = np.where((cn < n_cmp) & (bj < n_blk), ovl_t, 0.0).astype(np.float32)
    pos = np.arange(s)
    kaux = np.zeros((s, LANES), np.float32)
    kaux[pos, pos // SEL_BLOCK] = 1.0
    kaux[:, AUX_SLOPE:AUX_SLOPE + 3] = (pos // LANES * LANES)[:, None]
    kaux[:, AUX_SLOPE + 3:AUX_SLOPE + 6] = (pos % LANES)[:, None]
    kaux_w = np.zeros((NSA_WINDOW + s, LANES), np.float32)
    kaux_w[NSA_WINDOW:, AUX_SLOPE:AUX_SLOPE + 6] = kaux[:, AUX_SLOPE:AUX_SLOPE + 6]
    kaux_w[:NSA_WINDOW, AUX_PAD] = 1.0
    pieces = jnp.stack(_bf16_pieces(slopes2) * 2, axis=-1)
    qtab = jnp.zeros((N_HEADS, LANES), jnp.float32)
    qtab = qtab.at[:, AUX_SLOPE:AUX_SLOPE + 6].set(pieces).at[:, AUX_PAD].set(-MASK_BIG)
    qtab = jnp.pad(qtab.reshape(NSA_KV_HEADS, NSA_GROUP, LANES), ((0, 0), (0, 8 - NSA_GROUP), (0, 0)))
    return (n_cmp, n_blk, jnp.asarray(ovl_t), qtab, jnp.asarray(kaux, dtype=jnp.bfloat16),
            jnp.asarray(kaux_w, dtype=jnp.bfloat16))


def _nsa_attention(proj, cmp_kv, gates, slopes2, cast_ws=(), cast_place=(), cast_shapes=()):
    b, s, _ = proj.shape
    t = ATT_TILE
    n_steps = b * NSA_KV_HEADS * (s // t)
    chunks = [_cast_chunks(w.shape, n_steps) for w in cast_ws]
    assert n_steps >= 2
    any_spec = pl.BlockSpec(memory_space=pl.ANY)
    n_cmp, n_blk, ovl_t, qtab, kaux_s, kaux_w = _nsa_tables(s, slopes2)
    assert cmp_kv.shape[3] == LANES and s % SEL_CHUNK == 0 and NSA_WINDOW % t == 0
    qw = NSA_GROUP * HEAD_DIM
    rows = NSA_GROUP * t
    kv0 = ATT_WIDTH // HEAD_DIM

    def kv_spec(j):
        return pl.BlockSpec((1, s, HEAD_DIM), lambda bi, g, qi, j=j: (bi, 0, kv0 + j * NSA_KV_HEADS + g))

    def const(shape):
        return pl.BlockSpec(shape, lambda bi, g, qi: (0,) * len(shape))

    outs = pl.pallas_call(
        functools.partial(_nsa_attn_kernel, n_cmp=n_cmp, n_blk=n_blk, n_sel=min(N_SEL, n_blk),
                          cast_place=tuple(cast_place), n_cast_dst=len(cast_shapes)),
        grid=(b, NSA_KV_HEADS, s // t),
        in_specs=[pl.BlockSpec(memory_space=pltpu.SMEM),
                  pl.BlockSpec((1, t, qw), lambda bi, g, qi: (bi, qi, g)),
                  pl.BlockSpec((1, 1, 1, LANES, HEAD_DIM), lambda bi, g, qi: (bi, 0, g, 0, 0)),
                  pl.BlockSpec((1, 1, 1, LANES, HEAD_DIM), lambda bi, g, qi: (bi, 1, g, 0, 0)),
                  kv_spec(2), kv_spec(3), kv_spec(4), kv_spec(5),
                  pl.BlockSpec((1, t, LANES), lambda bi, g, qi: (bi, qi, g)),
                  const((LANES, LANES)),
                  pl.BlockSpec((1, 8, LANES), lambda bi, g, qi: (g, 0, 0)),
                  const((s, LANES)), const((NSA_WINDOW + s, LANES))] + [any_spec] * len(cast_ws),
        out_specs=[pl.BlockSpec((1, t, qw), lambda bi, g, qi: (bi, qi, g))] + [any_spec] * len(cast_shapes),
        out_shape=[jax.ShapeDtypeStruct((b, s, ATT_WIDTH), jnp.bfloat16)]
        + [jax.ShapeDtypeStruct(shape, jnp.bfloat16) for shape in cast_shapes],
        scratch_shapes=[pltpu.VMEM((1, rows, HEAD_DIM), jnp.float32),
                        pltpu.VMEM((1, t, LANES), jnp.float32),
                        pltpu.VMEM((rows, HEAD_DIM), jnp.float32),
                        pltpu.VMEM((rows, 2 * HEAD_DIM), jnp.bfloat16),
                        pltpu.VMEM((rows, 2 * HEAD_DIM), jnp.bfloat16),
                        pltpu.VMEM((rows, s), jnp.bfloat16),
                        pltpu.VMEM((rows, NSA_WINDOW + t), jnp.bfloat16),
                        pltpu.VMEM((s, 2 * HEAD_DIM), jnp.bfloat16),
                        pltpu.VMEM((s, 2 * HEAD_DIM), jnp.bfloat16),
                        pltpu.VMEM((NSA_WINDOW + s, 2 * HEAD_DIM), jnp.bfloat16),
                        pltpu.VMEM((NSA_WINDOW + s, 2 * HEAD_DIM), jnp.bfloat16)]
        + [pltpu.VMEM((2,) + c, jnp.float32) for c in chunks]
        + [pltpu.VMEM((2,) + c, jnp.bfloat16) for c in chunks]
        + [pltpu.SemaphoreType.DMA((2,)), pltpu.SemaphoreType.DMA((2,))],
        compiler_params=_params(("arbitrary", "arbitrary", "arbitrary")),
        name="nsa_attention",
    )(slopes2, proj, cmp_kv, cmp_kv, proj, proj, proj, proj, gates, ovl_t, qtab, kaux_s, kaux_w, *cast_ws)
    return outs[0], list(outs[1:])


def _dil_attn_kernel(sl_ref, q_ref, *refs, window, has_prev):
    if has_prev:
        kp_ref, kc_ref, vp_ref, vc_ref, o_ref, lse_ref, bias_sc = refs
    else:
        kc_ref, vc_ref, o_ref, lse_ref, bias_sc = refs
        kp_ref, vp_ref = kc_ref, vc_ref
    t = ATT_TILE
    i = pl.program_id(2)

    @pl.when((pl.program_id(0) == 0) & (pl.program_id(1) == 0) & (i == 0))
    def _():
        row = lax.broadcasted_iota(jnp.int32, (t, 2 * t), 0)
        col = lax.broadcasted_iota(jnp.int32, (t, 2 * t), 1)
        dist = row - col + t
        band = (dist >= 0) & (dist <= window)
        distf = dist.astype(jnp.float32)
        for h in range(N_HEADS):
            bias = sl_ref[h] * distf
            bias_sc[0, h] = jnp.where(band & (col >= t), bias, -NEG_INF)
            bias_sc[1, h] = jnp.where(band, bias, -NEG_INF)

    var = jnp.minimum(i, 1)
    lane = lax.broadcasted_iota(jnp.int32, (t, LANES), 1)
    lse_all = jnp.zeros((t, LANES), jnp.float32)
    for h in range(N_HEADS):
        hs = slice(h * HEAD_DIM, (h + 1) * HEAD_DIM)
        q = q_ref[0, 0, :, hs]
        k = jnp.concatenate([kp_ref[0, 0, :, hs], kc_ref[0, 0, :, hs]], axis=0)
        v = jnp.concatenate([vp_ref[0, 0, :, hs], vc_ref[0, 0, :, hs]], axis=0)
        s = lax.dot_general(q, k, _NT, preferred_element_type=jnp.float32) - bias_sc[var, h]
        m = jnp.max(s, axis=-1, keepdims=True)
        e = jnp.exp2(s - m)
        l = jnp.sum(e, axis=-1, keepdims=True)
        o = jnp.dot(e.astype(v.dtype), v, preferred_element_type=jnp.float32) / l
        o_ref[0, 0, :, hs] = o.astype(o_ref.dtype)
        lse_all = jnp.where(lane == h, m + jnp.log2(l), lse_all)
    lse_ref[0, 0] = lse_all


def _dil_attention(q, kv, slopes2, win, r):
    b, _, ln, _ = q.shape
    t = ATT_TILE

    def cur(c):
        return pl.BlockSpec((1, 1, t, ATT_WIDTH), lambda bi, rho, i, c=c: (bi, rho, i, c))

    def prev(c):
        return pl.BlockSpec((1, 1, t, ATT_WIDTH), lambda bi, rho, i, c=c: (bi, rho, jnp.maximum(i - 1, 0), c))

    has_prev = ln > t
    kv_specs = [prev(0), cur(0), prev(1), cur(1)] if has_prev else [cur(0), cur(1)]
    return pl.pallas_call(
        functools.partial(_dil_attn_kernel, window=win // r, has_prev=has_prev),
        grid=(b, r, ln // t),
        in_specs=[pl.BlockSpec(memory_space=pltpu.SMEM), cur(0)] + kv_specs,
        out_specs=[pl.BlockSpec((1, 1, t, ATT_WIDTH), lambda bi, rho, i: (bi, rho, i, 0)),
                   pl.BlockSpec((1, 1, t, LANES), lambda bi, rho, i: (bi, rho, i, 0))],
        out_shape=[jax.ShapeDtypeStruct((b, r, ln, ATT_WIDTH), jnp.bfloat16),
                   jax.ShapeDtypeStruct((b, r, ln, LANES), jnp.float32)],
        scratch_shapes=[pltpu.VMEM((2, N_HEADS, t, 2 * t), jnp.float32)],
        compiler_params=_params(("arbitrary", "arbitrary", "arbitrary")),
        name="dilated_attention_r%d" % r,
    )(slopes2 * r, q, *([kv] * len(kv_specs)))


def _dil_merge_kernel(*refs, dilations):
    ng = len(dilations)
    o_refs, l_refs, out_ref, o_sc, l_sc = refs[:ng], refs[ng:2 * ng], refs[2 * ng], refs[2 * ng + 1], refs[2 * ng + 2]
    ts = out_ref.shape[1]
    ls = []
    for gi, r in enumerate(dilations):
        if r == 1:
            ls.append(l_refs[gi][0, 0])
            continue
        for rho in range(r):
            l_sc[gi, pl.ds(rho, ts // r, stride=r), :] = l_refs[gi][0, rho]
            for h in range(N_HEADS):
                o_sc[gi, h, pl.ds(rho, ts // r, stride=r), :] = (
                    o_refs[gi][0, rho, :, h * HEAD_DIM:(h + 1) * HEAD_DIM].astype(jnp.float32))
        ls.append(l_sc[gi])
    m = functools.reduce(jnp.maximum, ls)
    es = [jnp.exp2(l - m) for l in ls]
    den = functools.reduce(lambda a, b: a + b, es)
    ws = [e / den for e in es]
    for h in range(N_HEADS):
        hs = slice(h * HEAD_DIM, (h + 1) * HEAD_DIM)
        acc = None
        for gi, r in enumerate(dilations):
            og = o_refs[gi][0, 0, :, hs].astype(jnp.float32) if r == 1 else o_sc[gi, h]
            term = ws[gi][:, h:h + 1] * og
            acc = term if acc is None else acc + term
        out_ref[0, :, hs] = acc.astype(out_ref.dtype)


def _dil_merge(outs, lses, dilations):
    b, _, _, w = outs[0].shape
    s = outs[0].shape[1] * outs[0].shape[2]
    ts = min(ROW_BLOCK, s)
    ng = len(dilations)

    def spec(r, width):
        return pl.BlockSpec((1, r, ts // r, width), lambda i, j: (i, 0, j, 0))

    return pl.pallas_call(
        functools.partial(_dil_merge_kernel, dilations=dilations),
        grid=(b, s // ts),
        in_specs=[spec(r, w) for r in dilations] + [spec(r, LANES) for r in dilations],
        out_specs=pl.BlockSpec((1, ts, w), lambda i, j: (i, j, 0)),
        out_shape=jax.ShapeDtypeStruct((b, s, w), jnp.bfloat16),
        scratch_shapes=[pltpu.VMEM((ng, w // HEAD_DIM, ts, HEAD_DIM), jnp.float32),
                        pltpu.VMEM((ng, ts, LANES), jnp.float32)],
        compiler_params=_params(("parallel", "parallel")),
        name="dilated_merge",
    )(*outs, *lses)


ROW_TILE = 8


def _gather_rows(idx_ref, src_hbm, dst, sem):
    def body(i, carry):
        for j in range(ROW_TILE):
            pltpu.make_async_copy(src_hbm.at[idx_ref[0, 0, i * ROW_TILE + j]], dst.at[i, :, j], sem).start()
        return carry
    lax.fori_loop(0, dst.shape[0], body, 0)


def _gather_tiled_rows(idx_ref, src_hbm, dst, sem):
    def body(i, carry):
        for j in range(ROW_TILE):
            pltpu.make_async_copy(src_hbm.at[pl.ds(idx_ref[0, 0, i * ROW_TILE + j], 1)],
                                  dst.at[i, pl.ds(j, 1)], sem).start()
        return carry
    lax.fori_loop(0, dst.shape[0], body, 0)


def _wait_rows(dst, sem):
    pltpu.make_async_copy(dst, dst, sem).wait()


def _expert_kernel(be_ref, nu_ref, tokc_ref, tokn_ref, h_hbm, wgu_ref, wd_ref, o_ref, xbuf, sem):
    del be_ref
    de = wd_ref.shape[1]
    blk = pl.program_id(0)
    n_used = nu_ref[0]
    slot = blk % 2

    @pl.when(blk == 0)
    def _():
        _gather_rows(tokc_ref, h_hbm, xbuf.at[0], sem.at[0])

    @pl.when(blk + 1 < n_used)
    def _():
        _gather_rows(tokn_ref, h_hbm, xbuf.at[1 - slot], sem.at[1 - slot])

    @pl.when(blk < n_used)
    def _():
        _wait_rows(xbuf.at[slot], sem.at[slot])
        x = _load_row_tiles(xbuf.at[slot]).astype(jnp.bfloat16)
        gate = jnp.dot(x, wgu_ref[0, :, :de], preferred_element_type=jnp.float32)
        up = jnp.dot(x, wgu_ref[0, :, de:], preferred_element_type=jnp.float32)
        hid = (gate * jax.nn.sigmoid(gate) * up).astype(jnp.bfloat16)
        o_ref[...] = jnp.dot(hid, wd_ref[0], preferred_element_type=jnp.float32)

    @pl.when(blk >= n_used)
    def _():
        o_ref[...] = jnp.zeros(o_ref.shape, o_ref.dtype)


def _expert_blocks(h3, slot_tok, blk_e, n_used, w_gate_up, w_down):
    t, ct, _ = h3.shape
    d = ct * LANES
    nblk = blk_e.shape[0]
    de = w_down.shape[1]
    assert de % LANES == 0
    tok = slot_tok.reshape(nblk, 1, MOE_BLOCK)
    grid_spec = pltpu.PrefetchScalarGridSpec(
        num_scalar_prefetch=2,
        grid=(nblk,),
        in_specs=[pl.BlockSpec((1, 1, MOE_BLOCK), lambda i, be, nu: (i, 0, 0), memory_space=pltpu.SMEM),
                  pl.BlockSpec((1, 1, MOE_BLOCK), lambda i, be, nu: (jnp.minimum(i + 1, nblk - 1), 0, 0),
                               memory_space=pltpu.SMEM),
                  pl.BlockSpec(memory_space=pl.ANY),
                  pl.BlockSpec((1, d, 2 * de), lambda i, be, nu: (be[i], 0, 0)),
                  pl.BlockSpec((1, de, d), lambda i, be, nu: (be[i], 0, 0))],
        out_specs=pl.BlockSpec((MOE_BLOCK, d), lambda i, be, nu: (i, 0)),
        scratch_shapes=[pltpu.VMEM((2, MOE_BLOCK // ROW_TILE, ct, ROW_TILE, LANES), jnp.float32),
                        pltpu.SemaphoreType.DMA((2,))],
    )
    return pl.pallas_call(
        _expert_kernel,
        grid_spec=grid_spec,
        out_shape=jax.ShapeDtypeStruct((nblk * MOE_BLOCK, d), jnp.float32),
        compiler_params=_params(("arbitrary",)),
        name="moe_expert_blocks",
    )(blk_e, n_used, tok, tok, h3, w_gate_up, w_down)


def _combine_kernel(d0c_ref, d1c_ref, d0n_ref, d1n_ref, x_ref, w0_ref, w1_ref, g_ref, gn_ref, yo_hbm, o_ref,
                    buf, sem, *, out_norm):
    i = pl.program_id(0)
    n = pl.num_programs(0)
    slot = i % 2
    tt = x_ref.shape[0]

    @pl.when(i == 0)
    def _():
        _gather_tiled_rows(d0c_ref, yo_hbm, buf.at[0, 0], sem.at[0])
        _gather_tiled_rows(d1c_ref, yo_hbm, buf.at[0, 1], sem.at[0])

    @pl.when(i + 1 < n)
    def _():
        _gather_tiled_rows(d0n_ref, yo_hbm, buf.at[1 - slot, 0], sem.at[1 - slot])
        _gather_tiled_rows(d1n_ref, yo_hbm, buf.at[1 - slot, 1], sem.at[1 - slot])

    _wait_rows(buf.at[slot, 0], sem.at[slot])
    _wait_rows(buf.at[slot, 1], sem.at[slot])
    y = (w0_ref[...][:, 0:1] * buf[slot, 0].reshape(x_ref.shape)
         + w1_ref[...][:, 0:1] * buf[slot, 1].reshape(x_ref.shape))
    out = x_ref[...] + g_ref[0] * y
    o_ref[...] = _rms(out, gn_ref[...]) if out_norm else out


def _moe_combine(x2d, yo, dest, wts, gate, s, out_norm_g=None):
    t, d = x2d.shape
    out_norm = out_norm_g is not None
    gn = (out_norm_g if out_norm else jnp.ones((d,), jnp.float32)).reshape(1, d)
    tt = min(ROW_BLOCK, s)
    nt = t // tt
    d0 = dest[:, 0].reshape(nt, 1, tt)
    d1 = dest[:, 1].reshape(nt, 1, tt)
    w0 = jnp.broadcast_to(wts[:, 0:1], (t, LANES))
    w1 = jnp.broadcast_to(wts[:, 1:2], (t, LANES))
    b = gate.shape[0]
    cur = pl.BlockSpec((1, 1, tt), lambda i: (i, 0, 0), memory_space=pltpu.SMEM)
    nxt = pl.BlockSpec((1, 1, tt), lambda i: (jnp.minimum(i + 1, nt - 1), 0, 0), memory_space=pltpu.SMEM)
    return pl.pallas_call(
        functools.partial(_combine_kernel, out_norm=out_norm),
        grid=(nt,),
        in_specs=[cur, cur, nxt, nxt,
                  pl.BlockSpec((tt, d), lambda i: (i, 0)),
                  pl.BlockSpec((tt, LANES), lambda i: (i, 0)),
                  pl.BlockSpec((tt, LANES), lambda i: (i, 0)),
                  pl.BlockSpec((1, 1, d), lambda i: (i * tt // s, 0, 0)),
                  pl.BlockSpec((1, d), lambda i: (0, 0)),
                  pl.BlockSpec(memory_space=pl.ANY)],
        out_specs=pl.BlockSpec((tt, d), lambda i: (i, 0)),
        out_shape=jax.ShapeDtypeStruct((t, d), jnp.float32),
        scratch_shapes=[pltpu.VMEM((2, 2, tt // ROW_TILE, ROW_TILE, d), jnp.float32),
                        pltpu.SemaphoreType.DMA((2,))],
        compiler_params=_params(("arbitrary",)),
        name="moe_combine_residual",
    )(d0, d1, d0, d1, x2d, w0, w1, gate.reshape(b, 1, d), gn, yo)


def _dispatch_tables(idx):
    t = idx.shape[0]
    a = t * TOP_K
    flat_e = idx.reshape(a)
    onehot = (flat_e[:, None] == jnp.arange(N_EXPERTS, dtype=jnp.int32)[None, :]).astype(jnp.int32)
    csum = jnp.cumsum(onehot, axis=0)
    rank = jnp.take_along_axis(csum, flat_e[:, None], axis=1)[:, 0] - 1
    counts = csum[-1]
    padded = (counts + MOE_BLOCK - 1) // MOE_BLOCK * MOE_BLOCK
    pad_end = jnp.cumsum(padded)
    pad_start = pad_end - padded
    dest = pad_start[flat_e] + rank
    nblk = -(-(a + N_EXPERTS * MOE_BLOCK) // MOE_BLOCK)
    cap = nblk * MOE_BLOCK
    slot_tok = jnp.zeros((cap,), jnp.int32).at[dest].set(jnp.arange(a, dtype=jnp.int32) // TOP_K,
                                                         unique_indices=True)
    blk_start = jnp.arange(nblk, dtype=jnp.int32) * MOE_BLOCK
    blk_e = jnp.minimum(jnp.sum((pad_end[None, :] <= blk_start[:, None]).astype(jnp.int32), axis=1), N_EXPERTS - 1)
    n_used = (pad_end[-1:] // MOE_BLOCK).astype(jnp.int32)
    return dest.reshape(t, TOP_K).astype(jnp.int32), slot_tok, blk_e.astype(jnp.int32), n_used


def _moe_layer(x, g, sc, sh, gate, router_w, router_b, layer, w_gate_up, w_down, out_norm_g=None):
    b, s, d = x.shape
    t = b * s
    h, idx, wts = _norm_route(x, g, sc, sh, router_w, router_b)
    idx = idx.transpose(0, 2, 1).reshape(t, TOP_K)
    wts = wts.transpose(0, 2, 1).reshape(t, TOP_K)
    dest, slot_tok, blk_e, n_used = _dispatch_tables(idx)
    yo = _expert_blocks(h, slot_tok, blk_e + layer * N_EXPERTS, n_used, w_gate_up, w_down)
    return _moe_combine(x.reshape(t, d), yo, dest, wts, gate, s, out_norm_g).reshape(b, s, d)


def _nsa_layer(h, x, gate, slopes, j, w_in, w_phi1, w_phi2, phi_pos, w_out, cast_job=((), (), ())):
    b, s, d = h.shape
    t = b * s
    h2d = h.reshape(t, d)
    proj, kv_chunks = _nsa_projection(h2d, w_in, j, s, HEAD_DIM ** -0.5 * LOG2E)
    proj = proj.reshape(b, s, NSA_QKV)
    wg = w_in[j, :, NSA_QKV:].reshape(d, NSA_KV_HEADS, NSA_GATES)
    wg = jnp.pad(wg, ((0, 0), (0, 0), (0, LANES - NSA_GATES))).reshape(1, d, NSA_KV_HEADS * LANES)
    gates = _matmul(h2d, wg, 0, 0, NSA_KV_HEADS * LANES, jnp.float32, act="sigmoid")
    gates = gates.reshape(b, s, NSA_KV_HEADS * LANES)
    cmp_kv = _compress(kv_chunks, w_phi1, w_phi2, phi_pos)
    o, casted = _nsa_attention(proj, cmp_kv, gates, slopes * LOG2E, *cast_job)
    return _matmul_residual(o, w_out, j, x, gate), casted


def _dil_layer(x, g, sc, sh, gate, slopes, j, w_in, w_out):
    b, s, d = x.shape
    dilations = tuple(r for _, r in DIL_PAIRS)
    assert dilations[0] == 1
    hs = _norm_mod_streams(x, g, sc, sh, dilations)
    qs = _matmul_streams(hs[0].reshape(b, s, d), w_in, j, 0, ATT_WIDTH, dilations,
                         scale=HEAD_DIM ** -0.5 * LOG2E)
    outs, lses = [], []
    for gidx, (win, r) in enumerate(DIL_PAIRS):
        off = ATT_WIDTH * (1 + 2 * gidx)
        kv = _matmul(hs[gidx].reshape(b * s, d), w_in, j, off, 2 * ATT_WIDTH, jnp.bfloat16)
        o, lse = _dil_attention(qs[gidx], kv.reshape(b, r, s // r, 2 * ATT_WIDTH), slopes * LOG2E, win, r)
        outs.append(o)
        lses.append(lse)
    o = _dil_merge(outs, lses, dilations)
    return _matmul_residual(o, w_out, j, x, gate)


def kernel(x, c, ada_w, ada_b, norm_mix, norm_ffn, norm_final, nsa_w_in, nsa_w_phi1, nsa_w_phi2, nsa_phi_pos,
           nsa_w_out, dil_w_in, dil_w_out, router_w, router_b, exp_w_gate, exp_w_up, exp_w_down):
    depth = ada_w.shape[0]
    d = x.shape[-1]
    mod = _modulation(c, ada_w, ada_b)
    slopes = 2.0 ** (-ALIBI_MAX_BIAS * jnp.arange(1, N_HEADS + 1, dtype=jnp.float32) / N_HEADS)
    n_exp, _, de = exp_w_gate.shape[1:]
    for i in range(depth):
        sh_m, sc_m, g_m, sh_f, sc_f, g_f = [mod[i, :, k * d:(k + 1) * d] for k in range(6)]
        j = i // 2
        if i % 2 == 0:
            h = _norm_mod(x, norm_mix[i], sc_m, sh_m, jnp.bfloat16)
            cast_job = ((), (), ())
            if i == 0:
                srcs = [w.reshape(-1, w.shape[-1]) for w in (exp_w_gate, exp_w_up, exp_w_down)]
                cast_job = (srcs, ((0, 0), (0, de), (1, 0)), ((srcs[0].shape[0], 2 * de), srcs[2].shape))
            x, casted = _nsa_layer(h, x, g_m, slopes, j, nsa_w_in, nsa_w_phi1[j], nsa_w_phi2[j], nsa_phi_pos[j],
                                   nsa_w_out, cast_job)
            if i == 0:
                e_gate_up = casted[0].reshape(depth * n_exp, d, 2 * de)
                e_down = casted[1].reshape(depth * n_exp, de, d)
        else:
            x = _dil_layer(x, norm_mix[i], sc_m, sh_m, g_m, slopes, j, dil_w_in, dil_w_out)
        x = _moe_layer(x, norm_ffn[i], sc_f, sh_f, g_f, router_w, router_b, i, e_gate_up, e_down,
                       norm_final if i == depth - 1 else None)
    return x
```

```python
import functools

import jax
import jax.numpy as jnp
import numpy as np
from jax import lax
from jax.experimental import pallas as pl
from jax.experimental.pallas import tpu as pltpu

HEAD_DIM = 128
N_HEADS = 16
ATT_WIDTH = N_HEADS * HEAD_DIM
ALIBI_MAX_BIAS = 8.0

NSA_KV_HEADS = 4
NSA_GROUP = N_HEADS // NSA_KV_HEADS
NSA_KV_WIDTH = NSA_KV_HEADS * HEAD_DIM
CMP_BLOCK = 32
CMP_STRIDE = 16
SEL_BLOCK = 64
N_SEL = 16
NSA_WINDOW = 512
FORCE_SCORE = 1.0e6
NSA_QKV = ATT_WIDTH + 6 * NSA_KV_WIDTH
NSA_GATES = 3 * NSA_GROUP

DIL_PAIRS = ((128, 1), (512, 4), (2048, 16))
DIL_PROJ = ATT_WIDTH * (1 + 2 * len(DIL_PAIRS))

N_EXPERTS = 16
N_EXPERT_GROUPS = 4
EXPERTS_PER_GROUP = N_EXPERTS // N_EXPERT_GROUPS
TOP_K = 2
MOE_BLOCK = 256

RMS_EPS = 1e-6
NEG_INF = -1.0e30

LANES = 128
ATT_TILE = 128
ROW_BLOCK = 512
MOE_ROW_BLOCK = 256
VMEM_LIMIT = 56 * 1024 * 1024

_HI = lax.Precision.HIGHEST
_NT = (((1,), (1,)), ((), ()))


def _params(sem, vmem=VMEM_LIMIT):
    return pltpu.CompilerParams(dimension_semantics=sem, vmem_limit_bytes=vmem)


def _mod_kernel(c_ref, w_ref, b_ref, o_ref):
    c = c_ref[...]
    cond = c * jax.nn.sigmoid(c)
    o_ref[0] = jnp.dot(cond, w_ref[0], precision=_HI, preferred_element_type=jnp.float32) + b_ref[0]


def _modulation(c, ada_w, ada_b):
    depth, d, n = ada_w.shape
    b = c.shape[0]
    tn = 512
    return pl.pallas_call(
        _mod_kernel,
        grid=(depth, n // tn),
        in_specs=[pl.BlockSpec((b, d), lambda i, j: (0, 0)),
                  pl.BlockSpec((1, d, tn), lambda i, j: (i, 0, j)),
                  pl.BlockSpec((1, 1, tn), lambda i, j: (i, 0, j))],
        out_specs=pl.BlockSpec((1, b, tn), lambda i, j: (i, 0, j)),
        out_shape=jax.ShapeDtypeStruct((depth, b, n), jnp.float32),
        compiler_params=_params(("parallel", "parallel")),
        name="adaln_modulation",
    )(c, ada_w, ada_b.reshape(depth, 1, n))


def _rms(x, g):
    return x * lax.rsqrt(jnp.mean(x * x, axis=-1, keepdims=True) + RMS_EPS) * g


def _norm_mod_kernel(x_ref, g_ref, sc_ref, sh_ref, o_ref):
    h = _rms(x_ref[0], g_ref[...]) * (1.0 + sc_ref[0]) + sh_ref[0]
    o_ref[0] = h.astype(o_ref.dtype)


def _norm_mod(x, g, sc, sh, out_dtype):
    b, s, d = x.shape
    ts = min(ROW_BLOCK, s)
    return pl.pallas_call(
        _norm_mod_kernel,
        grid=(b, s // ts),
        in_specs=[pl.BlockSpec((1, ts, d), lambda i, j: (i, j, 0)),
                  pl.BlockSpec((1, d), lambda i, j: (0, 0)),
                  pl.BlockSpec((1, 1, d), lambda i, j: (i, 0, 0)),
                  pl.BlockSpec((1, 1, d), lambda i, j: (i, 0, 0))],
        out_specs=pl.BlockSpec((1, ts, d), lambda i, j: (i, j, 0)),
        out_shape=jax.ShapeDtypeStruct((b, s, d), out_dtype),
        compiler_params=_params(("parallel", "parallel")),
        name="rmsnorm_adaln",
    )(x, g.reshape(1, d), sc.reshape(b, 1, d), sh.reshape(b, 1, d))


def _store_streams(val, src_sc, out_refs, dilations):
    n = val.shape[0]
    for c in range(src_sc.shape[0]):
        src_sc[c] = val[:, c * LANES:(c + 1) * LANES]
    for r, ref in zip(dilations, out_refs):
        if r == 1:
            ref[0, 0] = val.astype(ref.dtype)
            continue
        for rho in range(r):
            for c in range(src_sc.shape[0]):
                ref[0, rho, :, c * LANES:(c + 1) * LANES] = (
                    src_sc[c, pl.ds(rho, n // r, stride=r), :].astype(ref.dtype))


def _norm_mod_streams_kernel(x_ref, g_ref, sc_ref, sh_ref, *refs, dilations):
    out_refs, h_sc = refs[:-1], refs[-1]
    h = _rms(x_ref[0], g_ref[...]) * (1.0 + sc_ref[0]) + sh_ref[0]
    _store_streams(h, h_sc, out_refs, dilations)


def _norm_mod_streams(x, g, sc, sh, dilations):
    b, s, d = x.shape
    ts = min(ROW_BLOCK, s)
    return pl.pallas_call(
        functools.partial(_norm_mod_streams_kernel, dilations=dilations),
        grid=(b, s // ts),
        in_specs=[pl.BlockSpec((1, ts, d), lambda i, j: (i, j, 0)),
                  pl.BlockSpec((1, d), lambda i, j: (0, 0)),
                  pl.BlockSpec((1, 1, d), lambda i, j: (i, 0, 0)),
                  pl.BlockSpec((1, 1, d), lambda i, j: (i, 0, 0))],
        out_specs=[pl.BlockSpec((1, r, ts // r, d), lambda i, j: (i, 0, j, 0)) for r in dilations],
        out_shape=[jax.ShapeDtypeStruct((b, r, s // r, d), jnp.bfloat16) for r in dilations],
        scratch_shapes=[pltpu.VMEM((d // LANES, ts, LANES), jnp.float32)],
        compiler_params=_params(("parallel", "parallel")),
        name="rmsnorm_adaln_streams",
    )(x, g.reshape(1, d), sc.reshape(b, 1, d), sh.reshape(b, 1, d))


def _pair_max(vals):
    out = None
    for a in range(len(vals)):
        for b in range(a + 1, len(vals)):
            s = vals[a] + vals[b]
            out = s if out is None else jnp.maximum(out, s)
    return out


def _store_row_major(ref, val):
    rows = val.shape[0]
    c_tiles = val.shape[1] // LANES
    for c in range(c_tiles):
        ref[pl.ds(c, rows, stride=c_tiles), :] = val[:, c * LANES:(c + 1) * LANES]


def _load_row_tiles(buf):
    n, c_tiles = buf.shape[0], buf.shape[1]
    return jnp.concatenate([buf[:, c].reshape(n * ROW_TILE, LANES) for c in range(c_tiles)], axis=1)


def _norm_route_kernel(x_ref, g_ref, sc_ref, sh_ref, rw_ref, rb_ref, h_ref, idx_ref, wt_ref):
    h = _rms(x_ref[0], g_ref[...]) * (1.0 + sc_ref[0]) + sh_ref[0]
    _store_row_major(h_ref, h)

    logits = lax.dot_general(rw_ref[...], h, _NT, precision=_HI, preferred_element_type=jnp.float32)
    scores = jax.nn.sigmoid(logits)
    biased = scores + rb_ref[...][:, 0:1]
    rows = [biased[e:e + 1, :] for e in range(N_EXPERTS)]
    srow = [scores[e:e + 1, :] for e in range(N_EXPERTS)]
    grp = [_pair_max(rows[q * EXPERTS_PER_GROUP:(q + 1) * EXPERTS_PER_GROUP]) for q in range(N_EXPERT_GROUPS)]
    best_v = grp[0]
    best = jnp.zeros(best_v.shape, jnp.int32)
    for q in range(1, N_EXPERT_GROUPS):
        take = grp[q] > best_v
        best = jnp.where(take, q, best)
        best_v = jnp.where(take, grp[q], best_v)
    v1 = jnp.full(best_v.shape, NEG_INF, jnp.float32)
    i1 = jnp.zeros(best_v.shape, jnp.int32)
    for e in range(N_EXPERTS):
        cand = jnp.where(best == e // EXPERTS_PER_GROUP, rows[e], NEG_INF)
        take = cand > v1
        i1 = jnp.where(take, e, i1)
        v1 = jnp.where(take, cand, v1)
    v2 = jnp.full(best_v.shape, NEG_INF, jnp.float32)
    i2 = jnp.zeros(best_v.shape, jnp.int32)
    for e in range(N_EXPERTS):
        cand = jnp.where(best == e // EXPERTS_PER_GROUP, jnp.where(i1 == e, NEG_INF, rows[e]), NEG_INF)
        take = cand > v2
        i2 = jnp.where(take, e, i2)
        v2 = jnp.where(take, cand, v2)
    w1 = jnp.zeros(best_v.shape, jnp.float32)
    w2 = jnp.zeros(best_v.shape, jnp.float32)
    for e in range(N_EXPERTS):
        w1 = jnp.where(i1 == e, srow[e], w1)
        w2 = jnp.where(i2 == e, srow[e], w2)
    tot = w1 + w2
    idx_ref[0] = jnp.concatenate([i1, i2], axis=0)
    wt_ref[0] = jnp.concatenate([w1 / tot, w2 / tot], axis=0)


def _norm_route(x, g, sc, sh, router_w, router_b):
    b, s, d = x.shape
    ts = min(MOE_ROW_BLOCK, s)
    ct = d // LANES
    rwt = router_w.T
    rb = jnp.broadcast_to(router_b.reshape(N_EXPERTS, 1), (N_EXPERTS, LANES))
    h, idx, wts = pl.pallas_call(
        _norm_route_kernel,
        grid=(b, s // ts),
        in_specs=[pl.BlockSpec((1, ts, d), lambda i, j: (i, j, 0)),
                  pl.BlockSpec((1, d), lambda i, j: (0, 0)),
                  pl.BlockSpec((1, 1, d), lambda i, j: (i, 0, 0)),
                  pl.BlockSpec((1, 1, d), lambda i, j: (i, 0, 0)),
                  pl.BlockSpec((N_EXPERTS, d), lambda i, j: (0, 0)),
                  pl.BlockSpec((N_EXPERTS, LANES), lambda i, j: (0, 0))],
        out_specs=[pl.BlockSpec((ts * ct, LANES), lambda i, j: (i * (s // ts) + j, 0)),
                   pl.BlockSpec((1, TOP_K, ts), lambda i, j: (i, 0, j)),
                   pl.BlockSpec((1, TOP_K, ts), lambda i, j: (i, 0, j))],
        out_shape=[jax.ShapeDtypeStruct((b * s * ct, LANES), jnp.float32),
                   jax.ShapeDtypeStruct((b, TOP_K, s), jnp.int32),
                   jax.ShapeDtypeStruct((b, TOP_K, s), jnp.float32)],
        compiler_params=_params(("parallel", "parallel")),
        name="rmsnorm_adaln_router",
    )(x, g.reshape(1, d), sc.reshape(b, 1, d), sh.reshape(b, 1, d), rwt, rb)
    return h.reshape(b * s, ct, LANES), idx, wts


def _weight_tile(w_ref, wbf_sc, first, scale, scale_tiles):
    @pl.when(first)
    def _():
        w = w_ref[0]
        if scale_tiles:
            w = w * jnp.where(pl.program_id(0) < scale_tiles, scale, 1.0)
        wbf_sc[...] = w.astype(jnp.bfloat16)
    return wbf_sc[...]


def _weight_spec(layer, col0, k, tn, rank):
    assert col0 % tn == 0
    if rank == 2:
        return pl.BlockSpec((1, k, tn), lambda j, i: (layer, 0, col0 // tn + j))
    return pl.BlockSpec((1, k, tn), lambda j, bi, i: (layer, 0, col0 // tn + j))


def _mm_kernel(a_ref, w_ref, o_ref, wbf_sc, *, act, scale, scale_tiles):
    w = _weight_tile(w_ref, wbf_sc, pl.program_id(1) == 0, scale, scale_tiles)
    acc = jnp.dot(a_ref[...], w, preferred_element_type=jnp.float32)
    if act == "sigmoid":
        acc = jax.nn.sigmoid(acc)
    o_ref[...] = acc.astype(o_ref.dtype)


def _matmul(a, w, layer, col0, n, out_dtype, act=None, scale=1.0, scale_cols=0, tm=2048, tn=1024):
    m, k = a.shape
    tn = min(tn, n)
    tm = min(tm, m)
    assert scale_cols % tn == 0
    return pl.pallas_call(
        functools.partial(_mm_kernel, act=act, scale=scale, scale_tiles=scale_cols // tn),
        grid=(n // tn, m // tm),
        in_specs=[pl.BlockSpec((tm, k), lambda j, i: (i, 0)),
                  _weight_spec(layer, col0, k, tn, 2)],
        out_specs=pl.BlockSpec((tm, tn), lambda j, i: (i, j)),
        out_shape=jax.ShapeDtypeStruct((m, n), out_dtype),
        scratch_shapes=[pltpu.VMEM((k, tn), jnp.bfloat16)],
        compiler_params=_params(("arbitrary", "arbitrary")),
        name="matmul" if act is None else "matmul_" + act,
    )(a, w)


def _nsa_proj_kernel(a_ref, w_ref, o_ref, ch_ref, wbf_sc, stage_sc, *, scale, scale_tiles, kv_tile, n_tiles):
    tile = (pl.program_id(0) + kv_tile + 1) % n_tiles

    @pl.when(pl.program_id(1) == 0)
    def _():
        w = w_ref[0]
        if scale_tiles:
            w = w * jnp.where(tile < scale_tiles, scale, 1.0)
        wbf_sc[...] = w.astype(jnp.bfloat16)

    acc = jnp.dot(a_ref[...], wbf_sc[...], preferred_element_type=jnp.float32)
    o_ref[...] = acc.astype(o_ref.dtype)

    @pl.when(pl.program_id(0) == n_tiles - 1)
    def _():
        nch = ch_ref.shape[3]
        for c in range(acc.shape[1] // HEAD_DIM):
            kv, g = divmod(c, NSA_KV_HEADS)
            stage_sc[...] = acc[:, c * HEAD_DIM:(c + 1) * HEAD_DIM]
            for l in range(CMP_STRIDE):
                ch_ref[0, kv, g, :, l * HEAD_DIM:(l + 1) * HEAD_DIM] = (
                    stage_sc[pl.ds(l, nch, stride=CMP_STRIDE), :].astype(ch_ref.dtype))


def _nsa_projection(a, w, layer, s, scale, tm=1024, tn=1024):
    m, k = a.shape
    b = m // s
    n_tiles = NSA_QKV // tn
    kv_tile = ATT_WIDTH // tn
    assert ATT_WIDTH % tn == 0 and 2 * NSA_KV_WIDTH == tn and s % tm == 0 and HEAD_DIM == LANES
    per_b = s // tm
    nch = tm // CMP_STRIDE

    def tile_of(j):
        return (j + kv_tile + 1) % n_tiles

    def chunk_block(j, i):
        last = j == n_tiles - 1
        return (jnp.where(last, i // per_b, 0), 0, 0, jnp.where(last, i % per_b, 0), 0)

    return pl.pallas_call(
        functools.partial(_nsa_proj_kernel, scale=scale, scale_tiles=ATT_WIDTH // tn, kv_tile=kv_tile,
                          n_tiles=n_tiles),
        grid=(n_tiles, m // tm),
        in_specs=[pl.BlockSpec((tm, k), lambda j, i: (i, 0)),
                  pl.BlockSpec((1, k, tn), lambda j, i: (layer, 0, tile_of(j)))],
        out_specs=[pl.BlockSpec((tm, tn), lambda j, i: (i, tile_of(j))),
                   pl.BlockSpec((1, 2, NSA_KV_HEADS, nch, CMP_STRIDE * HEAD_DIM), chunk_block)],
        out_shape=[jax.ShapeDtypeStruct((m, NSA_QKV), jnp.bfloat16),
                   jax.ShapeDtypeStruct((b, 2, NSA_KV_HEADS, s // CMP_STRIDE, CMP_STRIDE * HEAD_DIM), jnp.bfloat16)],
        scratch_shapes=[pltpu.VMEM((k, tn), jnp.bfloat16), pltpu.VMEM((tm, HEAD_DIM), jnp.float32)],
        compiler_params=_params(("arbitrary", "arbitrary")),
        name="matmul_nsa_proj",
    )(a, w)


def _first_row_step():
    return (pl.program_id(1) == 0) & (pl.program_id(2) == 0)


def _mm_streams_kernel(a_ref, w_ref, *refs, dilations, scale, scale_tiles):
    out_refs, acc_sc, wbf_sc = refs[:-2], refs[-2], refs[-1]
    w = _weight_tile(w_ref, wbf_sc, _first_row_step(), scale, scale_tiles)
    acc = jnp.dot(a_ref[0], w, preferred_element_type=jnp.float32)
    _store_streams(acc, acc_sc, out_refs, dilations)


def _matmul_streams(a, w, layer, col0, n, dilations, scale=1.0, tm=1024, tn=1024):
    b, s, k = a.shape
    tn = min(tn, n)
    tm = min(tm, s)
    return pl.pallas_call(
        functools.partial(_mm_streams_kernel, dilations=dilations, scale=scale,
                          scale_tiles=n // tn if scale != 1.0 else 0),
        grid=(n // tn, b, s // tm),
        in_specs=[pl.BlockSpec((1, tm, k), lambda j, bi, i: (bi, i, 0)),
                  _weight_spec(layer, col0, k, tn, 3)],
        out_specs=[pl.BlockSpec((1, r, tm // r, tn), lambda j, bi, i: (bi, 0, i, j)) for r in dilations],
        out_shape=[jax.ShapeDtypeStruct((b, r, s // r, n), jnp.bfloat16) for r in dilations],
        scratch_shapes=[pltpu.VMEM((tn // LANES, tm, LANES), jnp.float32), pltpu.VMEM((k, tn), jnp.bfloat16)],
        compiler_params=_params(("arbitrary", "arbitrary", "arbitrary")),
        name="matmul_streams",
    )(a, w)


def _mm_residual_kernel(a_ref, w_ref, x_ref, g_ref, o_ref, wbf_sc):
    w = _weight_tile(w_ref, wbf_sc, _first_row_step(), 1.0, 0)
    acc = jnp.dot(a_ref[0], w, preferred_element_type=jnp.float32)
    o_ref[0] = x_ref[0] + g_ref[0] * acc


def _matmul_residual(a, w, layer, x, gate, tm=1024, tn=1024):
    b, s, k = a.shape
    n = w.shape[2]
    tn = min(tn, n)
    tm = min(tm, s)
    return pl.pallas_call(
        _mm_residual_kernel,
        grid=(n // tn, b, s // tm),
        in_specs=[pl.BlockSpec((1, tm, k), lambda j, bi, i: (bi, i, 0)),
                  _weight_spec(layer, 0, k, tn, 3),
                  pl.BlockSpec((1, tm, tn), lambda j, bi, i: (bi, i, j)),
                  pl.BlockSpec((1, 1, tn), lambda j, bi, i: (bi, 0, j))],
        out_specs=pl.BlockSpec((1, tm, tn), lambda j, bi, i: (bi, i, j)),
        out_shape=jax.ShapeDtypeStruct((b, s, n), jnp.float32),
        scratch_shapes=[pltpu.VMEM((k, tn), jnp.bfloat16)],
        compiler_params=_params(("arbitrary", "arbitrary", "arbitrary")),
        name="matmul_gated_residual",
    )(a, w, x, gate.reshape(b, 1, n))


def _gelu_tanh(x):
    return 0.5 * x * (1.0 + jnp.tanh(0.7978845608028654 * (x + 0.044715 * (x * x * x))))


def _compress_kernel(c_ref, pa_ref, pb_ref, wa_ref, wb_ref, w2_ref, o_ref):
    c = c_ref[0, 0, 0].astype(jnp.float32)
    lo = (c + pa_ref[0]).astype(jnp.bfloat16)
    hi = (c + pb_ref[0]).astype(jnp.bfloat16)
    ha = jnp.dot(lo, wa_ref[0], preferred_element_type=jnp.float32)
    hb = jnp.dot(hi, wb_ref[0], preferred_element_type=jnp.float32)
    n = ha.shape[0]
    hid = _gelu_tanh(ha + pltpu.roll(hb, n - 1, 0))
    out = jnp.dot(hid.astype(jnp.bfloat16), w2_ref[0], preferred_element_type=jnp.float32)
    row = lax.broadcasted_iota(jnp.int32, out.shape, 0)
    o_ref[0, 0, 0] = jnp.where(row < n - 1, out, 0.0).astype(o_ref.dtype)


def _compress(kv_chunks, w_phi1, w_phi2, phi_pos):
    b, two, hkv, nch, width = kv_chunks.shape
    dh = HEAD_DIM
    half = CMP_BLOCK // 2
    w1 = w_phi1.reshape(2, 2, half * dh, dh).astype(jnp.bfloat16)
    pos = phi_pos.reshape(2, 2, 1, half * dh)
    return pl.pallas_call(
        _compress_kernel,
        grid=(b, two, hkv),
        in_specs=[pl.BlockSpec((1, 1, 1, nch, width), lambda i, j, g: (i, j, g, 0, 0)),
                  pl.BlockSpec((1, 1, width), lambda i, j, g: (j, 0, 0)),
                  pl.BlockSpec((1, 1, width), lambda i, j, g: (j, 0, 0)),
                  pl.BlockSpec((1, width, dh), lambda i, j, g: (j, 0, 0)),
                  pl.BlockSpec((1, width, dh), lambda i, j, g: (j, 0, 0)),
                  pl.BlockSpec((1, dh, dh), lambda i, j, g: (j, 0, 0))],
        out_specs=pl.BlockSpec((1, 1, 1, nch, dh), lambda i, j, g: (i, j, g, 0, 0)),
        out_shape=jax.ShapeDtypeStruct((b, two, hkv, nch, dh), jnp.bfloat16),
        compiler_params=_params(("parallel", "parallel", "parallel")),
        name="nsa_compress",
    )(kv_chunks, pos[:, 0], pos[:, 1], w1[:, 0], w1[:, 1], w_phi2.astype(jnp.bfloat16))


MASK_BIG = 2.0 ** 100
AUX_SLOPE = 64
AUX_PAD = 70
SEL_CHUNK = 512
LOG2E = 1.4426950408889634


def _nsa_attn_kernel(sl_ref, q_ref, kc_ref, vc_ref, ks_ref, vs_ref, kw_ref, vw_ref, gt_ref, ovl_ref, qtab_ref,
                     kauxs_ref, kauxw_ref, *refs, n_cmp, n_blk, n_sel, cast_place, n_cast_dst):
    n_cast = len(cast_place)
    cast_src, refs = refs[:n_cast], refs[n_cast:]
    o_ref, refs = refs[0], refs[1:]
    cast_dst, refs = refs[:n_cast_dst], refs[n_cast_dst:]
    (ocmp_sc, maskq_sc, owin_sc, qaug_sc, qaugw_sc, p_sc, pw_sc, kaug_s, vaug_s, kaug_w, vaug_w), refs = refs[:11], refs[11:]
    cast_in, cast_out, (cast_in_sem, cast_out_sem) = refs[:n_cast], refs[n_cast:2 * n_cast], refs[2 * n_cast:]
    t = ATT_TILE
    dh = HEAD_DIM
    s_len = ks_ref.shape[1]
    wpad = NSA_WINDOW
    g = pl.program_id(1)
    qt = pl.program_id(2)
    qstart = pl.multiple_of(qt * t, t)

    if n_cast:
        step = (pl.program_id(0) * pl.num_programs(1) + g) * pl.num_programs(2) + qt
        n_steps = pl.num_programs(0) * pl.num_programs(1) * pl.num_programs(2)
        drain_cast = _cast_side_job(step, n_steps, cast_src, cast_dst, cast_place, cast_in, cast_out, cast_in_sem,
                                    cast_out_sem)

    @pl.when(qt == 0)
    def _():
        ones = jnp.ones((s_len, dh), jnp.bfloat16)
        kaug_s[:, :dh] = ks_ref[0]
        kaug_s[:, dh:] = kauxs_ref[...]
        vaug_s[:, :dh] = vs_ref[0]
        vaug_s[:, dh:] = ones
        kaug_w[:wpad, :dh] = jnp.zeros((wpad, dh), jnp.bfloat16)
        kaug_w[wpad:, :dh] = kw_ref[0]
        kaug_w[:, dh:] = kauxw_ref[...]
        vaug_w[:wpad, :dh] = jnp.zeros((wpad, dh), jnp.bfloat16)
        vaug_w[wpad:, :dh] = vw_ref[0]
        vaug_w[:wpad, dh:] = jnp.ones((wpad, dh), jnp.bfloat16)
        vaug_w[wpad:, dh:] = ones

    slopes = [sl_ref[g * NSA_GROUP + r] for r in range(NSA_GROUP)]
    row = lax.broadcasted_iota(jnp.int32, (t, LANES), 0)
    col = lax.broadcasted_iota(jnp.int32, (t, LANES), 1)
    prows = [slice(pair * 2 * t, (pair + 1) * 2 * t) for pair in range(NSA_GROUP // 2)]
    qtab = qtab_ref[0]

    def compress_and_select(qv, tile, par):
        qs = tile * t
        tpos = qs + row
        q4 = jnp.concatenate([qv[:, r * dh:(r + 1) * dh] for r in range(NSA_GROUP)], axis=0)
        kc = kc_ref[0, 0, 0]
        vc = vc_ref[0, 0, 0]
        s = lax.dot_general(q4, kc, _NT, preferred_element_type=jnp.float32)
        visible = (col * CMP_STRIDE + (CMP_BLOCK - 1) <= tpos) & (col < n_cmp)
        dist_c = tpos.astype(jnp.float32) - (col.astype(jnp.float32) * CMP_STRIDE + (CMP_BLOCK - 1) / 2.0)
        psum = jnp.zeros((t, LANES), jnp.float32)
        for r in range(NSA_GROUP):
            rows = slice(r * t, (r + 1) * t)
            sr = jnp.where(visible, s[rows] - slopes[r] * dist_c, NEG_INF)
            e = jnp.where(visible, jnp.exp2(sr - jnp.max(sr, axis=-1, keepdims=True)), 0.0)
            p = e / jnp.maximum(jnp.sum(e, axis=-1, keepdims=True), 1e-30)
            psum = psum + p
            ocmp_sc[par, rows] = jnp.dot(p.astype(vc.dtype), vc, preferred_element_type=jnp.float32)
        nb = -(-n_blk // 8) * 8
        imp_t = lax.dot_general(ovl_ref[...], psum, _NT, precision=_HI, preferred_element_type=jnp.float32)[:nb]
        jrow = lax.broadcasted_iota(jnp.int32, (nb, t), 0)
        tpos_t = qs + lax.broadcasted_iota(jnp.int32, (nb, t), 1)
        cur = jnp.right_shift(tpos_t, int(np.log2(SEL_BLOCK)))
        forced = (jrow == 0) | (jrow == cur) | (jrow == cur - 1)
        score = jnp.where(forced, FORCE_SCORE, jnp.where(jrow * SEL_BLOCK <= tpos_t, imp_t, -1.0))
        rank = jnp.zeros((nb, t), jnp.float32)
        for j in range(n_blk):
            cj = score[j:j + 1, :]
            ahead = (cj > score) | ((cj == score) & (jrow > j))
            rank = rank + jnp.where(ahead, 1.0, 0.0)
        mask_t = jnp.where((rank < n_sel) & (jrow < n_blk), 0.0, -MASK_BIG)
        maskq_sc[par] = jnp.concatenate([mask_t, jnp.zeros((LANES - nb, t), jnp.float32)], axis=0).T

    par = 0
    q = q_ref[0]

    for r in range(NSA_GROUP):
        rows = slice(r * t, (r + 1) * t)
        qaugw_sc[rows, :dh] = q[:, r * dh:(r + 1) * dh]
        qaugw_sc[rows, dh:] = jnp.broadcast_to(qtab[r:r + 1, :], (t, LANES)).astype(jnp.bfloat16)
    wk = wpad + t
    nw = wk // LANES
    kw = kaug_w[pl.ds(qstart, wk), :]
    vw = vaug_w[pl.ds(qstart, wk), :]
    sws = [lax.dot_general(qaugw_sc[prow], kw, _NT, preferred_element_type=jnp.float32) for prow in prows]

    compress_and_select(q, qt, par)

    mask_q = maskq_sc[par]
    for r in range(NSA_GROUP):
        rows = slice(r * t, (r + 1) * t)
        qaug_sc[rows, :dh] = q[:, r * dh:(r + 1) * dh]
        qaug_sc[rows, dh:] = (mask_q + qtab[r:r + 1, :]).astype(jnp.bfloat16)

    for pair, prow in enumerate(prows):
        sw = sws[pair]
        for r2 in range(2):
            rows = slice((2 * pair + r2) * t, (2 * pair + r2 + 1) * t)
            tiles = [sw[r2 * t:(r2 + 1) * t, j * LANES:(j + 1) * LANES] for j in range(nw)]
            tiles[0] = jnp.where(col > row, tiles[0], NEG_INF)
            tiles[-1] = jnp.where(col <= row, tiles[-1], NEG_INF)
            mx = functools.reduce(jnp.maximum, tiles)
            m = jnp.broadcast_to(jnp.max(mx, axis=-1, keepdims=True), (t, LANES))
            for j in range(nw):
                pw_sc[rows, j * LANES:(j + 1) * LANES] = jnp.exp2(tiles[j] - m).astype(jnp.bfloat16)
        ow = jnp.dot(pw_sc[prow], vw, preferred_element_type=jnp.float32)
        owin_sc[prow] = ow[:, :dh] / ow[:, dh:]

    ch = SEL_CHUNK
    nl = ch // LANES
    n_full = qt // (ch // t)
    gt = gt_ref[0]
    cmr = col - row

    def selected(k):
        nk = (k + 1) * ch
        kk = kaug_s[:nk, :]
        vv = vaug_s[:nk, :]
        sks = [lax.dot_general(qaug_sc[prow], kk, _NT, preferred_element_type=jnp.float32) for prow in prows]
        for pair, prow in enumerate(prows):
            sk = sks[pair]
            for r2 in range(2):
                rows = slice((2 * pair + r2) * t, (2 * pair + r2 + 1) * t)
                tiles = [sk[r2 * t:(r2 + 1) * t, j * LANES:(j + 1) * LANES] for j in range(nk // LANES)]
                for j in range(k * nl, (k + 1) * nl):
                    tiles[j] = jnp.where(cmr <= qstart - j * LANES, tiles[j], NEG_INF)
                mx = functools.reduce(jnp.maximum, tiles)
                m = jnp.broadcast_to(jnp.max(mx, axis=-1, keepdims=True), (t, LANES))
                for j in range(nk // LANES):
                    p_sc[rows, j * LANES:(j + 1) * LANES] = jnp.exp2(tiles[j] - m).astype(jnp.bfloat16)
            pv = jnp.dot(p_sc[prow, :nk], vv, preferred_element_type=jnp.float32)
            for r2 in range(2):
                r = 2 * pair + r2
                rows = slice(r * t, (r + 1) * t)
                orow = slice(r2 * t, (r2 + 1) * t)
                o = (gt[:, 3 * r:3 * r + 1] * ocmp_sc[par, rows]
                     + gt[:, 3 * r + 1:3 * r + 2] * (pv[orow, :dh] / pv[orow, dh:])
                     + gt[:, 3 * r + 2:3 * r + 3] * owin_sc[rows])
                o_ref[0, :, r * dh:(r + 1) * dh] = o.astype(o_ref.dtype)

    for k in range(s_len // ch):
        pl.when(n_full == k)(functools.partial(selected, k))
    if cast_place:
        drain_cast()


def _cast_chunks(shape, n_steps):
    r, c = shape
    for n_c in (1, 2, 4, 8, 16):
        if n_steps % n_c == 0 and c % (n_c * LANES) == 0 and r % (n_steps // n_c) == 0:
            rows = r // (n_steps // n_c)
            if rows % 16 == 0:
                return rows, c // n_c
    raise ValueError("no tile-aligned split of %s into %d chunks" % (shape, n_steps))


def _cast_side_job(n, n_steps, srcs, dsts, place, inbufs, outbufs, in_sem, out_sem):
    slot = n % 2

    def chunk(k, step, col0=0):
        rows, cols = inbufs[k].shape[1:]
        n_c = srcs[k].shape[1] // cols
        return (pl.ds(pl.multiple_of((step // n_c) * rows, 16), rows),
                pl.ds(pl.multiple_of(col0 + (step % n_c) * cols, LANES), cols))

    def in_copy(k, step, sl):
        return pltpu.make_async_copy(srcs[k].at[chunk(k, step)], inbufs[k].at[sl], in_sem.at[sl])

    def out_copy(k, step, sl):
        dst, col0 = place[k]
        return pltpu.make_async_copy(outbufs[k].at[sl], dsts[dst].at[chunk(k, step, col0)], out_sem.at[sl])

    ks = range(len(srcs))

    @pl.when(n == 0)
    def _():
        for k in ks:
            in_copy(k, 0, 0).start()
            outbufs[k][...] = jnp.zeros(outbufs[k].shape, outbufs[k].dtype)
            out_copy(k, 0, 0).start()
            out_copy(k, 1, 1).start()

    for k in ks:
        in_copy(k, n, slot).wait()
    nxt = jnp.minimum(n + 1, n_steps - 1)
    for k in ks:
        in_copy(k, nxt, 1 - slot).start()
    for k in ks:
        out_copy(k, n, slot).wait()

    for k in ks:
        outbufs[k][slot] = inbufs[k][slot].astype(outbufs[k].dtype)
    for k in ks:
        out_copy(k, n, slot).start()

    def drain():
        @pl.when(n == n_steps - 1)
        def _():
            for k in ks:
                in_copy(k, n, 1 - slot).wait()
                out_copy(k, n, 1 - slot).wait()
                out_copy(k, n, slot).wait()

    return drain


def _bf16_pieces(x):
    a0 = x.astype(jnp.bfloat16).astype(jnp.float32)
    a1 = (x - a0).astype(jnp.bfloat16).astype(jnp.float32)
    a2 = (x - a0 - a1).astype(jnp.bfloat16).astype(jnp.float32)
    return [a0, a1, a2]


def _nsa_tables(s, slopes2):
    n_cmp = s // CMP_STRIDE - CMP_BLOCK // CMP_STRIDE + 1
    n_blk = s // SEL_BLOCK
    assert n_cmp < LANES and n_blk <= AUX_SLOPE
    bj = np.arange(LANES)[:, None]
    cn = np.arange(LANES)[None, :]
    ovl_t = np.clip(np.minimum(cn * CMP_STRIDE + CMP_BLOCK, (bj + 1) * SEL_BLOCK)
                    - np.maximum(cn * CMP_STRIDE, bj * SEL_BLOCK), 0, None) / CMP_STRIDE
    ovl_t = np.where((cn < n_cmp) & (bj < n_blk), ovl_t, 0.0).astype(np.float32)
    pos = np.arange(s)
    kaux = np.zeros((s, LANES), np.float32)
    kaux[pos, pos // SEL_BLOCK] = 1.0
    kaux[:, AUX_SLOPE:AUX_SLOPE + 3] = (pos // LANES * LANES)[:, None]
    kaux[:, AUX_SLOPE + 3:AUX_SLOPE + 6] = (pos % LANES)[:, None]
    kaux_w = np.zeros((NSA_WINDOW + s, LANES), np.float32)
    kaux_w[NSA_WINDOW:, AUX_SLOPE:AUX_SLOPE + 6] = kaux[:, AUX_SLOPE:AUX_SLOPE + 6]
    kaux_w[:NSA_WINDOW, AUX_PAD] = 1.0
    pieces = jnp.stack(_bf16_pieces(slopes2) * 2, axis=-1)
    qtab = jnp.zeros((N_HEADS, LANES), jnp.float32)
    qtab = qtab.at[:, AUX_SLOPE:AUX_SLOPE + 6].set(pieces).at[:, AUX_PAD].set(-MASK_BIG)
    qtab = jnp.pad(qtab.reshape(NSA_KV_HEADS, NSA_GROUP, LANES), ((0, 0), (0, 8 - NSA_GROUP), (0, 0)))
    return (n_cmp, n_blk, jnp.asarray(ovl_t), qtab, jnp.asarray(kaux, dtype=jnp.bfloat16),
            jnp.asarray(kaux_w, dtype=jnp.bfloat16))


def _nsa_attention(proj, cmp_kv, gates, slopes2, cast_ws=(), cast_place=(), cast_shapes=()):
    b, s, _ = proj.shape
    t = ATT_TILE
    n_steps = b * NSA_KV_HEADS * (s // t)
    chunks = [_cast_chunks(w.shape, n_steps) for w in cast_ws]
    assert n_steps >= 2
    any_spec = pl.BlockSpec(memory_space=pl.ANY)
    n_cmp, n_blk, ovl_t, qtab, kaux_s, kaux_w = _nsa_tables(s, slopes2)
    assert cmp_kv.shape[3] == LANES and s % SEL_CHUNK == 0 and NSA_WINDOW % t == 0
    qw = NSA_GROUP * HEAD_DIM
    rows = NSA_GROUP * t
    kv0 = ATT_WIDTH // HEAD_DIM

    def kv_spec(j):
        return pl.BlockSpec((1, s, HEAD_DIM), lambda bi, g, qi, j=j: (bi, 0, kv0 + j * NSA_KV_HEADS + g))

    def const(shape):
        return pl.BlockSpec(shape, lambda bi, g, qi: (0,) * len(shape))

    outs = pl.pallas_call(
        functools.partial(_nsa_attn_kernel, n_cmp=n_cmp, n_blk=n_blk, n_sel=min(N_SEL, n_blk),
                          cast_place=tuple(cast_place), n_cast_dst=len(cast_shapes)),
        grid=(b, NSA_KV_HEADS, s // t),
        in_specs=[pl.BlockSpec(memory_space=pltpu.SMEM),
                  pl.BlockSpec((1, t, qw), lambda bi, g, qi: (bi, qi, g)),
                  pl.BlockSpec((1, 1, 1, LANES, HEAD_DIM), lambda bi, g, qi: (bi, 0, g, 0, 0)),
                  pl.BlockSpec((1, 1, 1, LANES, HEAD_DIM), lambda bi, g, qi: (bi, 1, g, 0, 0)),
                  kv_spec(2), kv_spec(3), kv_spec(4), kv_spec(5),
                  pl.BlockSpec((1, t, LANES), lambda bi, g, qi: (bi, qi, g)),
                  const((LANES, LANES)),
                  pl.BlockSpec((1, 8, LANES), lambda bi, g, qi: (g, 0, 0)),
                  const((s, LANES)), const((NSA_WINDOW + s, LANES))] + [any_spec] * len(cast_ws),
        out_specs=[pl.BlockSpec((1, t, qw), lambda bi, g, qi: (bi, qi, g))] + [any_spec] * len(cast_shapes),
        out_shape=[jax.ShapeDtypeStruct((b, s, ATT_WIDTH), jnp.bfloat16)]
        + [jax.ShapeDtypeStruct(shape, jnp.bfloat16) for shape in cast_shapes],
        scratch_shapes=[pltpu.VMEM((1, rows, HEAD_DIM), jnp.float32),
                        pltpu.VMEM((1, t, LANES), jnp.float32),
                        pltpu.VMEM((rows, HEAD_DIM), jnp.float32),
                        pltpu.VMEM((rows, 2 * HEAD_DIM), jnp.bfloat16),
                        pltpu.VMEM((rows, 2 * HEAD_DIM), jnp.bfloat16),
                        pltpu.VMEM((rows, s), jnp.bfloat16),
                        pltpu.VMEM((rows, NSA_WINDOW + t), jnp.bfloat16),
                        pltpu.VMEM((s, 2 * HEAD_DIM), jnp.bfloat16),
                        pltpu.VMEM((s, 2 * HEAD_DIM), jnp.bfloat16),
                        pltpu.VMEM((NSA_WINDOW + s, 2 * HEAD_DIM), jnp.bfloat16),
                        pltpu.VMEM((NSA_WINDOW + s, 2 * HEAD_DIM), jnp.bfloat16)]
        + [pltpu.VMEM((2,) + c, jnp.float32) for c in chunks]
        + [pltpu.VMEM((2,) + c, jnp.bfloat16) for c in chunks]
        + [pltpu.SemaphoreType.DMA((2,)), pltpu.SemaphoreType.DMA((2,))],
        compiler_params=_params(("arbitrary", "arbitrary", "arbitrary")),
        name="nsa_attention",
    )(slopes2, proj, cmp_kv, cmp_kv, proj, proj, proj, proj, gates, ovl_t, qtab, kaux_s, kaux_w, *cast_ws)
    return outs[0], list(outs[1:])


def _dil_attn_kernel(sl_ref, q_ref, *refs, window, has_prev):
    if has_prev:
        kp_ref, kc_ref, vp_ref, vc_ref, o_ref, lse_ref, bias_sc = refs
    else:
        kc_ref, vc_ref, o_ref, lse_ref, bias_sc = refs
        kp_ref, vp_ref = kc_ref, vc_ref
    t = ATT_TILE
    i = pl.program_id(2)

    @pl.when((pl.program_id(0) == 0) & (pl.program_id(1) == 0) & (i == 0))
    def _():
        row = lax.broadcasted_iota(jnp.int32, (t, 2 * t), 0)
        col = lax.broadcasted_iota(jnp.int32, (t, 2 * t), 1)
        dist = row - col + t
        band = (dist >= 0) & (dist <= window)
        distf = dist.astype(jnp.float32)
        for h in range(N_HEADS):
            bias = sl_ref[h] * distf
            bias_sc[0, h] = jnp.where(band & (col >= t), bias, -NEG_INF)
            bias_sc[1, h] = jnp.where(band, bias, -NEG_INF)

    var = jnp.minimum(i, 1)
    lane = lax.broadcasted_iota(jnp.int32, (t, LANES), 1)
    lse_all = jnp.zeros((t, LANES), jnp.float32)
    for h in range(N_HEADS):
        hs = slice(h * HEAD_DIM, (h + 1) * HEAD_DIM)
        q = q_ref[0, 0, :, hs]
        k = jnp.concatenate([kp_ref[0, 0, :, hs], kc_ref[0, 0, :, hs]], axis=0)
        v = jnp.concatenate([vp_ref[0, 0, :, hs], vc_ref[0, 0, :, hs]], axis=0)
        s = lax.dot_general(q, k, _NT, preferred_element_type=jnp.float32) - bias_sc[var, h]
        m = jnp.max(s, axis=-1, keepdims=True)
        e = jnp.exp2(s - m)
        l = jnp.sum(e, axis=-1, keepdims=True)
        o = jnp.dot(e.astype(v.dtype), v, preferred_element_type=jnp.float32) / l
        o_ref[0, 0, :, hs] = o.astype(o_ref.dtype)
        lse_all = jnp.where(lane == h, m + jnp.log2(l), lse_all)
    lse_ref[0, 0] = lse_all


def _dil_attention(q, kv, slopes2, win, r):
    b, _, ln, _ = q.shape
    t = ATT_TILE

    def cur(c):
        return pl.BlockSpec((1, 1, t, ATT_WIDTH), lambda bi, rho, i, c=c: (bi, rho, i, c))

    def prev(c):
        return pl.BlockSpec((1, 1, t, ATT_WIDTH), lambda bi, rho, i, c=c: (bi, rho, jnp.maximum(i - 1, 0), c))

    has_prev = ln > t
    kv_specs = [prev(0), cur(0), prev(1), cur(1)] if has_prev else [cur(0), cur(1)]
    return pl.pallas_call(
        functools.partial(_dil_attn_kernel, window=win // r, has_prev=has_prev),
        grid=(b, r, ln // t),
        in_specs=[pl.BlockSpec(memory_space=pltpu.SMEM), cur(0)] + kv_specs,
        out_specs=[pl.BlockSpec((1, 1, t, ATT_WIDTH), lambda bi, rho, i: (bi, rho, i, 0)),
                   pl.BlockSpec((1, 1, t, LANES), lambda bi, rho, i: (bi, rho, i, 0))],
        out_shape=[jax.ShapeDtypeStruct((b, r, ln, ATT_WIDTH), jnp.bfloat16),
                   jax.ShapeDtypeStruct((b, r, ln, LANES), jnp.float32)],
        scratch_shapes=[pltpu.VMEM((2, N_HEADS, t, 2 * t), jnp.float32)],
        compiler_params=_params(("arbitrary", "arbitrary", "arbitrary")),
        name="dilated_attention_r%d" % r,
    )(slopes2 * r, q, *([kv] * len(kv_specs)))


def _dil_merge_kernel(*refs, dilations):
    ng = len(dilations)
    o_refs, l_refs, out_ref, o_sc, l_sc = refs[:ng], refs[ng:2 * ng], refs[2 * ng], refs[2 * ng + 1], refs[2 * ng + 2]
    ts = out_ref.shape[1]
    ls = []
    for gi, r in enumerate(dilations):
        if r == 1:
            ls.append(l_refs[gi][0, 0])
            continue
        for rho in range(r):
            l_sc[gi, pl.ds(rho, ts // r, stride=r), :] = l_refs[gi][0, rho]
            for h in range(N_HEADS):
                o_sc[gi, h, pl.ds(rho, ts // r, stride=r), :] = (
                    o_refs[gi][0, rho, :, h * HEAD_DIM:(h + 1) * HEAD_DIM].astype(jnp.float32))
        ls.append(l_sc[gi])
    m = functools.reduce(jnp.maximum, ls)
    es = [jnp.exp2(l - m) for l in ls]
    den = functools.reduce(lambda a, b: a + b, es)
    ws = [e / den for e in es]
    for h in range(N_HEADS):
        hs = slice(h * HEAD_DIM, (h + 1) * HEAD_DIM)
        acc = None
        for gi, r in enumerate(dilations):
            og = o_refs[gi][0, 0, :, hs].astype(jnp.float32) if r == 1 else o_sc[gi, h]
            term = ws[gi][:, h:h + 1] * og
            acc = term if acc is None else acc + term
        out_ref[0, :, hs] = acc.astype(out_ref.dtype)


def _dil_merge(outs, lses, dilations):
    b, _, _, w = outs[0].shape
    s = outs[0].shape[1] * outs[0].shape[2]
    ts = min(ROW_BLOCK, s)
    ng = len(dilations)

    def spec(r, width):
        return pl.BlockSpec((1, r, ts // r, width), lambda i, j: (i, 0, j, 0))

    return pl.pallas_call(
        functools.partial(_dil_merge_kernel, dilations=dilations),
        grid=(b, s // ts),
        in_specs=[spec(r, w) for r in dilations] + [spec(r, LANES) for r in dilations],
        out_specs=pl.BlockSpec((1, ts, w), lambda i, j: (i, j, 0)),
        out_shape=jax.ShapeDtypeStruct((b, s, w), jnp.bfloat16),
        scratch_shapes=[pltpu.VMEM((ng, w // HEAD_DIM, ts, HEAD_DIM), jnp.float32),
                        pltpu.VMEM((ng, ts, LANES), jnp.float32)],
        compiler_params=_params(("parallel", "parallel")),
        name="dilated_merge",
    )(*outs, *lses)


ROW_TILE = 8


def _gather_rows(idx_ref, src_hbm, dst, sem):
    def body(i, carry):
        for j in range(ROW_TILE):
            pltpu.make_async_copy(src_hbm.at[idx_ref[0, 0, i * ROW_TILE + j]], dst.at[i, :, j], sem).start()
        return carry
    lax.fori_loop(0, dst.shape[0], body, 0)


def _gather_tiled_rows(idx_ref, src_hbm, dst, sem):
    def body(i, carry):
        for j in range(ROW_TILE):
            pltpu.make_async_copy(src_hbm.at[pl.ds(idx_ref[0, 0, i * ROW_TILE + j], 1)],
                                  dst.at[i, pl.ds(j, 1)], sem).start()
        return carry
    lax.fori_loop(0, dst.shape[0], body, 0)


def _wait_rows(dst, sem):
    pltpu.make_async_copy(dst, dst, sem).wait()


def _expert_kernel(be_ref, nu_ref, tokc_ref, tokn_ref, h_hbm, wgu_ref, wd_ref, o_ref, xbuf, sem):
    del be_ref
    de = wd_ref.shape[1]
    blk = pl.program_id(0)
    n_used = nu_ref[0]
    slot = blk % 2

    @pl.when(blk == 0)
    def _():
        _gather_rows(tokc_ref, h_hbm, xbuf.at[0], sem.at[0])

    @pl.when(blk + 1 < n_used)
    def _():
        _gather_rows(tokn_ref, h_hbm, xbuf.at[1 - slot], sem.at[1 - slot])

    @pl.when(blk < n_used)
    def _():
        _wait_rows(xbuf.at[slot], sem.at[slot])
        x = _load_row_tiles(xbuf.at[slot]).astype(jnp.bfloat16)
        gate = jnp.dot(x, wgu_ref[0, :, :de], preferred_element_type=jnp.float32)
        up = jnp.dot(x, wgu_ref[0, :, de:], preferred_element_type=jnp.float32)
        hid = (gate * jax.nn.sigmoid(gate) * up).astype(jnp.bfloat16)
        o_ref[...] = jnp.dot(hid, wd_ref[0], preferred_element_type=jnp.float32)

    @pl.when(blk >= n_used)
    def _():
        o_ref[...] = jnp.zeros(o_ref.shape, o_ref.dtype)


def _expert_blocks(h3, slot_tok, blk_e, n_used, w_gate_up, w_down):
    t, ct, _ = h3.shape
    d = ct * LANES
    nblk = blk_e.shape[0]
    de = w_down.shape[1]
    assert de % LANES == 0
    tok = slot_tok.reshape(nblk, 1, MOE_BLOCK)
    grid_spec = pltpu.PrefetchScalarGridSpec(
        num_scalar_prefetch=2,
        grid=(nblk,),
        in_specs=[pl.BlockSpec((1, 1, MOE_BLOCK), lambda i, be, nu: (i, 0, 0), memory_space=pltpu.SMEM),
                  pl.BlockSpec((1, 1, MOE_BLOCK), lambda i, be, nu: (jnp.minimum(i + 1, nblk - 1), 0, 0),
                               memory_space=pltpu.SMEM),
                  pl.BlockSpec(memory_space=pl.ANY),
                  pl.BlockSpec((1, d, 2 * de), lambda i, be, nu: (be[i], 0, 0)),
                  pl.BlockSpec((1, de, d), lambda i, be, nu: (be[i], 0, 0))],
        out_specs=pl.BlockSpec((MOE_BLOCK, d), lambda i, be, nu: (i, 0)),
        scratch_shapes=[pltpu.VMEM((2, MOE_BLOCK // ROW_TILE, ct, ROW_TILE, LANES), jnp.float32),
                        pltpu.SemaphoreType.DMA((2,))],
    )
    return pl.pallas_call(
        _expert_kernel,
        grid_spec=grid_spec,
        out_shape=jax.ShapeDtypeStruct((nblk * MOE_BLOCK, d), jnp.float32),
        compiler_params=_params(("arbitrary",)),
        name="moe_expert_blocks",
    )(blk_e, n_used, tok, tok, h3, w_gate_up, w_down)


def _combine_kernel(d0c_ref, d1c_ref, d0n_ref, d1n_ref, x_ref, w0_ref, w1_ref, g_ref, gn_ref, yo_hbm, o_ref,
                    buf, sem, *, out_norm):
    i = pl.program_id(0)
    n = pl.num_programs(0)
    slot = i % 2
    tt = x_ref.shape[0]

    @pl.when(i == 0)
    def _():
        _gather_tiled_rows(d0c_ref, yo_hbm, buf.at[0, 0], sem.at[0])
        _gather_tiled_rows(d1c_ref, yo_hbm, buf.at[0, 1], sem.at[0])

    @pl.when(i + 1 < n)
    def _():
        _gather_tiled_rows(d0n_ref, yo_hbm, buf.at[1 - slot, 0], sem.at[1 - slot])
        _gather_tiled_rows(d1n_ref, yo_hbm, buf.at[1 - slot, 1], sem.at[1 - slot])

    _wait_rows(buf.at[slot, 0], sem.at[slot])
    _wait_rows(buf.at[slot, 1], sem.at[slot])
    y = (w0_ref[...][:, 0:1] * buf[slot, 0].reshape(x_ref.shape)
         + w1_ref[...][:, 0:1] * buf[slot, 1].reshape(x_ref.shape))
    out = x_ref[...] + g_ref[0] * y
    o_ref[...] = _rms(out, gn_ref[...]) if out_norm else out


def _moe_combine(x2d, yo, dest, wts, gate, s, out_norm_g=None):
    t, d = x2d.shape
    out_norm = out_norm_g is not None
    gn = (out_norm_g if out_norm else jnp.ones((d,), jnp.float32)).reshape(1, d)
    tt = min(MOE_ROW_BLOCK, s)
    nt = t // tt
    d0 = dest[:, 0].reshape(nt, 1, tt)
    d1 = dest[:, 1].reshape(nt, 1, tt)
    w0 = jnp.broadcast_to(wts[:, 0:1], (t, LANES))
    w1 = jnp.broadcast_to(wts[:, 1:2], (t, LANES))
    b = gate.shape[0]
    cur = pl.BlockSpec((1, 1, tt), lambda i: (i, 0, 0), memory_space=pltpu.SMEM)
    nxt = pl.BlockSpec((1, 1, tt), lambda i: (jnp.minimum(i + 1, nt - 1), 0, 0), memory_space=pltpu.SMEM)
    return pl.pallas_call(
        functools.partial(_combine_kernel, out_norm=out_norm),
        grid=(nt,),
        in_specs=[cur, cur, nxt, nxt,
                  pl.BlockSpec((tt, d), lambda i: (i, 0)),
                  pl.BlockSpec((tt, LANES), lambda i: (i, 0)),
                  pl.BlockSpec((tt, LANES), lambda i: (i, 0)),
                  pl.BlockSpec((1, 1, d), lambda i: (i * tt // s, 0, 0)),
                  pl.BlockSpec((1, d), lambda i: (0, 0)),
                  pl.BlockSpec(memory_space=pl.ANY)],
        out_specs=pl.BlockSpec((tt, d), lambda i: (i, 0)),
        out_shape=jax.ShapeDtypeStruct((t, d), jnp.float32),
        scratch_shapes=[pltpu.VMEM((2, 2, tt // ROW_TILE, ROW_TILE, d), jnp.float32),
                        pltpu.SemaphoreType.DMA((2,))],
        compiler_params=_params(("arbitrary",)),
        name="moe_combine_residual",
    )(d0, d1, d0, d1, x2d, w0, w1, gate.reshape(b, 1, d), gn, yo)


def _dispatch_tables(idx):
    t = idx.shape[0]
    a = t * TOP_K
    flat_e = idx.reshape(a)
    onehot = (flat_e[:, None] == jnp.arange(N_EXPERTS, dtype=jnp.int32)[None, :]).astype(jnp.int32)
    csum = jnp.cumsum(onehot, axis=0)
    rank = jnp.take_along_axis(csum, flat_e[:, None], axis=1)[:, 0] - 1
    counts = csum[-1]
    padded = (counts + MOE_BLOCK - 1) // MOE_BLOCK * MOE_BLOCK
    pad_end = jnp.cumsum(padded)
    pad_start = pad_end - padded
    dest = pad_start[flat_e] + rank
    nblk = -(-(a + N_EXPERTS * MOE_BLOCK) // MOE_BLOCK)
    cap = nblk * MOE_BLOCK
    slot_tok = jnp.zeros((cap,), jnp.int32).at[dest].set(jnp.arange(a, dtype=jnp.int32) // TOP_K,
                                                         unique_indices=True)
    blk_start = jnp.arange(nblk, dtype=jnp.int32) * MOE_BLOCK
    blk_e = jnp.minimum(jnp.sum((pad_end[None, :] <= blk_start[:, None]).astype(jnp.int32), axis=1), N_EXPERTS - 1)
    n_used = (pad_end[-1:] // MOE_BLOCK).astype(jnp.int32)
    return dest.reshape(t, TOP_K).astype(jnp.int32), slot_tok, blk_e.astype(jnp.int32), n_used


def _moe_layer(x, g, sc, sh, gate, router_w, router_b, layer, w_gate_up, w_down, out_norm_g=None):
    b, s, d = x.shape
    t = b * s
    h, idx, wts = _norm_route(x, g, sc, sh, router_w, router_b)
    idx = idx.transpose(0, 2, 1).reshape(t, TOP_K)
    wts = wts.transpose(0, 2, 1).reshape(t, TOP_K)
    dest, slot_tok, blk_e, n_used = _dispatch_tables(idx)
    yo = _expert_blocks(h, slot_tok, blk_e + layer * N_EXPERTS, n_used, w_gate_up, w_down)
    return _moe_combine(x.reshape(t, d), yo, dest, wts, gate, s, out_norm_g).reshape(b, s, d)


def _nsa_layer(h, x, gate, slopes, j, w_in, w_phi1, w_phi2, phi_pos, w_out, cast_job=((), (), ())):
    b, s, d = h.shape
    t = b * s
    h2d = h.reshape(t, d)
    proj, kv_chunks = _nsa_projection(h2d, w_in, j, s, HEAD_DIM ** -0.5 * LOG2E)
    proj = proj.reshape(b, s, NSA_QKV)
    wg = w_in[j, :, NSA_QKV:].reshape(d, NSA_KV_HEADS, NSA_GATES)
    wg = jnp.pad(wg, ((0, 0), (0, 0), (0, LANES - NSA_GATES))).reshape(1, d, NSA_KV_HEADS * LANES)
    gates = _matmul(h2d, wg, 0, 0, NSA_KV_HEADS * LANES, jnp.float32, act="sigmoid")
    gates = gates.reshape(b, s, NSA_KV_HEADS * LANES)
    cmp_kv = _compress(kv_chunks, w_phi1, w_phi2, phi_pos)
    o, casted = _nsa_attention(proj, cmp_kv, gates, slopes * LOG2E, *cast_job)
    return _matmul_residual(o, w_out, j, x, gate), casted


def _dil_layer(x, g, sc, sh, gate, slopes, j, w_in, w_out):
    b, s, d = x.shape
    dilations = tuple(r for _, r in DIL_PAIRS)
    assert dilations[0] == 1
    hs = _norm_mod_streams(x, g, sc, sh, dilations)
    qs = _matmul_streams(hs[0].reshape(b, s, d), w_in, j, 0, ATT_WIDTH, dilations,
                         scale=HEAD_DIM ** -0.5 * LOG2E)
    outs, lses = [], []
    for gidx, (win, r) in enumerate(DIL_PAIRS):
        off = ATT_WIDTH * (1 + 2 * gidx)
        kv = _matmul(hs[gidx].reshape(b * s, d), w_in, j, off, 2 * ATT_WIDTH, jnp.bfloat16)
        o, lse = _dil_attention(qs[gidx], kv.reshape(b, r, s // r, 2 * ATT_WIDTH), slopes * LOG2E, win, r)
        outs.append(o)
        lses.append(lse)
    o = _dil_merge(outs, lses, dilations)
    return _matmul_residual(o, w_out, j, x, gate)


def kernel(x, c, ada_w, ada_b, norm_mix, norm_ffn, norm_final, nsa_w_in, nsa_w_phi1, nsa_w_phi2, nsa_phi_pos,
           nsa_w_out, dil_w_in, dil_w_out, router_w, router_b, exp_w_gate, exp_w_up, exp_w_down):
    depth = ada_w.shape[0]
    d = x.shape[-1]
    mod = _modulation(c, ada_w, ada_b)
    slopes = 2.0 ** (-ALIBI_MAX_BIAS * jnp.arange(1, N_HEADS + 1, dtype=jnp.float32) / N_HEADS)
    n_exp, _, de = exp_w_gate.shape[1:]
    for i in range(depth):
        sh_m, sc_m, g_m, sh_f, sc_f, g_f = [mod[i, :, k * d:(k + 1) * d] for k in range(6)]
        j = i // 2
        if i % 2 == 0:
            h = _norm_mod(x, norm_mix[i], sc_m, sh_m, jnp.bfloat16)
            cast_job = ((), (), ())
            if i == 0:
                srcs = [w.reshape(-1, w.shape[-1]) for w in (exp_w_gate, exp_w_up, exp_w_down)]
                cast_job = (srcs, ((0, 0), (0, de), (1, 0)), ((srcs[0].shape[0], 2 * de), srcs[2].shape))
            x, casted = _nsa_layer(h, x, g_m, slopes, j, nsa_w_in, nsa_w_phi1[j], nsa_w_phi2[j], nsa_phi_pos[j],
                                   nsa_w_out, cast_job)
            if i == 0:
                e_gate_up = casted[0].reshape(depth * n_exp, d, 2 * de)
                e_down = casted[1].reshape(depth * n_exp, de, d)
        else:
            x = _dil_layer(x, norm_mix[i], sc_m, sh_m, g_m, slopes, j, dil_w_in, dil_w_out)
        x = _moe_layer(x, norm_ffn[i], sc_f, sh_f, g_f, router_w, router_b, i, e_gate_up, e_down,
                       norm_final if i == depth - 1 else None)
    return x
```

```python
import functools

import jax
import jax.numpy as jnp
import numpy as np
from jax import lax
from jax.experimental import pallas as pl
from jax.experimental.pallas import tpu as pltpu

HEAD_DIM = 128
N_HEADS = 16
ATT_WIDTH = N_HEADS * HEAD_DIM
ALIBI_MAX_BIAS = 8.0

NSA_KV_HEADS = 4
NSA_GROUP = N_HEADS // NSA_KV_HEADS
NSA_KV_WIDTH = NSA_KV_HEADS * HEAD_DIM
CMP_BLOCK = 32
CMP_STRIDE = 16
SEL_BLOCK = 64
N_SEL = 16
NSA_WINDOW = 512
FORCE_SCORE = 1.0e6
NSA_QKV = ATT_WIDTH + 6 * NSA_KV_WIDTH
NSA_GATES = 3 * NSA_GROUP

DIL_PAIRS = ((128, 1), (512, 4), (2048, 16))
DIL_PROJ = ATT_WIDTH * (1 + 2 * len(DIL_PAIRS))

N_EXPERTS = 16
N_EXPERT_GROUPS = 4
EXPERTS_PER_GROUP = N_EXPERTS // N_EXPERT_GROUPS
TOP_K = 2
MOE_BLOCK = 256

RMS_EPS = 1e-6
NEG_INF = -1.0e30

LANES = 128
ATT_TILE = 128
ROW_BLOCK = 512
MOE_ROW_BLOCK = 256
VMEM_LIMIT = 56 * 1024 * 1024

_HI = lax.Precision.HIGHEST
_NT = (((1,), (1,)), ((), ()))


def _params(sem, vmem=VMEM_LIMIT):
    return pltpu.CompilerParams(dimension_semantics=sem, vmem_limit_bytes=vmem)


def _mod_kernel(c_ref, w_ref, b_ref, o_ref):
    c = c_ref[...]
    cond = c * jax.nn.sigmoid(c)
    o_ref[0] = jnp.dot(cond, w_ref[0], precision=_HI, preferred_element_type=jnp.float32) + b_ref[0]


def _modulation(c, ada_w, ada_b):
    depth, d, n = ada_w.shape
    b = c.shape[0]
    tn = 512
    return pl.pallas_call(
        _mod_kernel,
        grid=(depth, n // tn),
        in_specs=[pl.BlockSpec((b, d), lambda i, j: (0, 0)),
                  pl.BlockSpec((1, d, tn), lambda i, j: (i, 0, j)),
                  pl.BlockSpec((1, 1, tn), lambda i, j: (i, 0, j))],
        out_specs=pl.BlockSpec((1, b, tn), lambda i, j: (i, 0, j)),
        out_shape=jax.ShapeDtypeStruct((depth, b, n), jnp.float32),
        compiler_params=_params(("parallel", "parallel")),
        name="adaln_modulation",
    )(c, ada_w, ada_b.reshape(depth, 1, n))


def _rms(x, g):
    return x * lax.rsqrt(jnp.mean(x * x, axis=-1, keepdims=True) + RMS_EPS) * g


def _norm_mod_kernel(x_ref, g_ref, sc_ref, sh_ref, o_ref):
    h = _rms(x_ref[0], g_ref[...]) * (1.0 + sc_ref[0]) + sh_ref[0]
    o_ref[0] = h.astype(o_ref.dtype)


def _norm_mod(x, g, sc, sh, out_dtype):
    b, s, d = x.shape
    ts = min(ROW_BLOCK, s)
    return pl.pallas_call(
        _norm_mod_kernel,
        grid=(b, s // ts),
        in_specs=[pl.BlockSpec((1, ts, d), lambda i, j: (i, j, 0)),
                  pl.BlockSpec((1, d), lambda i, j: (0, 0)),
                  pl.BlockSpec((1, 1, d), lambda i, j: (i, 0, 0)),
                  pl.BlockSpec((1, 1, d), lambda i, j: (i, 0, 0))],
        out_specs=pl.BlockSpec((1, ts, d), lambda i, j: (i, j, 0)),
        out_shape=jax.ShapeDtypeStruct((b, s, d), out_dtype),
        compiler_params=_params(("parallel", "parallel")),
        name="rmsnorm_adaln",
    )(x, g.reshape(1, d), sc.reshape(b, 1, d), sh.reshape(b, 1, d))


def _store_streams(val, src_sc, out_refs, dilations):
    n = val.shape[0]
    for c in range(src_sc.shape[0]):
        src_sc[c] = val[:, c * LANES:(c + 1) * LANES]
    for r, ref in zip(dilations, out_refs):
        if r == 1:
            ref[0, 0] = val.astype(ref.dtype)
            continue
        for rho in range(r):
            for c in range(src_sc.shape[0]):
                ref[0, rho, :, c * LANES:(c + 1) * LANES] = (
                    src_sc[c, pl.ds(rho, n // r, stride=r), :].astype(ref.dtype))


def _norm_mod_streams_kernel(x_ref, g_ref, sc_ref, sh_ref, *refs, dilations):
    out_refs, h_sc = refs[:-1], refs[-1]
    h = _rms(x_ref[0], g_ref[...]) * (1.0 + sc_ref[0]) + sh_ref[0]
    _store_streams(h, h_sc, out_refs, dilations)


def _norm_mod_streams(x, g, sc, sh, dilations):
    b, s, d = x.shape
    ts = min(ROW_BLOCK, s)
    return pl.pallas_call(
        functools.partial(_norm_mod_streams_kernel, dilations=dilations),
        grid=(b, s // ts),
        in_specs=[pl.BlockSpec((1, ts, d), lambda i, j: (i, j, 0)),
                  pl.BlockSpec((1, d), lambda i, j: (0, 0)),
                  pl.BlockSpec((1, 1, d), lambda i, j: (i, 0, 0)),
                  pl.BlockSpec((1, 1, d), lambda i, j: (i, 0, 0))],
        out_specs=[pl.BlockSpec((1, r, ts // r, d), lambda i, j: (i, 0, j, 0)) for r in dilations],
        out_shape=[jax.ShapeDtypeStruct((b, r, s // r, d), jnp.bfloat16) for r in dilations],
        scratch_shapes=[pltpu.VMEM((d // LANES, ts, LANES), jnp.float32)],
        compiler_params=_params(("parallel", "parallel")),
        name="rmsnorm_adaln_streams",
    )(x, g.reshape(1, d), sc.reshape(b, 1, d), sh.reshape(b, 1, d))


def _pair_max(vals):
    out = None
    for a in range(len(vals)):
        for b in range(a + 1, len(vals)):
            s = vals[a] + vals[b]
            out = s if out is None else jnp.maximum(out, s)
    return out


def _store_row_major(ref, val):
    rows = val.shape[0]
    c_tiles = val.shape[1] // LANES
    for c in range(c_tiles):
        ref[pl.ds(c, rows, stride=c_tiles), :] = val[:, c * LANES:(c + 1) * LANES]


def _load_row_tiles(buf):
    n, c_tiles = buf.shape[0], buf.shape[1]
    return jnp.concatenate([buf[:, c].reshape(n * ROW_TILE, LANES) for c in range(c_tiles)], axis=1)


def _norm_route_kernel(x_ref, g_ref, sc_ref, sh_ref, rw_ref, rb_ref, h_ref, idx_ref, wt_ref):
    h = _rms(x_ref[0], g_ref[...]) * (1.0 + sc_ref[0]) + sh_ref[0]
    _store_row_major(h_ref, h)

    logits = lax.dot_general(rw_ref[...], h, _NT, precision=_HI, preferred_element_type=jnp.float32)
    scores = jax.nn.sigmoid(logits)
    biased = scores + rb_ref[...][:, 0:1]
    rows = [biased[e:e + 1, :] for e in range(N_EXPERTS)]
    srow = [scores[e:e + 1, :] for e in range(N_EXPERTS)]
    grp = [_pair_max(rows[q * EXPERTS_PER_GROUP:(q + 1) * EXPERTS_PER_GROUP]) for q in range(N_EXPERT_GROUPS)]
    best_v = grp[0]
    best = jnp.zeros(best_v.shape, jnp.int32)
    for q in range(1, N_EXPERT_GROUPS):
        take = grp[q] > best_v
        best = jnp.where(take, q, best)
        best_v = jnp.where(take, grp[q], best_v)
    v1 = jnp.full(best_v.shape, NEG_INF, jnp.float32)
    i1 = jnp.zeros(best_v.shape, jnp.int32)
    for e in range(N_EXPERTS):
        cand = jnp.where(best == e // EXPERTS_PER_GROUP, rows[e], NEG_INF)
        take = cand > v1
        i1 = jnp.where(take, e, i1)
        v1 = jnp.where(take, cand, v1)
    v2 = jnp.full(best_v.shape, NEG_INF, jnp.float32)
    i2 = jnp.zeros(best_v.shape, jnp.int32)
    for e in range(N_EXPERTS):
        cand = jnp.where(best == e // EXPERTS_PER_GROUP, jnp.where(i1 == e, NEG_INF, rows[e]), NEG_INF)
        take = cand > v2
        i2 = jnp.where(take, e, i2)
        v2 = jnp.where(take, cand, v2)
    w1 = jnp.zeros(best_v.shape, jnp.float32)
    w2 = jnp.zeros(best_v.shape, jnp.float32)
    for e in range(N_EXPERTS):
        w1 = jnp.where(i1 == e, srow[e], w1)
        w2 = jnp.where(i2 == e, srow[e], w2)
    tot = w1 + w2
    idx_ref[0] = jnp.concatenate([i1, i2], axis=0)
    wt_ref[0] = jnp.concatenate([w1 / tot, w2 / tot], axis=0)


def _norm_route(x, g, sc, sh, router_w, router_b):
    b, s, d = x.shape
    ts = min(MOE_ROW_BLOCK, s)
    ct = d // LANES
    rwt = router_w.T
    rb = jnp.broadcast_to(router_b.reshape(N_EXPERTS, 1), (N_EXPERTS, LANES))
    h, idx, wts = pl.pallas_call(
        _norm_route_kernel,
        grid=(b, s // ts),
        in_specs=[pl.BlockSpec((1, ts, d), lambda i, j: (i, j, 0)),
                  pl.BlockSpec((1, d), lambda i, j: (0, 0)),
                  pl.BlockSpec((1, 1, d), lambda i, j: (i, 0, 0)),
                  pl.BlockSpec((1, 1, d), lambda i, j: (i, 0, 0)),
                  pl.BlockSpec((N_EXPERTS, d), lambda i, j: (0, 0)),
                  pl.BlockSpec((N_EXPERTS, LANES), lambda i, j: (0, 0))],
        out_specs=[pl.BlockSpec((ts * ct, LANES), lambda i, j: (i * (s // ts) + j, 0)),
                   pl.BlockSpec((1, TOP_K, ts), lambda i, j: (i, 0, j)),
                   pl.BlockSpec((1, TOP_K, ts), lambda i, j: (i, 0, j))],
        out_shape=[jax.ShapeDtypeStruct((b * s * ct, LANES), jnp.float32),
                   jax.ShapeDtypeStruct((b, TOP_K, s), jnp.int32),
                   jax.ShapeDtypeStruct((b, TOP_K, s), jnp.float32)],
        compiler_params=_params(("parallel", "parallel")),
        name="rmsnorm_adaln_router",
    )(x, g.reshape(1, d), sc.reshape(b, 1, d), sh.reshape(b, 1, d), rwt, rb)
    return h.reshape(b * s, ct, LANES), idx, wts


def _weight_tile(w_ref, wbf_sc, first, scale, scale_tiles):
    @pl.when(first)
    def _():
        w = w_ref[0]
        if scale_tiles:
            w = w * jnp.where(pl.program_id(0) < scale_tiles, scale, 1.0)
        wbf_sc[...] = w.astype(jnp.bfloat16)
    return wbf_sc[...]


def _weight_spec(layer, col0, k, tn, rank):
    assert col0 % tn == 0
    if rank == 2:
        return pl.BlockSpec((1, k, tn), lambda j, i: (layer, 0, col0 // tn + j))
    return pl.BlockSpec((1, k, tn), lambda j, bi, i: (layer, 0, col0 // tn + j))


def _mm_kernel(a_ref, w_ref, o_ref, wbf_sc, *, act, scale, scale_tiles):
    w = _weight_tile(w_ref, wbf_sc, pl.program_id(1) == 0, scale, scale_tiles)
    acc = jnp.dot(a_ref[...], w, preferred_element_type=jnp.float32)
    if act == "sigmoid":
        acc = jax.nn.sigmoid(acc)
    o_ref[...] = acc.astype(o_ref.dtype)


def _matmul(a, w, layer, col0, n, out_dtype, act=None, scale=1.0, scale_cols=0, tm=2048, tn=1024):
    m, k = a.shape
    tn = min(tn, n)
    tm = min(tm, m)
    assert scale_cols % tn == 0
    return pl.pallas_call(
        functools.partial(_mm_kernel, act=act, scale=scale, scale_tiles=scale_cols // tn),
        grid=(n // tn, m // tm),
        in_specs=[pl.BlockSpec((tm, k), lambda j, i: (i, 0)),
                  _weight_spec(layer, col0, k, tn, 2)],
        out_specs=pl.BlockSpec((tm, tn), lambda j, i: (i, j)),
        out_shape=jax.ShapeDtypeStruct((m, n), out_dtype),
        scratch_shapes=[pltpu.VMEM((k, tn), jnp.bfloat16)],
        compiler_params=_params(("arbitrary", "arbitrary")),
        name="matmul" if act is None else "matmul_" + act,
    )(a, w)


def _nsa_proj_kernel(a_ref, w_ref, o_ref, ch_ref, wbf_sc, stage_sc, *, scale, scale_tiles, kv_tile, n_tiles):
    tile = (pl.program_id(0) + kv_tile + 1) % n_tiles

    @pl.when(pl.program_id(1) == 0)
    def _():
        w = w_ref[0]
        if scale_tiles:
            w = w * jnp.where(tile < scale_tiles, scale, 1.0)
        wbf_sc[...] = w.astype(jnp.bfloat16)

    acc = jnp.dot(a_ref[...], wbf_sc[...], preferred_element_type=jnp.float32)
    o_ref[...] = acc.astype(o_ref.dtype)

    @pl.when(pl.program_id(0) == n_tiles - 1)
    def _():
        nch = ch_ref.shape[3]
        for c in range(acc.shape[1] // HEAD_DIM):
            kv, g = divmod(c, NSA_KV_HEADS)
            stage_sc[...] = acc[:, c * HEAD_DIM:(c + 1) * HEAD_DIM]
            for l in range(CMP_STRIDE):
                ch_ref[0, kv, g, :, l * HEAD_DIM:(l + 1) * HEAD_DIM] = (
                    stage_sc[pl.ds(l, nch, stride=CMP_STRIDE), :].astype(ch_ref.dtype))


def _nsa_projection(a, w, layer, s, scale, tm=1024, tn=1024):
    m, k = a.shape
    b = m // s
    n_tiles = NSA_QKV // tn
    kv_tile = ATT_WIDTH // tn
    assert ATT_WIDTH % tn == 0 and 2 * NSA_KV_WIDTH == tn and s % tm == 0 and HEAD_DIM == LANES
    per_b = s // tm
    nch = tm // CMP_STRIDE

    def tile_of(j):
        return (j + kv_tile + 1) % n_tiles

    def chunk_block(j, i):
        last = j == n_tiles - 1
        return (jnp.where(last, i // per_b, 0), 0, 0, jnp.where(last, i % per_b, 0), 0)

    return pl.pallas_call(
        functools.partial(_nsa_proj_kernel, scale=scale, scale_tiles=ATT_WIDTH // tn, kv_tile=kv_tile,
                          n_tiles=n_tiles),
        grid=(n_tiles, m // tm),
        in_specs=[pl.BlockSpec((tm, k), lambda j, i: (i, 0)),
                  pl.BlockSpec((1, k, tn), lambda j, i: (layer, 0, tile_of(j)))],
        out_specs=[pl.BlockSpec((tm, tn), lambda j, i: (i, tile_of(j))),
                   pl.BlockSpec((1, 2, NSA_KV_HEADS, nch, CMP_STRIDE * HEAD_DIM), chunk_block)],
        out_shape=[jax.ShapeDtypeStruct((m, NSA_QKV), jnp.bfloat16),
                   jax.ShapeDtypeStruct((b, 2, NSA_KV_HEADS, s // CMP_STRIDE, CMP_STRIDE * HEAD_DIM), jnp.bfloat16)],
        scratch_shapes=[pltpu.VMEM((k, tn), jnp.bfloat16), pltpu.VMEM((tm, HEAD_DIM), jnp.float32)],
        compiler_params=_params(("arbitrary", "arbitrary")),
        name="matmul_nsa_proj",
    )(a, w)


def _first_row_step():
    return (pl.program_id(1) == 0) & (pl.program_id(2) == 0)


def _mm_streams_kernel(a_ref, w_ref, *refs, dilations, scale, scale_tiles):
    out_refs, acc_sc, wbf_sc = refs[:-2], refs[-2], refs[-1]
    w = _weight_tile(w_ref, wbf_sc, _first_row_step(), scale, scale_tiles)
    acc = jnp.dot(a_ref[0], w, preferred_element_type=jnp.float32)
    _store_streams(acc, acc_sc, out_refs, dilations)


def _matmul_streams(a, w, layer, col0, n, dilations, scale=1.0, tm=1024, tn=1024):
    b, s, k = a.shape
    tn = min(tn, n)
    tm = min(tm, s)
    return pl.pallas_call(
        functools.partial(_mm_streams_kernel, dilations=dilations, scale=scale,
                          scale_tiles=n // tn if scale != 1.0 else 0),
        grid=(n // tn, b, s // tm),
        in_specs=[pl.BlockSpec((1, tm, k), lambda j, bi, i: (bi, i, 0)),
                  _weight_spec(layer, col0, k, tn, 3)],
        out_specs=[pl.BlockSpec((1, r, tm // r, tn), lambda j, bi, i: (bi, 0, i, j)) for r in dilations],
        out_shape=[jax.ShapeDtypeStruct((b, r, s // r, n), jnp.bfloat16) for r in dilations],
        scratch_shapes=[pltpu.VMEM((tn // LANES, tm, LANES), jnp.float32), pltpu.VMEM((k, tn), jnp.bfloat16)],
        compiler_params=_params(("arbitrary", "arbitrary", "arbitrary")),
        name="matmul_streams",
    )(a, w)


def _mm_residual_kernel(a_ref, w_ref, x_ref, g_ref, o_ref, wbf_sc):
    w = _weight_tile(w_ref, wbf_sc, _first_row_step(), 1.0, 0)
    acc = jnp.dot(a_ref[0], w, preferred_element_type=jnp.float32)
    o_ref[0] = x_ref[0] + g_ref[0] * acc


def _matmul_residual(a, w, layer, x, gate, tm=1024, tn=1024):
    b, s, k = a.shape
    n = w.shape[2]
    tn = min(tn, n)
    tm = min(tm, s)
    return pl.pallas_call(
        _mm_residual_kernel,
        grid=(n // tn, b, s // tm),
        in_specs=[pl.BlockSpec((1, tm, k), lambda j, bi, i: (bi, i, 0)),
                  _weight_spec(layer, 0, k, tn, 3),
                  pl.BlockSpec((1, tm, tn), lambda j, bi, i: (bi, i, j)),
                  pl.BlockSpec((1, 1, tn), lambda j, bi, i: (bi, 0, j))],
        out_specs=pl.BlockSpec((1, tm, tn), lambda j, bi, i: (bi, i, j)),
        out_shape=jax.ShapeDtypeStruct((b, s, n), jnp.float32),
        scratch_shapes=[pltpu.VMEM((k, tn), jnp.bfloat16)],
        compiler_params=_params(("arbitrary", "arbitrary", "arbitrary")),
        name="matmul_gated_residual",
    )(a, w, x, gate.reshape(b, 1, n))


def _gelu_tanh(x):
    return 0.5 * x * (1.0 + jnp.tanh(0.7978845608028654 * (x + 0.044715 * (x * x * x))))


def _compress_kernel(c_ref, pa_ref, pb_ref, wa_ref, wb_ref, w2_ref, o_ref):
    c = c_ref[0, 0, 0].astype(jnp.float32)
    lo = (c + pa_ref[0]).astype(jnp.bfloat16)
    hi = (c + pb_ref[0]).astype(jnp.bfloat16)
    ha = jnp.dot(lo, wa_ref[0], preferred_element_type=jnp.float32)
    hb = jnp.dot(hi, wb_ref[0], preferred_element_type=jnp.float32)
    n = ha.shape[0]
    hid = _gelu_tanh(ha + pltpu.roll(hb, n - 1, 0))
    out = jnp.dot(hid.astype(jnp.bfloat16), w2_ref[0], preferred_element_type=jnp.float32)
    row = lax.broadcasted_iota(jnp.int32, out.shape, 0)
    o_ref[0, 0, 0] = jnp.where(row < n - 1, out, 0.0).astype(o_ref.dtype)


def _compress(kv_chunks, w_phi1, w_phi2, phi_pos):
    b, two, hkv, nch, width = kv_chunks.shape
    dh = HEAD_DIM
    half = CMP_BLOCK // 2
    w1 = w_phi1.reshape(2, 2, half * dh, dh).astype(jnp.bfloat16)
    pos = phi_pos.reshape(2, 2, 1, half * dh)
    return pl.pallas_call(
        _compress_kernel,
        grid=(b, two, hkv),
        in_specs=[pl.BlockSpec((1, 1, 1, nch, width), lambda i, j, g: (i, j, g, 0, 0)),
                  pl.BlockSpec((1, 1, width), lambda i, j, g: (j, 0, 0)),
                  pl.BlockSpec((1, 1, width), lambda i, j, g: (j, 0, 0)),
                  pl.BlockSpec((1, width, dh), lambda i, j, g: (j, 0, 0)),
                  pl.BlockSpec((1, width, dh), lambda i, j, g: (j, 0, 0)),
                  pl.BlockSpec((1, dh, dh), lambda i, j, g: (j, 0, 0))],
        out_specs=pl.BlockSpec((1, 1, 1, nch, dh), lambda i, j, g: (i, j, g, 0, 0)),
        out_shape=jax.ShapeDtypeStruct((b, two, hkv, nch, dh), jnp.bfloat16),
        compiler_params=_params(("parallel", "parallel", "parallel")),
        name="nsa_compress",
    )(kv_chunks, pos[:, 0], pos[:, 1], w1[:, 0], w1[:, 1], w_phi2.astype(jnp.bfloat16))


MASK_BIG = 2.0 ** 100
AUX_SLOPE = 64
AUX_PAD = 70
SEL_CHUNK = 256
LOG2E = 1.4426950408889634


def _nsa_attn_kernel(sl_ref, q_ref, kc_ref, vc_ref, ks_ref, vs_ref, kw_ref, vw_ref, gt_ref, ovl_ref, qtab_ref,
                     kauxs_ref, kauxw_ref, *refs, n_cmp, n_blk, n_sel, cast_place, n_cast_dst):
    n_cast = len(cast_place)
    cast_src, refs = refs[:n_cast], refs[n_cast:]
    o_ref, refs = refs[0], refs[1:]
    cast_dst, refs = refs[:n_cast_dst], refs[n_cast_dst:]
    (ocmp_sc, maskq_sc, owin_sc, qaug_sc, qaugw_sc, p_sc, pw_sc, kaug_s, vaug_s, kaug_w, vaug_w), refs = refs[:11], refs[11:]
    cast_in, cast_out, (cast_in_sem, cast_out_sem) = refs[:n_cast], refs[n_cast:2 * n_cast], refs[2 * n_cast:]
    t = ATT_TILE
    dh = HEAD_DIM
    s_len = ks_ref.shape[1]
    wpad = NSA_WINDOW
    g = pl.program_id(1)
    qt = pl.program_id(2)
    qstart = pl.multiple_of(qt * t, t)

    if n_cast:
        step = (pl.program_id(0) * pl.num_programs(1) + g) * pl.num_programs(2) + qt
        n_steps = pl.num_programs(0) * pl.num_programs(1) * pl.num_programs(2)
        drain_cast = _cast_side_job(step, n_steps, cast_src, cast_dst, cast_place, cast_in, cast_out, cast_in_sem,
                                    cast_out_sem)

    @pl.when(qt == 0)
    def _():
        ones = jnp.ones((s_len, dh), jnp.bfloat16)
        kaug_s[:, :dh] = ks_ref[0]
        kaug_s[:, dh:] = kauxs_ref[...]
        vaug_s[:, :dh] = vs_ref[0]
        vaug_s[:, dh:] = ones
        kaug_w[:wpad, :dh] = jnp.zeros((wpad, dh), jnp.bfloat16)
        kaug_w[wpad:, :dh] = kw_ref[0]
        kaug_w[:, dh:] = kauxw_ref[...]
        vaug_w[:wpad, :dh] = jnp.zeros((wpad, dh), jnp.bfloat16)
        vaug_w[wpad:, :dh] = vw_ref[0]
        vaug_w[:wpad, dh:] = jnp.ones((wpad, dh), jnp.bfloat16)
        vaug_w[wpad:, dh:] = ones

    slopes = [sl_ref[g * NSA_GROUP + r] for r in range(NSA_GROUP)]
    row = lax.broadcasted_iota(jnp.int32, (t, LANES), 0)
    col = lax.broadcasted_iota(jnp.int32, (t, LANES), 1)
    prows = [slice(pair * 2 * t, (pair + 1) * 2 * t) for pair in range(NSA_GROUP // 2)]
    qtab = qtab_ref[0]

    def compress_and_select(qv, tile, par):
        qs = tile * t
        tpos = qs + row
        q4 = jnp.concatenate([qv[:, r * dh:(r + 1) * dh] for r in range(NSA_GROUP)], axis=0)
        kc = kc_ref[0, 0, 0]
        vc = vc_ref[0, 0, 0]
        s = lax.dot_general(q4, kc, _NT, preferred_element_type=jnp.float32)
        visible = (col * CMP_STRIDE + (CMP_BLOCK - 1) <= tpos) & (col < n_cmp)
        dist_c = tpos.astype(jnp.float32) - (col.astype(jnp.float32) * CMP_STRIDE + (CMP_BLOCK - 1) / 2.0)
        psum = jnp.zeros((t, LANES), jnp.float32)
        for r in range(NSA_GROUP):
            rows = slice(r * t, (r + 1) * t)
            sr = jnp.where(visible, s[rows] - slopes[r] * dist_c, NEG_INF)
            e = jnp.where(visible, jnp.exp2(sr - jnp.max(sr, axis=-1, keepdims=True)), 0.0)
            p = e / jnp.maximum(jnp.sum(e, axis=-1, keepdims=True), 1e-30)
            psum = psum + p
            ocmp_sc[par, rows] = jnp.dot(p.astype(vc.dtype), vc, preferred_element_type=jnp.float32)
        nb = -(-n_blk // 8) * 8
        imp_t = lax.dot_general(ovl_ref[...], psum, _NT, precision=_HI, preferred_element_type=jnp.float32)[:nb]
        jrow = lax.broadcasted_iota(jnp.int32, (nb, t), 0)
        tpos_t = qs + lax.broadcasted_iota(jnp.int32, (nb, t), 1)
        cur = jnp.right_shift(tpos_t, int(np.log2(SEL_BLOCK)))
        forced = (jrow == 0) | (jrow == cur) | (jrow == cur - 1)
        score = jnp.where(forced, FORCE_SCORE, jnp.where(jrow * SEL_BLOCK <= tpos_t, imp_t, -1.0))
        rank = jnp.zeros((nb, t), jnp.float32)
        for j in range(n_blk):
            cj = score[j:j + 1, :]
            ahead = (cj > score) | ((cj == score) & (jrow > j))
            rank = rank + jnp.where(ahead, 1.0, 0.0)
        mask_t = jnp.where((rank < n_sel) & (jrow < n_blk), 0.0, -MASK_BIG)
        maskq_sc[par] = jnp.concatenate([mask_t, jnp.zeros((LANES - nb, t), jnp.float32)], axis=0).T

    par = 0
    q = q_ref[0]

    for r in range(NSA_GROUP):
        rows = slice(r * t, (r + 1) * t)
        qaugw_sc[rows, :dh] = q[:, r * dh:(r + 1) * dh]
        qaugw_sc[rows, dh:] = jnp.broadcast_to(qtab[r:r + 1, :], (t, LANES)).astype(jnp.bfloat16)
    wk = wpad + t
    nw = wk // LANES
    kw = kaug_w[pl.ds(qstart, wk), :]
    vw = vaug_w[pl.ds(qstart, wk), :]
    sws = [lax.dot_general(qaugw_sc[prow], kw, _NT, preferred_element_type=jnp.float32) for prow in prows]

    compress_and_select(q, qt, par)

    mask_q = maskq_sc[par]
    for r in range(NSA_GROUP):
        rows = slice(r * t, (r + 1) * t)
        qaug_sc[rows, :dh] = q[:, r * dh:(r + 1) * dh]
        qaug_sc[rows, dh:] = (mask_q + qtab[r:r + 1, :]).astype(jnp.bfloat16)

    for pair, prow in enumerate(prows):
        sw = sws[pair]
        for r2 in range(2):
            rows = slice((2 * pair + r2) * t, (2 * pair + r2 + 1) * t)
            tiles = [sw[r2 * t:(r2 + 1) * t, j * LANES:(j + 1) * LANES] for j in range(nw)]
            tiles[0] = jnp.where(col > row, tiles[0], NEG_INF)
            tiles[-1] = jnp.where(col <= row, tiles[-1], NEG_INF)
            mx = functools.reduce(jnp.maximum, tiles)
            m = jnp.broadcast_to(jnp.max(mx, axis=-1, keepdims=True), (t, LANES))
            for j in range(nw):
                pw_sc[rows, j * LANES:(j + 1) * LANES] = jnp.exp2(tiles[j] - m).astype(jnp.bfloat16)
        ow = jnp.dot(pw_sc[prow], vw, preferred_element_type=jnp.float32)
        owin_sc[prow] = ow[:, :dh] / ow[:, dh:]

    ch = SEL_CHUNK
    nl = ch // LANES
    n_full = qt // (ch // t)
    gt = gt_ref[0]
    cmr = col - row

    def selected(k):
        nk = (k + 1) * ch
        kk = kaug_s[:nk, :]
        vv = vaug_s[:nk, :]
        sks = [lax.dot_general(qaug_sc[prow], kk, _NT, preferred_element_type=jnp.float32) for prow in prows]
        for pair, prow in enumerate(prows):
            sk = sks[pair]
            for r2 in range(2):
                rows = slice((2 * pair + r2) * t, (2 * pair + r2 + 1) * t)
                tiles = [sk[r2 * t:(r2 + 1) * t, j * LANES:(j + 1) * LANES] for j in range(nk // LANES)]
                for j in range(k * nl, (k + 1) * nl):
                    tiles[j] = jnp.where(cmr <= qstart - j * LANES, tiles[j], NEG_INF)
                mx = functools.reduce(jnp.maximum, tiles)
                m = jnp.broadcast_to(jnp.max(mx, axis=-1, keepdims=True), (t, LANES))
                for j in range(nk // LANES):
                    p_sc[rows, j * LANES:(j + 1) * LANES] = jnp.exp2(tiles[j] - m).astype(jnp.bfloat16)
            pv = jnp.dot(p_sc[prow, :nk], vv, preferred_element_type=jnp.float32)
            for r2 in range(2):
                r = 2 * pair + r2
                rows = slice(r * t, (r + 1) * t)
                orow = slice(r2 * t, (r2 + 1) * t)
                o = (gt[:, 3 * r:3 * r + 1] * ocmp_sc[par, rows]
                     + gt[:, 3 * r + 1:3 * r + 2] * (pv[orow, :dh] / pv[orow, dh:])
                     + gt[:, 3 * r + 2:3 * r + 3] * owin_sc[rows])
                o_ref[0, :, r * dh:(r + 1) * dh] = o.astype(o_ref.dtype)

    for k in range(s_len // ch):
        pl.when(n_full == k)(functools.partial(selected, k))
    if cast_place:
        drain_cast()


def _cast_chunks(shape, n_steps):
    r, c = shape
    for n_c in (1, 2, 4, 8, 16):
        if n_steps % n_c == 0 and c % (n_c * LANES) == 0 and r % (n_steps // n_c) == 0:
            rows = r // (n_steps // n_c)
            if rows % 16 == 0:
                return rows, c // n_c
    raise ValueError("no tile-aligned split of %s into %d chunks" % (shape, n_steps))


def _cast_side_job(n, n_steps, srcs, dsts, place, inbufs, outbufs, in_sem, out_sem):
    slot = n % 2

    def chunk(k, step, col0=0):
        rows, cols = inbufs[k].shape[1:]
        n_c = srcs[k].shape[1] // cols
        return (pl.ds(pl.multiple_of((step // n_c) * rows, 16), rows),
                pl.ds(pl.multiple_of(col0 + (step % n_c) * cols, LANES), cols))

    def in_copy(k, step, sl):
        return pltpu.make_async_copy(srcs[k].at[chunk(k, step)], inbufs[k].at[sl], in_sem.at[sl])

    def out_copy(k, step, sl):
        dst, col0 = place[k]
        return pltpu.make_async_copy(outbufs[k].at[sl], dsts[dst].at[chunk(k, step, col0)], out_sem.at[sl])

    ks = range(len(srcs))

    @pl.when(n == 0)
    def _():
        for k in ks:
            in_copy(k, 0, 0).start()
            outbufs[k][...] = jnp.zeros(outbufs[k].shape, outbufs[k].dtype)
            out_copy(k, 0, 0).start()
            out_copy(k, 1, 1).start()

    for k in ks:
        in_copy(k, n, slot).wait()
    nxt = jnp.minimum(n + 1, n_steps - 1)
    for k in ks:
        in_copy(k, nxt, 1 - slot).start()
    for k in ks:
        out_copy(k, n, slot).wait()

    for k in ks:
        outbufs[k][slot] = inbufs[k][slot].astype(outbufs[k].dtype)
    for k in ks:
        out_copy(k, n, slot).start()

    def drain():
        @pl.when(n == n_steps - 1)
        def _():
            for k in ks:
                in_copy(k, n, 1 - slot).wait()
                out_copy(k, n, 1 - slot).wait()
                out_copy(k, n, slot).wait()

    return drain


def _bf16_pieces(x):
    a0 = x.astype(jnp.bfloat16).astype(jnp.float32)
    a1 = (x - a0).astype(jnp.bfloat16).astype(jnp.float32)
    a2 = (x - a0 - a1).astype(jnp.bfloat16).astype(jnp.float32)
    return [a0, a1, a2]


def _nsa_tables(s, slopes2):
    n_cmp = s // CMP_STRIDE - CMP_BLOCK // CMP_STRIDE + 1
    n_blk = s // SEL_BLOCK
    assert n_cmp < LANES and n_blk <= AUX_SLOPE
    bj = np.arange(LANES)[:, None]
    cn = np.arange(LANES)[None, :]
    ovl_t = np.clip(np.minimum(cn * CMP_STRIDE + CMP_BLOCK, (bj + 1) * SEL_BLOCK)
                    - np.maximum(cn * CMP_STRIDE, bj * SEL_BLOCK), 0, None) / CMP_STRIDE
    ovl_t = np.where((cn < n_cmp) & (bj < n_blk), ovl_t, 0.0).astype(np.float32)
    pos = np.arange(s)
    kaux = np.zeros((s, LANES), np.float32)
    kaux[pos, pos // SEL_BLOCK] = 1.0
    kaux[:, AUX_SLOPE:AUX_SLOPE + 3] = (pos // LANES * LANES)[:, None]
    kaux[:, AUX_SLOPE + 3:AUX_SLOPE + 6] = (pos % LANES)[:, None]
    kaux_w = np.zeros((NSA_WINDOW + s, LANES), np.float32)
    kaux_w[NSA_WINDOW:, AUX_SLOPE:AUX_SLOPE + 6] = kaux[:, AUX_SLOPE:AUX_SLOPE + 6]
    kaux_w[:NSA_WINDOW, AUX_PAD] = 1.0
    pieces = jnp.stack(_bf16_pieces(slopes2) * 2, axis=-1)
    qtab = jnp.zeros((N_HEADS, LANES), jnp.float32)
    qtab = qtab.at[:, AUX_SLOPE:AUX_SLOPE + 6].set(pieces).at[:, AUX_PAD].set(-MASK_BIG)
    qtab = jnp.pad(qtab.reshape(NSA_KV_HEADS, NSA_GROUP, LANES), ((0, 0), (0, 8 - NSA_GROUP), (0, 0)))
    return (n_cmp, n_blk, jnp.asarray(ovl_t), qtab, jnp.asarray(kaux, dtype=jnp.bfloat16),
            jnp.asarray(kaux_w, dtype=jnp.bfloat16))


def _nsa_attention(proj, cmp_kv, gates, slopes2, cast_ws=(), cast_place=(), cast_shapes=()):
    b, s, _ = proj.shape
    t = ATT_TILE
    n_steps = b * NSA_KV_HEADS * (s // t)
    chunks = [_cast_chunks(w.shape, n_steps) for w in cast_ws]
    assert n_steps >= 2
    any_spec = pl.BlockSpec(memory_space=pl.ANY)
    n_cmp, n_blk, ovl_t, qtab, kaux_s, kaux_w = _nsa_tables(s, slopes2)
    assert cmp_kv.shape[3] == LANES and s % SEL_CHUNK == 0 and NSA_WINDOW % t == 0
    qw = NSA_GROUP * HEAD_DIM
    rows = NSA_GROUP * t
    kv0 = ATT_WIDTH // HEAD_DIM

    def kv_spec(j):
        return pl.BlockSpec((1, s, HEAD_DIM), lambda bi, g, qi, j=j: (bi, 0, kv0 + j * NSA_KV_HEADS + g))

    def const(shape):
        return pl.BlockSpec(shape, lambda bi, g, qi: (0,) * len(shape))

    outs = pl.pallas_call(
        functools.partial(_nsa_attn_kernel, n_cmp=n_cmp, n_blk=n_blk, n_sel=min(N_SEL, n_blk),
                          cast_place=tuple(cast_place), n_cast_dst=len(cast_shapes)),
        grid=(b, NSA_KV_HEADS, s // t),
        in_specs=[pl.BlockSpec(memory_space=pltpu.SMEM),
                  pl.BlockSpec((1, t, qw), lambda bi, g, qi: (bi, qi, g)),
                  pl.BlockSpec((1, 1, 1, LANES, HEAD_DIM), lambda bi, g, qi: (bi, 0, g, 0, 0)),
                  pl.BlockSpec((1, 1, 1, LANES, HEAD_DIM), lambda bi, g, qi: (bi, 1, g, 0, 0)),
                  kv_spec(2), kv_spec(3), kv_spec(4), kv_spec(5),
                  pl.BlockSpec((1, t, LANES), lambda bi, g, qi: (bi, qi, g)),
                  const((LANES, LANES)),
                  pl.BlockSpec((1, 8, LANES), lambda bi, g, qi: (g, 0, 0)),
                  const((s, LANES)), const((NSA_WINDOW + s, LANES))] + [any_spec] * len(cast_ws),
        out_specs=[pl.BlockSpec((1, t, qw), lambda bi, g, qi: (bi, qi, g))] + [any_spec] * len(cast_shapes),
        out_shape=[jax.ShapeDtypeStruct((b, s, ATT_WIDTH), jnp.bfloat16)]
        + [jax.ShapeDtypeStruct(shape, jnp.bfloat16) for shape in cast_shapes],
        scratch_shapes=[pltpu.VMEM((1, rows, HEAD_DIM), jnp.float32),
                        pltpu.VMEM((1, t, LANES), jnp.float32),
                        pltpu.VMEM((rows, HEAD_DIM), jnp.float32),
                        pltpu.VMEM((rows, 2 * HEAD_DIM), jnp.bfloat16),
                        pltpu.VMEM((rows, 2 * HEAD_DIM), jnp.bfloat16),
                        pltpu.VMEM((rows, s), jnp.bfloat16),
                        pltpu.VMEM((rows, NSA_WINDOW + t), jnp.bfloat16),
                        pltpu.VMEM((s, 2 * HEAD_DIM), jnp.bfloat16),
                        pltpu.VMEM((s, 2 * HEAD_DIM), jnp.bfloat16),
                        pltpu.VMEM((NSA_WINDOW + s, 2 * HEAD_DIM), jnp.bfloat16),
                        pltpu.VMEM((NSA_WINDOW + s, 2 * HEAD_DIM), jnp.bfloat16)]
        + [pltpu.VMEM((2,) + c, jnp.float32) for c in chunks]
        + [pltpu.VMEM((2,) + c, jnp.bfloat16) for c in chunks]
        + [pltpu.SemaphoreType.DMA((2,)), pltpu.SemaphoreType.DMA((2,))],
        compiler_params=_params(("arbitrary", "arbitrary", "arbitrary")),
        name="nsa_attention",
    )(slopes2, proj, cmp_kv, cmp_kv, proj, proj, proj, proj, gates, ovl_t, qtab, kaux_s, kaux_w, *cast_ws)
    return outs[0], list(outs[1:])


def _dil_attn_kernel(sl_ref, q_ref, *refs, window, has_prev):
    if has_prev:
        kp_ref, kc_ref, vp_ref, vc_ref, o_ref, lse_ref, bias_sc = refs
    else:
        kc_ref, vc_ref, o_ref, lse_ref, bias_sc = refs
        kp_ref, vp_ref = kc_ref, vc_ref
    t = ATT_TILE
    i = pl.program_id(2)

    @pl.when((pl.program_id(0) == 0) & (pl.program_id(1) == 0) & (i == 0))
    def _():
        row = lax.broadcasted_iota(jnp.int32, (t, 2 * t), 0)
        col = lax.broadcasted_iota(jnp.int32, (t, 2 * t), 1)
        dist = row - col + t
        band = (dist >= 0) & (dist <= window)
        distf = dist.astype(jnp.float32)
        for h in range(N_HEADS):
            bias = sl_ref[h] * distf
            bias_sc[0, h] = jnp.where(band & (col >= t), bias, -NEG_INF)
            bias_sc[1, h] = jnp.where(band, bias, -NEG_INF)

    var = jnp.minimum(i, 1)
    lane = lax.broadcasted_iota(jnp.int32, (t, LANES), 1)
    lse_all = jnp.zeros((t, LANES), jnp.float32)
    for h in range(N_HEADS):
        hs = slice(h * HEAD_DIM, (h + 1) * HEAD_DIM)
        q = q_ref[0, 0, :, hs]
        k = jnp.concatenate([kp_ref[0, 0, :, hs], kc_ref[0, 0, :, hs]], axis=0)
        v = jnp.concatenate([vp_ref[0, 0, :, hs], vc_ref[0, 0, :, hs]], axis=0)
        s = lax.dot_general(q, k, _NT, preferred_element_type=jnp.float32) - bias_sc[var, h]
        m = jnp.max(s, axis=-1, keepdims=True)
        e = jnp.exp2(s - m)
        l = jnp.sum(e, axis=-1, keepdims=True)
        o = jnp.dot(e.astype(v.dtype), v, preferred_element_type=jnp.float32) / l
        o_ref[0, 0, :, hs] = o.astype(o_ref.dtype)
        lse_all = jnp.where(lane == h, m + jnp.log2(l), lse_all)
    lse_ref[0, 0] = lse_all


def _dil_attention(q, kv, slopes2, win, r):
    b, _, ln, _ = q.shape
    t = ATT_TILE

    def cur(c):
        return pl.BlockSpec((1, 1, t, ATT_WIDTH), lambda bi, rho, i, c=c: (bi, rho, i, c))

    def prev(c):
        return pl.BlockSpec((1, 1, t, ATT_WIDTH), lambda bi, rho, i, c=c: (bi, rho, jnp.maximum(i - 1, 0), c))

    has_prev = ln > t
    kv_specs = [prev(0), cur(0), prev(1), cur(1)] if has_prev else [cur(0), cur(1)]
    return pl.pallas_call(
        functools.partial(_dil_attn_kernel, window=win // r, has_prev=has_prev),
        grid=(b, r, ln // t),
        in_specs=[pl.BlockSpec(memory_space=pltpu.SMEM), cur(0)] + kv_specs,
        out_specs=[pl.BlockSpec((1, 1, t, ATT_WIDTH), lambda bi, rho, i: (bi, rho, i, 0)),
                   pl.BlockSpec((1, 1, t, LANES), lambda bi, rho, i: (bi, rho, i, 0))],
        out_shape=[jax.ShapeDtypeStruct((b, r, ln, ATT_WIDTH), jnp.bfloat16),
                   jax.ShapeDtypeStruct((b, r, ln, LANES), jnp.float32)],
        scratch_shapes=[pltpu.VMEM((2, N_HEADS, t, 2 * t), jnp.float32)],
        compiler_params=_params(("arbitrary", "arbitrary", "arbitrary")),
        name="dilated_attention_r%d" % r,
    )(slopes2 * r, q, *([kv] * len(kv_specs)))


def _dil_merge_kernel(*refs, dilations):
    ng = len(dilations)
    o_refs, l_refs, out_ref, o_sc, l_sc = refs[:ng], refs[ng:2 * ng], refs[2 * ng], refs[2 * ng + 1], refs[2 * ng + 2]
    ts = out_ref.shape[1]
    ls = []
    for gi, r in enumerate(dilations):
        if r == 1:
            ls.append(l_refs[gi][0, 0])
            continue
        for rho in range(r):
            l_sc[gi, pl.ds(rho, ts // r, stride=r), :] = l_refs[gi][0, rho]
            for h in range(N_HEADS):
                o_sc[gi, h, pl.ds(rho, ts // r, stride=r), :] = (
                    o_refs[gi][0, rho, :, h * HEAD_DIM:(h + 1) * HEAD_DIM].astype(jnp.float32))
        ls.append(l_sc[gi])
    m = functools.reduce(jnp.maximum, ls)
    es = [jnp.exp2(l - m) for l in ls]
    den = functools.reduce(lambda a, b: a + b, es)
    ws = [e / den for e in es]
    for h in range(N_HEADS):
        hs = slice(h * HEAD_DIM, (h + 1) * HEAD_DIM)
        acc = None
        for gi, r in enumerate(dilations):
            og = o_refs[gi][0, 0, :, hs].astype(jnp.float32) if r == 1 else o_sc[gi, h]
            term = ws[gi][:, h:h + 1] * og
            acc = term if acc is None else acc + term
        out_ref[0, :, hs] = acc.astype(out_ref.dtype)


def _dil_merge(outs, lses, dilations):
    b, _, _, w = outs[0].shape
    s = outs[0].shape[1] * outs[0].shape[2]
    ts = min(ROW_BLOCK, s)
    ng = len(dilations)

    def spec(r, width):
        return pl.BlockSpec((1, r, ts // r, width), lambda i, j: (i, 0, j, 0))

    return pl.pallas_call(
        functools.partial(_dil_merge_kernel, dilations=dilations),
        grid=(b, s // ts),
        in_specs=[spec(r, w) for r in dilations] + [spec(r, LANES) for r in dilations],
        out_specs=pl.BlockSpec((1, ts, w), lambda i, j: (i, j, 0)),
        out_shape=jax.ShapeDtypeStruct((b, s, w), jnp.bfloat16),
        scratch_shapes=[pltpu.VMEM((ng, w // HEAD_DIM, ts, HEAD_DIM), jnp.float32),
                        pltpu.VMEM((ng, ts, LANES), jnp.float32)],
        compiler_params=_params(("parallel", "parallel")),
        name="dilated_merge",
    )(*outs, *lses)


ROW_TILE = 8


def _gather_rows(idx_ref, src_hbm, dst, sem):
    def body(i, carry):
        for j in range(ROW_TILE):
            pltpu.make_async_copy(src_hbm.at[idx_ref[0, 0, i * ROW_TILE + j]], dst.at[i, :, j], sem).start()
        return carry
    lax.fori_loop(0, dst.shape[0], body, 0)


def _gather_tiled_rows(idx_ref, src_hbm, dst, sem):
    def body(i, carry):
        for j in range(ROW_TILE):
            pltpu.make_async_copy(src_hbm.at[pl.ds(idx_ref[0, 0, i * ROW_TILE + j], 1)],
                                  dst.at[i, pl.ds(j, 1)], sem).start()
        return carry
    lax.fori_loop(0, dst.shape[0], body, 0)


def _wait_rows(dst, sem):
    pltpu.make_async_copy(dst, dst, sem).wait()


def _expert_kernel(be_ref, nu_ref, tokc_ref, tokn_ref, h_hbm, wgu_ref, wd_ref, o_ref, xbuf, sem):
    del be_ref
    de = wd_ref.shape[1]
    blk = pl.program_id(0)
    n_used = nu_ref[0]
    slot = blk % 2

    @pl.when(blk == 0)
    def _():
        _gather_rows(tokc_ref, h_hbm, xbuf.at[0], sem.at[0])

    @pl.when(blk + 1 < n_used)
    def _():
        _gather_rows(tokn_ref, h_hbm, xbuf.at[1 - slot], sem.at[1 - slot])

    @pl.when(blk < n_used)
    def _():
        _wait_rows(xbuf.at[slot], sem.at[slot])
        x = _load_row_tiles(xbuf.at[slot]).astype(jnp.bfloat16)
        gate = jnp.dot(x, wgu_ref[0, :, :de], preferred_element_type=jnp.float32)
        up = jnp.dot(x, wgu_ref[0, :, de:], preferred_element_type=jnp.float32)
        hid = (gate * jax.nn.sigmoid(gate) * up).astype(jnp.bfloat16)
        o_ref[...] = jnp.dot(hid, wd_ref[0], preferred_element_type=jnp.float32)

    @pl.when(blk >= n_used)
    def _():
        o_ref[...] = jnp.zeros(o_ref.shape, o_ref.dtype)


def _expert_blocks(h3, slot_tok, blk_e, n_used, w_gate_up, w_down):
    t, ct, _ = h3.shape
    d = ct * LANES
    nblk = blk_e.shape[0]
    de = w_down.shape[1]
    assert de % LANES == 0
    tok = slot_tok.reshape(nblk, 1, MOE_BLOCK)
    grid_spec = pltpu.PrefetchScalarGridSpec(
        num_scalar_prefetch=2,
        grid=(nblk,),
        in_specs=[pl.BlockSpec((1, 1, MOE_BLOCK), lambda i, be, nu: (i, 0, 0), memory_space=pltpu.SMEM),
                  pl.BlockSpec((1, 1, MOE_BLOCK), lambda i, be, nu: (jnp.minimum(i + 1, nblk - 1), 0, 0),
                               memory_space=pltpu.SMEM),
                  pl.BlockSpec(memory_space=pl.ANY),
                  pl.BlockSpec((1, d, 2 * de), lambda i, be, nu: (be[i], 0, 0)),
                  pl.BlockSpec((1, de, d), lambda i, be, nu: (be[i], 0, 0))],
        out_specs=pl.BlockSpec((MOE_BLOCK, d), lambda i, be, nu: (i, 0)),
        scratch_shapes=[pltpu.VMEM((2, MOE_BLOCK // ROW_TILE, ct, ROW_TILE, LANES), jnp.float32),
                        pltpu.SemaphoreType.DMA((2,))],
    )
    return pl.pallas_call(
        _expert_kernel,
        grid_spec=grid_spec,
        out_shape=jax.ShapeDtypeStruct((nblk * MOE_BLOCK, d), jnp.float32),
        compiler_params=_params(("arbitrary",)),
        name="moe_expert_blocks",
    )(blk_e, n_used, tok, tok, h3, w_gate_up, w_down)


def _combine_kernel(d0c_ref, d1c_ref, d0n_ref, d1n_ref, x_ref, w0_ref, w1_ref, g_ref, gn_ref, yo_hbm, o_ref,
                    buf, sem, *, out_norm):
    i = pl.program_id(0)
    n = pl.num_programs(0)
    slot = i % 2
    tt = x_ref.shape[0]

    @pl.when(i == 0)
    def _():
        _gather_tiled_rows(d0c_ref, yo_hbm, buf.at[0, 0], sem.at[0])
        _gather_tiled_rows(d1c_ref, yo_hbm, buf.at[0, 1], sem.at[0])

    @pl.when(i + 1 < n)
    def _():
        _gather_tiled_rows(d0n_ref, yo_hbm, buf.at[1 - slot, 0], sem.at[1 - slot])
        _gather_tiled_rows(d1n_ref, yo_hbm, buf.at[1 - slot, 1], sem.at[1 - slot])

    _wait_rows(buf.at[slot, 0], sem.at[slot])
    _wait_rows(buf.at[slot, 1], sem.at[slot])
    y = (w0_ref[...][:, 0:1] * buf[slot, 0].reshape(x_ref.shape)
         + w1_ref[...][:, 0:1] * buf[slot, 1].reshape(x_ref.shape))
    out = x_ref[...] + g_ref[0] * y
    o_ref[...] = _rms(out, gn_ref[...]) if out_norm else out


def _moe_combine(x2d, yo, dest, wts, gate, s, out_norm_g=None):
    t, d = x2d.shape
    out_norm = out_norm_g is not None
    gn = (out_norm_g if out_norm else jnp.ones((d,), jnp.float32)).reshape(1, d)
    tt = min(MOE_ROW_BLOCK, s)
    nt = t // tt
    d0 = dest[:, 0].reshape(nt, 1, tt)
    d1 = dest[:, 1].reshape(nt, 1, tt)
    w0 = jnp.broadcast_to(wts[:, 0:1], (t, LANES))
    w1 = jnp.broadcast_to(wts[:, 1:2], (t, LANES))
    b = gate.shape[0]
    cur = pl.BlockSpec((1, 1, tt), lambda i: (i, 0, 0), memory_space=pltpu.SMEM)
    nxt = pl.BlockSpec((1, 1, tt), lambda i: (jnp.minimum(i + 1, nt - 1), 0, 0), memory_space=pltpu.SMEM)
    return pl.pallas_call(
        functools.partial(_combine_kernel, out_norm=out_norm),
        grid=(nt,),
        in_specs=[cur, cur, nxt, nxt,
                  pl.BlockSpec((tt, d), lambda i: (i, 0)),
                  pl.BlockSpec((tt, LANES), lambda i: (i, 0)),
                  pl.BlockSpec((tt, LANES), lambda i: (i, 0)),
                  pl.BlockSpec((1, 1, d), lambda i: (i * tt // s, 0, 0)),
                  pl.BlockSpec((1, d), lambda i: (0, 0)),
                  pl.BlockSpec(memory_space=pl.ANY)],
        out_specs=pl.BlockSpec((tt, d), lambda i: (i, 0)),
        out_shape=jax.ShapeDtypeStruct((t, d), jnp.float32),
        scratch_shapes=[pltpu.VMEM((2, 2, tt // ROW_TILE, ROW_TILE, d), jnp.float32),
                        pltpu.SemaphoreType.DMA((2,))],
        compiler_params=_params(("arbitrary",)),
        name="moe_combine_residual",
    )(d0, d1, d0, d1, x2d, w0, w1, gate.reshape(b, 1, d), gn, yo)


def _dispatch_tables(idx):
    t = idx.shape[0]
    a = t * TOP_K
    flat_e = idx.reshape(a)
    onehot = (flat_e[:, None] == jnp.arange(N_EXPERTS, dtype=jnp.int32)[None, :]).astype(jnp.int32)
    csum = jnp.cumsum(onehot, axis=0)
    rank = jnp.take_along_axis(csum, flat_e[:, None], axis=1)[:, 0] - 1
    counts = csum[-1]
    padded = (counts + MOE_BLOCK - 1) // MOE_BLOCK * MOE_BLOCK
    pad_end = jnp.cumsum(padded)
    pad_start = pad_end - padded
    dest = pad_start[flat_e] + rank
    nblk = -(-(a + N_EXPERTS * MOE_BLOCK) // MOE_BLOCK)
    cap = nblk * MOE_BLOCK
    slot_tok = jnp.zeros((cap,), jnp.int32).at[dest].set(jnp.arange(a, dtype=jnp.int32) // TOP_K,
                                                         unique_indices=True)
    blk_start = jnp.arange(nblk, dtype=jnp.int32) * MOE_BLOCK
    blk_e = jnp.minimum(jnp.sum((pad_end[None, :] <= blk_start[:, None]).astype(jnp.int32), axis=1), N_EXPERTS - 1)
    n_used = (pad_end[-1:] // MOE_BLOCK).astype(jnp.int32)
    return dest.reshape(t, TOP_K).astype(jnp.int32), slot_tok, blk_e.astype(jnp.int32), n_used


def _moe_layer(x, g, sc, sh, gate, router_w, router_b, layer, w_gate_up, w_down, out_norm_g=None):
    b, s, d = x.shape
    t = b * s
    h, idx, wts = _norm_route(x, g, sc, sh, router_w, router_b)
    idx = idx.transpose(0, 2, 1).reshape(t, TOP_K)
    wts = wts.transpose(0, 2, 1).reshape(t, TOP_K)
    dest, slot_tok, blk_e, n_used = _dispatch_tables(idx)
    yo = _expert_blocks(h, slot_tok, blk_e + layer * N_EXPERTS, n_used, w_gate_up, w_down)
    return _moe_combine(x.reshape(t, d), yo, dest, wts, gate, s, out_norm_g).reshape(b, s, d)


def _nsa_layer(h, x, gate, slopes, j, w_in, w_phi1, w_phi2, phi_pos, w_out, cast_job=((), (), ())):
    b, s, d = h.shape
    t = b * s
    h2d = h.reshape(t, d)
    proj, kv_chunks = _nsa_projection(h2d, w_in, j, s, HEAD_DIM ** -0.5 * LOG2E)
    proj = proj.reshape(b, s, NSA_QKV)
    wg = w_in[j, :, NSA_QKV:].reshape(d, NSA_KV_HEADS, NSA_GATES)
    wg = jnp.pad(wg, ((0, 0), (0, 0), (0, LANES - NSA_GATES))).reshape(1, d, NSA_KV_HEADS * LANES)
    gates = _matmul(h2d, wg, 0, 0, NSA_KV_HEADS * LANES, jnp.float32, act="sigmoid")
    gates = gates.reshape(b, s, NSA_KV_HEADS * LANES)
    cmp_kv = _compress(kv_chunks, w_phi1, w_phi2, phi_pos)
    o, casted = _nsa_attention(proj, cmp_kv, gates, slopes * LOG2E, *cast_job)
    return _matmul_residual(o, w_out, j, x, gate), casted


def _dil_layer(x, g, sc, sh, gate, slopes, j, w_in, w_out):
    b, s, d = x.shape
    dilations = tuple(r for _, r in DIL_PAIRS)
    assert dilations[0] == 1
    hs = _norm_mod_streams(x, g, sc, sh, dilations)
    qs = _matmul_streams(hs[0].reshape(b, s, d), w_in, j, 0, ATT_WIDTH, dilations,
                         scale=HEAD_DIM ** -0.5 * LOG2E)
    outs, lses = [], []
    for gidx, (win, r) in enumerate(DIL_PAIRS):
        off = ATT_WIDTH * (1 + 2 * gidx)
        kv = _matmul(hs[gidx].reshape(b * s, d), w_in, j, off, 2 * ATT_WIDTH, jnp.bfloat16)
        o, lse = _dil_attention(qs[gidx], kv.reshape(b, r, s // r, 2 * ATT_WIDTH), slopes * LOG2E, win, r)
        outs.append(o)
        lses.append(lse)
    o = _dil_merge(outs, lses, dilations)
    return _matmul_residual(o, w_out, j, x, gate)


def kernel(x, c, ada_w, ada_b, norm_mix, norm_ffn, norm_final, nsa_w_in, nsa_w_phi1, nsa_w_phi2, nsa_phi_pos,
           nsa_w_out, dil_w_in, dil_w_out, router_w, router_b, exp_w_gate, exp_w_up, exp_w_down):
    depth = ada_w.shape[0]
    d = x.shape[-1]
    mod = _modulation(c, ada_w, ada_b)
    slopes = 2.0 ** (-ALIBI_MAX_BIAS * jnp.arange(1, N_HEADS + 1, dtype=jnp.float32) / N_HEADS)
    n_exp, _, de = exp_w_gate.shape[1:]
    for i in range(depth):
        sh_m, sc_m, g_m, sh_f, sc_f, g_f = [mod[i, :, k * d:(k + 1) * d] for k in range(6)]
        j = i // 2
        if i % 2 == 0:
            h = _norm_mod(x, norm_mix[i], sc_m, sh_m, jnp.bfloat16)
            cast_job = ((), (), ())
            if i == 0:
                srcs = [w.reshape(-1, w.shape[-1]) for w in (exp_w_gate, exp_w_up, exp_w_down)]
                cast_job = (srcs, ((0, 0), (0, de), (1, 0)), ((srcs[0].shape[0], 2 * de), srcs[2].shape))
            x, casted = _nsa_layer(h, x, g_m, slopes, j, nsa_w_in, nsa_w_phi1[j], nsa_w_phi2[j], nsa_phi_pos[j],
                                   nsa_w_out, cast_job)
            if i == 0:
                e_gate_up = casted[0].reshape(depth * n_exp, d, 2 * de)
                e_down = casted[1].reshape(depth * n_exp, de, d)
        else:
            x = _dil_layer(x, norm_mix[i], sc_m, sh_m, g_m, slopes, j, dil_w_in, dil_w_out)
        x = _moe_layer(x, norm_ffn[i], sc_f, sh_f, g_f, router_w, router_b, i, e_gate_up, e_down,
                       norm_final if i == depth - 1 else None)
    return x
```

```python
import functools

import jax
import jax.numpy as jnp
import numpy as np
from jax import lax
from jax.experimental import pallas as pl
from jax.experimental.pallas import tpu as pltpu

HEAD_DIM = 128
N_HEADS = 16
ATT_WIDTH = N_HEADS * HEAD_DIM
ALIBI_MAX_BIAS = 8.0

NSA_KV_HEADS = 4
NSA_GROUP = N_HEADS // NSA_KV_HEADS
NSA_KV_WIDTH = NSA_KV_HEADS * HEAD_DIM
CMP_BLOCK = 32
CMP_STRIDE = 16
SEL_BLOCK = 64
N_SEL = 16
NSA_WINDOW = 512
FORCE_SCORE = 1.0e6
NSA_QKV = ATT_WIDTH + 6 * NSA_KV_WIDTH
NSA_GATES = 3 * NSA_GROUP

DIL_PAIRS = ((128, 1), (512, 4), (2048, 16))
DIL_PROJ = ATT_WIDTH * (1 + 2 * len(DIL_PAIRS))

N_EXPERTS = 16
N_EXPERT_GROUPS = 4
EXPERTS_PER_GROUP = N_EXPERTS // N_EXPERT_GROUPS
TOP_K = 2
MOE_BLOCK = 256

RMS_EPS = 1e-6
NEG_INF = -1.0e30

LANES = 128
ATT_TILE = 128
ROW_BLOCK = 512
MOE_ROW_BLOCK = 256
VMEM_LIMIT = 56 * 1024 * 1024

_HI = lax.Precision.HIGHEST
_NT = (((1,), (1,)), ((), ()))


def _params(sem, vmem=VMEM_LIMIT):
    return pltpu.CompilerParams(dimension_semantics=sem, vmem_limit_bytes=vmem)


def _mod_kernel(c_ref, w_ref, b_ref, o_ref):
    c = c_ref[...]
    cond = c * jax.nn.sigmoid(c)
    o_ref[0] = jnp.dot(cond, w_ref[0], precision=_HI, preferred_element_type=jnp.float32) + b_ref[0]


def _modulation(c, ada_w, ada_b):
    depth, d, n = ada_w.shape
    b = c.shape[0]
    tn = 512
    return pl.pallas_call(
        _mod_kernel,
        grid=(depth, n // tn),
        in_specs=[pl.BlockSpec((b, d), lambda i, j: (0, 0)),
                  pl.BlockSpec((1, d, tn), lambda i, j: (i, 0, j)),
                  pl.BlockSpec((1, 1, tn), lambda i, j: (i, 0, j))],
        out_specs=pl.BlockSpec((1, b, tn), lambda i, j: (i, 0, j)),
        out_shape=jax.ShapeDtypeStruct((depth, b, n), jnp.float32),
        compiler_params=_params(("parallel", "parallel")),
        name="adaln_modulation",
    )(c, ada_w, ada_b.reshape(depth, 1, n))


def _rms(x, g):
    return x * lax.rsqrt(jnp.mean(x * x, axis=-1, keepdims=True) + RMS_EPS) * g


def _norm_mod_kernel(x_ref, g_ref, sc_ref, sh_ref, o_ref):
    h = _rms(x_ref[0], g_ref[...]) * (1.0 + sc_ref[0]) + sh_ref[0]
    o_ref[0] = h.astype(o_ref.dtype)


def _norm_mod(x, g, sc, sh, out_dtype):
    b, s, d = x.shape
    ts = min(ROW_BLOCK, s)
    return pl.pallas_call(
        _norm_mod_kernel,
        grid=(b, s // ts),
        in_specs=[pl.BlockSpec((1, ts, d), lambda i, j: (i, j, 0)),
                  pl.BlockSpec((1, d), lambda i, j: (0, 0)),
                  pl.BlockSpec((1, 1, d), lambda i, j: (i, 0, 0)),
                  pl.BlockSpec((1, 1, d), lambda i, j: (i, 0, 0))],
        out_specs=pl.BlockSpec((1, ts, d), lambda i, j: (i, j, 0)),
        out_shape=jax.ShapeDtypeStruct((b, s, d), out_dtype),
        compiler_params=_params(("parallel", "parallel")),
        name="rmsnorm_adaln",
    )(x, g.reshape(1, d), sc.reshape(b, 1, d), sh.reshape(b, 1, d))


def _store_streams(val, src_sc, out_refs, dilations):
    n = val.shape[0]
    for c in range(src_sc.shape[0]):
        src_sc[c] = val[:, c * LANES:(c + 1) * LANES]
    for r, ref in zip(dilations, out_refs):
        if r == 1:
            ref[0, 0] = val.astype(ref.dtype)
            continue
        for rho in range(r):
            for c in range(src_sc.shape[0]):
                ref[0, rho, :, c * LANES:(c + 1) * LANES] = (
                    src_sc[c, pl.ds(rho, n // r, stride=r), :].astype(ref.dtype))


def _norm_mod_streams_kernel(x_ref, g_ref, sc_ref, sh_ref, *refs, dilations):
    out_refs, h_sc = refs[:-1], refs[-1]
    h = _rms(x_ref[0], g_ref[...]) * (1.0 + sc_ref[0]) + sh_ref[0]
    _store_streams(h, h_sc, out_refs, dilations)


def _norm_mod_streams(x, g, sc, sh, dilations):
    b, s, d = x.shape
    ts = min(ROW_BLOCK, s)
    return pl.pallas_call(
        functools.partial(_norm_mod_streams_kernel, dilations=dilations),
        grid=(b, s // ts),
        in_specs=[pl.BlockSpec((1, ts, d), lambda i, j: (i, j, 0)),
                  pl.BlockSpec((1, d), lambda i, j: (0, 0)),
                  pl.BlockSpec((1, 1, d), lambda i, j: (i, 0, 0)),
                  pl.BlockSpec((1, 1, d), lambda i, j: (i, 0, 0))],
        out_specs=[pl.BlockSpec((1, r, ts // r, d), lambda i, j: (i, 0, j, 0)) for r in dilations],
        out_shape=[jax.ShapeDtypeStruct((b, r, s // r, d), jnp.bfloat16) for r in dilations],
        scratch_shapes=[pltpu.VMEM((d // LANES, ts, LANES), jnp.float32)],
        compiler_params=_params(("parallel", "parallel")),
        name="rmsnorm_adaln_streams",
    )(x, g.reshape(1, d), sc.reshape(b, 1, d), sh.reshape(b, 1, d))


def _pair_max(vals):
    out = None
    for a in range(len(vals)):
        for b in range(a + 1, len(vals)):
            s = vals[a] + vals[b]
            out = s if out is None else jnp.maximum(out, s)
    return out


def _store_row_major(ref, val):
    rows = val.shape[0]
    c_tiles = val.shape[1] // LANES
    for c in range(c_tiles):
        ref[pl.ds(c, rows, stride=c_tiles), :] = val[:, c * LANES:(c + 1) * LANES]


def _load_row_tiles(buf):
    n, c_tiles = buf.shape[0], buf.shape[1]
    return jnp.concatenate([buf[:, c].reshape(n * ROW_TILE, LANES) for c in range(c_tiles)], axis=1)


def _norm_route_kernel(x_ref, g_ref, sc_ref, sh_ref, rw_ref, rb_ref, h_ref, idx_ref, wt_ref):
    h = _rms(x_ref[0], g_ref[...]) * (1.0 + sc_ref[0]) + sh_ref[0]
    _store_row_major(h_ref, h)

    logits = lax.dot_general(rw_ref[...], h, _NT, precision=_HI, preferred_element_type=jnp.float32)
    scores = jax.nn.sigmoid(logits)
    biased = scores + rb_ref[...][:, 0:1]
    rows = [biased[e:e + 1, :] for e in range(N_EXPERTS)]
    srow = [scores[e:e + 1, :] for e in range(N_EXPERTS)]
    grp = [_pair_max(rows[q * EXPERTS_PER_GROUP:(q + 1) * EXPERTS_PER_GROUP]) for q in range(N_EXPERT_GROUPS)]
    best_v = grp[0]
    best = jnp.zeros(best_v.shape, jnp.int32)
    for q in range(1, N_EXPERT_GROUPS):
        take = grp[q] > best_v
        best = jnp.where(take, q, best)
        best_v = jnp.where(take, grp[q], best_v)
    v1 = jnp.full(best_v.shape, NEG_INF, jnp.float32)
    i1 = jnp.zeros(best_v.shape, jnp.int32)
    for e in range(N_EXPERTS):
        cand = jnp.where(best == e // EXPERTS_PER_GROUP, rows[e], NEG_INF)
        take = cand > v1
        i1 = jnp.where(take, e, i1)
        v1 = jnp.where(take, cand, v1)
    v2 = jnp.full(best_v.shape, NEG_INF, jnp.float32)
    i2 = jnp.zeros(best_v.shape, jnp.int32)
    for e in range(N_EXPERTS):
        cand = jnp.where(best == e // EXPERTS_PER_GROUP, jnp.where(i1 == e, NEG_INF, rows[e]), NEG_INF)
        take = cand > v2
        i2 = jnp.where(take, e, i2)
        v2 = jnp.where(take, cand, v2)
    w1 = jnp.zeros(best_v.shape, jnp.float32)
    w2 = jnp.zeros(best_v.shape, jnp.float32)
    for e in range(N_EXPERTS):
        w1 = jnp.where(i1 == e, srow[e], w1)
        w2 = jnp.where(i2 == e, srow[e], w2)
    tot = w1 + w2
    idx_ref[0] = jnp.concatenate([i1, i2], axis=0)
    wt_ref[0] = jnp.concatenate([w1 / tot, w2 / tot], axis=0)


def _norm_route(x, g, sc, sh, router_w, router_b):
    b, s, d = x.shape
    ts = min(MOE_ROW_BLOCK, s)
    ct = d // LANES
    rwt = router_w.T
    rb = jnp.broadcast_to(router_b.reshape(N_EXPERTS, 1), (N_EXPERTS, LANES))
    h, idx, wts = pl.pallas_call(
        _norm_route_kernel,
        grid=(b, s // ts),
        in_specs=[pl.BlockSpec((1, ts, d), lambda i, j: (i, j, 0)),
                  pl.BlockSpec((1, d), lambda i, j: (0, 0)),
                  pl.BlockSpec((1, 1, d), lambda i, j: (i, 0, 0)),
                  pl.BlockSpec((1, 1, d), lambda i, j: (i, 0, 0)),
                  pl.BlockSpec((N_EXPERTS, d), lambda i, j: (0, 0)),
                  pl.BlockSpec((N_EXPERTS, LANES), lambda i, j: (0, 0))],
        out_specs=[pl.BlockSpec((ts * ct, LANES), lambda i, j: (i * (s // ts) + j, 0)),
                   pl.BlockSpec((1, TOP_K, ts), lambda i, j: (i, 0, j)),
                   pl.BlockSpec((1, TOP_K, ts), lambda i, j: (i, 0, j))],
        out_shape=[jax.ShapeDtypeStruct((b * s * ct, LANES), jnp.float32),
                   jax.ShapeDtypeStruct((b, TOP_K, s), jnp.int32),
                   jax.ShapeDtypeStruct((b, TOP_K, s), jnp.float32)],
        compiler_params=_params(("parallel", "parallel")),
        name="rmsnorm_adaln_router",
    )(x, g.reshape(1, d), sc.reshape(b, 1, d), sh.reshape(b, 1, d), rwt, rb)
    return h.reshape(b * s, ct, LANES), idx, wts


def _weight_tile(w_ref, wbf_sc, first, scale, scale_tiles):
    @pl.when(first)
    def _():
        w = w_ref[0]
        if scale_tiles:
            w = w * jnp.where(pl.program_id(0) < scale_tiles, scale, 1.0)
        wbf_sc[...] = w.astype(jnp.bfloat16)
    return wbf_sc[...]


def _weight_spec(layer, col0, k, tn, rank):
    assert col0 % tn == 0
    if rank == 2:
        return pl.BlockSpec((1, k, tn), lambda j, i: (layer, 0, col0 // tn + j))
    return pl.BlockSpec((1, k, tn), lambda j, bi, i: (layer, 0, col0 // tn + j))


def _mm_kernel(a_ref, w_ref, o_ref, wbf_sc, *, act, scale, scale_tiles):
    w = _weight_tile(w_ref, wbf_sc, pl.program_id(1) == 0, scale, scale_tiles)
    acc = jnp.dot(a_ref[...], w, preferred_element_type=jnp.float32)
    if act == "sigmoid":
        acc = jax.nn.sigmoid(acc)
    o_ref[...] = acc.astype(o_ref.dtype)


def _matmul(a, w, layer, col0, n, out_dtype, act=None, scale=1.0, scale_cols=0, tm=2048, tn=1024):
    m, k = a.shape
    tn = min(tn, n)
    tm = min(tm, m)
    assert scale_cols % tn == 0
    return pl.pallas_call(
        functools.partial(_mm_kernel, act=act, scale=scale, scale_tiles=scale_cols // tn),
        grid=(n // tn, m // tm),
        in_specs=[pl.BlockSpec((tm, k), lambda j, i: (i, 0)),
                  _weight_spec(layer, col0, k, tn, 2)],
        out_specs=pl.BlockSpec((tm, tn), lambda j, i: (i, j)),
        out_shape=jax.ShapeDtypeStruct((m, n), out_dtype),
        scratch_shapes=[pltpu.VMEM((k, tn), jnp.bfloat16)],
        compiler_params=_params(("arbitrary", "arbitrary")),
        name="matmul" if act is None else "matmul_" + act,
    )(a, w)


def _nsa_proj_kernel(a_ref, w_ref, o_ref, ch_ref, wbf_sc, stage_sc, *, scale, scale_tiles, kv_tile, n_tiles):
    tile = (pl.program_id(0) + kv_tile + 1) % n_tiles

    @pl.when(pl.program_id(1) == 0)
    def _():
        w = w_ref[0]
        if scale_tiles:
            w = w * jnp.where(tile < scale_tiles, scale, 1.0)
        wbf_sc[...] = w.astype(jnp.bfloat16)

    acc = jnp.dot(a_ref[...], wbf_sc[...], preferred_element_type=jnp.float32)
    o_ref[...] = acc.astype(o_ref.dtype)

    @pl.when(pl.program_id(0) == n_tiles - 1)
    def _():
        nch = ch_ref.shape[3]
        for c in range(acc.shape[1] // HEAD_DIM):
            kv, g = divmod(c, NSA_KV_HEADS)
            stage_sc[...] = acc[:, c * HEAD_DIM:(c + 1) * HEAD_DIM]
            for l in range(CMP_STRIDE):
                ch_ref[0, kv, g, :, l * HEAD_DIM:(l + 1) * HEAD_DIM] = (
                    stage_sc[pl.ds(l, nch, stride=CMP_STRIDE), :].astype(ch_ref.dtype))


def _nsa_projection(a, w, layer, s, scale, tm=1024, tn=1024):
    m, k = a.shape
    b = m // s
    n_tiles = NSA_QKV // tn
    kv_tile = ATT_WIDTH // tn
    assert ATT_WIDTH % tn == 0 and 2 * NSA_KV_WIDTH == tn and s % tm == 0 and HEAD_DIM == LANES
    per_b = s // tm
    nch = tm // CMP_STRIDE

    def tile_of(j):
        return (j + kv_tile + 1) % n_tiles

    def chunk_block(j, i):
        last = j == n_tiles - 1
        return (jnp.where(last, i // per_b, 0), 0, 0, jnp.where(last, i % per_b, 0), 0)

    return pl.pallas_call(
        functools.partial(_nsa_proj_kernel, scale=scale, scale_tiles=ATT_WIDTH // tn, kv_tile=kv_tile,
                          n_tiles=n_tiles),
        grid=(n_tiles, m // tm),
        in_specs=[pl.BlockSpec((tm, k), lambda j, i: (i, 0)),
                  pl.BlockSpec((1, k, tn), lambda j, i: (layer, 0, tile_of(j)))],
        out_specs=[pl.BlockSpec((tm, tn), lambda j, i: (i, tile_of(j))),
                   pl.BlockSpec((1, 2, NSA_KV_HEADS, nch, CMP_STRIDE * HEAD_DIM), chunk_block)],
        out_shape=[jax.ShapeDtypeStruct((m, NSA_QKV), jnp.bfloat16),
                   jax.ShapeDtypeStruct((b, 2, NSA_KV_HEADS, s // CMP_STRIDE, CMP_STRIDE * HEAD_DIM), jnp.bfloat16)],
        scratch_shapes=[pltpu.VMEM((k, tn), jnp.bfloat16), pltpu.VMEM((tm, HEAD_DIM), jnp.float32)],
        compiler_params=_params(("arbitrary", "arbitrary")),
        name="matmul_nsa_proj",
    )(a, w)


def _first_row_step():
    return (pl.program_id(1) == 0) & (pl.program_id(2) == 0)


def _mm_streams_kernel(a_ref, w_ref, *refs, dilations, scale, scale_tiles):
    out_refs, acc_sc, wbf_sc = refs[:-2], refs[-2], refs[-1]
    w = _weight_tile(w_ref, wbf_sc, _first_row_step(), scale, scale_tiles)
    acc = jnp.dot(a_ref[0], w, preferred_element_type=jnp.float32)
    _store_streams(acc, acc_sc, out_refs, dilations)


def _matmul_streams(a, w, layer, col0, n, dilations, scale=1.0, tm=1024, tn=1024):
    b, s, k = a.shape
    tn = min(tn, n)
    tm = min(tm, s)
    return pl.pallas_call(
        functools.partial(_mm_streams_kernel, dilations=dilations, scale=scale,
                          scale_tiles=n // tn if scale != 1.0 else 0),
        grid=(n // tn, b, s // tm),
        in_specs=[pl.BlockSpec((1, tm, k), lambda j, bi, i: (bi, i, 0)),
                  _weight_spec(layer, col0, k, tn, 3)],
        out_specs=[pl.BlockSpec((1, r, tm // r, tn), lambda j, bi, i: (bi, 0, i, j)) for r in dilations],
        out_shape=[jax.ShapeDtypeStruct((b, r, s // r, n), jnp.bfloat16) for r in dilations],
        scratch_shapes=[pltpu.VMEM((tn // LANES, tm, LANES), jnp.float32), pltpu.VMEM((k, tn), jnp.bfloat16)],
        compiler_params=_params(("arbitrary", "arbitrary", "arbitrary")),
        name="matmul_streams",
    )(a, w)


def _mm_residual_kernel(a_ref, w_ref, x_ref, g_ref, o_ref, wbf_sc):
    w = _weight_tile(w_ref, wbf_sc, _first_row_step(), 1.0, 0)
    acc = jnp.dot(a_ref[0], w, preferred_element_type=jnp.float32)
    o_ref[0] = x_ref[0] + g_ref[0] * acc


def _matmul_residual(a, w, layer, x, gate, tm=1024, tn=1024):
    b, s, k = a.shape
    n = w.shape[2]
    tn = min(tn, n)
    tm = min(tm, s)
    return pl.pallas_call(
        _mm_residual_kernel,
        grid=(n // tn, b, s // tm),
        in_specs=[pl.BlockSpec((1, tm, k), lambda j, bi, i: (bi, i, 0)),
                  _weight_spec(layer, 0, k, tn, 3),
                  pl.BlockSpec((1, tm, tn), lambda j, bi, i: (bi, i, j)),
                  pl.BlockSpec((1, 1, tn), lambda j, bi, i: (bi, 0, j))],
        out_specs=pl.BlockSpec((1, tm, tn), lambda j, bi, i: (bi, i, j)),
        out_shape=jax.ShapeDtypeStruct((b, s, n), jnp.float32),
        scratch_shapes=[pltpu.VMEM((k, tn), jnp.bfloat16)],
        compiler_params=_params(("arbitrary", "arbitrary", "arbitrary")),
        name="matmul_gated_residual",
    )(a, w, x, gate.reshape(b, 1, n))


def _gelu_tanh(x):
    return 0.5 * x * (1.0 + jnp.tanh(0.7978845608028654 * (x + 0.044715 * (x * x * x))))


def _compress_kernel(c_ref, pa_ref, pb_ref, wa_ref, wb_ref, w2_ref, o_ref):
    c = c_ref[0, 0, 0].astype(jnp.float32)
    lo = (c + pa_ref[0]).astype(jnp.bfloat16)
    hi = (c + pb_ref[0]).astype(jnp.bfloat16)
    ha = jnp.dot(lo, wa_ref[0], preferred_element_type=jnp.float32)
    hb = jnp.dot(hi, wb_ref[0], preferred_element_type=jnp.float32)
    n = ha.shape[0]
    hid = _gelu_tanh(ha + pltpu.roll(hb, n - 1, 0))
    out = jnp.dot(hid.astype(jnp.bfloat16), w2_ref[0], preferred_element_type=jnp.float32)
    row = lax.broadcasted_iota(jnp.int32, out.shape, 0)
    o_ref[0, 0, 0] = jnp.where(row < n - 1, out, 0.0).astype(o_ref.dtype)


def _compress(kv_chunks, w_phi1, w_phi2, phi_pos):
    b, two, hkv, nch, width = kv_chunks.shape
    dh = HEAD_DIM
    half = CMP_BLOCK // 2
    w1 = w_phi1.reshape(2, 2, half * dh, dh).astype(jnp.bfloat16)
    pos = phi_pos.reshape(2, 2, 1, half * dh)
    return pl.pallas_call(
        _compress_kernel,
        grid=(b, two, hkv),
        in_specs=[pl.BlockSpec((1, 1, 1, nch, width), lambda i, j, g: (i, j, g, 0, 0)),
                  pl.BlockSpec((1, 1, width), lambda i, j, g: (j, 0, 0)),
                  pl.BlockSpec((1, 1, width), lambda i, j, g: (j, 0, 0)),
                  pl.BlockSpec((1, width, dh), lambda i, j, g: (j, 0, 0)),
                  pl.BlockSpec((1, width, dh), lambda i, j, g: (j, 0, 0)),
                  pl.BlockSpec((1, dh, dh), lambda i, j, g: (j, 0, 0))],
        out_specs=pl.BlockSpec((1, 1, 1, nch, dh), lambda i, j, g: (i, j, g, 0, 0)),
        out_shape=jax.ShapeDtypeStruct((b, two, hkv, nch, dh), jnp.bfloat16),
        compiler_params=_params(("parallel", "parallel", "parallel")),
        name="nsa_compress",
    )(kv_chunks, pos[:, 0], pos[:, 1], w1[:, 0], w1[:, 1], w_phi2.astype(jnp.bfloat16))


MASK_BIG = 2.0 ** 100
AUX_SLOPE = 64
AUX_PAD = 70
SEL_CHUNK = 512
LOG2E = 1.4426950408889634


def _nsa_attn_kernel(sl_ref, q_ref, kc_ref, vc_ref, ks_ref, vs_ref, kw_ref, vw_ref, gt_ref, ovl_ref, qtab_ref,
                     kauxs_ref, kauxw_ref, *refs, n_cmp, n_blk, n_sel, cast_place, n_cast_dst):
    n_cast = len(cast_place)
    cast_src, refs = refs[:n_cast], refs[n_cast:]
    o_ref, refs = refs[0], refs[1:]
    cast_dst, refs = refs[:n_cast_dst], refs[n_cast_dst:]
    (ocmp_sc, maskq_sc, owin_sc, qaug_sc, qaugw_sc, p_sc, pw_sc, kaug_s, vaug_s, kaug_w, vaug_w), refs = refs[:11], refs[11:]
    cast_in, cast_out, (cast_in_sem, cast_out_sem) = refs[:n_cast], refs[n_cast:2 * n_cast], refs[2 * n_cast:]
    t = ATT_TILE
    dh = HEAD_DIM
    s_len = ks_ref.shape[1]
    wpad = NSA_WINDOW
    g = pl.program_id(1)
    qt = pl.program_id(2)
    qstart = pl.multiple_of(qt * t, t)

    if n_cast:
        step = (pl.program_id(0) * pl.num_programs(1) + g) * pl.num_programs(2) + qt
        n_steps = pl.num_programs(0) * pl.num_programs(1) * pl.num_programs(2)
        drain_cast = _cast_side_job(step, n_steps, cast_src, cast_dst, cast_place, cast_in, cast_out, cast_in_sem,
                                    cast_out_sem)

    @pl.when(qt == 0)
    def _():
        ones = jnp.ones((s_len, dh), jnp.bfloat16)
        kaug_s[:, :dh] = ks_ref[0]
        kaug_s[:, dh:] = kauxs_ref[...]
        vaug_s[:, :dh] = vs_ref[0]
        vaug_s[:, dh:] = ones
        kaug_w[:wpad, :dh] = jnp.zeros((wpad, dh), jnp.bfloat16)
        kaug_w[wpad:, :dh] = kw_ref[0]
        kaug_w[:, dh:] = kauxw_ref[...]
        vaug_w[:wpad, :dh] = jnp.zeros((wpad, dh), jnp.bfloat16)
        vaug_w[wpad:, :dh] = vw_ref[0]
        vaug_w[:wpad, dh:] = jnp.ones((wpad, dh), jnp.bfloat16)
        vaug_w[wpad:, dh:] = ones

    slopes = [sl_ref[g * NSA_GROUP + r] for r in range(NSA_GROUP)]
    row = lax.broadcasted_iota(jnp.int32, (t, LANES), 0)
    col = lax.broadcasted_iota(jnp.int32, (t, LANES), 1)
    prows = [slice(pair * 2 * t, (pair + 1) * 2 * t) for pair in range(NSA_GROUP // 2)]
    qtab = qtab_ref[0]

    def compress_and_select(qv, tile, par):
        qs = tile * t
        tpos = qs + row
        q4 = jnp.concatenate([qv[:, r * dh:(r + 1) * dh] for r in range(NSA_GROUP)], axis=0)
        kc = kc_ref[0, 0, 0]
        vc = vc_ref[0, 0, 0]
        s = lax.dot_general(q4, kc, _NT, preferred_element_type=jnp.float32)
        visible = (col * CMP_STRIDE + (CMP_BLOCK - 1) <= tpos) & (col < n_cmp)
        dist_c = tpos.astype(jnp.float32) - (col.astype(jnp.float32) * CMP_STRIDE + (CMP_BLOCK - 1) / 2.0)
        psum = jnp.zeros((t, LANES), jnp.float32)
        for r in range(NSA_GROUP):
            rows = slice(r * t, (r + 1) * t)
            sr = jnp.where(visible, s[rows] - slopes[r] * dist_c, NEG_INF)
            e = jnp.where(visible, jnp.exp2(sr - jnp.max(sr, axis=-1, keepdims=True)), 0.0)
            p = e / jnp.maximum(jnp.sum(e, axis=-1, keepdims=True), 1e-30)
            psum = psum + p
            ocmp_sc[par, rows] = jnp.dot(p.astype(vc.dtype), vc, preferred_element_type=jnp.float32)
        nb = -(-n_blk // 8) * 8
        imp_t = lax.dot_general(ovl_ref[...], psum, _NT, precision=_HI, preferred_element_type=jnp.float32)[:nb]
        jrow = lax.broadcasted_iota(jnp.int32, (nb, t), 0)
        tpos_t = qs + lax.broadcasted_iota(jnp.int32, (nb, t), 1)
        cur = jnp.right_shift(tpos_t, int(np.log2(SEL_BLOCK)))
        forced = (jrow == 0) | (jrow == cur) | (jrow == cur - 1)
        score = jnp.where(forced, FORCE_SCORE, jnp.where(jrow * SEL_BLOCK <= tpos_t, imp_t, -1.0))
        rank = jnp.zeros((nb, t), jnp.float32)
        for j in range(n_blk):
            cj = score[j:j + 1, :]
            ahead = (cj > score) | ((cj == score) & (jrow > j))
            rank = rank + jnp.where(ahead, 1.0, 0.0)
        mask_t = jnp.where((rank < n_sel) & (jrow < n_blk), 0.0, -MASK_BIG)
        maskq_sc[par] = jnp.concatenate([mask_t, jnp.zeros((LANES - nb, t), jnp.float32)], axis=0).T

    par = 0
    q = q_ref[0]

    for r in range(NSA_GROUP):
        rows = slice(r * t, (r + 1) * t)
        qaugw_sc[rows, :dh] = q[:, r * dh:(r + 1) * dh]
        qaugw_sc[rows, dh:] = jnp.broadcast_to(qtab[r:r + 1, :], (t, LANES)).astype(jnp.bfloat16)
    wk = wpad + t
    nw = wk // LANES
    kw = kaug_w[pl.ds(qstart, wk), :]
    vw = vaug_w[pl.ds(qstart, wk), :]
    sws = [lax.dot_general(qaugw_sc[prow], kw, _NT, preferred_element_type=jnp.float32) for prow in prows]

    compress_and_select(q, qt, par)

    mask_q = maskq_sc[par]
    for r in range(NSA_GROUP):
        rows = slice(r * t, (r + 1) * t)
        qaug_sc[rows, :dh] = q[:, r * dh:(r + 1) * dh]
        qaug_sc[rows, dh:] = (mask_q + qtab[r:r + 1, :]).astype(jnp.bfloat16)

    for pair, prow in enumerate(prows):
        sw = sws[pair]
        for r2 in range(2):
            rows = slice((2 * pair + r2) * t, (2 * pair + r2 + 1) * t)
            tiles = [sw[r2 * t:(r2 + 1) * t, j * LANES:(j + 1) * LANES] for j in range(nw)]
            tiles[0] = jnp.where(col > row, tiles[0], NEG_INF)
            tiles[-1] = jnp.where(col <= row, tiles[-1], NEG_INF)
            mx = functools.reduce(jnp.maximum, tiles)
            m = jnp.broadcast_to(jnp.max(mx, axis=-1, keepdims=True), (t, LANES))
            for j in range(nw):
                pw_sc[rows, j * LANES:(j + 1) * LANES] = jnp.exp2(tiles[j] - m).astype(jnp.bfloat16)
        ow = jnp.dot(pw_sc[prow], vw, preferred_element_type=jnp.float32)
        owin_sc[prow] = ow[:, :dh] / ow[:, dh:]

    ch = SEL_CHUNK
    nl = ch // LANES
    n_full = qt // (ch // t)
    gt = gt_ref[0]
    cmr = col - row

    def selected(k):
        nk = (k + 1) * ch
        kk = kaug_s[:nk, :]
        vv = vaug_s[:nk, :]
        sks = [lax.dot_general(qaug_sc[prow], kk, _NT, preferred_element_type=jnp.float32) for prow in prows]
        for pair, prow in enumerate(prows):
            sk = sks[pair]
            for r2 in range(2):
                rows = slice((2 * pair + r2) * t, (2 * pair + r2 + 1) * t)
                tiles = [sk[r2 * t:(r2 + 1) * t, j * LANES:(j + 1) * LANES] for j in range(nk // LANES)]
                for j in range(k * nl, (k + 1) * nl):
                    tiles[j] = jnp.where(cmr <= qstart - j * LANES, tiles[j], NEG_INF)
                mx = functools.reduce(jnp.maximum, tiles)
                m = jnp.broadcast_to(jnp.max(mx, axis=-1, keepdims=True), (t, LANES))
                for j in range(nk // LANES):
                    p_sc[rows, j * LANES:(j + 1) * LANES] = jnp.exp2(tiles[j] - m).astype(jnp.bfloat16)
            pv = jnp.dot(p_sc[prow, :nk], vv, preferred_element_type=jnp.float32)
            for r2 in range(2):
                r = 2 * pair + r2
                rows = slice(r * t, (r + 1) * t)
                orow = slice(r2 * t, (r2 + 1) * t)
                o = (gt[:, 3 * r:3 * r + 1] * ocmp_sc[par, rows]
                     + gt[:, 3 * r + 1:3 * r + 2] * (pv[orow, :dh] / pv[orow, dh:])
                     + gt[:, 3 * r + 2:3 * r + 3] * owin_sc[rows])
                o_ref[0, :, r * dh:(r + 1) * dh] = o.astype(o_ref.dtype)

    for k in range(s_len // ch):
        pl.when(n_full == k)(functools.partial(selected, k))
    if cast_place:
        drain_cast()


def _cast_chunks(shape, n_steps):
    r, c = shape
    for n_c in (1, 2, 4, 8, 16):
        if n_steps % n_c == 0 and c % (n_c * LANES) == 0 and r % (n_steps // n_c) == 0:
            rows = r // (n_steps // n_c)
            if rows % 16 == 0:
                return rows, c // n_c
    raise ValueError("no tile-aligned split of %s into %d chunks" % (shape, n_steps))


def _cast_side_job(n, n_steps, srcs, dsts, place, inbufs, outbufs, in_sem, out_sem):
    slot = n % 2

    def chunk(k, step, col0=0):
        rows, cols = inbufs[k].shape[1:]
        n_c = srcs[k].shape[1] // cols
        return (pl.ds(pl.multiple_of((step // n_c) * rows, 16), rows),
                pl.ds(pl.multiple_of(col0 + (step % n_c) * cols, LANES), cols))

    def in_copy(k, step, sl):
        return pltpu.make_async_copy(srcs[k].at[chunk(k, step)], inbufs[k].at[sl], in_sem.at[sl])

    def out_copy(k, step, sl):
        dst, col0 = place[k]
        return pltpu.make_async_copy(outbufs[k].at[sl], dsts[dst].at[chunk(k, step, col0)], out_sem.at[sl])

    ks = range(len(srcs))

    @pl.when(n == 0)
    def _():
        for k in ks:
            in_copy(k, 0, 0).start()
            outbufs[k][...] = jnp.zeros(outbufs[k].shape, outbufs[k].dtype)
            out_copy(k, 0, 0).start()
            out_copy(k, 1, 1).start()

    for k in ks:
        in_copy(k, n, slot).wait()
    nxt = jnp.minimum(n + 1, n_steps - 1)
    for k in ks:
        in_copy(k, nxt, 1 - slot).start()
    for k in ks:
        out_copy(k, n, slot).wait()

    for k in ks:
        outbufs[k][slot] = inbufs[k][slot].astype(outbufs[k].dtype)
    for k in ks:
        out_copy(k, n, slot).start()

    def drain():
        @pl.when(n == n_steps - 1)
        def _():
            for k in ks:
                in_copy(k, n, 1 - slot).wait()
                out_copy(k, n, 1 - slot).wait()
                out_copy(k, n, slot).wait()

    return drain


def _bf16_pieces(x):
    a0 = x.astype(jnp.bfloat16).astype(jnp.float32)
    a1 = (x - a0).astype(jnp.bfloat16).astype(jnp.float32)
    a2 = (x - a0 - a1).astype(jnp.bfloat16).astype(jnp.float32)
    return [a0, a1, a2]


def _nsa_tables(s, slopes2):
    n_cmp = s // CMP_STRIDE - CMP_BLOCK // CMP_STRIDE + 1
    n_blk = s // SEL_BLOCK
    assert n_cmp < LANES and n_blk <= AUX_SLOPE
    bj = np.arange(LANES)[:, None]
    cn = np.arange(LANES)[None, :]
    ovl_t = np.clip(np.minimum(cn * CMP_STRIDE + CMP_BLOCK, (bj + 1) * SEL_BLOCK)
                    - np.maximum(cn * CMP_STRIDE, bj * SEL_BLOCK), 0, None) / CMP_STRIDE
    ovl_t = np.where((cn < n_cmp) & (bj < n_blk), ovl_t, 0.0).astype(np.float32)
    pos = np.arange(s)
    kaux = np.zeros((s, LANES), np.float32)
    kaux[pos, pos // SEL_BLOCK] = 1.0
    kaux[:, AUX_SLOPE:AUX_SLOPE + 3] = (pos // LANES * LANES)[:, None]
    kaux[:, AUX_SLOPE + 3:AUX_SLOPE + 6] = (pos % LANES)[:, None]
    kaux_w = np.zeros((NSA_WINDOW + s, LANES), np.float32)
    kaux_w[NSA_WINDOW:, AUX_SLOPE:AUX_SLOPE + 6] = kaux[:, AUX_SLOPE:AUX_SLOPE + 6]
    kaux_w[:NSA_WINDOW, AUX_PAD] = 1.0
    pieces = jnp.stack(_bf16_pieces(slopes2) * 2, axis=-1)
    qtab = jnp.zeros((N_HEADS, LANES), jnp.float32)
    qtab = qtab.at[:, AUX_SLOPE:AUX_SLOPE + 6].set(pieces).at[:, AUX_PAD].set(-MASK_BIG)
    qtab = jnp.pad(qtab.reshape(NSA_KV_HEADS, NSA_GROUP, LANES), ((0, 0), (0, 8 - NSA_GROUP), (0, 0)))
    return (n_cmp, n_blk, jnp.asarray(ovl_t), qtab, jnp.asarray(kaux, dtype=jnp.bfloat16),
            jnp.asarray(kaux_w, dtype=jnp.bfloat16))


def _nsa_attention(proj, cmp_kv, gates, slopes2, cast_ws=(), cast_place=(), cast_shapes=()):
    b, s, _ = proj.shape
    t = ATT_TILE
    n_steps = b * NSA_KV_HEADS * (s // t)
    chunks = [_cast_chunks(w.shape, n_steps) for w in cast_ws]
    assert n_steps >= 2
    any_spec = pl.BlockSpec(memory_space=pl.ANY)
    n_cmp, n_blk, ovl_t, qtab, kaux_s, kaux_w = _nsa_tables(s, slopes2)
    assert cmp_kv.shape[3] == LANES and s % SEL_CHUNK == 0 and NSA_WINDOW % t == 0
    qw = NSA_GROUP * HEAD_DIM
    rows = NSA_GROUP * t
    kv0 = ATT_WIDTH // HEAD_DIM

    def kv_spec(j):
        return pl.BlockSpec((1, s, HEAD_DIM), lambda bi, g, qi, j=j: (bi, 0, kv0 + j * NSA_KV_HEADS + g))

    def const(shape):
        return pl.BlockSpec(shape, lambda bi, g, qi: (0,) * len(shape))

    outs = pl.pallas_call(
        functools.partial(_nsa_attn_kernel, n_cmp=n_cmp, n_blk=n_blk, n_sel=min(N_SEL, n_blk),
                          cast_place=tuple(cast_place), n_cast_dst=len(cast_shapes)),
        grid=(b, NSA_KV_HEADS, s // t),
        in_specs=[pl.BlockSpec(memory_space=pltpu.SMEM),
                  pl.BlockSpec((1, t, qw), lambda bi, g, qi: (bi, qi, g)),
                  pl.BlockSpec((1, 1, 1, LANES, HEAD_DIM), lambda bi, g, qi: (bi, 0, g, 0, 0)),
                  pl.BlockSpec((1, 1, 1, LANES, HEAD_DIM), lambda bi, g, qi: (bi, 1, g, 0, 0)),
                  kv_spec(2), kv_spec(3), kv_spec(4), kv_spec(5),
                  pl.BlockSpec((1, t, LANES), lambda bi, g, qi: (bi, qi, g)),
                  const((LANES, LANES)),
                  pl.BlockSpec((1, 8, LANES), lambda bi, g, qi: (g, 0, 0)),
                  const((s, LANES)), const((NSA_WINDOW + s, LANES))] + [any_spec] * len(cast_ws),
        out_specs=[pl.BlockSpec((1, t, qw), lambda bi, g, qi: (bi, qi, g))] + [any_spec] * len(cast_shapes),
        out_shape=[jax.ShapeDtypeStruct((b, s, ATT_WIDTH), jnp.bfloat16)]
        + [jax.ShapeDtypeStruct(shape, jnp.bfloat16) for shape in cast_shapes],
        scratch_shapes=[pltpu.VMEM((1, rows, HEAD_DIM), jnp.float32),
                        pltpu.VMEM((1, t, LANES), jnp.float32),
                        pltpu.VMEM((rows, HEAD_DIM), jnp.float32),
                        pltpu.VMEM((rows, 2 * HEAD_DIM), jnp.bfloat16),
                        pltpu.VMEM((rows, 2 * HEAD_DIM), jnp.bfloat16),
                        pltpu.VMEM((rows, s), jnp.bfloat16),
                        pltpu.VMEM((rows, NSA_WINDOW + t), jnp.bfloat16),
                        pltpu.VMEM((s, 2 * HEAD_DIM), jnp.bfloat16),
                        pltpu.VMEM((s, 2 * HEAD_DIM), jnp.bfloat16),
                        pltpu.VMEM((NSA_WINDOW + s, 2 * HEAD_DIM), jnp.bfloat16),
                        pltpu.VMEM((NSA_WINDOW + s, 2 * HEAD_DIM), jnp.bfloat16)]
        + [pltpu.VMEM((2,) + c, jnp.float32) for c in chunks]
        + [pltpu.VMEM((2,) + c, jnp.bfloat16) for c in chunks]
        + [pltpu.SemaphoreType.DMA((2,)), pltpu.SemaphoreType.DMA((2,))],
        compiler_params=_params(("arbitrary", "arbitrary", "arbitrary")),
        name="nsa_attention",
    )(slopes2, proj, cmp_kv, cmp_kv, proj, proj, proj, proj, gates, ovl_t, qtab, kaux_s, kaux_w, *cast_ws)
    return outs[0], list(outs[1:])


def _dil_attn_kernel(sl_ref, q_ref, *refs, window, has_prev):
    if has_prev:
        kp_ref, kc_ref, vp_ref, vc_ref, o_ref, lse_ref, bias_sc = refs
    else:
        kc_ref, vc_ref, o_ref, lse_ref, bias_sc = refs
        kp_ref, vp_ref = kc_ref, vc_ref
    t = ATT_TILE
    i = pl.program_id(2)

    @pl.when((pl.program_id(0) == 0) & (pl.program_id(1) == 0) & (i == 0))
    def _():
        row = lax.broadcasted_iota(jnp.int32, (t, 2 * t), 0)
        col = lax.broadcasted_iota(jnp.int32, (t, 2 * t), 1)
        dist = row - col + t
        band = (dist >= 0) & (dist <= window)
        distf = dist.astype(jnp.float32)
        for h in range(N_HEADS):
            bias = sl_ref[h] * distf
            bias_sc[0, h] = jnp.where(band & (col >= t), bias, -NEG_INF)
            bias_sc[1, h] = jnp.where(band, bias, -NEG_INF)

    var = jnp.minimum(i, 1)
    lane = lax.broadcasted_iota(jnp.int32, (t, LANES), 1)
    lse_all = jnp.zeros((t, LANES), jnp.float32)
    for h in range(N_HEADS):
        hs = slice(h * HEAD_DIM, (h + 1) * HEAD_DIM)
        q = q_ref[0, 0, :, hs]
        k = jnp.concatenate([kp_ref[0, 0, :, hs], kc_ref[0, 0, :, hs]], axis=0)
        v = jnp.concatenate([vp_ref[0, 0, :, hs], vc_ref[0, 0, :, hs]], axis=0)
        s = lax.dot_general(q, k, _NT, preferred_element_type=jnp.float32) - bias_sc[var, h]
        m = jnp.max(s, axis=-1, keepdims=True)
        e = jnp.exp2(s - m)
        l = jnp.sum(e, axis=-1, keepdims=True)
        o = jnp.dot(e.astype(v.dtype), v, preferred_element_type=jnp.float32) / l
        o_ref[0, 0, :, hs] = o.astype(o_ref.dtype)
        lse_all = jnp.where(lane == h, m + jnp.log2(l), lse_all)
    lse_ref[0, 0] = lse_all


def _dil_attention(q, kv, slopes2, win, r):
    b, _, ln, _ = q.shape
    t = ATT_TILE

    def cur(c):
        return pl.BlockSpec((1, 1, t, ATT_WIDTH), lambda bi, rho, i, c=c: (bi, rho, i, c))

    def prev(c):
        return pl.BlockSpec((1, 1, t, ATT_WIDTH), lambda bi, rho, i, c=c: (bi, rho, jnp.maximum(i - 1, 0), c))

    has_prev = ln > t
    kv_specs = [prev(0), cur(0), prev(1), cur(1)] if has_prev else [cur(0), cur(1)]
    return pl.pallas_call(
        functools.partial(_dil_attn_kernel, window=win // r, has_prev=has_prev),
        grid=(b, r, ln // t),
        in_specs=[pl.BlockSpec(memory_space=pltpu.SMEM), cur(0)] + kv_specs,
        out_specs=[pl.BlockSpec((1, 1, t, ATT_WIDTH), lambda bi, rho, i: (bi, rho, i, 0)),
                   pl.BlockSpec((1, 1, t, LANES), lambda bi, rho, i: (bi, rho, i, 0))],
        out_shape=[jax.ShapeDtypeStruct((b, r, ln, ATT_WIDTH), jnp.bfloat16),
                   jax.ShapeDtypeStruct((b, r, ln, LANES), jnp.float32)],
        scratch_shapes=[pltpu.VMEM((2, N_HEADS, t, 2 * t), jnp.float32)],
        compiler_params=_params(("arbitrary", "arbitrary", "arbitrary")),
        name="dilated_attention_r%d" % r,
    )(slopes2 * r, q, *([kv] * len(kv_specs)))


def _dil_merge_kernel(*refs, dilations):
    ng = len(dilations)
    o_refs, l_refs, out_ref, o_sc, l_sc = refs[:ng], refs[ng:2 * ng], refs[2 * ng], refs[2 * ng + 1], refs[2 * ng + 2]
    ts = out_ref.shape[1]
    ls = []
    for gi, r in enumerate(dilations):
        if r == 1:
            ls.append(l_refs[gi][0, 0])
            continue
        for rho in range(r):
            l_sc[gi, pl.ds(rho, ts // r, stride=r), :] = l_refs[gi][0, rho]
            for h in range(N_HEADS):
                o_sc[gi, h, pl.ds(rho, ts // r, stride=r), :] = (
                    o_refs[gi][0, rho, :, h * HEAD_DIM:(h + 1) * HEAD_DIM].astype(jnp.float32))
        ls.append(l_sc[gi])
    m = functools.reduce(jnp.maximum, ls)
    es = [jnp.exp2(l - m) for l in ls]
    den = functools.reduce(lambda a, b: a + b, es)
    ws = [e / den for e in es]
    for h in range(N_HEADS):
        hs = slice(h * HEAD_DIM, (h + 1) * HEAD_DIM)
        acc = None
        for gi, r in enumerate(dilations):
            og = o_refs[gi][0, 0, :, hs].astype(jnp.float32) if r == 1 else o_sc[gi, h]
            term = ws[gi][:, h:h + 1] * og
            acc = term if acc is None else acc + term
        out_ref[0, :, hs] = acc.astype(out_ref.dtype)


def _dil_merge(outs, lses, dilations):
    b, _, _, w = outs[0].shape
    s = outs[0].shape[1] * outs[0].shape[2]
    ts = min(ROW_BLOCK, s)
    ng = len(dilations)

    def spec(r, width):
        return pl.BlockSpec((1, r, ts // r, width), lambda i, j: (i, 0, j, 0))

    return pl.pallas_call(
        functools.partial(_dil_merge_kernel, dilations=dilations),
        grid=(b, s // ts),
        in_specs=[spec(r, w) for r in dilations] + [spec(r, LANES) for r in dilations],
        out_specs=pl.BlockSpec((1, ts, w), lambda i, j: (i, j, 0)),
        out_shape=jax.ShapeDtypeStruct((b, s, w), jnp.bfloat16),
        scratch_shapes=[pltpu.VMEM((ng, w // HEAD_DIM, ts, HEAD_DIM), jnp.float32),
                        pltpu.VMEM((ng, ts, LANES), jnp.float32)],
        compiler_params=_params(("parallel", "parallel")),
        name="dilated_merge",
    )(*outs, *lses)


ROW_TILE = 8


def _gather_rows(idx_ref, src_hbm, dst, sem):
    def body(i, carry):
        for j in range(ROW_TILE):
            pltpu.make_async_copy(src_hbm.at[idx_ref[0, 0, i * ROW_TILE + j]], dst.at[i, :, j], sem).start()
        return carry
    lax.fori_loop(0, dst.shape[0], body, 0)


def _gather_tiled_rows(idx_ref, src_hbm, dst, sem):
    def body(i, carry):
        for j in range(ROW_TILE):
            pltpu.make_async_copy(src_hbm.at[pl.ds(idx_ref[0, 0, i * ROW_TILE + j], 1)],
                                  dst.at[i, pl.ds(j, 1)], sem).start(priority=j % 2)
        return carry
    lax.fori_loop(0, dst.shape[0], body, 0)


def _wait_rows(dst, sem):
    pltpu.make_async_copy(dst, dst, sem).wait()


def _expert_kernel(be_ref, nu_ref, tokc_ref, tokn_ref, h_hbm, wgu_ref, wd_ref, o_ref, xbuf, sem):
    del be_ref
    de = wd_ref.shape[1]
    blk = pl.program_id(0)
    n_used = nu_ref[0]
    slot = blk % 2

    @pl.when(blk == 0)
    def _():
        _gather_rows(tokc_ref, h_hbm, xbuf.at[0], sem.at[0])

    @pl.when(blk + 1 < n_used)
    def _():
        _gather_rows(tokn_ref, h_hbm, xbuf.at[1 - slot], sem.at[1 - slot])

    @pl.when(blk < n_used)
    def _():
        _wait_rows(xbuf.at[slot], sem.at[slot])
        x = _load_row_tiles(xbuf.at[slot]).astype(jnp.bfloat16)
        gate = jnp.dot(x, wgu_ref[0, :, :de], preferred_element_type=jnp.float32)
        up = jnp.dot(x, wgu_ref[0, :, de:], preferred_element_type=jnp.float32)
        hid = (gate * jax.nn.sigmoid(gate) * up).astype(jnp.bfloat16)
        o_ref[...] = jnp.dot(hid, wd_ref[0], preferred_element_type=jnp.float32)

    @pl.when(blk >= n_used)
    def _():
        o_ref[...] = jnp.zeros(o_ref.shape, o_ref.dtype)


def _expert_blocks(h3, slot_tok, blk_e, n_used, w_gate_up, w_down):
    t, ct, _ = h3.shape
    d = ct * LANES
    nblk = blk_e.shape[0]
    de = w_down.shape[1]
    assert de % LANES == 0
    tok = slot_tok.reshape(nblk, 1, MOE_BLOCK)
    grid_spec = pltpu.PrefetchScalarGridSpec(
        num_scalar_prefetch=2,
        grid=(nblk,),
        in_specs=[pl.BlockSpec((1, 1, MOE_BLOCK), lambda i, be, nu: (i, 0, 0), memory_space=pltpu.SMEM),
                  pl.BlockSpec((1, 1, MOE_BLOCK), lambda i, be, nu: (jnp.minimum(i + 1, nblk - 1), 0, 0),
                               memory_space=pltpu.SMEM),
                  pl.BlockSpec(memory_space=pl.ANY),
                  pl.BlockSpec((1, d, 2 * de), lambda i, be, nu: (be[i], 0, 0)),
                  pl.BlockSpec((1, de, d), lambda i, be, nu: (be[i], 0, 0))],
        out_specs=pl.BlockSpec((MOE_BLOCK, d), lambda i, be, nu: (i, 0)),
        scratch_shapes=[pltpu.VMEM((2, MOE_BLOCK // ROW_TILE, ct, ROW_TILE, LANES), jnp.float32),
                        pltpu.SemaphoreType.DMA((2,))],
    )
    return pl.pallas_call(
        _expert_kernel,
        grid_spec=grid_spec,
        out_shape=jax.ShapeDtypeStruct((nblk * MOE_BLOCK, d), jnp.float32),
        compiler_params=_params(("arbitrary",)),
        name="moe_expert_blocks",
    )(blk_e, n_used, tok, tok, h3, w_gate_up, w_down)


def _combine_kernel(d0c_ref, d1c_ref, d0n_ref, d1n_ref, x_ref, w0_ref, w1_ref, g_ref, gn_ref, yo_hbm, o_ref,
                    buf, sem, *, out_norm):
    i = pl.program_id(0)
    n = pl.num_programs(0)
    slot = i % 2
    tt = x_ref.shape[0]

    @pl.when(i == 0)
    def _():
        _gather_tiled_rows(d0c_ref, yo_hbm, buf.at[0, 0], sem.at[0])
        _gather_tiled_rows(d1c_ref, yo_hbm, buf.at[0, 1], sem.at[0])

    @pl.when(i + 1 < n)
    def _():
        _gather_tiled_rows(d0n_ref, yo_hbm, buf.at[1 - slot, 0], sem.at[1 - slot])
        _gather_tiled_rows(d1n_ref, yo_hbm, buf.at[1 - slot, 1], sem.at[1 - slot])

    _wait_rows(buf.at[slot, 0], sem.at[slot])
    _wait_rows(buf.at[slot, 1], sem.at[slot])
    y = (w0_ref[...][:, 0:1] * buf[slot, 0].reshape(x_ref.shape)
         + w1_ref[...][:, 0:1] * buf[slot, 1].reshape(x_ref.shape))
    out = x_ref[...] + g_ref[0] * y
    o_ref[...] = _rms(out, gn_ref[...]) if out_norm else out


def _moe_combine(x2d, yo, dest, wts, gate, s, out_norm_g=None):
    t, d = x2d.shape
    out_norm = out_norm_g is not None
    gn = (out_norm_g if out_norm else jnp.ones((d,), jnp.float32)).reshape(1, d)
    tt = min(MOE_ROW_BLOCK, s)
    nt = t // tt
    d0 = dest[:, 0].reshape(nt, 1, tt)
    d1 = dest[:, 1].reshape(nt, 1, tt)
    w0 = jnp.broadcast_to(wts[:, 0:1], (t, LANES))
    w1 = jnp.broadcast_to(wts[:, 1:2], (t, LANES))
    b = gate.shape[0]
    cur = pl.BlockSpec((1, 1, tt), lambda i: (i, 0, 0), memory_space=pltpu.SMEM)
    nxt = pl.BlockSpec((1, 1, tt), lambda i: (jnp.minimum(i + 1, nt - 1), 0, 0), memory_space=pltpu.SMEM)
    return pl.pallas_call(
        functools.partial(_combine_kernel, out_norm=out_norm),
        grid=(nt,),
        in_specs=[cur, cur, nxt, nxt,
                  pl.BlockSpec((tt, d), lambda i: (i, 0)),
                  pl.BlockSpec((tt, LANES), lambda i: (i, 0)),
                  pl.BlockSpec((tt, LANES), lambda i: (i, 0)),
                  pl.BlockSpec((1, 1, d), lambda i: (i * tt // s, 0, 0)),
                  pl.BlockSpec((1, d), lambda i: (0, 0)),
                  pl.BlockSpec(memory_space=pl.ANY)],
        out_specs=pl.BlockSpec((tt, d), lambda i: (i, 0)),
        out_shape=jax.ShapeDtypeStruct((t, d), jnp.float32),
        scratch_shapes=[pltpu.VMEM((2, 2, tt // ROW_TILE, ROW_TILE, d), jnp.float32),
                        pltpu.SemaphoreType.DMA((2,))],
        compiler_params=_params(("arbitrary",)),
        name="moe_combine_residual",
    )(d0, d1, d0, d1, x2d, w0, w1, gate.reshape(b, 1, d), gn, yo)


def _dispatch_tables(idx):
    t = idx.shape[0]
    a = t * TOP_K
    flat_e = idx.reshape(a)
    onehot = (flat_e[:, None] == jnp.arange(N_EXPERTS, dtype=jnp.int32)[None, :]).astype(jnp.int32)
    csum = jnp.cumsum(onehot, axis=0)
    rank = jnp.take_along_axis(csum, flat_e[:, None], axis=1)[:, 0] - 1
    counts = csum[-1]
    padded = (counts + MOE_BLOCK - 1) // MOE_BLOCK * MOE_BLOCK
    pad_end = jnp.cumsum(padded)
    pad_start = pad_end - padded
    dest = pad_start[flat_e] + rank
    nblk = -(-(a + N_EXPERTS * MOE_BLOCK) // MOE_BLOCK)
    cap = nblk * MOE_BLOCK
    slot_tok = jnp.zeros((cap,), jnp.int32).at[dest].set(jnp.arange(a, dtype=jnp.int32) // TOP_K,
                                                         unique_indices=True)
    blk_start = jnp.arange(nblk, dtype=jnp.int32) * MOE_BLOCK
    blk_e = jnp.minimum(jnp.sum((pad_end[None, :] <= blk_start[:, None]).astype(jnp.int32), axis=1), N_EXPERTS - 1)
    n_used = (pad_end[-1:] // MOE_BLOCK).astype(jnp.int32)
    return dest.reshape(t, TOP_K).astype(jnp.int32), slot_tok, blk_e.astype(jnp.int32), n_used


def _moe_layer(x, g, sc, sh, gate, router_w, router_b, layer, w_gate_up, w_down, out_norm_g=None):
    b, s, d = x.shape
    t = b * s
    h, idx, wts = _norm_route(x, g, sc, sh, router_w, router_b)
    idx = idx.transpose(0, 2, 1).reshape(t, TOP_K)
    wts = wts.transpose(0, 2, 1).reshape(t, TOP_K)
    dest, slot_tok, blk_e, n_used = _dispatch_tables(idx)
    yo = _expert_blocks(h, slot_tok, blk_e + layer * N_EXPERTS, n_used, w_gate_up, w_down)
    return _moe_combine(x.reshape(t, d), yo, dest, wts, gate, s, out_norm_g).reshape(b, s, d)


def _nsa_layer(h, x, gate, slopes, j, w_in, w_phi1, w_phi2, phi_pos, w_out, cast_job=((), (), ())):
    b, s, d = h.shape
    t = b * s
    h2d = h.reshape(t, d)
    proj, kv_chunks = _nsa_projection(h2d, w_in, j, s, HEAD_DIM ** -0.5 * LOG2E)
    proj = proj.reshape(b, s, NSA_QKV)
    wg = w_in[j, :, NSA_QKV:].reshape(d, NSA_KV_HEADS, NSA_GATES)
    wg = jnp.pad(wg, ((0, 0), (0, 0), (0, LANES - NSA_GATES))).reshape(1, d, NSA_KV_HEADS * LANES)
    gates = _matmul(h2d, wg, 0, 0, NSA_KV_HEADS * LANES, jnp.float32, act="sigmoid")
    gates = gates.reshape(b, s, NSA_KV_HEADS * LANES)
    cmp_kv = _compress(kv_chunks, w_phi1, w_phi2, phi_pos)
    o, casted = _nsa_attention(proj, cmp_kv, gates, slopes * LOG2E, *cast_job)
    return _matmul_residual(o, w_out, j, x, gate), casted


def _dil_layer(x, g, sc, sh, gate, slopes, j, w_in, w_out):
    b, s, d = x.shape
    dilations = tuple(r for _, r in DIL_PAIRS)
    assert dilations[0] == 1
    hs = _norm_mod_streams(x, g, sc, sh, dilations)
    qs = _matmul_streams(hs[0].reshape(b, s, d), w_in, j, 0, ATT_WIDTH, dilations,
                         scale=HEAD_DIM ** -0.5 * LOG2E)
    outs, lses = [], []
    for gidx, (win, r) in enumerate(DIL_PAIRS):
        off = ATT_WIDTH * (1 + 2 * gidx)
        kv = _matmul(hs[gidx].reshape(b * s, d), w_in, j, off, 2 * ATT_WIDTH, jnp.bfloat16)
        o, lse = _dil_attention(qs[gidx], kv.reshape(b, r, s // r, 2 * ATT_WIDTH), slopes * LOG2E, win, r)
        outs.append(o)
        lses.append(lse)
    o = _dil_merge(outs, lses, dilations)
    return _matmul_residual(o, w_out, j, x, gate)


def kernel(x, c, ada_w, ada_b, norm_mix, norm_ffn, norm_final, nsa_w_in, nsa_w_phi1, nsa_w_phi2, nsa_phi_pos,
           nsa_w_out, dil_w_in, dil_w_out, router_w, router_b, exp_w_gate, exp_w_up, exp_w_down):
    depth = ada_w.shape[0]
    d = x.shape[-1]
    mod = _modulation(c, ada_w, ada_b)
    slopes = 2.0 ** (-ALIBI_MAX_BIAS * jnp.arange(1, N_HEADS + 1, dtype=jnp.float32) / N_HEADS)
    n_exp, _, de = exp_w_gate.shape[1:]
    for i in range(depth):
        sh_m, sc_m, g_m, sh_f, sc_f, g_f = [mod[i, :, k * d:(k + 1) * d] for k in range(6)]
        j = i // 2
        if i % 2 == 0:
            h = _norm_mod(x, norm_mix[i], sc_m, sh_m, jnp.bfloat16)
            cast_job = ((), (), ())
            if i == 0:
                srcs = [w.reshape(-1, w.shape[-1]) for w in (exp_w_gate, exp_w_up, exp_w_down)]
                cast_job = (srcs, ((0, 0), (0, de), (1, 0)), ((srcs[0].shape[0], 2 * de), srcs[2].shape))
            x, casted = _nsa_layer(h, x, g_m, slopes, j, nsa_w_in, nsa_w_phi1[j], nsa_w_phi2[j], nsa_phi_pos[j],
                                   nsa_w_out, cast_job)
            if i == 0:
                e_gate_up = casted[0].reshape(depth * n_exp, d, 2 * de)
                e_down = casted[1].reshape(depth * n_exp, de, d)
        else:
            x = _dil_layer(x, norm_mix[i], sc_m, sh_m, g_m, slopes, j, dil_w_in, dil_w_out)
        x = _moe_layer(x, norm_ffn[i], sc_f, sh_f, g_f, router_w, router_b, i, e_gate_up, e_down,
                       norm_final if i == depth - 1 else None)
    return x
```
